```python
import math
import jax, jax.numpy as jnp
from jax import lax
import numpy as np

D_MODEL = 1024
BATCH = 8
SEQ = 2048
DEPTH = 1
DEC_BATCH = 128
DEC_SEQ = 8
PAST_LEN = 16384
PAGE_SIZE = 128

D_FF = 2816
POOL_WINDOWS = (2, 4, 8, 16)
N_POOL_GROUPS = 4
POOL_WIDTH = D_MODEL
POOL_GROUP = POOL_WIDTH // N_POOL_GROUPS
POOL_HIST = 16 - 1
SSD_EXPAND = 2
D_INNER = SSD_EXPAND * D_MODEL
SSD_HEAD_DIM = 64
N_SSD_HEADS = D_INNER // SSD_HEAD_DIM
N_SSD_GROUPS = 4
HEADS_PER_GROUP = N_SSD_HEADS // N_SSD_GROUPS
D_STATE = 128
CONV_WIDTH = 4
BC_DIM = N_SSD_GROUPS * D_STATE
CONV_DIM = D_INNER + 2 * BC_DIM
CHUNK = 128
N_BRANCHES = 2
IN_PROJ_DIM = POOL_WIDTH + D_INNER + CONV_DIM + N_SSD_HEADS + N_BRANCHES * D_MODEL
EPS = 1e-6

kernel_name = 'hybrid_pool_ssd_macaron_step'


def rmsnorm(x, g):
    xf = x.astype(jnp.float32)
    y = xf * lax.rsqrt(jnp.mean(xf * xf, axis=-1, keepdims=True) + EPS)
    return (y * g.astype(jnp.float32)).astype(x.dtype)


def swiglu(x, w_in, w_out):
    gu = x @ w_in
    g, u = gu[..., :D_FF], gu[..., D_FF:]
    return (jax.nn.silu(g) * u) @ w_out


def pool_mix(u_ext, T, out_pos0, w_group, scale):
    Bsz = u_ext.shape[0]
    f32 = jnp.float32
    uf = u_ext.astype(f32)
    cs = jnp.concatenate([jnp.zeros((Bsz, 1, POOL_WIDTH), f32), jnp.cumsum(uf, axis=1)], axis=1)
    pos = out_pos0 + jnp.arange(T)
    means = []
    for gi, w in enumerate(POOL_WINDOWS):
        c0, c1 = gi * POOL_GROUP, (gi + 1) * POOL_GROUP
        s = cs[:, POOL_HIST + 1:, c0:c1] - cs[:, POOL_HIST + 1 - w:POOL_HIST + 1 - w + T, c0:c1]
        cnt = jnp.minimum(pos + 1, w).astype(f32)
        means.append(s / cnt[None, :, None])
    d = jnp.concatenate(means, axis=-1) - uf[:, POOL_HIST:]
    d = d.astype(u_ext.dtype).reshape(Bsz, T, N_POOL_GROUPS, POOL_GROUP)
    y = jnp.einsum('btgc,gcd->btgd', d, w_group).reshape(Bsz, T, POOL_WIDTH)
    return y * scale


def causal_conv(xbc, hist, w, b):
    T = xbc.shape[1]
    xp = jnp.concatenate([hist, xbc], axis=1)
    y = b + xp[:, 0:T] * w[0]
    for k in range(1, CONV_WIDTH):
        y = y + xp[:, k:k + T] * w[k]
    return jax.nn.silu(y), xp[:, -(CONV_WIDTH - 1):]


def ssd_scan(x, dt, A, bm, cm, h0):
    f32 = jnp.float32
    Bsz, T = x.shape[0], x.shape[1]
    L = min(CHUNK, T)
    pad = (-T) % L
    if pad:
        padf = lambda a: jnp.pad(a, [(0, 0), (0, pad)] + [(0, 0)] * (a.ndim - 2))
        x, dt, bm, cm = padf(x), padf(dt), padf(bm), padf(cm)
    Tp = T + pad
    nc = Tp // L
    G, R, P, N = N_SSD_GROUPS, HEADS_PER_GROUP, SSD_HEAD_DIM, D_STATE
    xr = x.astype(f32).reshape(Bsz, nc, L, G, R, P)
    dtr = dt.astype(f32).reshape(Bsz, nc, L, G, R)
    br = bm.astype(f32).reshape(Bsz, nc, L, G, N)
    cr = cm.astype(f32).reshape(Bsz, nc, L, G, N)
    a_cs = jnp.cumsum(dtr * A.reshape(G, R), axis=2)
    xdt = xr * dtr[..., None]
    acs_t = jnp.moveaxis(a_cs, 2, -1)
    causal = jnp.tril(jnp.ones((L, L), dtype=bool))
    seg = jnp.where(causal, acs_t[..., :, None] - acs_t[..., None, :], -jnp.inf)
    decay = jnp.exp(seg)
    cb = jnp.einsum('bclgn,bcsgn->bcgls', cr, br)
    y_diag = jnp.einsum('bcgrls,bcsgrp->bclgrp', cb[:, :, :, None] * decay, xdt)
    decay_end = jnp.exp(a_cs[:, :, -1:] - a_cs)
    st = jnp.einsum('bclgn,bclgr,bclgrp->bcgrpn', br, decay_end, xdt)
    chunk_decay = jnp.exp(a_cs[:, :, -1])

    def step(h, inp):
        s_c, d_c = inp
        return h * d_c[..., None, None] + s_c, h

    h_last, h_prev = lax.scan(step, h0.astype(f32).reshape(Bsz, G, R, P, N),
                              (jnp.moveaxis(st, 1, 0), jnp.moveaxis(chunk_decay, 1, 0)))
    h_prev = jnp.moveaxis(h_prev, 0, 1)
    y_off = jnp.einsum('bclgn,bcgrpn,bclgr->bclgrp', cr, h_prev, jnp.exp(a_cs))
    y = (y_diag + y_off).reshape(Bsz, Tp, N_SSD_HEADS, P)[:, :T]
    return y, h_last.reshape(Bsz, N_SSD_HEADS, P, N)


def token_mixing(hn, pool_hist, out_pos0, conv_hist, ssm_h0, w_in, pool_w_group, pool_scale,
                 pool_w_out, conv_w, conv_b, dt_bias, a_log, d_skip, ssd_norm, ssd_w_out, w_o):
    f32 = jnp.float32
    Bsz, T, _ = hn.shape
    proj = hn @ w_in
    s1 = POOL_WIDTH
    s2 = s1 + D_INNER
    s3 = s2 + CONV_DIM
    s4 = s3 + N_SSD_HEADS
    u, z, xbc, dt_raw, gate_logits = (proj[..., :s1], proj[..., s1:s2], proj[..., s2:s3],
                                      proj[..., s3:s4], proj[..., s4:])
    u_ext = jnp.concatenate([pool_hist.astype(u.dtype), u], axis=1)
    branch_pool = pool_mix(u_ext, T, out_pos0, pool_w_group, pool_scale) @ pool_w_out
    new_pool = u_ext[:, -POOL_HIST:]
    xbc_c, new_conv = causal_conv(xbc, conv_hist.astype(xbc.dtype), conv_w, conv_b)
    xs = xbc_c[..., :D_INNER].reshape(Bsz, T, N_SSD_HEADS, SSD_HEAD_DIM)
    bm = xbc_c[..., D_INNER:D_INNER + BC_DIM].reshape(Bsz, T, N_SSD_GROUPS, D_STATE)
    cm = xbc_c[..., D_INNER + BC_DIM:].reshape(Bsz, T, N_SSD_GROUPS, D_STATE)
    dt = jax.nn.softplus(dt_raw.astype(f32) + dt_bias.astype(f32))
    A = -jnp.exp(a_log.astype(f32))
    y, new_ssm = ssd_scan(xs, dt, A, bm, cm, ssm_h0)
    y = y + d_skip.astype(f32)[:, None] * xs.astype(f32)
    yg = (y.reshape(Bsz, T, D_INNER) * jax.nn.silu(z.astype(f32)))
    yg = yg.reshape(Bsz, T, N_SSD_GROUPS, D_INNER // N_SSD_GROUPS)
    yg = yg * lax.rsqrt(jnp.mean(yg * yg, axis=-1, keepdims=True) + EPS)
    y = (yg.reshape(Bsz, T, D_INNER) * ssd_norm.astype(f32)).astype(hn.dtype)
    branch_ssd = y @ ssd_w_out
    gates = jax.nn.sigmoid(gate_logits.astype(f32)).reshape(Bsz, T, N_BRANCHES, D_MODEL)
    merged = (gates[:, :, 0] * branch_pool.astype(f32) + gates[:, :, 1] * branch_ssd.astype(f32)).astype(hn.dtype)
    return merged @ w_o, new_pool, new_conv, new_ssm


def setup_inputs(seed: int = 0) -> dict:
    key = jax.random.key(seed)
    ks = iter(jax.random.split(key, 32))
    nrm = lambda shape, scale: scale * jax.random.normal(next(ks), shape, jnp.float32)
    gain = lambda shape: 1.0 + nrm(shape, 0.05)
    Lr = DEPTH
    x_prompt = nrm((BATCH, SEQ, D_MODEL), 1.0)
    x_sample = nrm((DEC_BATCH, DEC_SEQ, D_MODEL), 1.0)
    state_pool = nrm((Lr, DEC_BATCH, POOL_HIST, POOL_WIDTH), 1.0)
    state_conv = nrm((Lr, DEC_BATCH, CONV_WIDTH - 1, CONV_DIM), 1.0)
    state_ssm = nrm((Lr, DEC_BATCH, N_SSD_HEADS, SSD_HEAD_DIM, D_STATE), 0.5)
    norm_ffn1 = gain((Lr, D_MODEL))
    ffn1_w_in = nrm((Lr, D_MODEL, 2 * D_FF), D_MODEL ** -0.5)
    ffn1_w_out = nrm((Lr, D_FF, D_MODEL), D_FF ** -0.5)
    norm_mix = gain((Lr, D_MODEL))
    w_in = nrm((Lr, D_MODEL, IN_PROJ_DIM), D_MODEL ** -0.5)
    pool_w_group = nrm((Lr, N_POOL_GROUPS, POOL_GROUP, POOL_GROUP), POOL_GROUP ** -0.5)
    pool_scale = gain((Lr, POOL_WIDTH))
    pool_w_out = nrm((Lr, POOL_WIDTH, D_MODEL), POOL_WIDTH ** -0.5)
    conv_w = nrm((Lr, CONV_WIDTH, CONV_DIM), CONV_WIDTH ** -0.5)
    conv_b = nrm((Lr, CONV_DIM), 0.02)
    dt0 = jnp.exp(jax.random.uniform(next(ks), (Lr, N_SSD_HEADS), jnp.float32,
                                     minval=math.log(1e-3), maxval=math.log(1e-1)))
    dt_bias = dt0 + jnp.log(-jnp.expm1(-dt0))
    a_log = jnp.log(jax.random.uniform(next(ks), (Lr, N_SSD_HEADS), jnp.float32, minval=1.0, maxval=16.0))
    d_skip = gain((Lr, N_SSD_HEADS))
    ssd_norm = gain((Lr, D_INNER))
    ssd_w_out = nrm((Lr, D_INNER, D_MODEL), D_INNER ** -0.5)
    w_o = nrm((Lr, D_MODEL, D_MODEL), D_MODEL ** -0.5)
    norm_ffn2 = gain((Lr, D_MODEL))
    ffn2_w_in = nrm((Lr, D_MODEL, 2 * D_FF), D_MODEL ** -0.5)
    ffn2_w_out = nrm((Lr, D_FF, D_MODEL), D_FF ** -0.5)
    norm_final = gain((D_MODEL,))
    return {'x_prompt': x_prompt, 'x_sample': x_sample, 'state_pool': state_pool,
            'state_conv': state_conv, 'state_ssm': state_ssm, 'norm_ffn1': norm_ffn1,
            'ffn1_w_in': ffn1_w_in, 'ffn1_w_out': ffn1_w_out, 'norm_mix': norm_mix, 'w_in': w_in,
            'pool_w_group': pool_w_group, 'pool_scale': pool_scale, 'pool_w_out': pool_w_out,
            'conv_w': conv_w, 'conv_b': conv_b, 'dt_bias': dt_bias, 'a_log': a_log,
            'd_skip': d_skip, 'ssd_norm': ssd_norm, 'ssd_w_out': ssd_w_out, 'w_o': w_o,
            'norm_ffn2': norm_ffn2, 'ffn2_w_in': ffn2_w_in, 'ffn2_w_out': ffn2_w_out,
            'norm_final': norm_final}


def reference(x_prompt, x_sample, state_pool, state_conv, state_ssm, norm_ffn1, ffn1_w_in,
              ffn1_w_out, norm_mix, w_in, pool_w_group, pool_scale, pool_w_out, conv_w, conv_b,
              dt_bias, a_log, d_skip, ssd_norm, ssd_w_out, w_o, norm_ffn2, ffn2_w_in,
              ffn2_w_out, norm_final):
    def layer(x, pool_hist, out_pos0, conv_hist, h0, l):
        x = x + 0.5 * swiglu(rmsnorm(x, norm_ffn1[l]), ffn1_w_in[l], ffn1_w_out[l])
        mix, new_pool, new_conv, new_ssm = token_mixing(
            rmsnorm(x, norm_mix[l]), pool_hist, out_pos0, conv_hist, h0, w_in[l], pool_w_group[l],
            pool_scale[l], pool_w_out[l], conv_w[l], conv_b[l], dt_bias[l], a_log[l], d_skip[l],
            ssd_norm[l], ssd_w_out[l], w_o[l])
        x = x + mix
        x = x + 0.5 * swiglu(rmsnorm(x, norm_ffn2[l]), ffn2_w_in[l], ffn2_w_out[l])
        return x, new_pool, new_conv, new_ssm

    xp, xs = x_prompt, x_sample
    pp, cp, sp, ps, cs, ss = [], [], [], [], [], []
    for l in range(DEPTH):
        xp, a, b, c = layer(xp, jnp.zeros((BATCH, POOL_HIST, POOL_WIDTH), xp.dtype), 0,
                            jnp.zeros((BATCH, CONV_WIDTH - 1, CONV_DIM), xp.dtype),
                            jnp.zeros((BATCH, N_SSD_HEADS, SSD_HEAD_DIM, D_STATE), jnp.float32), l)
        pp.append(a); cp.append(b); sp.append(c)
        xs, d, e, f = layer(xs, state_pool[l], PAST_LEN, state_conv[l], state_ssm[l], l)
        ps.append(d); cs.append(e); ss.append(f)
    y_prompt = rmsnorm(xp, norm_final)
    y_sample = rmsnorm(xs, norm_final)
    return (y_prompt, y_sample, jnp.stack(pp), jnp.stack(cp), jnp.stack(sp),
            jnp.stack(ps), jnp.stack(cs), jnp.stack(ss))
```

```python
import functools

import jax
import jax.numpy as jnp
from jax import lax
from jax.experimental import pallas as pl
from jax.experimental.pallas import tpu as pltpu

f32 = jnp.float32
bf16 = jnp.bfloat16

D_MODEL = 1024
D_FF = 2816
POOL_WINDOWS = (2, 4, 8, 16)
POOL_WIDTH = D_MODEL
POOL_GROUP = POOL_WIDTH // len(POOL_WINDOWS)
POOL_HIST = max(POOL_WINDOWS) - 1
D_INNER = 2 * D_MODEL
SSD_HEAD_DIM = 64
N_SSD_HEADS = D_INNER // SSD_HEAD_DIM
N_SSD_GROUPS = 4
HEADS_PER_GROUP = N_SSD_HEADS // N_SSD_GROUPS
GROUP_WIDTH = D_INNER // N_SSD_GROUPS
D_STATE = 128
CONV_WIDTH = 4
BC_DIM = N_SSD_GROUPS * D_STATE
CONV_DIM = D_INNER + 2 * BC_DIM
CHUNK = 128
PAST_LEN = 16384
EPS = 1e-6

LANES = 128
SUBLANES = 8
VMEM_LIMIT_BYTES = 56 * 1024 * 1024

FFN_ROWS = 512
FFN_CHUNK = 256
PROMPT_TILE = 256
SAMPLE_SEQS = 16
DECODE_SEQS = 4
POOL_PAD = 16
CONV_PAD = 8
HEAD_LANES = LANES

_NT = (((1,), (1,)), ((), ()))
_TN = (((0,), (0,)), ((), ()))


def _rms(x, g):
    return x * lax.rsqrt(jnp.mean(x * x, axis=-1, keepdims=True) + EPS) * g


def _silu(v):
    return v * jax.nn.sigmoid(v)


def _softplus(v):
    return jnp.maximum(v, 0.0) + jnp.log1p(jnp.exp(-jnp.abs(v)))


def _dot(a, b):
    return jnp.dot(a, b, preferred_element_type=f32)


def _split3(v):
    hi = v.astype(bf16)
    r1 = v - hi.astype(f32)
    mid = r1.astype(bf16)
    lo = (r1 - mid.astype(f32)).astype(bf16)
    return hi, mid, lo


def _const_spec(shape):
    nd = len(shape)
    return pl.BlockSpec(shape, lambda *_: (0,) * nd, pipeline_mode=pl.Buffered(1))


def _params(n_grid):
    return pltpu.CompilerParams(dimension_semantics=("arbitrary",) * n_grid,
                                vmem_limit_bytes=VMEM_LIMIT_BYTES)


def _ffn_kernel(x_ref, g_ref, win_ref, wout_ref, gfin_ref, o_ref, hn_s, act_s, *, final_norm):
    hn_s[...] = _rms(x_ref[...], g_ref[...]).astype(bf16)
    for c in range(D_FF // FFN_CHUNK):
        lo = c * FFN_CHUNK
        gate = _dot(hn_s[...], win_ref[:, lo:lo + FFN_CHUNK])
        up = _dot(hn_s[...], win_ref[:, D_FF + lo:D_FF + lo + FFN_CHUNK])
        act_s[:, lo:lo + FFN_CHUNK] = (_silu(gate) * up).astype(bf16)
    out = x_ref[...] + 0.5 * _dot(act_s[...], wout_ref[...])
    if final_norm:
        out = _rms(out, gfin_ref[...])
    o_ref[...] = out


def _ffn(x2d, g, w_in_b, w_out_b, gfin, *, final_norm):
    rows = x2d.shape[0]
    assert rows % FFN_ROWS == 0
    return pl.pallas_call(
        functools.partial(_ffn_kernel, final_norm=final_norm),
        grid=(rows // FFN_ROWS,),
        in_specs=[pl.BlockSpec((FFN_ROWS, D_MODEL), lambda i: (i, 0)),
                  _const_spec((1, D_MODEL)),
                  _const_spec((D_MODEL, 2 * D_FF)),
                  _const_spec((D_FF, D_MODEL)),
                  _const_spec((1, D_MODEL))],
        out_specs=pl.BlockSpec((FFN_ROWS, D_MODEL), lambda i: (i, 0)),
        out_shape=jax.ShapeDtypeStruct((rows, D_MODEL), f32),
        scratch_shapes=[pltpu.VMEM((FFN_ROWS, D_MODEL), bf16), pltpu.VMEM((FFN_ROWS, D_FF), bf16)],
        compiler_params=_params(1),
        name="ffn_final" if final_norm else "ffn",
    )(x2d, g, w_in_b, w_out_b, gfin)


def _pool_branch(uext_ref, T, pos0, pgw_ref, pscale_ref, pwo_ref):
    S = uext_ref.shape[0]
    t_idx = lax.broadcasted_iota(jnp.int32, (S, T, 1), 1)
    n_seen = pos0 + 1 + t_idx
    mixed = []
    for gi, w in enumerate(POOL_WINDOWS):
        cols = slice(gi * POOL_GROUP, (gi + 1) * POOL_GROUP)
        cur = uext_ref[:, pl.ds(POOL_PAD, T), cols]
        s = cur
        for k in range(1, w):
            s = s + uext_ref[:, pl.ds(POOL_PAD - k, T), cols]
        cnt = jnp.minimum(n_seen, w).astype(f32)
        d = (s / cnt - cur).reshape(S * T, POOL_GROUP).astype(bf16)
        mixed.append(_dot(d, pgw_ref[gi]))
    y = jnp.concatenate(mixed, axis=-1) * pscale_ref[...]
    return _dot(y.astype(bf16), pwo_ref[...])


def _conv_silu(xp_ref, T, convw_ref, convb_ref, c0, width):
    S = xp_ref.shape[0]
    cols = slice(c0, c0 + width)
    y = convb_ref[:, cols]
    for k in range(CONV_WIDTH):
        y = y + xp_ref[:, pl.ds(CONV_PAD - (CONV_WIDTH - 1) + k, T), cols] * convw_ref[k:k + 1, cols]
    return _silu(y).reshape(S * T, width)


def _gate_merge_out(x, hn_s, y_ref, bp, wz_ref, wgate_ref, snorm_ref, swo_ref, wo_ref, yn_s):
    for g in range(N_SSD_GROUPS):
        cols = slice(g * GROUP_WIDTH, (g + 1) * GROUP_WIDTH)
        z = _dot(hn_s[...], wz_ref[:, cols])
        yg = y_ref[:, cols] * _silu(z)
        yg = yg * lax.rsqrt(jnp.mean(yg * yg, axis=-1, keepdims=True) + EPS)
        yn_s[:, cols] = (yg * snorm_ref[:, cols]).astype(bf16)
    branch_ssd = _dot(yn_s[...], swo_ref[...])
    gate_pool = jax.nn.sigmoid(_dot(hn_s[...], wgate_ref[:, 0:D_MODEL]))
    gate_ssd = jax.nn.sigmoid(_dot(hn_s[...], wgate_ref[:, D_MODEL:2 * D_MODEL]))
    merged = (gate_pool * bp + gate_ssd * branch_ssd).astype(bf16)
    return x + _dot(merged, wo_ref[...])


def _ssd_chunk(rows, xs_s, b_s, c_s, dt_s, y_s, ht_s, alog_ref, dskip_ref, e_ref):
    L = CHUNK
    ri = lax.broadcasted_iota(jnp.int32, (L, L), 0)
    ci = lax.broadcasted_iota(jnp.int32, (L, L), 1)
    causal = ri >= ci
    tril = jnp.where(causal, 1.0, 0.0).astype(bf16)
    first_head = lax.broadcasted_iota(jnp.int32, (L, LANES), 1) < SSD_HEAD_DIM

    dt = dt_s[rows, :]
    dA = dt * (-jnp.exp(alog_ref[...]))
    hi, mid, lo = _split3(dA)
    acc = _dot(tril, jnp.concatenate([hi, mid, lo], axis=1))
    a = acc[:, 0:LANES] + acc[:, LANES:2 * LANES] + acc[:, 2 * LANES:3 * LANES]
    aT = a.T
    dtT = dt.T
    wT = jnp.exp(aT[:, L - 1:L] - aT) * dtT
    end_decay = jnp.broadcast_to(jnp.exp(a[L - 1:L, :]), (2 * SUBLANES, HEAD_LANES))
    e_hi, e_mid, e_lo = (t.astype(f32) for t in _split3(end_decay))
    sel = lax.broadcasted_iota(jnp.int32, (2 * SUBLANES, HEAD_LANES), 0)
    stacked = jnp.where(sel == 0, e_hi, jnp.where(sel == 1, e_mid, jnp.where(sel == 2, e_lo, 0.0)))
    cdec = jnp.sum(_dot(stacked.astype(bf16), e_ref[...]), axis=0, keepdims=True)

    def head_mats(h, cb, c_f, bT):
        acol = jnp.broadcast_to(a[:, h:h + 1], (L, L))
        decay = jnp.exp(jnp.where(causal, acol - aT[h:h + 1, :], -jnp.inf))
        m = (cb * decay * dtT[h:h + 1, :]).astype(bf16)
        ec = (c_f * jnp.exp(acol)).astype(bf16)
        bw = (bT * wT[h:h + 1, :]).astype(bf16)
        return m, ec, bw

    for q in range(N_SSD_HEADS // 2):
        g = (2 * q) // HEADS_PER_GROUP
        if (2 * q) % HEADS_PER_GROUP == 0:
            gcols = slice(g * D_STATE, (g + 1) * D_STATE)
            b_f = b_s[rows, gcols]
            c_f = c_s[rows, gcols]
            cb = lax.dot_general(c_f.astype(bf16), b_f.astype(bf16), _NT, preferred_element_type=f32)
            bT = b_f.T
        cols = slice(q * LANES, (q + 1) * LANES)
        xq = xs_s[rows, cols]
        x_lo = jnp.where(first_head, xq, 0.0).astype(bf16)
        x_hi = jnp.where(first_head, 0.0, xq).astype(bf16)
        hq = ht_s[:, cols]
        h_lo = jnp.where(first_head, hq, 0.0).astype(bf16)
        h_hi = jnp.where(first_head, 0.0, hq).astype(bf16)
        m_a, ec_a, bw_a = head_mats(2 * q, cb, c_f, bT)
        m_b, ec_b, bw_b = head_mats(2 * q + 1, cb, c_f, bT)
        lhs = jnp.concatenate([m_a, ec_a, m_b, ec_b], axis=1)
        rhs = jnp.concatenate([x_lo, h_lo, x_hi, h_hi], axis=0)
        y_s[rows, cols] = _dot(lhs, rhs) + dskip_ref[:, cols] * xq
        st = _dot(jnp.concatenate([bw_a, bw_b], axis=1), jnp.concatenate([x_lo, x_hi], axis=0))
        ht_s[:, cols] = hq * cdec[:, cols] + st


def _mixer_prompt_kernel(x_ref, gmix_ref, wu_ref, wz_ref, wxbc_ref, wdt_ref, wgate_ref, pgw_ref, pscale_ref,
                         pwo_ref, convw_ref, convb_ref, dtb_ref, alog_ref, dskip_ref, snorm_ref, swo_ref,
                         wo_ref, e_ref,
                         o_ref, npool_ref, nconv_ref, nssm_ref,
                         hn_s, uext_s, xp_s, xs_s, b_s, c_s, dt_s, y_s, ht_s, yn_s):
    T = PROMPT_TILE
    ti = pl.program_id(1)

    @pl.when(ti == 0)
    def _():
        uext_s[:, 0:POOL_PAD, :] = jnp.zeros((1, POOL_PAD, POOL_WIDTH), f32)
        xp_s[:, 0:CONV_PAD, :] = jnp.zeros((1, CONV_PAD, CONV_DIM), f32)
        ht_s[...] = jnp.zeros(ht_s.shape, f32)

    x = x_ref[0]
    hn_s[...] = _rms(x, gmix_ref[...]).astype(bf16)

    uext_s[:, POOL_PAD:POOL_PAD + T, :] = _dot(hn_s[...], wu_ref[...]).reshape(1, T, POOL_WIDTH)
    bp = _pool_branch(uext_s, T, ti * T, pgw_ref, pscale_ref, pwo_ref)

    xp_s[:, CONV_PAD:CONV_PAD + T, :] = _dot(hn_s[...], wxbc_ref[...]).reshape(1, T, CONV_DIM)
    for g in range(N_SSD_GROUPS):
        xs_s[:, g * GROUP_WIDTH:(g + 1) * GROUP_WIDTH] = _conv_silu(
            xp_s, T, convw_ref, convb_ref, g * GROUP_WIDTH, GROUP_WIDTH)
    b_s[...] = _conv_silu(xp_s, T, convw_ref, convb_ref, D_INNER, BC_DIM)
    c_s[...] = _conv_silu(xp_s, T, convw_ref, convb_ref, D_INNER + BC_DIM, BC_DIM)
    dt_s[...] = _softplus(_dot(hn_s[...], wdt_ref[...]) + dtb_ref[...])

    def chunk(c, carry):
        rows = pl.ds(pl.multiple_of(c * CHUNK, CHUNK), CHUNK)
        _ssd_chunk(rows, xs_s, b_s, c_s, dt_s, y_s, ht_s, alog_ref, dskip_ref, e_ref)
        return carry

    lax.fori_loop(0, T // CHUNK, chunk, 0)

    o_ref[0] = _gate_merge_out(x, hn_s, y_s, bp, wz_ref, wgate_ref, snorm_ref, swo_ref, wo_ref, yn_s)

    @pl.when(ti == pl.num_programs(1) - 1)
    def _():
        npool_ref[0, 0] = uext_s[0, pl.ds(POOL_PAD + T - POOL_HIST, POOL_HIST), :]
        nconv_ref[0, 0] = xp_s[0, pl.ds(CONV_PAD + T - (CONV_WIDTH - 1), CONV_WIDTH - 1), :]
        for g in range(N_SSD_GROUPS):
            hg = ht_s[:, g * GROUP_WIDTH:(g + 1) * GROUP_WIDTH].T
            nssm_ref[0, 0, g * HEADS_PER_GROUP:(g + 1) * HEADS_PER_GROUP] = hg.reshape(
                HEADS_PER_GROUP, SSD_HEAD_DIM, D_STATE)

    uext_s[:, 0:POOL_PAD, :] = uext_s[:, T:T + POOL_PAD, :]
    xp_s[:, 0:CONV_PAD, :] = xp_s[:, T:T + CONV_PAD, :]


def _mixer_prompt(x1, w):
    B, S, _ = x1.shape
    T = PROMPT_TILE
    assert S % T == 0 and T % CHUNK == 0 and T >= POOL_PAD
    consts = [w["gmix"], w["wu"], w["wz"], w["wxbc"], w["wdt"], w["wgate"], w["pgw"], w["pscale"], w["pwo"],
              w["convw"], w["convb"], w["dtb"], w["alog"], w["dskip"], w["snorm"], w["swo"], w["wo"], w["expand"]]
    return pl.pallas_call(
        _mixer_prompt_kernel,
        grid=(B, S // T),
        in_specs=[pl.BlockSpec((1, T, D_MODEL), lambda b, t: (b, t, 0))] + [_const_spec(c.shape) for c in consts],
        out_specs=(pl.BlockSpec((1, T, D_MODEL), lambda b, t: (b, t, 0)),
                   pl.BlockSpec((1, 1, POOL_HIST, POOL_WIDTH), lambda b, t: (0, b, 0, 0)),
                   pl.BlockSpec((1, 1, CONV_WIDTH - 1, CONV_DIM), lambda b, t: (0, b, 0, 0)),
                   pl.BlockSpec((1, 1, N_SSD_HEADS, SSD_HEAD_DIM, D_STATE), lambda b, t: (0, b, 0, 0, 0))),
        out_shape=(jax.ShapeDtypeStruct((B, S, D_MODEL), f32),
                   jax.ShapeDtypeStruct((1, B, POOL_HIST, POOL_WIDTH), f32),
                   jax.ShapeDtypeStruct((1, B, CONV_WIDTH - 1, CONV_DIM), f32),
                   jax.ShapeDtypeStruct((1, B, N_SSD_HEADS, SSD_HEAD_DIM, D_STATE), f32)),
        scratch_shapes=[pltpu.VMEM((T, D_MODEL), bf16),
                        pltpu.VMEM((1, POOL_PAD + T, POOL_WIDTH), f32),
                        pltpu.VMEM((1, CONV_PAD + T, CONV_DIM), f32),
                        pltpu.VMEM((T, D_INNER), f32),
                        pltpu.VMEM((T, BC_DIM), f32),
                        pltpu.VMEM((T, BC_DIM), f32),
                        pltpu.VMEM((T, HEAD_LANES), f32),
                        pltpu.VMEM((T, D_INNER), f32),
                        pltpu.VMEM((D_STATE, D_INNER), f32),
                        pltpu.VMEM((T, D_INNER), bf16)],
        compiler_params=_params(2),
        name="mixer_prompt",
    )(x1, *consts)


def _sample_pre_kernel(x_ref, ph_ref, ch_ref, gmix_ref, wu_ref, wxbc_ref, wdt_ref, pgw_ref, pscale_ref, pwo_ref,
                       convw_ref, convb_ref, dtb_ref,
                       bp_ref, xs_ref, b_ref, c_ref, dt_ref, npool_ref, nconv_ref,
                       uext_s, xp_s, *, T):
    S = SAMPLE_SEQS
    hn = _rms(x_ref[...], gmix_ref[...]).astype(bf16)

    uext_s[:, POOL_PAD - POOL_HIST:POOL_PAD, :] = ph_ref[0]
    uext_s[:, POOL_PAD:POOL_PAD + T, :] = _dot(hn, wu_ref[...]).reshape(S, T, POOL_WIDTH)
    bp_ref[...] = _pool_branch(uext_s, T, PAST_LEN, pgw_ref, pscale_ref, pwo_ref)
    npool_ref[0] = uext_s[:, pl.ds(POOL_PAD + T - POOL_HIST, POOL_HIST), :]

    xp_s[:, CONV_PAD - (CONV_WIDTH - 1):CONV_PAD, :] = ch_ref[0]
    xp_s[:, CONV_PAD:CONV_PAD + T, :] = _dot(hn, wxbc_ref[...]).reshape(S, T, CONV_DIM)
    for g in range(N_SSD_GROUPS):
        xs_ref[:, g * GROUP_WIDTH:(g + 1) * GROUP_WIDTH] = _conv_silu(
            xp_s, T, convw_ref, convb_ref, g * GROUP_WIDTH, GROUP_WIDTH)
    b_ref[...] = _conv_silu(xp_s, T, convw_ref, convb_ref, D_INNER, BC_DIM)
    c_ref[...] = _conv_silu(xp_s, T, convw_ref, convb_ref, D_INNER + BC_DIM, BC_DIM)
    nconv_ref[0] = xp_s[:, pl.ds(CONV_PAD + T - (CONV_WIDTH - 1), CONV_WIDTH - 1), :]
    dt_ref[...] = _softplus(_dot(hn, wdt_ref[...]) + dtb_ref[...])


def _sample_pre(x1, pool_hist, conv_hist, w, T):
    rows = x1.shape[0]
    nseq = rows // T
    S = SAMPLE_SEQS
    R = S * T
    assert nseq % S == 0 and T % SUBLANES == 0
    consts = [w["gmix"], w["wu"], w["wxbc"], w["wdt"], w["pgw"], w["pscale"], w["pwo"], w["convw"], w["convb"],
              w["dtb"]]
    row_spec = lambda width: pl.BlockSpec((R, width), lambda i: (i, 0))
    return pl.pallas_call(
        functools.partial(_sample_pre_kernel, T=T),
        grid=(nseq // S,),
        in_specs=[row_spec(D_MODEL),
                  pl.BlockSpec((1, S, POOL_HIST, POOL_WIDTH), lambda i: (0, i, 0, 0)),
                  pl.BlockSpec((1, S, CONV_WIDTH - 1, CONV_DIM), lambda i: (0, i, 0, 0))]
                 + [_const_spec(c.shape) for c in consts],
        out_specs=(row_spec(D_MODEL), row_spec(D_INNER), row_spec(BC_DIM), row_spec(BC_DIM), row_spec(HEAD_LANES),
                   pl.BlockSpec((1, S, POOL_HIST, POOL_WIDTH), lambda i: (0, i, 0, 0)),
                   pl.BlockSpec((1, S, CONV_WIDTH - 1, CONV_DIM), lambda i: (0, i, 0, 0))),
        out_shape=(jax.ShapeDtypeStruct((rows, D_MODEL), f32),
                   jax.ShapeDtypeStruct((rows, D_INNER), f32),
                   jax.ShapeDtypeStruct((rows, BC_DIM), f32),
                   jax.ShapeDtypeStruct((rows, BC_DIM), f32),
                   jax.ShapeDtypeStruct((rows, HEAD_LANES), f32),
                   jax.ShapeDtypeStruct((1, nseq, POOL_HIST, POOL_WIDTH), f32),
                   jax.ShapeDtypeStruct((1, nseq, CONV_WIDTH - 1, CONV_DIM), f32)),
        scratch_shapes=[pltpu.VMEM((S, POOL_PAD + T, POOL_WIDTH), f32),
                        pltpu.VMEM((S, CONV_PAD + T, CONV_DIM), f32)],
        compiler_params=_params(1),
        name="sample_pre",
    )(x1, pool_hist, conv_hist, *consts)


def _ssd_decode_kernel(xs_ref, b_ref, c_ref, dt_ref, h0_ref, alog_ref, dskip_ref, e_ref, y_ref, hn_ref, *, T):
    assert T == SUBLANES
    row = lax.broadcasted_iota(jnp.int32, (T, HEAD_LANES), 0)
    lane_group = lax.broadcasted_iota(jnp.int32, (T, HEAD_LANES), 1) // HEADS_PER_GROUP
    neg_a = -jnp.exp(alog_ref[...])

    def shift(v, d):
        if d == 0:
            return v
        r = lax.broadcasted_iota(jnp.int32, v.shape, 0)
        return jnp.where(r >= d, pltpu.roll(v, d, axis=0), 0.0)

    def per_seq(s, carry):
        dt = dt_ref[s]
        a = dt * neg_a
        d = 1
        while d < T:
            a = a + shift(a, d)
            d *= 2
        a_end = a[T - 1:T, :]
        x = xs_ref[s]
        bm = b_ref[s]
        cm = c_ref[s]
        terms = []
        for d in range(T):
            cbv = jnp.zeros((T, HEAD_LANES), f32)
            for g in range(N_SSD_GROUPS):
                gc = slice(g * D_STATE, (g + 1) * D_STATE)
                cb = jnp.sum(cm[:, gc] * shift(bm[:, gc], d), axis=-1, keepdims=True)
                cbv = jnp.where(lane_group == g, cb, cbv)
            terms.append(cbv * jnp.exp(a - shift(a, d)) * shift(dt, d))
        terms.append(jnp.exp(a))
        terms.append(jnp.exp(a_end - a) * dt)
        v = jnp.concatenate(terms, axis=0)
        v_hi = v.astype(bf16)
        v_lo = (v - v_hi.astype(f32)).astype(bf16)
        ex = _dot(jnp.concatenate([v_hi, v_lo], axis=0), e_ref[...])
        n = (T + 2) * T
        ex = ex[0:n] + ex[n:2 * n]

        y = dskip_ref[...] * x
        for d in range(T):
            y = y + ex[d * T:(d + 1) * T] * shift(x, d)
        ea = ex[T * T:(T + 1) * T]
        xw = (x * ex[(T + 1) * T:(T + 2) * T]).astype(bf16)
        cdec = jnp.exp(a_end)
        y_groups = []
        for g in range(N_SSD_GROUPS):
            gc = slice(g * D_STATE, (g + 1) * D_STATE)
            cols = slice(g * GROUP_WIDTH, (g + 1) * GROUP_WIDTH)
            heads = slice(g * HEADS_PER_GROUP, (g + 1) * HEADS_PER_GROUP)
            h0g = h0_ref[0, s, heads].reshape(GROUP_WIDTH, D_STATE)
            ch = lax.dot_general(cm[:, gc].astype(bf16), h0g.astype(bf16), _NT, preferred_element_type=f32)
            y_groups.append(y[:, cols] + ea[:, cols] * ch)
            st = lax.dot_general(xw[:, cols], bm[:, gc].astype(bf16), _TN, preferred_element_type=f32)
            for r in range(HEADS_PER_GROUP):
                h = g * HEADS_PER_GROUP + r
                hn_ref[0, s, h] = (h0_ref[0, s, h] * cdec[:, h:h + 1]
                                   + st[r * SSD_HEAD_DIM:(r + 1) * SSD_HEAD_DIM, :])
        y_ref[s] = jnp.concatenate(y_groups, axis=-1)
        return carry

    lax.fori_loop(0, DECODE_SEQS, per_seq, 0)


def _ssd_decode(xs, bm, cm, dt, h0, w, T):
    nseq = xs.shape[0] // T
    Q = DECODE_SEQS
    assert nseq % Q == 0
    seq_spec = lambda width: pl.BlockSpec((Q, T, width), lambda i: (i, 0, 0))
    state_spec = pl.BlockSpec((1, Q, N_SSD_HEADS, SSD_HEAD_DIM, D_STATE), lambda i: (0, i, 0, 0, 0))
    consts = [w["alog"], w["dskip"], w["expand"]]
    return pl.pallas_call(
        functools.partial(_ssd_decode_kernel, T=T),
        grid=(nseq // Q,),
        in_specs=[seq_spec(D_INNER), seq_spec(BC_DIM), seq_spec(BC_DIM), seq_spec(HEAD_LANES), state_spec]
                 + [_const_spec(c.shape) for c in consts],
        out_specs=(seq_spec(D_INNER), state_spec),
        out_shape=(jax.ShapeDtypeStruct((nseq, T, D_INNER), f32), jax.ShapeDtypeStruct(h0.shape, f32)),
        compiler_params=_params(1),
        name="ssd_decode",
    )(xs.reshape(nseq, T, D_INNER), bm.reshape(nseq, T, BC_DIM), cm.reshape(nseq, T, BC_DIM),
      dt.reshape(nseq, T, HEAD_LANES), h0, *consts)


def _sample_post_kernel(x_ref, y_ref, bp_ref, gmix_ref, wz_ref, wgate_ref, snorm_ref, swo_ref, wo_ref,
                        o_ref, hn_s, yn_s):
    x = x_ref[...]
    hn_s[...] = _rms(x, gmix_ref[...]).astype(bf16)
    o_ref[...] = _gate_merge_out(x, hn_s, y_ref, bp_ref[...], wz_ref, wgate_ref, snorm_ref, swo_ref, wo_ref, yn_s)


def _sample_post(x1, y, bp, w, T):
    rows = x1.shape[0]
    R = SAMPLE_SEQS * T
    consts = [w["gmix"], w["wz"], w["wgate"], w["snorm"], w["swo"], w["wo"]]
    row_spec = lambda width: pl.BlockSpec((R, width), lambda i: (i, 0))
    return pl.pallas_call(
        _sample_post_kernel,
        grid=(rows // R,),
        in_specs=[row_spec(D_MODEL), row_spec(D_INNER), row_spec(D_MODEL)] + [_const_spec(c.shape) for c in consts],
        out_specs=row_spec(D_MODEL),
        out_shape=jax.ShapeDtypeStruct((rows, D_MODEL), f32),
        scratch_shapes=[pltpu.VMEM((R, D_MODEL), bf16), pltpu.VMEM((R, D_INNER), bf16)],
        compiler_params=_params(1),
        name="sample_post",
    )(x1, y, bp, *consts)


def _layer_weights(l, norm_mix, w_in, pool_w_group, pool_scale, pool_w_out, conv_w, conv_b, dt_bias, a_log, d_skip,
                   ssd_norm, ssd_w_out, w_o):
    s1 = POOL_WIDTH
    s2 = s1 + D_INNER
    s3 = s2 + CONV_DIM
    s4 = s3 + N_SSD_HEADS
    wl = w_in[l]
    pad_heads = lambda v: jnp.pad(v, ((0, 0), (0, HEAD_LANES - N_SSD_HEADS)))
    head_of_lane = jnp.arange(D_INNER, dtype=jnp.int32) // SSD_HEAD_DIM
    expand = (jnp.arange(HEAD_LANES, dtype=jnp.int32)[:, None] == head_of_lane[None, :]).astype(bf16)
    return dict(
        gmix=norm_mix[l][None, :],
        wu=wl[:, :s1].astype(bf16), wz=wl[:, s1:s2].astype(bf16), wxbc=wl[:, s2:s3].astype(bf16),
        wdt=pad_heads(wl[:, s3:s4]).astype(bf16), wgate=wl[:, s4:].astype(bf16),
        pgw=pool_w_group[l].astype(bf16), pscale=pool_scale[l][None, :], pwo=pool_w_out[l].astype(bf16),
        convw=conv_w[l], convb=conv_b[l][None, :],
        dtb=pad_heads(dt_bias[l][None, :]), alog=pad_heads(a_log[l][None, :]),
        dskip=jnp.repeat(d_skip[l], SSD_HEAD_DIM)[None, :], snorm=ssd_norm[l][None, :],
        swo=ssd_w_out[l].astype(bf16), wo=w_o[l].astype(bf16), expand=expand)


def kernel(x_prompt, x_sample, state_pool, state_conv, state_ssm, norm_ffn1, ffn1_w_in, ffn1_w_out, norm_mix, w_in,
           pool_w_group, pool_scale, pool_w_out, conv_w, conv_b, dt_bias, a_log, d_skip, ssd_norm, ssd_w_out, w_o,
           norm_ffn2, ffn2_w_in, ffn2_w_out, norm_final):
    depth = w_in.shape[0]
    B, S, _ = x_prompt.shape
    DB, T, _ = x_sample.shape
    gfin = norm_final[None, :]
    xp = x_prompt.reshape(B * S, D_MODEL)
    xs = x_sample.reshape(DB * T, D_MODEL)
    outs = [[] for _ in range(6)]
    for l in range(depth):
        last = l == depth - 1
        f1_in, f1_out = ffn1_w_in[l].astype(bf16), ffn1_w_out[l].astype(bf16)
        f2_in, f2_out = ffn2_w_in[l].astype(bf16), ffn2_w_out[l].astype(bf16)
        g1, g2 = norm_ffn1[l][None, :], norm_ffn2[l][None, :]
        w = _layer_weights(l, norm_mix, w_in, pool_w_group, pool_scale, pool_w_out, conv_w, conv_b, dt_bias, a_log,
                           d_skip, ssd_norm, ssd_w_out, w_o)
        xp = _ffn(xp, g1, f1_in, f1_out, gfin, final_norm=False)
        xp3, npool, nconv, nssm = _mixer_prompt(xp.reshape(B, S, D_MODEL), w)
        xp = _ffn(xp3.reshape(B * S, D_MODEL), g2, f2_in, f2_out, gfin, final_norm=last)
        xs = _ffn(xs, g1, f1_in, f1_out, gfin, final_norm=False)
        bp, cxs, cb, cc, cdt, spool, sconv = _sample_pre(xs, state_pool[l:l + 1], state_conv[l:l + 1], w, T)
        y, sssm = _ssd_decode(cxs, cb, cc, cdt, state_ssm[l:l + 1], w, T)
        xs = _sample_post(xs, y.reshape(DB * T, D_INNER), bp, w, T)
        xs = _ffn(xs, g2, f2_in, f2_out, gfin, final_norm=last)
        for acc, v in zip(outs, (npool, nconv, nssm, spool, sconv, sssm)):
            acc.append(v)
    stack = lambda vs: vs[0] if len(vs) == 1 else jnp.concatenate(vs, axis=0)
    return (xp.reshape(B, S, D_MODEL), xs.reshape(DB, T, D_MODEL),
            stack(outs[0]), stack(outs[1]), stack(outs[2]), stack(outs[3]), stack(outs[4]), stack(outs[5]))
```

```python
import functools

import jax
import jax.numpy as jnp
from jax import lax
from jax.experimental import pallas as pl
from jax.experimental.pallas import tpu as pltpu

f32 = jnp.float32
bf16 = jnp.bfloat16

D_MODEL = 1024
D_FF = 2816
POOL_WINDOWS = (2, 4, 8, 16)
POOL_WIDTH = D_MODEL
POOL_GROUP = POOL_WIDTH // len(POOL_WINDOWS)
POOL_HIST = max(POOL_WINDOWS) - 1
D_INNER = 2 * D_MODEL
SSD_HEAD_DIM = 64
N_SSD_HEADS = D_INNER // SSD_HEAD_DIM
N_SSD_GROUPS = 4
HEADS_PER_GROUP = N_SSD_HEADS // N_SSD_GROUPS
GROUP_WIDTH = D_INNER // N_SSD_GROUPS
D_STATE = 128
CONV_WIDTH = 4
BC_DIM = N_SSD_GROUPS * D_STATE
CONV_DIM = D_INNER + 2 * BC_DIM
CHUNK = 128
PAST_LEN = 16384
EPS = 1e-6

LANES = 128
SUBLANES = 8
VMEM_LIMIT_BYTES = 56 * 1024 * 1024

FFN_ROWS = 512
FFN_CHUNK = 256
PROMPT_TILE = 256
SAMPLE_SEQS = 16
DECODE_SEQS = 4
POOL_PAD = 16
CONV_PAD = 8
HEAD_LANES = LANES

_NT = (((1,), (1,)), ((), ()))
_TN = (((0,), (0,)), ((), ()))


def _rms(x, g):
    return x * lax.rsqrt(jnp.mean(x * x, axis=-1, keepdims=True) + EPS) * g


def _silu(v):
    return v * jax.nn.sigmoid(v)


def _softplus(v):
    return jnp.maximum(v, 0.0) + jnp.log1p(jnp.exp(-jnp.abs(v)))


def _dot(a, b):
    return jnp.dot(a, b, preferred_element_type=f32)


def _split3(v):
    hi = v.astype(bf16)
    r1 = v - hi.astype(f32)
    mid = r1.astype(bf16)
    lo = (r1 - mid.astype(f32)).astype(bf16)
    return hi, mid, lo


def _const_spec(shape):
    nd = len(shape)
    return pl.BlockSpec(shape, lambda *_: (0,) * nd, pipeline_mode=pl.Buffered(1))


def _params(n_grid):
    return pltpu.CompilerParams(dimension_semantics=("arbitrary",) * n_grid,
                                vmem_limit_bytes=VMEM_LIMIT_BYTES)


def _ffn_kernel(x_ref, g_ref, win_ref, wout_ref, gfin_ref, o_ref, hn_s, act_s, *, final_norm):
    hn_s[...] = _rms(x_ref[...], g_ref[...]).astype(bf16)
    for c in range(D_FF // FFN_CHUNK):
        lo = c * FFN_CHUNK
        gate = _dot(hn_s[...], win_ref[:, lo:lo + FFN_CHUNK])
        up = _dot(hn_s[...], win_ref[:, D_FF + lo:D_FF + lo + FFN_CHUNK])
        act_s[:, lo:lo + FFN_CHUNK] = (_silu(gate) * up).astype(bf16)
    out = x_ref[...] + 0.5 * _dot(act_s[...], wout_ref[...])
    if final_norm:
        out = _rms(out, gfin_ref[...])
    o_ref[...] = out


def _ffn(x2d, g, w_in_b, w_out_b, gfin, *, final_norm):
    rows = x2d.shape[0]
    assert rows % FFN_ROWS == 0
    return pl.pallas_call(
        functools.partial(_ffn_kernel, final_norm=final_norm),
        grid=(rows // FFN_ROWS,),
        in_specs=[pl.BlockSpec((FFN_ROWS, D_MODEL), lambda i: (i, 0)),
                  _const_spec((1, D_MODEL)),
                  _const_spec((D_MODEL, 2 * D_FF)),
                  _const_spec((D_FF, D_MODEL)),
                  _const_spec((1, D_MODEL))],
        out_specs=pl.BlockSpec((FFN_ROWS, D_MODEL), lambda i: (i, 0)),
        out_shape=jax.ShapeDtypeStruct((rows, D_MODEL), f32),
        scratch_shapes=[pltpu.VMEM((FFN_ROWS, D_MODEL), bf16), pltpu.VMEM((FFN_ROWS, D_FF), bf16)],
        compiler_params=_params(1),
        name="ffn_final" if final_norm else "ffn",
    )(x2d, g, w_in_b, w_out_b, gfin)


def _store_rows(ref, row0, val, T):
    S = ref.shape[1]
    for j in range(val.shape[1] // LANES):
        blk = val[:, j * LANES:(j + 1) * LANES]
        if S == 1:
            ref[j, 0, row0:row0 + T, :] = blk
        else:
            ref[j, :, row0:row0 + T, :] = blk.reshape(S, T, LANES)


def _window(ref, j, row0, n):
    S = ref.shape[1]
    rows = pl.ds(row0, n) if row0 % SUBLANES == 0 else pl.ds(row0, n, stride=1)
    if S == 1:
        return ref[j, 0, rows, :]
    v = ref[j, :, rows, :]
    return v.reshape(S * n, LANES) if n % SUBLANES == 0 else v


def _pool_branch(uext_ref, T, pos0, pgw_ref, pscale_ref, pwo_ref):
    S = uext_ref.shape[1]
    assert T & (T - 1) == 0
    t_idx = jnp.bitwise_and(lax.broadcasted_iota(jnp.int32, (S * T, 1), 0), T - 1)
    n_seen = pos0 + 1 + t_idx
    slabs_per_group = POOL_GROUP // LANES
    mixed = []
    for gi, w in enumerate(POOL_WINDOWS):
        cnt = jnp.minimum(n_seen, w).astype(f32)
        d = []
        for j in range(gi * slabs_per_group, (gi + 1) * slabs_per_group):
            cur = _window(uext_ref, j, POOL_PAD, T)
            s = cur
            for k in range(1, w):
                s = s + _window(uext_ref, j, POOL_PAD - k, T)
            d.append((s / cnt - cur).astype(bf16))
        mixed.append(_dot(jnp.concatenate(d, axis=-1), pgw_ref[gi]))
    y = jnp.concatenate(mixed, axis=-1) * pscale_ref[...]
    return _dot(y.astype(bf16), pwo_ref[...])


def _conv_all(xp_ref, T, convw_ref, convb_ref, xs_ref, b_ref, c_ref):
    for j in range(CONV_DIM // LANES):
        c0 = j * LANES
        cols = slice(c0, c0 + LANES)
        y = convb_ref[:, cols]
        for k in range(CONV_WIDTH):
            y = y + _window(xp_ref, j, CONV_PAD - (CONV_WIDTH - 1) + k, T) * convw_ref[k:k + 1, cols]
        v = _silu(y)
        if c0 < D_INNER:
            xs_ref[:, cols] = v
        elif c0 < D_INNER + BC_DIM:
            b_ref[:, c0 - D_INNER:c0 - D_INNER + LANES] = v
        else:
            c_ref[:, c0 - D_INNER - BC_DIM:c0 - D_INNER - BC_DIM + LANES] = v


def _gate_merge_out(x, hn_s, y_ref, bp, wz_ref, wgate_ref, snorm_ref, swo_ref, wo_ref, yn_s):
    for g in range(N_SSD_GROUPS):
        cols = slice(g * GROUP_WIDTH, (g + 1) * GROUP_WIDTH)
        z = _dot(hn_s[...], wz_ref[:, cols])
        yg = y_ref[:, cols] * _silu(z)
        yg = yg * lax.rsqrt(jnp.mean(yg * yg, axis=-1, keepdims=True) + EPS)
        yn_s[:, cols] = (yg * snorm_ref[:, cols]).astype(bf16)
    branch_ssd = _dot(yn_s[...], swo_ref[...])
    gate_pool = jax.nn.sigmoid(_dot(hn_s[...], wgate_ref[:, 0:D_MODEL]))
    gate_ssd = jax.nn.sigmoid(_dot(hn_s[...], wgate_ref[:, D_MODEL:2 * D_MODEL]))
    merged = (gate_pool * bp + gate_ssd * branch_ssd).astype(bf16)
    return x + _dot(merged, wo_ref[...])


def _ssd_chunk(rows, xs_s, b_s, c_s, dt_s, y_s, ht_s, alog_ref, dskip_ref, e_ref):
    L = CHUNK
    ri = lax.broadcasted_iota(jnp.int32, (L, L), 0)
    ci = lax.broadcasted_iota(jnp.int32, (L, L), 1)
    causal = ri >= ci
    tril = jnp.where(causal, 1.0, 0.0).astype(bf16)
    first_head = lax.broadcasted_iota(jnp.int32, (L, LANES), 1) < SSD_HEAD_DIM

    dt = dt_s[rows, :]
    dA = dt * (-jnp.exp(alog_ref[...]))
    hi, mid, lo = _split3(dA)
    acc = _dot(tril, jnp.concatenate([hi, mid, lo], axis=1))
    a = acc[:, 0:LANES] + acc[:, LANES:2 * LANES] + acc[:, 2 * LANES:3 * LANES]
    aT = a.T
    dtT = dt.T
    wT = jnp.exp(aT[:, L - 1:L] - aT) * dtT
    end_decay = jnp.broadcast_to(jnp.exp(a[L - 1:L, :]), (2 * SUBLANES, HEAD_LANES))
    e_hi, e_mid, e_lo = (t.astype(f32) for t in _split3(end_decay))
    sel = lax.broadcasted_iota(jnp.int32, (2 * SUBLANES, HEAD_LANES), 0)
    stacked = jnp.where(sel == 0, e_hi, jnp.where(sel == 1, e_mid, jnp.where(sel == 2, e_lo, 0.0)))
    cdec = jnp.sum(_dot(stacked.astype(bf16), e_ref[...]), axis=0, keepdims=True)

    def head_mats(h, cb, c_f, bT):
        acol = jnp.broadcast_to(a[:, h:h + 1], (L, L))
        decay = jnp.exp(jnp.where(causal, acol - aT[h:h + 1, :], -jnp.inf))
        m = (cb * decay * dtT[h:h + 1, :]).astype(bf16)
        ec = (c_f * jnp.exp(acol)).astype(bf16)
        bw = (bT * wT[h:h + 1, :]).astype(bf16)
        return m, ec, bw

    for q in range(N_SSD_HEADS // 2):
        g = (2 * q) // HEADS_PER_GROUP
        if (2 * q) % HEADS_PER_GROUP == 0:
            gcols = slice(g * D_STATE, (g + 1) * D_STATE)
            b_f = b_s[rows, gcols]
            c_f = c_s[rows, gcols]
            cb = lax.dot_general(c_f.astype(bf16), b_f.astype(bf16), _NT, preferred_element_type=f32)
            bT = b_f.T
        cols = slice(q * LANES, (q + 1) * LANES)
        xq = xs_s[rows, cols]
        x_lo = jnp.where(first_head, xq, 0.0).astype(bf16)
        x_hi = jnp.where(first_head, 0.0, xq).astype(bf16)
        hq = ht_s[:, cols]
        h_lo = jnp.where(first_head, hq, 0.0).astype(bf16)
        h_hi = jnp.where(first_head, 0.0, hq).astype(bf16)
        m_a, ec_a, bw_a = head_mats(2 * q, cb, c_f, bT)
        m_b, ec_b, bw_b = head_mats(2 * q + 1, cb, c_f, bT)
        lhs = jnp.concatenate([m_a, ec_a, m_b, ec_b], axis=1)
        rhs = jnp.concatenate([x_lo, h_lo, x_hi, h_hi], axis=0)
        y_s[rows, cols] = _dot(lhs, rhs) + dskip_ref[:, cols] * xq
        st = _dot(jnp.concatenate([bw_a, bw_b], axis=1), jnp.concatenate([x_lo, x_hi], axis=0))
        ht_s[:, cols] = hq * cdec[:, cols] + st


def _mixer_prompt_kernel(x_ref, gmix_ref, wu_ref, wz_ref, wxbc_ref, wdt_ref, wgate_ref, pgw_ref, pscale_ref,
                         pwo_ref, convw_ref, convb_ref, dtb_ref, alog_ref, dskip_ref, snorm_ref, swo_ref,
                         wo_ref, e_ref,
                         o_ref, npool_ref, nconv_ref, nssm_ref,
                         hn_s, uext_s, xp_s, xs_s, b_s, c_s, dt_s, y_s, ht_s, yn_s):
    T = PROMPT_TILE
    ti = pl.program_id(1)

    @pl.when(ti == 0)
    def _():
        uext_s[:, :, 0:POOL_PAD, :] = jnp.zeros((POOL_WIDTH // LANES, 1, POOL_PAD, LANES), f32)
        xp_s[:, :, 0:CONV_PAD, :] = jnp.zeros((CONV_DIM // LANES, 1, CONV_PAD, LANES), f32)
        ht_s[...] = jnp.zeros(ht_s.shape, f32)

    x = x_ref[0]
    hn_s[...] = _rms(x, gmix_ref[...]).astype(bf16)

    _store_rows(uext_s, POOL_PAD, _dot(hn_s[...], wu_ref[...]), T)
    bp = _pool_branch(uext_s, T, ti * T, pgw_ref, pscale_ref, pwo_ref)

    _store_rows(xp_s, CONV_PAD, _dot(hn_s[...], wxbc_ref[...]), T)
    _conv_all(xp_s, T, convw_ref, convb_ref, xs_s, b_s, c_s)
    dt_s[...] = _softplus(_dot(hn_s[...], wdt_ref[...]) + dtb_ref[...])

    def chunk(c, carry):
        rows = pl.ds(pl.multiple_of(c * CHUNK, CHUNK), CHUNK)
        _ssd_chunk(rows, xs_s, b_s, c_s, dt_s, y_s, ht_s, alog_ref, dskip_ref, e_ref)
        return carry

    lax.fori_loop(0, T // CHUNK, chunk, 0)

    o_ref[0] = _gate_merge_out(x, hn_s, y_s, bp, wz_ref, wgate_ref, snorm_ref, swo_ref, wo_ref, yn_s)

    @pl.when(ti == pl.num_programs(1) - 1)
    def _():
        for j in range(POOL_WIDTH // LANES):
            npool_ref[0, 0, :, j * LANES:(j + 1) * LANES] = _window(uext_s, j, POOL_PAD + T - POOL_HIST, POOL_HIST)
        for j in range(CONV_DIM // LANES):
            nconv_ref[0, 0, :, j * LANES:(j + 1) * LANES] = _window(
                xp_s, j, CONV_PAD + T - (CONV_WIDTH - 1), CONV_WIDTH - 1)
        for g in range(N_SSD_GROUPS):
            hg = ht_s[:, g * GROUP_WIDTH:(g + 1) * GROUP_WIDTH].T
            nssm_ref[0, 0, g * HEADS_PER_GROUP:(g + 1) * HEADS_PER_GROUP] = hg.reshape(
                HEADS_PER_GROUP, SSD_HEAD_DIM, D_STATE)

    uext_s[:, :, 0:POOL_PAD, :] = uext_s[:, :, T:T + POOL_PAD, :]
    xp_s[:, :, 0:CONV_PAD, :] = xp_s[:, :, T:T + CONV_PAD, :]


def _mixer_prompt(x1, w):
    B, S, _ = x1.shape
    T = PROMPT_TILE
    assert S % T == 0 and T % CHUNK == 0 and T >= POOL_PAD
    consts = [w["gmix"], w["wu"], w["wz"], w["wxbc"], w["wdt"], w["wgate"], w["pgw"], w["pscale"], w["pwo"],
              w["convw"], w["convb"], w["dtb"], w["alog"], w["dskip"], w["snorm"], w["swo"], w["wo"], w["expand"]]
    return pl.pallas_call(
        _mixer_prompt_kernel,
        grid=(B, S // T),
        in_specs=[pl.BlockSpec((1, T, D_MODEL), lambda b, t: (b, t, 0))] + [_const_spec(c.shape) for c in consts],
        out_specs=(pl.BlockSpec((1, T, D_MODEL), lambda b, t: (b, t, 0)),
                   pl.BlockSpec((1, 1, POOL_HIST, POOL_WIDTH), lambda b, t: (0, b, 0, 0)),
                   pl.BlockSpec((1, 1, CONV_WIDTH - 1, CONV_DIM), lambda b, t: (0, b, 0, 0)),
                   pl.BlockSpec((1, 1, N_SSD_HEADS, SSD_HEAD_DIM, D_STATE), lambda b, t: (0, b, 0, 0, 0))),
        out_shape=(jax.ShapeDtypeStruct((B, S, D_MODEL), f32),
                   jax.ShapeDtypeStruct((1, B, POOL_HIST, POOL_WIDTH), f32),
                   jax.ShapeDtypeStruct((1, B, CONV_WIDTH - 1, CONV_DIM), f32),
                   jax.ShapeDtypeStruct((1, B, N_SSD_HEADS, SSD_HEAD_DIM, D_STATE), f32)),
        scratch_shapes=[pltpu.VMEM((T, D_MODEL), bf16),
                        pltpu.VMEM((POOL_WIDTH // LANES, 1, POOL_PAD + T, LANES), f32),
                        pltpu.VMEM((CONV_DIM // LANES, 1, CONV_PAD + T, LANES), f32),
                        pltpu.VMEM((T, D_INNER), f32),
                        pltpu.VMEM((T, BC_DIM), f32),
                        pltpu.VMEM((T, BC_DIM), f32),
                        pltpu.VMEM((T, HEAD_LANES), f32),
                        pltpu.VMEM((T, D_INNER), f32),
                        pltpu.VMEM((D_STATE, D_INNER), f32),
                        pltpu.VMEM((T, D_INNER), bf16)],
        compiler_params=_params(2),
        name="mixer_prompt",
    )(x1, *consts)


def _sample_pre_kernel(x_ref, ph_ref, ch_ref, gmix_ref, wu_ref, wxbc_ref, wdt_ref, pgw_ref, pscale_ref, pwo_ref,
                       convw_ref, convb_ref, dtb_ref,
                       bp_ref, xs_ref, b_ref, c_ref, dt_ref, npool_ref, nconv_ref,
                       uext_s, xp_s, *, T):
    hn = _rms(x_ref[...], gmix_ref[...]).astype(bf16)

    for j in range(POOL_WIDTH // LANES):
        uext_s[j, :, POOL_PAD - POOL_HIST:POOL_PAD, :] = ph_ref[0, :, :, j * LANES:(j + 1) * LANES]
    _store_rows(uext_s, POOL_PAD, _dot(hn, wu_ref[...]), T)
    bp_ref[...] = _pool_branch(uext_s, T, PAST_LEN, pgw_ref, pscale_ref, pwo_ref)
    for j in range(POOL_WIDTH // LANES):
        npool_ref[0, :, :, j * LANES:(j + 1) * LANES] = _window(uext_s, j, POOL_PAD + T - POOL_HIST, POOL_HIST)

    for j in range(CONV_DIM // LANES):
        xp_s[j, :, CONV_PAD - (CONV_WIDTH - 1):CONV_PAD, :] = ch_ref[0, :, :, j * LANES:(j + 1) * LANES]
    _store_rows(xp_s, CONV_PAD, _dot(hn, wxbc_ref[...]), T)
    _conv_all(xp_s, T, convw_ref, convb_ref, xs_ref, b_ref, c_ref)
    for j in range(CONV_DIM // LANES):
        nconv_ref[0, :, :, j * LANES:(j + 1) * LANES] = _window(
            xp_s, j, CONV_PAD + T - (CONV_WIDTH - 1), CONV_WIDTH - 1)
    dt_ref[...] = _softplus(_dot(hn, wdt_ref[...]) + dtb_ref[...])


def _sample_pre(x1, pool_hist, conv_hist, w, T):
    rows = x1.shape[0]
    nseq = rows // T
    S = SAMPLE_SEQS
    R = S * T
    assert nseq % S == 0 and T % SUBLANES == 0
    consts = [w["gmix"], w["wu"], w["wxbc"], w["wdt"], w["pgw"], w["pscale"], w["pwo"], w["convw"], w["convb"],
              w["dtb"]]
    row_spec = lambda width: pl.BlockSpec((R, width), lambda i: (i, 0))
    return pl.pallas_call(
        functools.partial(_sample_pre_kernel, T=T),
        grid=(nseq // S,),
        in_specs=[row_spec(D_MODEL),
                  pl.BlockSpec((1, S, POOL_HIST, POOL_WIDTH), lambda i: (0, i, 0, 0)),
                  pl.BlockSpec((1, S, CONV_WIDTH - 1, CONV_DIM), lambda i: (0, i, 0, 0))]
                 + [_const_spec(c.shape) for c in consts],
        out_specs=(row_spec(D_MODEL), row_spec(D_INNER), row_spec(BC_DIM), row_spec(BC_DIM), row_spec(HEAD_LANES),
                   pl.BlockSpec((1, S, POOL_HIST, POOL_WIDTH), lambda i: (0, i, 0, 0)),
                   pl.BlockSpec((1, S, CONV_WIDTH - 1, CONV_DIM), lambda i: (0, i, 0, 0))),
        out_shape=(jax.ShapeDtypeStruct((rows, D_MODEL), f32),
                   jax.ShapeDtypeStruct((rows, D_INNER), f32),
                   jax.ShapeDtypeStruct((rows, BC_DIM), f32),
                   jax.ShapeDtypeStruct((rows, BC_DIM), f32),
                   jax.ShapeDtypeStruct((rows, HEAD_LANES), f32),
                   jax.ShapeDtypeStruct((1, nseq, POOL_HIST, POOL_WIDTH), f32),
                   jax.ShapeDtypeStruct((1, nseq, CONV_WIDTH - 1, CONV_DIM), f32)),
        scratch_shapes=[pltpu.VMEM((POOL_WIDTH // LANES, S, POOL_PAD + T, LANES), f32),
                        pltpu.VMEM((CONV_DIM // LANES, S, CONV_PAD + T, LANES), f32)],
        compiler_params=_params(1),
        name="sample_pre",
    )(x1, pool_hist, conv_hist, *consts)


def _ssd_decode_kernel(xs_ref, b_ref, c_ref, dt_ref, h0_ref, alog_ref, dskip_ref, e_ref, y_ref, hn_ref, *, T):
    assert T == SUBLANES
    row = lax.broadcasted_iota(jnp.int32, (T, HEAD_LANES), 0)
    lane_group = lax.broadcasted_iota(jnp.int32, (T, HEAD_LANES), 1) // HEADS_PER_GROUP
    neg_a = -jnp.exp(alog_ref[...])

    def shift(v, d):
        if d == 0:
            return v
        r = lax.broadcasted_iota(jnp.int32, v.shape, 0)
        return jnp.where(r >= d, pltpu.roll(v, d, axis=0), 0.0)

    def per_seq(s, carry):
        dt = dt_ref[s]
        a = dt * neg_a
        d = 1
        while d < T:
            a = a + shift(a, d)
            d *= 2
        a_end = a[T - 1:T, :]
        x = xs_ref[s]
        bm = b_ref[s]
        cm = c_ref[s]
        terms = []
        for d in range(T):
            cbv = jnp.zeros((T, HEAD_LANES), f32)
            for g in range(N_SSD_GROUPS):
                gc = slice(g * D_STATE, (g + 1) * D_STATE)
                cb = jnp.sum(cm[:, gc] * shift(bm[:, gc], d), axis=-1, keepdims=True)
                cbv = jnp.where(lane_group == g, cb, cbv)
            terms.append(cbv * jnp.exp(a - shift(a, d)) * shift(dt, d))
        terms.append(jnp.exp(a))
        terms.append(jnp.exp(a_end - a) * dt)
        v = jnp.concatenate(terms, axis=0)
        v_hi = v.astype(bf16)
        v_lo = (v - v_hi.astype(f32)).astype(bf16)
        ex = _dot(jnp.concatenate([v_hi, v_lo], axis=0), e_ref[...])
        n = (T + 2) * T
        ex = ex[0:n] + ex[n:2 * n]

        y = dskip_ref[...] * x
        for d in range(T):
            y = y + ex[d * T:(d + 1) * T] * shift(x, d)
        ea = ex[T * T:(T + 1) * T]
        xw = (x * ex[(T + 1) * T:(T + 2) * T]).astype(bf16)
        cdec = jnp.exp(a_end)
        y_groups = []
        for g in range(N_SSD_GROUPS):
            gc = slice(g * D_STATE, (g + 1) * D_STATE)
            cols = slice(g * GROUP_WIDTH, (g + 1) * GROUP_WIDTH)
            heads = slice(g * HEADS_PER_GROUP, (g + 1) * HEADS_PER_GROUP)
            h0g = h0_ref[0, s, heads].reshape(GROUP_WIDTH, D_STATE)
            ch = lax.dot_general(cm[:, gc].astype(bf16), h0g.astype(bf16), _NT, preferred_element_type=f32)
            y_groups.append(y[:, cols] + ea[:, cols] * ch)
            st = lax.dot_general(xw[:, cols], bm[:, gc].astype(bf16), _TN, preferred_element_type=f32)
            for r in range(HEADS_PER_GROUP):
                h = g * HEADS_PER_GROUP + r
                hn_ref[0, s, h] = (h0_ref[0, s, h] * cdec[:, h:h + 1]
                                   + st[r * SSD_HEAD_DIM:(r + 1) * SSD_HEAD_DIM, :])
        y_ref[s] = jnp.concatenate(y_groups, axis=-1)
        return carry

    lax.fori_loop(0, DECODE_SEQS, per_seq, 0)


def _ssd_decode(xs, bm, cm, dt, h0, w, T):
    nseq = xs.shape[0] // T
    Q = DECODE_SEQS
    assert nseq % Q == 0
    seq_spec = lambda width: pl.BlockSpec((Q, T, width), lambda i: (i, 0, 0))
    state_spec = pl.BlockSpec((1, Q, N_SSD_HEADS, SSD_HEAD_DIM, D_STATE), lambda i: (0, i, 0, 0, 0))
    consts = [w["alog"], w["dskip"], w["expand"]]
    return pl.pallas_call(
        functools.partial(_ssd_decode_kernel, T=T),
        grid=(nseq // Q,),
        in_specs=[seq_spec(D_INNER), seq_spec(BC_DIM), seq_spec(BC_DIM), seq_spec(HEAD_LANES), state_spec]
                 + [_const_spec(c.shape) for c in consts],
        out_specs=(seq_spec(D_INNER), state_spec),
        out_shape=(jax.ShapeDtypeStruct((nseq, T, D_INNER), f32), jax.ShapeDtypeStruct(h0.shape, f32)),
        compiler_params=_params(1),
        name="ssd_decode",
    )(xs.reshape(nseq, T, D_INNER), bm.reshape(nseq, T, BC_DIM), cm.reshape(nseq, T, BC_DIM),
      dt.reshape(nseq, T, HEAD_LANES), h0, *consts)


def _sample_post_kernel(x_ref, y_ref, bp_ref, gmix_ref, wz_ref, wgate_ref, snorm_ref, swo_ref, wo_ref,
                        o_ref, hn_s, yn_s):
    x = x_ref[...]
    hn_s[...] = _rms(x, gmix_ref[...]).astype(bf16)
    o_ref[...] = _gate_merge_out(x, hn_s, y_ref, bp_ref[...], wz_ref, wgate_ref, snorm_ref, swo_ref, wo_ref, yn_s)


def _sample_post(x1, y, bp, w, T):
    rows = x1.shape[0]
    R = SAMPLE_SEQS * T
    consts = [w["gmix"], w["wz"], w["wgate"], w["snorm"], w["swo"], w["wo"]]
    row_spec = lambda width: pl.BlockSpec((R, width), lambda i: (i, 0))
    return pl.pallas_call(
        _sample_post_kernel,
        grid=(rows // R,),
        in_specs=[row_spec(D_MODEL), row_spec(D_INNER), row_spec(D_MODEL)] + [_const_spec(c.shape) for c in consts],
        out_specs=row_spec(D_MODEL),
        out_shape=jax.ShapeDtypeStruct((rows, D_MODEL), f32),
        scratch_shapes=[pltpu.VMEM((R, D_MODEL), bf16), pltpu.VMEM((R, D_INNER), bf16)],
        compiler_params=_params(1),
        name="sample_post",
    )(x1, y, bp, *consts)


def _layer_weights(l, norm_mix, w_in, pool_w_group, pool_scale, pool_w_out, conv_w, conv_b, dt_bias, a_log, d_skip,
                   ssd_norm, ssd_w_out, w_o):
    s1 = POOL_WIDTH
    s2 = s1 + D_INNER
    s3 = s2 + CONV_DIM
    s4 = s3 + N_SSD_HEADS
    wl = w_in[l]
    pad_heads = lambda v: jnp.pad(v, ((0, 0), (0, HEAD_LANES - N_SSD_HEADS)))
    head_of_lane = jnp.arange(D_INNER, dtype=jnp.int32) // SSD_HEAD_DIM
    expand = (jnp.arange(HEAD_LANES, dtype=jnp.int32)[:, None] == head_of_lane[None, :]).astype(bf16)
    return dict(
        gmix=norm_mix[l][None, :],
        wu=wl[:, :s1].astype(bf16), wz=wl[:, s1:s2].astype(bf16), wxbc=wl[:, s2:s3].astype(bf16),
        wdt=pad_heads(wl[:, s3:s4]).astype(bf16), wgate=wl[:, s4:].astype(bf16),
        pgw=pool_w_group[l].astype(bf16), pscale=pool_scale[l][None, :], pwo=pool_w_out[l].astype(bf16),
        convw=conv_w[l], convb=conv_b[l][None, :],
        dtb=pad_heads(dt_bias[l][None, :]), alog=pad_heads(a_log[l][None, :]),
        dskip=jnp.repeat(d_skip[l], SSD_HEAD_DIM)[None, :], snorm=ssd_norm[l][None, :],
        swo=ssd_w_out[l].astype(bf16), wo=w_o[l].astype(bf16), expand=expand)


def kernel(x_prompt, x_sample, state_pool, state_conv, state_ssm, norm_ffn1, ffn1_w_in, ffn1_w_out, norm_mix, w_in,
           pool_w_group, pool_scale, pool_w_out, conv_w, conv_b, dt_bias, a_log, d_skip, ssd_norm, ssd_w_out, w_o,
           norm_ffn2, ffn2_w_in, ffn2_w_out, norm_final):
    depth = w_in.shape[0]
    B, S, _ = x_prompt.shape
    DB, T, _ = x_sample.shape
    gfin = norm_final[None, :]
    xp = x_prompt.reshape(B * S, D_MODEL)
    xs = x_sample.reshape(DB * T, D_MODEL)
    outs = [[] for _ in range(6)]
    for l in range(depth):
        last = l == depth - 1
        f1_in, f1_out = ffn1_w_in[l].astype(bf16), ffn1_w_out[l].astype(bf16)
        f2_in, f2_out = ffn2_w_in[l].astype(bf16), ffn2_w_out[l].astype(bf16)
        g1, g2 = norm_ffn1[l][None, :], norm_ffn2[l][None, :]
        w = _layer_weights(l, norm_mix, w_in, pool_w_group, pool_scale, pool_w_out, conv_w, conv_b, dt_bias, a_log,
                           d_skip, ssd_norm, ssd_w_out, w_o)
        xp = _ffn(xp, g1, f1_in, f1_out, gfin, final_norm=False)
        xp3, npool, nconv, nssm = _mixer_prompt(xp.reshape(B, S, D_MODEL), w)
        xp = _ffn(xp3.reshape(B * S, D_MODEL), g2, f2_in, f2_out, gfin, final_norm=last)
        xs = _ffn(xs, g1, f1_in, f1_out, gfin, final_norm=False)
        bp, cxs, cb, cc, cdt, spool, sconv = _sample_pre(xs, state_pool[l:l + 1], state_conv[l:l + 1], w, T)
        y, sssm = _ssd_decode(cxs, cb, cc, cdt, state_ssm[l:l + 1], w, T)
        xs = _sample_post(xs, y.reshape(DB * T, D_INNER), bp, w, T)
        xs = _ffn(xs, g2, f2_in, f2_out, gfin, final_norm=last)
        for acc, v in zip(outs, (npool, nconv, nssm, spool, sconv, sssm)):
            acc.append(v)
    stack = lambda vs: vs[0] if len(vs) == 1 else jnp.concatenate(vs, axis=0)
    return (xp.reshape(B, S, D_MODEL), xs.reshape(DB, T, D_MODEL),
            stack(outs[0]), stack(outs[1]), stack(outs[2]), stack(outs[3]), stack(outs[4]), stack(outs[5]))
```

```python
import functools

import jax
import jax.numpy as jnp
from jax import lax
from jax.experimental import pallas as pl
from jax.experimental.pallas import tpu as pltpu

f32 = jnp.float32
bf16 = jnp.bfloat16

D_MODEL = 1024
D_FF = 2816
POOL_WINDOWS = (2, 4, 8, 16)
POOL_WIDTH = D_MODEL
POOL_GROUP = POOL_WIDTH // len(POOL_WINDOWS)
POOL_HIST = max(POOL_WINDOWS) - 1
D_INNER = 2 * D_MODEL
SSD_HEAD_DIM = 64
N_SSD_HEADS = D_INNER // SSD_HEAD_DIM
N_SSD_GROUPS = 4
HEADS_PER_GROUP = N_SSD_HEADS // N_SSD_GROUPS
GROUP_WIDTH = D_INNER // N_SSD_GROUPS
D_STATE = 128
CONV_WIDTH = 4
BC_DIM = N_SSD_GROUPS * D_STATE
CONV_DIM = D_INNER + 2 * BC_DIM
CHUNK = 128
PAST_LEN = 16384
EPS = 1e-6

LANES = 128
SUBLANES = 8
VMEM_LIMIT_BYTES = 56 * 1024 * 1024

FFN_ROWS = 512
FFN_CHUNK = 256
PROMPT_TILE = 256
SAMPLE_SEQS = 16
DECODE_SEQS = 4
POOL_PAD = 16
CONV_PAD = 8
HEAD_LANES = LANES

_NT = (((1,), (1,)), ((), ()))
_TN = (((0,), (0,)), ((), ()))


def _rms(x, g):
    return x * lax.rsqrt(jnp.mean(x * x, axis=-1, keepdims=True) + EPS) * g


def _silu(v):
    return v * jax.nn.sigmoid(v)


def _softplus(v):
    return jnp.maximum(v, 0.0) + jnp.log1p(jnp.exp(-jnp.abs(v)))


def _dot(a, b):
    return jnp.dot(a, b, preferred_element_type=f32)


def _split3(v):
    hi = v.astype(bf16)
    r1 = v - hi.astype(f32)
    mid = r1.astype(bf16)
    lo = (r1 - mid.astype(f32)).astype(bf16)
    return hi, mid, lo


def _const_spec(shape):
    nd = len(shape)
    return pl.BlockSpec(shape, lambda *_: (0,) * nd, pipeline_mode=pl.Buffered(1))


def _params(n_grid):
    return pltpu.CompilerParams(dimension_semantics=("arbitrary",) * n_grid,
                                vmem_limit_bytes=VMEM_LIMIT_BYTES)


def _ffn_kernel(x_ref, g_ref, win_ref, wout_ref, gfin_ref, o_ref, hn_s, act_s, *, final_norm):
    hn_s[...] = _rms(x_ref[...], g_ref[...]).astype(bf16)
    for c in range(D_FF // FFN_CHUNK):
        lo = c * FFN_CHUNK
        gate = _dot(hn_s[...], win_ref[:, lo:lo + FFN_CHUNK])
        up = _dot(hn_s[...], win_ref[:, D_FF + lo:D_FF + lo + FFN_CHUNK])
        act_s[:, lo:lo + FFN_CHUNK] = (_silu(gate) * up).astype(bf16)
    out = x_ref[...] + 0.5 * _dot(act_s[...], wout_ref[...])
    if final_norm:
        out = _rms(out, gfin_ref[...])
    o_ref[...] = out


def _ffn_specs(rows):
    in_specs = [pl.BlockSpec((FFN_ROWS, D_MODEL), lambda i: (i, 0)),
                _const_spec((1, D_MODEL)),
                _const_spec((D_MODEL, 2 * D_FF)),
                _const_spec((D_FF, D_MODEL)),
                _const_spec((1, D_MODEL))]
    out_spec = pl.BlockSpec((FFN_ROWS, D_MODEL), lambda i: (i, 0))
    scratch = [pltpu.VMEM((FFN_ROWS, D_MODEL), bf16), pltpu.VMEM((FFN_ROWS, D_FF), bf16)]
    return in_specs, out_spec, jax.ShapeDtypeStruct((rows, D_MODEL), f32), scratch


def _ffn(x2d, g, w_in_b, w_out_b, gfin, *, final_norm):
    rows = x2d.shape[0]
    assert rows % FFN_ROWS == 0
    in_specs, out_spec, out_shape, scratch = _ffn_specs(rows)
    return pl.pallas_call(
        functools.partial(_ffn_kernel, final_norm=final_norm),
        grid=(rows // FFN_ROWS,),
        in_specs=in_specs, out_specs=out_spec, out_shape=out_shape, scratch_shapes=scratch,
        compiler_params=_params(1),
        name="ffn_final" if final_norm else "ffn",
    )(x2d, g, w_in_b, w_out_b, gfin)


_IN_PROJ_SPLITS = (POOL_WIDTH, POOL_WIDTH + D_INNER, POOL_WIDTH + D_INNER + CONV_DIM,
                   POOL_WIDTH + D_INNER + CONV_DIM + N_SSD_HEADS)
BF16_SUBLANES = 2 * SUBLANES


def _cast_block(n_rows, n_steps):
    period = 1
    while (n_rows * period) % n_steps or ((n_rows * period) // n_steps) % BF16_SUBLANES:
        period *= 2
        assert period <= n_steps
    return (n_rows * period) // n_steps, period


def _ffn_cast_kernel(x_ref, g_ref, win_ref, wout_ref, gfin_ref, w_in_f, *rest, n_plain):
    plain_f = rest[:n_plain]
    o_ref = rest[n_plain]
    wu_o, wz_o, wxbc_o, wdt_o, wgate_o = rest[n_plain + 1:n_plain + 6]
    plain_o = rest[n_plain + 6:2 * n_plain + 6]
    hn_s, act_s = rest[2 * n_plain + 6:]
    _ffn_kernel(x_ref, g_ref, win_ref, wout_ref, gfin_ref, o_ref, hn_s, act_s, final_norm=False)
    s1, s2, s3, s4 = _IN_PROJ_SPLITS
    v = w_in_f[...]
    wu_o[...] = v[:, :s1].astype(bf16)
    wz_o[...] = v[:, s1:s2].astype(bf16)
    wxbc_o[...] = v[:, s2:s3].astype(bf16)
    pad = jnp.zeros((v.shape[0], HEAD_LANES - N_SSD_HEADS), f32)
    wdt_o[...] = jnp.concatenate([v[:, s3:s4], pad], axis=1).astype(bf16)
    wgate_o[...] = v[:, s4:].astype(bf16)
    for src, dst in zip(plain_f, plain_o):
        dst[...] = src[...].astype(bf16)


def _ffn_and_casts(x2d, g, w_in_b, w_out_b, gfin, w_in_proj, plain):
    rows = x2d.shape[0]
    assert rows % FFN_ROWS == 0
    n_steps = rows // FFN_ROWS
    in_specs, out_spec, out_shape, scratch = _ffn_specs(rows)

    def blocked(arr, width=None):
        blk, period = _cast_block(arr.shape[0], n_steps)
        w = arr.shape[1] if width is None else width
        return (pl.BlockSpec((blk, w), lambda i: (i // period, 0)),
                jax.ShapeDtypeStruct((arr.shape[0], w), bf16))

    piece_widths = (POOL_WIDTH, D_INNER, CONV_DIM, HEAD_LANES, 2 * D_MODEL)
    pieces = [blocked(w_in_proj, width) for width in piece_widths]
    plains = [blocked(p) for p in plain]
    res = pl.pallas_call(
        functools.partial(_ffn_cast_kernel, n_plain=len(plain)),
        grid=(n_steps,),
        in_specs=in_specs + [blocked(w_in_proj)[0]] + [s for s, _ in plains],
        out_specs=tuple([out_spec] + [s for s, _ in pieces] + [s for s, _ in plains]),
        out_shape=tuple([out_shape] + [o for _, o in pieces] + [o for _, o in plains]),
        scratch_shapes=scratch,
        compiler_params=_params(1),
        name="ffn_casts",
    )(x2d, g, w_in_b, w_out_b, gfin, w_in_proj, *plain)
    return res[0], res[1:6], res[6:]


def _store_rows(ref, row0, val, T, slab0=0):
    S = ref.shape[1]
    for j in range(val.shape[1] // LANES):
        blk = val[:, j * LANES:(j + 1) * LANES]
        if S == 1:
            ref[slab0 + j, 0, row0:row0 + T, :] = blk
        else:
            ref[slab0 + j, :, row0:row0 + T, :] = blk.reshape(S, T, LANES)


def _window(ref, j, row0, n):
    S = ref.shape[1]
    rows = pl.ds(row0, n) if row0 % SUBLANES == 0 else pl.ds(row0, n, stride=1)
    if S == 1:
        return ref[j, 0, rows, :]
    v = ref[j, :, rows, :]
    return v.reshape(S * n, LANES) if n % SUBLANES == 0 else v


def _pool_branch(uext_ref, T, pos0, pgw_ref, pscale_ref, pwo_ref):
    S = uext_ref.shape[1]
    assert T & (T - 1) == 0
    t_idx = jnp.bitwise_and(lax.broadcasted_iota(jnp.int32, (S * T, 1), 0), T - 1)
    n_seen = pos0 + 1 + t_idx
    slabs_per_group = POOL_GROUP // LANES
    mixed = []
    for gi, w in enumerate(POOL_WINDOWS):
        cnt = jnp.minimum(n_seen, w).astype(f32)
        d = []
        for j in range(gi * slabs_per_group, (gi + 1) * slabs_per_group):
            cur = _window(uext_ref, j, POOL_PAD, T)
            s = cur
            for k in range(1, w):
                s = s + _window(uext_ref, j, POOL_PAD - k, T)
            d.append((s / cnt - cur).astype(bf16))
        mixed.append(_dot(jnp.concatenate(d, axis=-1), pgw_ref[gi]))
    y = jnp.concatenate(mixed, axis=-1) * pscale_ref[...]
    return _dot(y.astype(bf16), pwo_ref[...])


def _conv_slab(xp_ref, T, convw_ref, convb_ref, j, xs_ref, b_ref, c_ref):
    c0 = j * LANES
    cols = slice(c0, c0 + LANES)
    y = convb_ref[:, cols]
    for k in range(CONV_WIDTH):
        y = y + _window(xp_ref, j, CONV_PAD - (CONV_WIDTH - 1) + k, T) * convw_ref[k:k + 1, cols]
    v = _silu(y)
    if c0 < D_INNER:
        xs_ref[:, cols] = v
    elif c0 < D_INNER + BC_DIM:
        b_ref[:, c0 - D_INNER:c0 - D_INNER + LANES] = v
    else:
        c_ref[:, c0 - D_INNER - BC_DIM:c0 - D_INNER - BC_DIM + LANES] = v


def _project_conv(hn_s, wxbc_ref, xp_ref, T, convw_ref, convb_ref, xs_ref, b_ref, c_ref):
    tile = 2 * LANES
    for jt in range(CONV_DIM // tile):
        slab0 = jt * (tile // LANES)
        _store_rows(xp_ref, CONV_PAD, _dot(hn_s[...], wxbc_ref[:, jt * tile:(jt + 1) * tile]), T, slab0=slab0)
        for j in range(slab0, slab0 + tile // LANES):
            _conv_slab(xp_ref, T, convw_ref, convb_ref, j, xs_ref, b_ref, c_ref)


def _gate_merge_out(x, hn_s, y_ref, bp, wz_ref, wgate_ref, snorm_ref, swo_ref, wo_ref, yn_s):
    for g in range(N_SSD_GROUPS):
        cols = slice(g * GROUP_WIDTH, (g + 1) * GROUP_WIDTH)
        z = _dot(hn_s[...], wz_ref[:, cols])
        yg = y_ref[:, cols] * _silu(z)
        yg = yg * lax.rsqrt(jnp.mean(yg * yg, axis=-1, keepdims=True) + EPS)
        yn_s[:, cols] = (yg * snorm_ref[:, cols]).astype(bf16)
    branch_ssd = _dot(yn_s[...], swo_ref[...])
    gate_pool = jax.nn.sigmoid(_dot(hn_s[...], wgate_ref[:, 0:D_MODEL]))
    gate_ssd = jax.nn.sigmoid(_dot(hn_s[...], wgate_ref[:, D_MODEL:2 * D_MODEL]))
    merged = (gate_pool * bp + gate_ssd * branch_ssd).astype(bf16)
    return x + _dot(merged, wo_ref[...])


def _ssd_chunk(rows, xs_s, b_s, c_s, dt_s, y_s, ht_s, alog_ref, dskip_ref, e_ref):
    L = CHUNK
    ri = lax.broadcasted_iota(jnp.int32, (L, L), 0)
    ci = lax.broadcasted_iota(jnp.int32, (L, L), 1)
    causal = ri >= ci
    tril = jnp.where(causal, 1.0, 0.0).astype(bf16)
    first_head = lax.broadcasted_iota(jnp.int32, (L, LANES), 1) < SSD_HEAD_DIM

    dt = dt_s[rows, :]
    dA = dt * (-jnp.exp(alog_ref[...]))
    hi, mid, lo = _split3(dA)
    acc = _dot(tril, jnp.concatenate([hi, mid, lo], axis=1))
    a = acc[:, 0:LANES] + acc[:, LANES:2 * LANES] + acc[:, 2 * LANES:3 * LANES]
    aT = a.T[0:N_SSD_HEADS]
    dtT = dt.T[0:N_SSD_HEADS]
    wT = jnp.exp(aT[:, L - 1:L] - aT) * dtT
    srcT = aT - jnp.log(dtT)
    end_decay = jnp.broadcast_to(jnp.exp(a[L - 1:L, :]), (2 * SUBLANES, HEAD_LANES))
    e_hi, e_mid, e_lo = (t.astype(f32) for t in _split3(end_decay))
    sel = lax.broadcasted_iota(jnp.int32, (2 * SUBLANES, HEAD_LANES), 0)
    stacked = jnp.where(sel == 0, e_hi, jnp.where(sel == 1, e_mid, jnp.where(sel == 2, e_lo, 0.0)))
    cdec = jnp.sum(_dot(stacked.astype(bf16), e_ref[...]), axis=0, keepdims=True)

    def head_mats(h, cb, bT):
        acol = jnp.broadcast_to(a[:, h:h + 1], (L, L))
        m = (cb * jnp.exp(jnp.where(causal, acol - srcT[h:h + 1, :], -jnp.inf))).astype(bf16)
        bw = (bT * wT[h:h + 1, :]).astype(bf16)
        return m, bw, jnp.exp(acol)

    pairs_per_group = HEADS_PER_GROUP // 2
    for q in range(N_SSD_HEADS // 2):
        g, qg = divmod(q, pairs_per_group)
        if qg == 0:
            gcols = slice(g * D_STATE, (g + 1) * D_STATE)
            b_f = b_s[rows, gcols]
            c_b = c_s[rows, gcols].astype(bf16)
            cb = lax.dot_general(c_b, b_f.astype(bf16), _NT, preferred_element_type=f32)
            bT = b_f.T
            ch = _dot(c_b, ht_s[:, g * GROUP_WIDTH:(g + 1) * GROUP_WIDTH].astype(bf16))
        cols = slice(q * LANES, (q + 1) * LANES)
        xq = xs_s[rows, cols]
        x2 = jnp.concatenate([jnp.where(first_head, xq, 0.0), jnp.where(first_head, 0.0, xq)], axis=0).astype(bf16)
        m_a, bw_a, ea_a = head_mats(2 * q, cb, bT)
        m_b, bw_b, ea_b = head_mats(2 * q + 1, cb, bT)
        y_off = ch[:, qg * LANES:(qg + 1) * LANES] * jnp.where(first_head, ea_a, ea_b)
        y_s[rows, cols] = _dot(jnp.concatenate([m_a, m_b], axis=1), x2) + y_off + dskip_ref[:, cols] * xq
        st = _dot(jnp.concatenate([bw_a, bw_b], axis=1), x2)
        ht_s[:, cols] = ht_s[:, cols] * cdec[:, cols] + st


def _mixer_prompt_kernel(x_ref, gmix_ref, wu_ref, wz_ref, wxbc_ref, wdt_ref, wgate_ref, pgw_ref, pscale_ref,
                         pwo_ref, convw_ref, convb_ref, dtb_ref, alog_ref, dskip_ref, snorm_ref, swo_ref,
                         wo_ref, e_ref,
                         o_ref, npool_ref, nconv_ref, nssm_ref,
                         hn_s, uext_s, xp_s, xs_s, b_s, c_s, dt_s, y_s, ht_s, yn_s):
    T = PROMPT_TILE
    ti = pl.program_id(1)

    @pl.when(ti == 0)
    def _():
        uext_s[:, :, 0:POOL_PAD, :] = jnp.zeros((POOL_WIDTH // LANES, 1, POOL_PAD, LANES), f32)
        xp_s[:, :, 0:CONV_PAD, :] = jnp.zeros((CONV_DIM // LANES, 1, CONV_PAD, LANES), f32)
        ht_s[...] = jnp.zeros(ht_s.shape, f32)

    x = x_ref[0]
    hn_s[...] = _rms(x, gmix_ref[...]).astype(bf16)

    _store_rows(uext_s, POOL_PAD, _dot(hn_s[...], wu_ref[...]), T)
    bp = _pool_branch(uext_s, T, ti * T, pgw_ref, pscale_ref, pwo_ref)

    _project_conv(hn_s, wxbc_ref, xp_s, T, convw_ref, convb_ref, xs_s, b_s, c_s)
    dt_s[...] = _softplus(_dot(hn_s[...], wdt_ref[...]) + dtb_ref[...])

    def chunk(c, carry):
        rows = pl.ds(pl.multiple_of(c * CHUNK, CHUNK), CHUNK)
        _ssd_chunk(rows, xs_s, b_s, c_s, dt_s, y_s, ht_s, alog_ref, dskip_ref, e_ref)
        return carry

    lax.fori_loop(0, T // CHUNK, chunk, 0)

    o_ref[0] = _gate_merge_out(x, hn_s, y_s, bp, wz_ref, wgate_ref, snorm_ref, swo_ref, wo_ref, yn_s)

    @pl.when(ti == pl.num_programs(1) - 1)
    def _():
        for j in range(POOL_WIDTH // LANES):
            npool_ref[0, 0, :, j * LANES:(j + 1) * LANES] = _window(uext_s, j, POOL_PAD + T - POOL_HIST, POOL_HIST)
        for j in range(CONV_DIM // LANES):
            nconv_ref[0, 0, :, j * LANES:(j + 1) * LANES] = _window(
                xp_s, j, CONV_PAD + T - (CONV_WIDTH - 1), CONV_WIDTH - 1)
        for g in range(N_SSD_GROUPS):
            hg = ht_s[:, g * GROUP_WIDTH:(g + 1) * GROUP_WIDTH].T
            nssm_ref[0, 0, g * HEADS_PER_GROUP:(g + 1) * HEADS_PER_GROUP] = hg.reshape(
                HEADS_PER_GROUP, SSD_HEAD_DIM, D_STATE)

    uext_s[:, :, 0:POOL_PAD, :] = uext_s[:, :, T:T + POOL_PAD, :]
    xp_s[:, :, 0:CONV_PAD, :] = xp_s[:, :, T:T + CONV_PAD, :]


def _mixer_prompt(x1, w):
    B, S, _ = x1.shape
    T = PROMPT_TILE
    assert S % T == 0 and T % CHUNK == 0 and T >= POOL_PAD
    consts = [w["gmix"], w["wu"], w["wz"], w["wxbc"], w["wdt"], w["wgate"], w["pgw"], w["pscale"], w["pwo"],
              w["convw"], w["convb"], w["dtb"], w["alog"], w["dskip"], w["snorm"], w["swo"], w["wo"], w["expand"]]
    return pl.pallas_call(
        _mixer_prompt_kernel,
        grid=(B, S // T),
        in_specs=[pl.BlockSpec((1, T, D_MODEL), lambda b, t: (b, t, 0))] + [_const_spec(c.shape) for c in consts],
        out_specs=(pl.BlockSpec((1, T, D_MODEL), lambda b, t: (b, t, 0)),
                   pl.BlockSpec((1, 1, POOL_HIST, POOL_WIDTH), lambda b, t: (0, b, 0, 0)),
                   pl.BlockSpec((1, 1, CONV_WIDTH - 1, CONV_DIM), lambda b, t: (0, b, 0, 0)),
                   pl.BlockSpec((1, 1, N_SSD_HEADS, SSD_HEAD_DIM, D_STATE), lambda b, t: (0, b, 0, 0, 0))),
        out_shape=(jax.ShapeDtypeStruct((B, S, D_MODEL), f32),
                   jax.ShapeDtypeStruct((1, B, POOL_HIST, POOL_WIDTH), f32),
                   jax.ShapeDtypeStruct((1, B, CONV_WIDTH - 1, CONV_DIM), f32),
                   jax.ShapeDtypeStruct((1, B, N_SSD_HEADS, SSD_HEAD_DIM, D_STATE), f32)),
        scratch_shapes=[pltpu.VMEM((T, D_MODEL), bf16),
                        pltpu.VMEM((POOL_WIDTH // LANES, 1, POOL_PAD + T, LANES), f32),
                        pltpu.VMEM((CONV_DIM // LANES, 1, CONV_PAD + T, LANES), f32),
                        pltpu.VMEM((T, D_INNER), f32),
                        pltpu.VMEM((T, BC_DIM), f32),
                        pltpu.VMEM((T, BC_DIM), f32),
                        pltpu.VMEM((T, HEAD_LANES), f32),
                        pltpu.VMEM((T, D_INNER), f32),
                        pltpu.VMEM((D_STATE, D_INNER), f32),
                        pltpu.VMEM((T, D_INNER), bf16)],
        compiler_params=_params(2),
        name="mixer_prompt",
    )(x1, *consts)


def _sample_pre_kernel(x_ref, ph_ref, ch_ref, gmix_ref, wu_ref, wxbc_ref, wdt_ref, pgw_ref, pscale_ref, pwo_ref,
                       convw_ref, convb_ref, dtb_ref,
                       bp_ref, xs_ref, b_ref, c_ref, dt_ref, npool_ref, nconv_ref,
                       uext_s, xp_s, *, T):
    hn = _rms(x_ref[...], gmix_ref[...]).astype(bf16)

    for j in range(POOL_WIDTH // LANES):
        uext_s[j, :, POOL_PAD - POOL_HIST:POOL_PAD, :] = ph_ref[0, :, :, j * LANES:(j + 1) * LANES]
    _store_rows(uext_s, POOL_PAD, _dot(hn, wu_ref[...]), T)
    bp_ref[...] = _pool_branch(uext_s, T, PAST_LEN, pgw_ref, pscale_ref, pwo_ref)
    for j in range(POOL_WIDTH // LANES):
        npool_ref[0, :, :, j * LANES:(j + 1) * LANES] = _window(uext_s, j, POOL_PAD + T - POOL_HIST, POOL_HIST)

    for j in range(CONV_DIM // LANES):
        xp_s[j, :, CONV_PAD - (CONV_WIDTH - 1):CONV_PAD, :] = ch_ref[0, :, :, j * LANES:(j + 1) * LANES]
    _project_conv(hn, wxbc_ref, xp_s, T, convw_ref, convb_ref, xs_ref, b_ref, c_ref)
    for j in range(CONV_DIM // LANES):
        nconv_ref[0, :, :, j * LANES:(j + 1) * LANES] = _window(
            xp_s, j, CONV_PAD + T - (CONV_WIDTH - 1), CONV_WIDTH - 1)
    dt_ref[...] = _softplus(_dot(hn, wdt_ref[...]) + dtb_ref[...])


def _sample_pre(x1, pool_hist, conv_hist, w, T):
    rows = x1.shape[0]
    nseq = rows // T
    S = SAMPLE_SEQS
    R = S * T
    assert nseq % S == 0 and T % SUBLANES == 0
    consts = [w["gmix"], w["wu"], w["wxbc"], w["wdt"], w["pgw"], w["pscale"], w["pwo"], w["convw"], w["convb"],
              w["dtb"]]
    row_spec = lambda width: pl.BlockSpec((R, width), lambda i: (i, 0))
    return pl.pallas_call(
        functools.partial(_sample_pre_kernel, T=T),
        grid=(nseq // S,),
        in_specs=[row_spec(D_MODEL),
                  pl.BlockSpec((1, S, POOL_HIST, POOL_WIDTH), lambda i: (0, i, 0, 0)),
                  pl.BlockSpec((1, S, CONV_WIDTH - 1, CONV_DIM), lambda i: (0, i, 0, 0))]
                 + [_const_spec(c.shape) for c in consts],
        out_specs=(row_spec(D_MODEL), row_spec(D_INNER), row_spec(BC_DIM), row_spec(BC_DIM), row_spec(HEAD_LANES),
                   pl.BlockSpec((1, S, POOL_HIST, POOL_WIDTH), lambda i: (0, i, 0, 0)),
                   pl.BlockSpec((1, S, CONV_WIDTH - 1, CONV_DIM), lambda i: (0, i, 0, 0))),
        out_shape=(jax.ShapeDtypeStruct((rows, D_MODEL), f32),
                   jax.ShapeDtypeStruct((rows, D_INNER), f32),
                   jax.ShapeDtypeStruct((rows, BC_DIM), f32),
                   jax.ShapeDtypeStruct((rows, BC_DIM), f32),
                   jax.ShapeDtypeStruct((rows, HEAD_LANES), f32),
                   jax.ShapeDtypeStruct((1, nseq, POOL_HIST, POOL_WIDTH), f32),
                   jax.ShapeDtypeStruct((1, nseq, CONV_WIDTH - 1, CONV_DIM), f32)),
        scratch_shapes=[pltpu.VMEM((POOL_WIDTH // LANES, S, POOL_PAD + T, LANES), f32),
                        pltpu.VMEM((CONV_DIM // LANES, S, CONV_PAD + T, LANES), f32)],
        compiler_params=_params(1),
        name="sample_pre",
    )(x1, pool_hist, conv_hist, *consts)


def _ssd_decode_kernel(xs_ref, b_ref, c_ref, dt_ref, h0_ref, alog_ref, dskip_ref, e_ref, y_ref, hn_ref, *, T):
    assert T == SUBLANES
    row = lax.broadcasted_iota(jnp.int32, (T, HEAD_LANES), 0)
    lane_group = lax.broadcasted_iota(jnp.int32, (T, HEAD_LANES), 1) // HEADS_PER_GROUP
    neg_a = -jnp.exp(alog_ref[...])

    def shift(v, d):
        if d == 0:
            return v
        r = lax.broadcasted_iota(jnp.int32, v.shape, 0)
        return jnp.where(r >= d, pltpu.roll(v, d, axis=0), 0.0)

    def per_seq(s, carry):
        dt = dt_ref[s]
        a = dt * neg_a
        d = 1
        while d < T:
            a = a + shift(a, d)
            d *= 2
        a_end = a[T - 1:T, :]
        x = xs_ref[s]
        bm = b_ref[s]
        cm = c_ref[s]
        terms = []
        for d in range(T):
            cbv = jnp.zeros((T, HEAD_LANES), f32)
            for g in range(N_SSD_GROUPS):
                gc = slice(g * D_STATE, (g + 1) * D_STATE)
                cb = jnp.sum(cm[:, gc] * shift(bm[:, gc], d), axis=-1, keepdims=True)
                cbv = jnp.where(lane_group == g, cb, cbv)
            terms.append(cbv * jnp.exp(a - shift(a, d)) * shift(dt, d))
        terms.append(jnp.exp(a))
        terms.append(jnp.exp(a_end - a) * dt)
        v = jnp.concatenate(terms, axis=0)
        v_hi = v.astype(bf16)
        v_lo = (v - v_hi.astype(f32)).astype(bf16)
        ex = _dot(jnp.concatenate([v_hi, v_lo], axis=0), e_ref[...])
        n = (T + 2) * T
        ex = ex[0:n] + ex[n:2 * n]

        y = dskip_ref[...] * x
        for d in range(T):
            y = y + ex[d * T:(d + 1) * T] * shift(x, d)
        ea = ex[T * T:(T + 1) * T]
        xw = (x * ex[(T + 1) * T:(T + 2) * T]).astype(bf16)
        cdec = jnp.exp(a_end)
        y_groups = []
        for g in range(N_SSD_GROUPS):
            gc = slice(g * D_STATE, (g + 1) * D_STATE)
            cols = slice(g * GROUP_WIDTH, (g + 1) * GROUP_WIDTH)
            heads = slice(g * HEADS_PER_GROUP, (g + 1) * HEADS_PER_GROUP)
            h0g = h0_ref[0, s, heads].reshape(GROUP_WIDTH, D_STATE)
            ch = lax.dot_general(cm[:, gc].astype(bf16), h0g.astype(bf16), _NT, preferred_element_type=f32)
            y_groups.append(y[:, cols] + ea[:, cols] * ch)
            st = lax.dot_general(xw[:, cols], bm[:, gc].astype(bf16), _TN, preferred_element_type=f32)
            for r in range(HEADS_PER_GROUP):
                h = g * HEADS_PER_GROUP + r
                hn_ref[0, s, h] = (h0_ref[0, s, h] * cdec[:, h:h + 1]
                                   + st[r * SSD_HEAD_DIM:(r + 1) * SSD_HEAD_DIM, :])
        y_ref[s] = jnp.concatenate(y_groups, axis=-1)
        return carry

    lax.fori_loop(0, DECODE_SEQS, per_seq, 0)


def _ssd_decode(xs, bm, cm, dt, h0, w, T):
    nseq = xs.shape[0] // T
    Q = DECODE_SEQS
    assert nseq % Q == 0
    seq_spec = lambda width: pl.BlockSpec((Q, T, width), lambda i: (i, 0, 0))
    state_spec = pl.BlockSpec((1, Q, N_SSD_HEADS, SSD_HEAD_DIM, D_STATE), lambda i: (0, i, 0, 0, 0))
    consts = [w["alog"], w["dskip"], w["expand"]]
    return pl.pallas_call(
        functools.partial(_ssd_decode_kernel, T=T),
        grid=(nseq // Q,),
        in_specs=[seq_spec(D_INNER), seq_spec(BC_DIM), seq_spec(BC_DIM), seq_spec(HEAD_LANES), state_spec]
                 + [_const_spec(c.shape) for c in consts],
        out_specs=(seq_spec(D_INNER), state_spec),
        out_shape=(jax.ShapeDtypeStruct((nseq, T, D_INNER), f32), jax.ShapeDtypeStruct(h0.shape, f32)),
        compiler_params=_params(1),
        name="ssd_decode",
    )(xs.reshape(nseq, T, D_INNER), bm.reshape(nseq, T, BC_DIM), cm.reshape(nseq, T, BC_DIM),
      dt.reshape(nseq, T, HEAD_LANES), h0, *consts)


def _sample_post_kernel(x_ref, y_ref, bp_ref, gmix_ref, wz_ref, wgate_ref, snorm_ref, swo_ref, wo_ref,
                        o_ref, hn_s, yn_s):
    x = x_ref[...]
    hn_s[...] = _rms(x, gmix_ref[...]).astype(bf16)
    o_ref[...] = _gate_merge_out(x, hn_s, y_ref, bp_ref[...], wz_ref, wgate_ref, snorm_ref, swo_ref, wo_ref, yn_s)


def _sample_post(x1, y, bp, w, T):
    rows = x1.shape[0]
    R = SAMPLE_SEQS * T
    consts = [w["gmix"], w["wz"], w["wgate"], w["snorm"], w["swo"], w["wo"]]
    row_spec = lambda width: pl.BlockSpec((R, width), lambda i: (i, 0))
    return pl.pallas_call(
        _sample_post_kernel,
        grid=(rows // R,),
        in_specs=[row_spec(D_MODEL), row_spec(D_INNER), row_spec(D_MODEL)] + [_const_spec(c.shape) for c in consts],
        out_specs=row_spec(D_MODEL),
        out_shape=jax.ShapeDtypeStruct((rows, D_MODEL), f32),
        scratch_shapes=[pltpu.VMEM((R, D_MODEL), bf16), pltpu.VMEM((R, D_INNER), bf16)],
        compiler_params=_params(1),
        name="sample_post",
    )(x1, y, bp, *consts)


def _small_params(l, norm_mix, pool_scale, conv_w, conv_b, dt_bias, a_log, d_skip, ssd_norm):
    pad_heads = lambda v: jnp.pad(v, ((0, 0), (0, HEAD_LANES - N_SSD_HEADS)))
    head_of_lane = jnp.arange(D_INNER, dtype=jnp.int32) // SSD_HEAD_DIM
    expand = (jnp.arange(HEAD_LANES, dtype=jnp.int32)[:, None] == head_of_lane[None, :]).astype(bf16)
    return dict(
        gmix=norm_mix[l][None, :], pscale=pool_scale[l][None, :], convw=conv_w[l], convb=conv_b[l][None, :],
        dtb=pad_heads(dt_bias[l][None, :]), alog=pad_heads(a_log[l][None, :]),
        dskip=jnp.repeat(d_skip[l], SSD_HEAD_DIM)[None, :], snorm=ssd_norm[l][None, :], expand=expand)


def kernel(x_prompt, x_sample, state_pool, state_conv, state_ssm, norm_ffn1, ffn1_w_in, ffn1_w_out, norm_mix, w_in,
           pool_w_group, pool_scale, pool_w_out, conv_w, conv_b, dt_bias, a_log, d_skip, ssd_norm, ssd_w_out, w_o,
           norm_ffn2, ffn2_w_in, ffn2_w_out, norm_final):
    depth = w_in.shape[0]
    B, S, _ = x_prompt.shape
    DB, T, _ = x_sample.shape
    gfin = norm_final[None, :]
    xp = x_prompt.reshape(B * S, D_MODEL)
    xs = x_sample.reshape(DB * T, D_MODEL)
    outs = [[] for _ in range(6)]
    for l in range(depth):
        last = l == depth - 1
        f1_in, f1_out = ffn1_w_in[l].astype(bf16), ffn1_w_out[l].astype(bf16)
        g1, g2 = norm_ffn1[l][None, :], norm_ffn2[l][None, :]
        w = _small_params(l, norm_mix, pool_scale, conv_w, conv_b, dt_bias, a_log, d_skip, ssd_norm)
        n_pg = pool_w_group.shape[1]
        xp, in_proj, casts = _ffn_and_casts(
            xp, g1, f1_in, f1_out, gfin, w_in[l],
            [pool_w_group[l].reshape(n_pg * POOL_GROUP, POOL_GROUP), pool_w_out[l], ssd_w_out[l], w_o[l],
             ffn2_w_in[l], ffn2_w_out[l]])
        w.update(zip(("wu", "wz", "wxbc", "wdt", "wgate"), in_proj))
        w.update(pgw=casts[0].reshape(n_pg, POOL_GROUP, POOL_GROUP), pwo=casts[1], swo=casts[2], wo=casts[3])
        f2_in, f2_out = casts[4], casts[5]
        xp3, npool, nconv, nssm = _mixer_prompt(xp.reshape(B, S, D_MODEL), w)
        xp = _ffn(xp3.reshape(B * S, D_MODEL), g2, f2_in, f2_out, gfin, final_norm=last)
        xs = _ffn(xs, g1, f1_in, f1_out, gfin, final_norm=False)
        bp, cxs, cb, cc, cdt, spool, sconv = _sample_pre(xs, state_pool[l:l + 1], state_conv[l:l + 1], w, T)
        y, sssm = _ssd_decode(cxs, cb, cc, cdt, state_ssm[l:l + 1], w, T)
        xs = _sample_post(xs, y.reshape(DB * T, D_INNER), bp, w, T)
        xs = _ffn(xs, g2, f2_in, f2_out, gfin, final_norm=last)
        for acc, v in zip(outs, (npool, nconv, nssm, spool, sconv, sssm)):
            acc.append(v)
    stack = lambda vs: vs[0] if len(vs) == 1 else jnp.concatenate(vs, axis=0)
    return (xp.reshape(B, S, D_MODEL), xs.reshape(DB, T, D_MODEL),
            stack(outs[0]), stack(outs[1]), stack(outs[2]), stack(outs[3]), stack(outs[4]), stack(outs[5]))
```

```python
import functools

import jax
import jax.numpy as jnp
from jax import lax
from jax.experimental import pallas as pl
from jax.experimental.pallas import tpu as pltpu

f32 = jnp.float32
bf16 = jnp.bfloat16

D_MODEL = 1024
D_FF = 2816
POOL_WINDOWS = (2, 4, 8, 16)
POOL_WIDTH = D_MODEL
POOL_GROUP = POOL_WIDTH // len(POOL_WINDOWS)
POOL_HIST = max(POOL_WINDOWS) - 1
D_INNER = 2 * D_MODEL
SSD_HEAD_DIM = 64
N_SSD_HEADS = D_INNER // SSD_HEAD_DIM
N_SSD_GROUPS = 4
HEADS_PER_GROUP = N_SSD_HEADS // N_SSD_GROUPS
GROUP_WIDTH = D_INNER // N_SSD_GROUPS
D_STATE = 128
CONV_WIDTH = 4
BC_DIM = N_SSD_GROUPS * D_STATE
CONV_DIM = D_INNER + 2 * BC_DIM
CHUNK = 128
PAST_LEN = 16384
EPS = 1e-6

LANES = 128
SUBLANES = 8
VMEM_LIMIT_BYTES = 56 * 1024 * 1024

FFN_ROWS = 1024
FFN_CHUNK = 256
PROMPT_TILE = 256
SAMPLE_SEQS = 16
DECODE_SEQS = 8
DECODE_UNROLL = 4
POOL_PAD = 16
CONV_PAD = 8
HEAD_LANES = LANES

_NT = (((1,), (1,)), ((), ()))
_TN = (((0,), (0,)), ((), ()))


def _rms(x, g):
    return x * lax.rsqrt(jnp.mean(x * x, axis=-1, keepdims=True) + EPS) * g


def _silu(v):
    return v * jax.nn.sigmoid(v)


def _softplus(v):
    return jnp.maximum(v, 0.0) + jnp.log1p(jnp.exp(-jnp.abs(v)))


def _dot(a, b):
    return jnp.dot(a, b, preferred_element_type=f32)


def _split3(v):
    hi = v.astype(bf16)
    r1 = v - hi.astype(f32)
    mid = r1.astype(bf16)
    lo = (r1 - mid.astype(f32)).astype(bf16)
    return hi, mid, lo


def _const_spec(shape):
    nd = len(shape)
    return pl.BlockSpec(shape, lambda *_: (0,) * nd, pipeline_mode=pl.Buffered(1))


def _params(n_grid):
    return pltpu.CompilerParams(dimension_semantics=("arbitrary",) * n_grid,
                                vmem_limit_bytes=VMEM_LIMIT_BYTES)


def _ffn_kernel(x_ref, g_ref, win_ref, wout_ref, gfin_ref, o_ref, hn_s, act_s, *, final_norm):
    hn_s[...] = _rms(x_ref[...], g_ref[...]).astype(bf16)
    for c in range(D_FF // FFN_CHUNK):
        lo = c * FFN_CHUNK
        gate = _dot(hn_s[...], win_ref[:, lo:lo + FFN_CHUNK])
        up = _dot(hn_s[...], win_ref[:, D_FF + lo:D_FF + lo + FFN_CHUNK])
        act_s[:, lo:lo + FFN_CHUNK] = (_silu(gate) * up).astype(bf16)
    out = x_ref[...] + 0.5 * _dot(act_s[...], wout_ref[...])
    if final_norm:
        out = _rms(out, gfin_ref[...])
    o_ref[...] = out


def _ffn_specs(rows):
    in_specs = [pl.BlockSpec((FFN_ROWS, D_MODEL), lambda i: (i, 0)),
                _const_spec((1, D_MODEL)),
                _const_spec((D_MODEL, 2 * D_FF)),
                _const_spec((D_FF, D_MODEL)),
                _const_spec((1, D_MODEL))]
    out_spec = pl.BlockSpec((FFN_ROWS, D_MODEL), lambda i: (i, 0))
    scratch = [pltpu.VMEM((FFN_ROWS, D_MODEL), bf16), pltpu.VMEM((FFN_ROWS, D_FF), bf16)]
    return in_specs, out_spec, jax.ShapeDtypeStruct((rows, D_MODEL), f32), scratch


def _ffn(x2d, g, w_in_b, w_out_b, gfin, *, final_norm):
    rows = x2d.shape[0]
    assert rows % FFN_ROWS == 0
    in_specs, out_spec, out_shape, scratch = _ffn_specs(rows)
    return pl.pallas_call(
        functools.partial(_ffn_kernel, final_norm=final_norm),
        grid=(rows // FFN_ROWS,),
        in_specs=in_specs, out_specs=out_spec, out_shape=out_shape, scratch_shapes=scratch,
        compiler_params=_params(1),
        name="ffn_final" if final_norm else "ffn",
    )(x2d, g, w_in_b, w_out_b, gfin)


_IN_PROJ_SPLITS = (POOL_WIDTH, POOL_WIDTH + D_INNER, POOL_WIDTH + D_INNER + CONV_DIM,
                   POOL_WIDTH + D_INNER + CONV_DIM + N_SSD_HEADS)
BF16_SUBLANES = 2 * SUBLANES


def _cast_block(n_rows, n_steps):
    period = 1
    while (n_rows * period) % n_steps or ((n_rows * period) // n_steps) % BF16_SUBLANES:
        period *= 2
        assert period <= n_steps
    return (n_rows * period) // n_steps, period


def _ffn_cast_kernel(x_ref, g_ref, win_ref, wout_ref, gfin_ref, w_in_f, *rest, n_plain):
    plain_f = rest[:n_plain]
    o_ref = rest[n_plain]
    wu_o, wz_o, wxbc_o, wdt_o, wgate_o = rest[n_plain + 1:n_plain + 6]
    plain_o = rest[n_plain + 6:2 * n_plain + 6]
    hn_s, act_s = rest[2 * n_plain + 6:]
    _ffn_kernel(x_ref, g_ref, win_ref, wout_ref, gfin_ref, o_ref, hn_s, act_s, final_norm=False)
    s1, s2, s3, s4 = _IN_PROJ_SPLITS
    v = w_in_f[...]
    wu_o[...] = v[:, :s1].astype(bf16)
    wz_o[...] = v[:, s1:s2].astype(bf16)
    wxbc_o[...] = v[:, s2:s3].astype(bf16)
    pad = jnp.zeros((v.shape[0], HEAD_LANES - N_SSD_HEADS), f32)
    wdt_o[...] = jnp.concatenate([v[:, s3:s4], pad], axis=1).astype(bf16)
    wgate_o[...] = v[:, s4:].astype(bf16)
    for src, dst in zip(plain_f, plain_o):
        dst[...] = src[...].astype(bf16)


def _ffn_and_casts(x2d, g, w_in_b, w_out_b, gfin, w_in_proj, plain):
    rows = x2d.shape[0]
    assert rows % FFN_ROWS == 0
    n_steps = rows // FFN_ROWS
    in_specs, out_spec, out_shape, scratch = _ffn_specs(rows)

    def blocked(arr, width=None):
        blk, period = _cast_block(arr.shape[0], n_steps)
        w = arr.shape[1] if width is None else width
        return (pl.BlockSpec((blk, w), lambda i: (i // period, 0)),
                jax.ShapeDtypeStruct((arr.shape[0], w), bf16))

    piece_widths = (POOL_WIDTH, D_INNER, CONV_DIM, HEAD_LANES, 2 * D_MODEL)
    pieces = [blocked(w_in_proj, width) for width in piece_widths]
    plains = [blocked(p) for p in plain]
    res = pl.pallas_call(
        functools.partial(_ffn_cast_kernel, n_plain=len(plain)),
        grid=(n_steps,),
        in_specs=in_specs + [blocked(w_in_proj)[0]] + [s for s, _ in plains],
        out_specs=tuple([out_spec] + [s for s, _ in pieces] + [s for s, _ in plains]),
        out_shape=tuple([out_shape] + [o for _, o in pieces] + [o for _, o in plains]),
        scratch_shapes=scratch,
        compiler_params=_params(1),
        name="ffn_casts",
    )(x2d, g, w_in_b, w_out_b, gfin, w_in_proj, *plain)
    return res[0], res[1:6], res[6:]


def _store_rows(ref, row0, val, T, slab0=0):
    S = ref.shape[1]
    for j in range(val.shape[1] // LANES):
        blk = val[:, j * LANES:(j + 1) * LANES]
        if S == 1:
            ref[slab0 + j, 0, row0:row0 + T, :] = blk
        else:
            ref[slab0 + j, :, row0:row0 + T, :] = blk.reshape(S, T, LANES)


def _window(ref, j, row0, n):
    S = ref.shape[1]
    rows = pl.ds(row0, n) if row0 % SUBLANES == 0 else pl.ds(row0, n, stride=1)
    if S == 1:
        return ref[j, 0, rows, :]
    v = ref[j, :, rows, :]
    return v.reshape(S * n, LANES) if n % SUBLANES == 0 else v


def _pool_branch(uext_ref, T, pos0, pgw_ref, pscale_ref, pwo_ref):
    S = uext_ref.shape[1]
    assert T & (T - 1) == 0
    t_idx = jnp.bitwise_and(lax.broadcasted_iota(jnp.int32, (S * T, 1), 0), T - 1)
    n_seen = pos0 + 1 + t_idx
    slabs_per_group = POOL_GROUP // LANES
    mixed = []
    for gi, w in enumerate(POOL_WINDOWS):
        cnt = jnp.minimum(n_seen, w).astype(f32)
        d = []
        for j in range(gi * slabs_per_group, (gi + 1) * slabs_per_group):
            cur = _window(uext_ref, j, POOL_PAD, T)
            s = cur
            for k in range(1, w):
                s = s + _window(uext_ref, j, POOL_PAD - k, T)
            d.append((s / cnt - cur).astype(bf16))
        mixed.append(_dot(jnp.concatenate(d, axis=-1), pgw_ref[gi]))
    y = jnp.concatenate(mixed, axis=-1) * pscale_ref[...]
    return _dot(y.astype(bf16), pwo_ref[...])


def _conv_slab(xp_ref, T, convw_ref, convb_ref, j, xs_ref, b_ref, c_ref):
    c0 = j * LANES
    cols = slice(c0, c0 + LANES)
    y = convb_ref[:, cols]
    for k in range(CONV_WIDTH):
        y = y + _window(xp_ref, j, CONV_PAD - (CONV_WIDTH - 1) + k, T) * convw_ref[k:k + 1, cols]
    v = _silu(y)
    if c0 < D_INNER:
        xs_ref[:, cols] = v
    elif c0 < D_INNER + BC_DIM:
        b_ref[:, c0 - D_INNER:c0 - D_INNER + LANES] = v
    else:
        c_ref[:, c0 - D_INNER - BC_DIM:c0 - D_INNER - BC_DIM + LANES] = v


def _project_conv(hn_s, wxbc_ref, xp_ref, T, convw_ref, convb_ref, xs_ref, b_ref, c_ref):
    tile = 2 * LANES
    for jt in range(CONV_DIM // tile):
        slab0 = jt * (tile // LANES)
        _store_rows(xp_ref, CONV_PAD, _dot(hn_s[...], wxbc_ref[:, jt * tile:(jt + 1) * tile]), T, slab0=slab0)
        for j in range(slab0, slab0 + tile // LANES):
            _conv_slab(xp_ref, T, convw_ref, convb_ref, j, xs_ref, b_ref, c_ref)


def _gate_merge_out(x, hn_s, y_ref, bp, wz_ref, wgate_ref, snorm_ref, swo_ref, wo_ref, yn_s):
    for g in range(N_SSD_GROUPS):
        cols = slice(g * GROUP_WIDTH, (g + 1) * GROUP_WIDTH)
        z = _dot(hn_s[...], wz_ref[:, cols])
        yg = y_ref[:, cols] * _silu(z)
        yg = yg * lax.rsqrt(jnp.mean(yg * yg, axis=-1, keepdims=True) + EPS)
        yn_s[:, cols] = (yg * snorm_ref[:, cols]).astype(bf16)
    branch_ssd = _dot(yn_s[...], swo_ref[...])
    gate_pool = jax.nn.sigmoid(_dot(hn_s[...], wgate_ref[:, 0:D_MODEL]))
    gate_ssd = jax.nn.sigmoid(_dot(hn_s[...], wgate_ref[:, D_MODEL:2 * D_MODEL]))
    merged = (gate_pool * bp + gate_ssd * branch_ssd).astype(bf16)
    return x + _dot(merged, wo_ref[...])


def _ssd_chunk(rows, xs_s, b_s, c_s, dt_s, y_s, ht_s, alog_ref, dskip_ref, e_ref):
    L = CHUNK
    ri = lax.broadcasted_iota(jnp.int32, (L, L), 0)
    ci = lax.broadcasted_iota(jnp.int32, (L, L), 1)
    causal = ri >= ci
    tril = jnp.where(causal, 1.0, 0.0).astype(bf16)
    first_head = lax.broadcasted_iota(jnp.int32, (L, LANES), 1) < SSD_HEAD_DIM

    dt = dt_s[rows, :]
    dA = dt * (-jnp.exp(alog_ref[...]))
    hi, mid, lo = _split3(dA)
    acc = _dot(tril, jnp.concatenate([hi, mid, lo], axis=1))
    a = acc[:, 0:LANES] + acc[:, LANES:2 * LANES] + acc[:, 2 * LANES:3 * LANES]
    aT = a.T[0:N_SSD_HEADS]
    dtT = dt.T[0:N_SSD_HEADS]
    wT = jnp.exp(aT[:, L - 1:L] - aT) * dtT
    srcT = aT - jnp.log(dtT)
    end_decay = jnp.broadcast_to(jnp.exp(a[L - 1:L, :]), (2 * SUBLANES, HEAD_LANES))
    e_hi, e_mid, e_lo = (t.astype(f32) for t in _split3(end_decay))
    sel = lax.broadcasted_iota(jnp.int32, (2 * SUBLANES, HEAD_LANES), 0)
    stacked = jnp.where(sel == 0, e_hi, jnp.where(sel == 1, e_mid, jnp.where(sel == 2, e_lo, 0.0)))
    cdec = jnp.sum(_dot(stacked.astype(bf16), e_ref[...]), axis=0, keepdims=True)

    def head_mats(h, cb, bT):
        acol = jnp.broadcast_to(a[:, h:h + 1], (L, L))
        m = (cb * jnp.exp(jnp.where(causal, acol - srcT[h:h + 1, :], -jnp.inf))).astype(bf16)
        bw = (bT * wT[h:h + 1, :]).astype(bf16)
        return m, bw, jnp.exp(acol)

    pairs_per_group = HEADS_PER_GROUP // 2
    for q in range(N_SSD_HEADS // 2):
        g, qg = divmod(q, pairs_per_group)
        if qg == 0:
            gcols = slice(g * D_STATE, (g + 1) * D_STATE)
            b_f = b_s[rows, gcols]
            c_b = c_s[rows, gcols].astype(bf16)
            cb = lax.dot_general(c_b, b_f.astype(bf16), _NT, preferred_element_type=f32)
            bT = b_f.T
            ch = _dot(c_b, ht_s[:, g * GROUP_WIDTH:(g + 1) * GROUP_WIDTH].astype(bf16))
        cols = slice(q * LANES, (q + 1) * LANES)
        xq = xs_s[rows, cols]
        x2 = jnp.concatenate([jnp.where(first_head, xq, 0.0), jnp.where(first_head, 0.0, xq)], axis=0).astype(bf16)
        m_a, bw_a, ea_a = head_mats(2 * q, cb, bT)
        m_b, bw_b, ea_b = head_mats(2 * q + 1, cb, bT)
        y_off = ch[:, qg * LANES:(qg + 1) * LANES] * jnp.where(first_head, ea_a, ea_b)
        y_s[rows, cols] = _dot(jnp.concatenate([m_a, m_b], axis=1), x2) + y_off + dskip_ref[:, cols] * xq
        st = _dot(jnp.concatenate([bw_a, bw_b], axis=1), x2)
        ht_s[:, cols] = ht_s[:, cols] * cdec[:, cols] + st


def _mixer_prompt_kernel(x_ref, gmix_ref, wu_ref, wz_ref, wxbc_ref, wdt_ref, wgate_ref, pgw_ref, pscale_ref,
                         pwo_ref, convw_ref, convb_ref, dtb_ref, alog_ref, dskip_ref, snorm_ref, swo_ref,
                         wo_ref, e_ref,
                         o_ref, npool_ref, nconv_ref, nssm_ref,
                         hn_s, uext_s, xp_s, xs_s, b_s, c_s, dt_s, y_s, ht_s, yn_s):
    T = PROMPT_TILE
    ti = pl.program_id(1)

    @pl.when(ti == 0)
    def _():
        uext_s[:, :, 0:POOL_PAD, :] = jnp.zeros((POOL_WIDTH // LANES, 1, POOL_PAD, LANES), f32)
        xp_s[:, :, 0:CONV_PAD, :] = jnp.zeros((CONV_DIM // LANES, 1, CONV_PAD, LANES), f32)
        ht_s[...] = jnp.zeros(ht_s.shape, f32)

    x = x_ref[0]
    hn_s[...] = _rms(x, gmix_ref[...]).astype(bf16)

    _store_rows(uext_s, POOL_PAD, _dot(hn_s[...], wu_ref[...]), T)
    bp = _pool_branch(uext_s, T, ti * T, pgw_ref, pscale_ref, pwo_ref)

    _project_conv(hn_s, wxbc_ref, xp_s, T, convw_ref, convb_ref, xs_s, b_s, c_s)
    dt_s[...] = _softplus(_dot(hn_s[...], wdt_ref[...]) + dtb_ref[...])

    def chunk(c, carry):
        rows = pl.ds(pl.multiple_of(c * CHUNK, CHUNK), CHUNK)
        _ssd_chunk(rows, xs_s, b_s, c_s, dt_s, y_s, ht_s, alog_ref, dskip_ref, e_ref)
        return carry

    lax.fori_loop(0, T // CHUNK, chunk, 0)

    o_ref[0] = _gate_merge_out(x, hn_s, y_s, bp, wz_ref, wgate_ref, snorm_ref, swo_ref, wo_ref, yn_s)

    @pl.when(ti == pl.num_programs(1) - 1)
    def _():
        for j in range(POOL_WIDTH // LANES):
            npool_ref[0, 0, :, j * LANES:(j + 1) * LANES] = _window(uext_s, j, POOL_PAD + T - POOL_HIST, POOL_HIST)
        for j in range(CONV_DIM // LANES):
            nconv_ref[0, 0, :, j * LANES:(j + 1) * LANES] = _window(
                xp_s, j, CONV_PAD + T - (CONV_WIDTH - 1), CONV_WIDTH - 1)
        for g in range(N_SSD_GROUPS):
            hg = ht_s[:, g * GROUP_WIDTH:(g + 1) * GROUP_WIDTH].T
            nssm_ref[0, 0, g * HEADS_PER_GROUP:(g + 1) * HEADS_PER_GROUP] = hg.reshape(
                HEADS_PER_GROUP, SSD_HEAD_DIM, D_STATE)

    uext_s[:, :, 0:POOL_PAD, :] = uext_s[:, :, T:T + POOL_PAD, :]
    xp_s[:, :, 0:CONV_PAD, :] = xp_s[:, :, T:T + CONV_PAD, :]


def _mixer_prompt(x1, w):
    B, S, _ = x1.shape
    T = PROMPT_TILE
    assert S % T == 0 and T % CHUNK == 0 and T >= POOL_PAD
    consts = [w["gmix"], w["wu"], w["wz"], w["wxbc"], w["wdt"], w["wgate"], w["pgw"], w["pscale"], w["pwo"],
              w["convw"], w["convb"], w["dtb"], w["alog"], w["dskip"], w["snorm"], w["swo"], w["wo"], w["expand"]]
    return pl.pallas_call(
        _mixer_prompt_kernel,
        grid=(B, S // T),
        in_specs=[pl.BlockSpec((1, T, D_MODEL), lambda b, t: (b, t, 0))] + [_const_spec(c.shape) for c in consts],
        out_specs=(pl.BlockSpec((1, T, D_MODEL), lambda b, t: (b, t, 0)),
                   pl.BlockSpec((1, 1, POOL_HIST, POOL_WIDTH), lambda b, t: (0, b, 0, 0)),
                   pl.BlockSpec((1, 1, CONV_WIDTH - 1, CONV_DIM), lambda b, t: (0, b, 0, 0)),
                   pl.BlockSpec((1, 1, N_SSD_HEADS, SSD_HEAD_DIM, D_STATE), lambda b, t: (0, b, 0, 0, 0))),
        out_shape=(jax.ShapeDtypeStruct((B, S, D_MODEL), f32),
                   jax.ShapeDtypeStruct((1, B, POOL_HIST, POOL_WIDTH), f32),
                   jax.ShapeDtypeStruct((1, B, CONV_WIDTH - 1, CONV_DIM), f32),
                   jax.ShapeDtypeStruct((1, B, N_SSD_HEADS, SSD_HEAD_DIM, D_STATE), f32)),
        scratch_shapes=[pltpu.VMEM((T, D_MODEL), bf16),
                        pltpu.VMEM((POOL_WIDTH // LANES, 1, POOL_PAD + T, LANES), f32),
                        pltpu.VMEM((CONV_DIM // LANES, 1, CONV_PAD + T, LANES), f32),
                        pltpu.VMEM((T, D_INNER), f32),
                        pltpu.VMEM((T, BC_DIM), f32),
                        pltpu.VMEM((T, BC_DIM), f32),
                        pltpu.VMEM((T, HEAD_LANES), f32),
                        pltpu.VMEM((T, D_INNER), f32),
                        pltpu.VMEM((D_STATE, D_INNER), f32),
                        pltpu.VMEM((T, D_INNER), bf16)],
        compiler_params=_params(2),
        name="mixer_prompt",
    )(x1, *consts)


def _sample_pre_kernel(x_ref, ph_ref, ch_ref, gmix_ref, wu_ref, wxbc_ref, wdt_ref, pgw_ref, pscale_ref, pwo_ref,
                       convw_ref, convb_ref, dtb_ref,
                       bp_ref, xs_ref, b_ref, c_ref, dt_ref, npool_ref, nconv_ref,
                       uext_s, xp_s, *, T):
    hn = _rms(x_ref[...], gmix_ref[...]).astype(bf16)

    for j in range(POOL_WIDTH // LANES):
        uext_s[j, :, POOL_PAD - POOL_HIST:POOL_PAD, :] = ph_ref[0, :, :, j * LANES:(j + 1) * LANES]
    _store_rows(uext_s, POOL_PAD, _dot(hn, wu_ref[...]), T)
    bp_ref[...] = _pool_branch(uext_s, T, PAST_LEN, pgw_ref, pscale_ref, pwo_ref)
    for j in range(POOL_WIDTH // LANES):
        npool_ref[0, :, :, j * LANES:(j + 1) * LANES] = _window(uext_s, j, POOL_PAD + T - POOL_HIST, POOL_HIST)

    for j in range(CONV_DIM // LANES):
        xp_s[j, :, CONV_PAD - (CONV_WIDTH - 1):CONV_PAD, :] = ch_ref[0, :, :, j * LANES:(j + 1) * LANES]
    _project_conv(hn, wxbc_ref, xp_s, T, convw_ref, convb_ref, xs_ref, b_ref, c_ref)
    for j in range(CONV_DIM // LANES):
        nconv_ref[0, :, :, j * LANES:(j + 1) * LANES] = _window(
            xp_s, j, CONV_PAD + T - (CONV_WIDTH - 1), CONV_WIDTH - 1)
    dt_ref[...] = _softplus(_dot(hn, wdt_ref[...]) + dtb_ref[...])


def _sample_pre(x1, pool_hist, conv_hist, w, T):
    rows = x1.shape[0]
    nseq = rows // T
    S = SAMPLE_SEQS
    R = S * T
    assert nseq % S == 0 and T % SUBLANES == 0
    consts = [w["gmix"], w["wu"], w["wxbc"], w["wdt"], w["pgw"], w["pscale"], w["pwo"], w["convw"], w["convb"],
              w["dtb"]]
    row_spec = lambda width: pl.BlockSpec((R, width), lambda i: (i, 0))
    return pl.pallas_call(
        functools.partial(_sample_pre_kernel, T=T),
        grid=(nseq // S,),
        in_specs=[row_spec(D_MODEL),
                  pl.BlockSpec((1, S, POOL_HIST, POOL_WIDTH), lambda i: (0, i, 0, 0)),
                  pl.BlockSpec((1, S, CONV_WIDTH - 1, CONV_DIM), lambda i: (0, i, 0, 0))]
                 + [_const_spec(c.shape) for c in consts],
        out_specs=(row_spec(D_MODEL), row_spec(D_INNER), row_spec(BC_DIM), row_spec(BC_DIM), row_spec(HEAD_LANES),
                   pl.BlockSpec((1, S, POOL_HIST, POOL_WIDTH), lambda i: (0, i, 0, 0)),
                   pl.BlockSpec((1, S, CONV_WIDTH - 1, CONV_DIM), lambda i: (0, i, 0, 0))),
        out_shape=(jax.ShapeDtypeStruct((rows, D_MODEL), f32),
                   jax.ShapeDtypeStruct((rows, D_INNER), f32),
                   jax.ShapeDtypeStruct((rows, BC_DIM), f32),
                   jax.ShapeDtypeStruct((rows, BC_DIM), f32),
                   jax.ShapeDtypeStruct((rows, HEAD_LANES), f32),
                   jax.ShapeDtypeStruct((1, nseq, POOL_HIST, POOL_WIDTH), f32),
                   jax.ShapeDtypeStruct((1, nseq, CONV_WIDTH - 1, CONV_DIM), f32)),
        scratch_shapes=[pltpu.VMEM((POOL_WIDTH // LANES, S, POOL_PAD + T, LANES), f32),
                        pltpu.VMEM((CONV_DIM // LANES, S, CONV_PAD + T, LANES), f32)],
        compiler_params=_params(1),
        name="sample_pre",
    )(x1, pool_hist, conv_hist, *consts)


def _ssd_decode_kernel(xs_ref, b_ref, c_ref, dt_ref, h0_ref, alog_ref, dskip_ref, e_ref, y_ref, hn_ref, *, T):
    assert T == SUBLANES
    row = lax.broadcasted_iota(jnp.int32, (T, HEAD_LANES), 0)
    lane_group = lax.broadcasted_iota(jnp.int32, (T, HEAD_LANES), 1) // HEADS_PER_GROUP
    neg_a = -jnp.exp(alog_ref[...])

    def shift(v, d):
        if d == 0:
            return v
        r = lax.broadcasted_iota(jnp.int32, v.shape, 0)
        return jnp.where(r >= d, pltpu.roll(v, d, axis=0), 0.0)

    def per_seq(s, carry):
        dt = dt_ref[s]
        a = dt * neg_a
        d = 1
        while d < T:
            a = a + shift(a, d)
            d *= 2
        a_end = a[T - 1:T, :]
        x = xs_ref[s]
        bm = b_ref[s]
        cm = c_ref[s]
        terms = []
        for d in range(T):
            cbv = jnp.zeros((T, HEAD_LANES), f32)
            for g in range(N_SSD_GROUPS):
                gc = slice(g * D_STATE, (g + 1) * D_STATE)
                cb = jnp.sum(cm[:, gc] * shift(bm[:, gc], d), axis=-1, keepdims=True)
                cbv = jnp.where(lane_group == g, cb, cbv)
            terms.append(cbv * jnp.exp(a - shift(a, d)) * shift(dt, d))
        terms.append(jnp.exp(a))
        terms.append(jnp.exp(a_end - a) * dt)
        v = jnp.concatenate(terms, axis=0)
        v_hi = v.astype(bf16)
        v_lo = (v - v_hi.astype(f32)).astype(bf16)
        ex = _dot(jnp.concatenate([v_hi, v_lo], axis=0), e_ref[...])
        n = (T + 2) * T
        ex = ex[0:n] + ex[n:2 * n]

        y = dskip_ref[...] * x
        for d in range(T):
            y = y + ex[d * T:(d + 1) * T] * shift(x, d)
        ea = ex[T * T:(T + 1) * T]
        xw = (x * ex[(T + 1) * T:(T + 2) * T]).astype(bf16)
        cdec = jnp.exp(a_end)
        y_groups = []
        for g in range(N_SSD_GROUPS):
            gc = slice(g * D_STATE, (g + 1) * D_STATE)
            cols = slice(g * GROUP_WIDTH, (g + 1) * GROUP_WIDTH)
            heads = slice(g * HEADS_PER_GROUP, (g + 1) * HEADS_PER_GROUP)
            h0g = h0_ref[0, s, heads].reshape(GROUP_WIDTH, D_STATE)
            ch = lax.dot_general(cm[:, gc].astype(bf16), h0g.astype(bf16), _NT, preferred_element_type=f32)
            y_groups.append(y[:, cols] + ea[:, cols] * ch)
            st = lax.dot_general(xw[:, cols], bm[:, gc].astype(bf16), _TN, preferred_element_type=f32)
            for r in range(HEADS_PER_GROUP):
                h = g * HEADS_PER_GROUP + r
                hn_ref[0, s, h] = (h0_ref[0, s, h] * cdec[:, h:h + 1]
                                   + st[r * SSD_HEAD_DIM:(r + 1) * SSD_HEAD_DIM, :])
        y_ref[s] = jnp.concatenate(y_groups, axis=-1)
        return carry

    lax.fori_loop(0, DECODE_SEQS, per_seq, 0, unroll=DECODE_UNROLL)


def _ssd_decode(xs, bm, cm, dt, h0, w, T):
    nseq = xs.shape[0] // T
    Q = DECODE_SEQS
    assert nseq % Q == 0
    seq_spec = lambda width: pl.BlockSpec((Q, T, width), lambda i: (i, 0, 0))
    state_spec = pl.BlockSpec((1, Q, N_SSD_HEADS, SSD_HEAD_DIM, D_STATE), lambda i: (0, i, 0, 0, 0))
    consts = [w["alog"], w["dskip"], w["expand"]]
    return pl.pallas_call(
        functools.partial(_ssd_decode_kernel, T=T),
        grid=(nseq // Q,),
        in_specs=[seq_spec(D_INNER), seq_spec(BC_DIM), seq_spec(BC_DIM), seq_spec(HEAD_LANES), state_spec]
                 + [_const_spec(c.shape) for c in consts],
        out_specs=(seq_spec(D_INNER), state_spec),
        out_shape=(jax.ShapeDtypeStruct((nseq, T, D_INNER), f32), jax.ShapeDtypeStruct(h0.shape, f32)),
        compiler_params=_params(1),
        name="ssd_decode",
    )(xs.reshape(nseq, T, D_INNER), bm.reshape(nseq, T, BC_DIM), cm.reshape(nseq, T, BC_DIM),
      dt.reshape(nseq, T, HEAD_LANES), h0, *consts)


def _sample_post_kernel(x_ref, y_ref, bp_ref, gmix_ref, wz_ref, wgate_ref, snorm_ref, swo_ref, wo_ref,
                        o_ref, hn_s, yn_s):
    x = x_ref[...]
    hn_s[...] = _rms(x, gmix_ref[...]).astype(bf16)
    o_ref[...] = _gate_merge_out(x, hn_s, y_ref, bp_ref[...], wz_ref, wgate_ref, snorm_ref, swo_ref, wo_ref, yn_s)


def _sample_post(x1, y, bp, w, T):
    rows = x1.shape[0]
    R = SAMPLE_SEQS * T
    consts = [w["gmix"], w["wz"], w["wgate"], w["snorm"], w["swo"], w["wo"]]
    row_spec = lambda width: pl.BlockSpec((R, width), lambda i: (i, 0))
    return pl.pallas_call(
        _sample_post_kernel,
        grid=(rows // R,),
        in_specs=[row_spec(D_MODEL), row_spec(D_INNER), row_spec(D_MODEL)] + [_const_spec(c.shape) for c in consts],
        out_specs=row_spec(D_MODEL),
        out_shape=jax.ShapeDtypeStruct((rows, D_MODEL), f32),
        scratch_shapes=[pltpu.VMEM((R, D_MODEL), bf16), pltpu.VMEM((R, D_INNER), bf16)],
        compiler_params=_params(1),
        name="sample_post",
    )(x1, y, bp, *consts)


def _small_params(l, norm_mix, pool_scale, conv_w, conv_b, dt_bias, a_log, d_skip, ssd_norm):
    pad_heads = lambda v: jnp.pad(v, ((0, 0), (0, HEAD_LANES - N_SSD_HEADS)))
    head_of_lane = jnp.arange(D_INNER, dtype=jnp.int32) // SSD_HEAD_DIM
    expand = (jnp.arange(HEAD_LANES, dtype=jnp.int32)[:, None] == head_of_lane[None, :]).astype(bf16)
    return dict(
        gmix=norm_mix[l][None, :], pscale=pool_scale[l][None, :], convw=conv_w[l], convb=conv_b[l][None, :],
        dtb=pad_heads(dt_bias[l][None, :]), alog=pad_heads(a_log[l][None, :]),
        dskip=jnp.repeat(d_skip[l], SSD_HEAD_DIM)[None, :], snorm=ssd_norm[l][None, :], expand=expand)


def kernel(x_prompt, x_sample, state_pool, state_conv, state_ssm, norm_ffn1, ffn1_w_in, ffn1_w_out, norm_mix, w_in,
           pool_w_group, pool_scale, pool_w_out, conv_w, conv_b, dt_bias, a_log, d_skip, ssd_norm, ssd_w_out, w_o,
           norm_ffn2, ffn2_w_in, ffn2_w_out, norm_final):
    depth = w_in.shape[0]
    B, S, _ = x_prompt.shape
    DB, T, _ = x_sample.shape
    gfin = norm_final[None, :]
    xp = x_prompt.reshape(B * S, D_MODEL)
    xs = x_sample.reshape(DB * T, D_MODEL)
    outs = [[] for _ in range(6)]
    for l in range(depth):
        last = l == depth - 1
        f1_in, f1_out = ffn1_w_in[l].astype(bf16), ffn1_w_out[l].astype(bf16)
        g1, g2 = norm_ffn1[l][None, :], norm_ffn2[l][None, :]
        w = _small_params(l, norm_mix, pool_scale, conv_w, conv_b, dt_bias, a_log, d_skip, ssd_norm)
        n_pg = pool_w_group.shape[1]
        xp, in_proj, casts = _ffn_and_casts(
            xp, g1, f1_in, f1_out, gfin, w_in[l],
            [pool_w_group[l].reshape(n_pg * POOL_GROUP, POOL_GROUP), pool_w_out[l], ssd_w_out[l], w_o[l],
             ffn2_w_in[l], ffn2_w_out[l]])
        w.update(zip(("wu", "wz", "wxbc", "wdt", "wgate"), in_proj))
        w.update(pgw=casts[0].reshape(n_pg, POOL_GROUP, POOL_GROUP), pwo=casts[1], swo=casts[2], wo=casts[3])
        f2_in, f2_out = casts[4], casts[5]
        xp3, npool, nconv, nssm = _mixer_prompt(xp.reshape(B, S, D_MODEL), w)
        xp = _ffn(xp3.reshape(B * S, D_MODEL), g2, f2_in, f2_out, gfin, final_norm=last)
        xs = _ffn(xs, g1, f1_in, f1_out, gfin, final_norm=False)
        bp, cxs, cb, cc, cdt, spool, sconv = _sample_pre(xs, state_pool[l:l + 1], state_conv[l:l + 1], w, T)
        y, sssm = _ssd_decode(cxs, cb, cc, cdt, state_ssm[l:l + 1], w, T)
        xs = _sample_post(xs, y.reshape(DB * T, D_INNER), bp, w, T)
        xs = _ffn(xs, g2, f2_in, f2_out, gfin, final_norm=last)
        for acc, v in zip(outs, (npool, nconv, nssm, spool, sconv, sssm)):
            acc.append(v)
    stack = lambda vs: vs[0] if len(vs) == 1 else jnp.concatenate(vs, axis=0)
    return (xp.reshape(B, S, D_MODEL), xs.reshape(DB, T, D_MODEL),
            stack(outs[0]), stack(outs[1]), stack(outs[2]), stack(outs[3]), stack(outs[4]), stack(outs[5]))
```

```python
import functools

import jax
import jax.numpy as jnp
from jax import lax
from jax.experimental import pallas as pl
from jax.experimental.pallas import tpu as pltpu

f32 = jnp.float32
bf16 = jnp.bfloat16

D_MODEL = 1024
D_FF = 2816
POOL_WINDOWS = (2, 4, 8, 16)
POOL_WIDTH = D_MODEL
POOL_GROUP = POOL_WIDTH // len(POOL_WINDOWS)
POOL_HIST = max(POOL_WINDOWS) - 1
D_INNER = 2 * D_MODEL
SSD_HEAD_DIM = 64
N_SSD_HEADS = D_INNER // SSD_HEAD_DIM
N_SSD_GROUPS = 4
HEADS_PER_GROUP = N_SSD_HEADS // N_SSD_GROUPS
GROUP_WIDTH = D_INNER // N_SSD_GROUPS
D_STATE = 128
CONV_WIDTH = 4
BC_DIM = N_SSD_GROUPS * D_STATE
CONV_DIM = D_INNER + 2 * BC_DIM
CHUNK = 128
PAST_LEN = 16384
EPS = 1e-6

LANES = 128
SUBLANES = 8
MXU_COLS = 256
VMEM_LIMIT_BYTES = 56 * 1024 * 1024

FFN_ROWS = 512
FFN_CHUNK = 256
PROMPT_TILE = 256
SAMPLE_SEQS = 16
SAMPLE_POST_SEQS = 32
DECODE_SEQS = 8
DECODE_UNROLL = 4
POOL_PAD = 16
CONV_PAD = 8
HEAD_LANES = LANES
SLABS_PER_BUF = MXU_COLS // LANES
N_POOL_BUFS = POOL_WIDTH // MXU_COLS
N_CONV_BUFS = CONV_DIM // MXU_COLS

_NT = (((1,), (1,)), ((), ()))
_TN = (((0,), (0,)), ((), ()))


def _rms(x, g):
    return x * lax.rsqrt(jnp.mean(x * x, axis=-1, keepdims=True) + EPS) * g


def _sigmoid(v):
    return 0.5 * jnp.tanh(0.5 * v) + 0.5


def _silu(v):
    h = 0.5 * v
    return h * jnp.tanh(h) + h


def _softplus(v):
    return jnp.maximum(v, 0.0) + jnp.log1p(jnp.exp(-jnp.abs(v)))


def _dot(a, b):
    return jnp.dot(a, b, preferred_element_type=f32)


def _split3(v):
    hi = v.astype(bf16)
    r1 = v - hi.astype(f32)
    mid = r1.astype(bf16)
    lo = (r1 - mid.astype(f32)).astype(bf16)
    return hi, mid, lo


def _const_spec(shape):
    nd = len(shape)
    return pl.BlockSpec(shape, lambda *_: (0,) * nd, pipeline_mode=pl.Buffered(1))


def _params(n_grid):
    return pltpu.CompilerParams(dimension_semantics=("arbitrary",) * n_grid,
                                vmem_limit_bytes=VMEM_LIMIT_BYTES)


def _ffn_kernel(xa_ref, xb_ref, g_ref, win_ref, wout_ref, gfin_ref, oa_ref, ob_ref, hn_s, act_s, *, n_a, final_norm):
    def tile(x_ref, o_ref):
        hn_s[...] = _rms(x_ref[...], g_ref[...]).astype(bf16)
        for c in range(D_FF // FFN_CHUNK):
            lo = c * FFN_CHUNK
            gate = _dot(hn_s[...], win_ref[:, lo:lo + FFN_CHUNK])
            up = _dot(hn_s[...], win_ref[:, D_FF + lo:D_FF + lo + FFN_CHUNK])
            act_s[:, lo:lo + FFN_CHUNK] = (_silu(gate) * up).astype(bf16)
        out = x_ref[...] + 0.5 * _dot(act_s[...], wout_ref[...])
        if final_norm:
            out = _rms(out, gfin_ref[...])
        o_ref[...] = out

    on_a = pl.program_id(0) < n_a
    pl.when(on_a)(lambda: tile(xa_ref, oa_ref))
    pl.when(jnp.logical_not(on_a))(lambda: tile(xb_ref, ob_ref))


def _ffn_specs(rows_a, rows_b):
    assert rows_a % FFN_ROWS == 0 and rows_b % FFN_ROWS == 0
    n_a, n_b = rows_a // FFN_ROWS, rows_b // FFN_ROWS
    a_spec = pl.BlockSpec((FFN_ROWS, D_MODEL), lambda i: (jnp.minimum(i, n_a - 1), 0))
    b_spec = pl.BlockSpec((FFN_ROWS, D_MODEL), lambda i: (jnp.maximum(i - n_a, 0), 0))
    in_specs = [a_spec, b_spec,
                _const_spec((1, D_MODEL)),
                _const_spec((D_MODEL, 2 * D_FF)),
                _const_spec((D_FF, D_MODEL)),
                _const_spec((1, D_MODEL))]
    out_shapes = [jax.ShapeDtypeStruct((rows_a, D_MODEL), f32), jax.ShapeDtypeStruct((rows_b, D_MODEL), f32)]
    scratch = [pltpu.VMEM((FFN_ROWS, D_MODEL), bf16), pltpu.VMEM((FFN_ROWS, D_FF), bf16)]
    return n_a, n_b, in_specs, [a_spec, b_spec], out_shapes, scratch


def _ffn(xa, xb, g, w_in_b, w_out_b, gfin, *, final_norm):
    n_a, n_b, in_specs, out_specs, out_shapes, scratch = _ffn_specs(xa.shape[0], xb.shape[0])
    return pl.pallas_call(
        functools.partial(_ffn_kernel, n_a=n_a, final_norm=final_norm),
        grid=(n_a + n_b,),
        in_specs=in_specs, out_specs=tuple(out_specs), out_shape=tuple(out_shapes), scratch_shapes=scratch,
        compiler_params=_params(1),
        name="ffn_final" if final_norm else "ffn",
    )(xa, xb, g, w_in_b, w_out_b, gfin)


_IN_PROJ_SPLITS = (POOL_WIDTH, POOL_WIDTH + D_INNER, POOL_WIDTH + D_INNER + CONV_DIM,
                   POOL_WIDTH + D_INNER + CONV_DIM + N_SSD_HEADS)
BF16_SUBLANES = 2 * SUBLANES


def _cast_block(n_rows, n_steps):
    period = 1
    while (n_rows * period) % n_steps or ((n_rows * period) // n_steps) % BF16_SUBLANES:
        period *= 2
        assert period <= n_steps
    return (n_rows * period) // n_steps, period


def _ffn_cast_kernel(xa_ref, xb_ref, g_ref, win_ref, wout_ref, gfin_ref, w_in_f, *rest, n_a, n_plain):
    plain_f = rest[:n_plain]
    oa_ref, ob_ref = rest[n_plain:n_plain + 2]
    wu_o, wz_o, wxbc_o, wdt_o, wgate_o = rest[n_plain + 2:n_plain + 7]
    plain_o = rest[n_plain + 7:2 * n_plain + 7]
    hn_s, act_s = rest[2 * n_plain + 7:]
    _ffn_kernel(xa_ref, xb_ref, g_ref, win_ref, wout_ref, gfin_ref, oa_ref, ob_ref, hn_s, act_s,
                n_a=n_a, final_norm=False)

    @pl.when(pl.program_id(0) < n_a)
    def _():
        s1, s2, s3, s4 = _IN_PROJ_SPLITS
        v = w_in_f[...]
        wu_o[...] = v[:, :s1].astype(bf16)
        wz_o[...] = v[:, s1:s2].astype(bf16)
        wxbc_o[...] = v[:, s2:s3].astype(bf16)
        pad = jnp.zeros((v.shape[0], HEAD_LANES - N_SSD_HEADS), f32)
        wdt_o[...] = jnp.concatenate([v[:, s3:s4], pad], axis=1).astype(bf16)
        wgate_o[...] = v[:, s4:].astype(bf16)
        for src, dst in zip(plain_f, plain_o):
            dst[...] = src[...].astype(bf16)


def _ffn_and_casts(xa, xb, g, w_in_b, w_out_b, gfin, w_in_proj, plain):
    n_a, n_b, in_specs, out_specs, out_shapes, scratch = _ffn_specs(xa.shape[0], xb.shape[0])

    def blocked(arr, width=None):
        blk, period = _cast_block(arr.shape[0], n_a)
        w = arr.shape[1] if width is None else width
        return (pl.BlockSpec((blk, w), lambda i: (jnp.minimum(i, n_a - 1) // period, 0)),
                jax.ShapeDtypeStruct((arr.shape[0], w), bf16))

    piece_widths = (POOL_WIDTH, D_INNER, CONV_DIM, HEAD_LANES, 2 * D_MODEL)
    pieces = [blocked(w_in_proj, width) for width in piece_widths]
    plains = [blocked(p) for p in plain]
    res = pl.pallas_call(
        functools.partial(_ffn_cast_kernel, n_a=n_a, n_plain=len(plain)),
        grid=(n_a + n_b,),
        in_specs=in_specs + [blocked(w_in_proj)[0]] + [s for s, _ in plains],
        out_specs=tuple(out_specs + [s for s, _ in pieces] + [s for s, _ in plains]),
        out_shape=tuple(out_shapes + [o for _, o in pieces] + [o for _, o in plains]),
        scratch_shapes=scratch,
        compiler_params=_params(1),
        name="ffn_casts",
    )(xa, xb, g, w_in_b, w_out_b, gfin, w_in_proj, *plain)
    return res[0], res[1], res[2:7], res[7:]


def _slab(bufs, j):
    return bufs[j // SLABS_PER_BUF], j % SLABS_PER_BUF


def _window_scratch(n_bufs, n_seqs, n_rows):
    return [pltpu.VMEM((SLABS_PER_BUF, n_seqs, n_rows, LANES), f32) for _ in range(n_bufs)]


def _store_rows(bufs, row0, val, T, slab0=0):
    for j in range(val.shape[1] // LANES):
        ref, k = _slab(bufs, slab0 + j)
        S = ref.shape[1]
        blk = val[:, j * LANES:(j + 1) * LANES]
        if S == 1:
            ref[k, 0, row0:row0 + T, :] = blk
        else:
            ref[k, :, row0:row0 + T, :] = blk.reshape(S, T, LANES)


def _window(bufs, j, row0, n):
    ref, k = _slab(bufs, j)
    S = ref.shape[1]
    rows = pl.ds(row0, n) if row0 % SUBLANES == 0 else pl.ds(row0, n, stride=1)
    if S == 1:
        return ref[k, 0, rows, :]
    v = ref[k, :, rows, :]
    return v.reshape(S * n, LANES) if n % SUBLANES == 0 else v


def _pool_branch(hn_s, wu_ref, ubufs, T, pos0, pgw_ref, pscale_ref, pwo_ref):
    S = ubufs[0].shape[1]
    assert T & (T - 1) == 0 and POOL_GROUP == MXU_COLS
    t_idx = jnp.bitwise_and(lax.broadcasted_iota(jnp.int32, (S * T, 1), 0), T - 1)
    n_seen = pos0 + 1 + t_idx
    _store_rows(ubufs, POOL_PAD, _dot(hn_s[...], wu_ref[...]), T)
    mixed = []
    for gi, w in enumerate(POOL_WINDOWS):
        cnt = jnp.minimum(n_seen, w).astype(f32)
        d = []
        for j in range(gi * SLABS_PER_BUF, (gi + 1) * SLABS_PER_BUF):
            cur = _window(ubufs, j, POOL_PAD, T)
            s = cur
            for k in range(1, w):
                s = s + _window(ubufs, j, POOL_PAD - k, T)
            d.append((s / cnt - cur).astype(bf16))
        mixed.append(_dot(jnp.concatenate(d, axis=-1), pgw_ref[gi]))
    y = jnp.concatenate(mixed, axis=-1) * pscale_ref[...]
    return _dot(y.astype(bf16), pwo_ref[...])


def _conv_slab(xbufs, T, convw_ref, convb_ref, j, xs_ref, b_ref, c_ref):
    c0 = j * LANES
    cols = slice(c0, c0 + LANES)
    y = convb_ref[:, cols]
    for k in range(CONV_WIDTH):
        y = y + _window(xbufs, j, CONV_PAD - (CONV_WIDTH - 1) + k, T) * convw_ref[k:k + 1, cols]
    v = _silu(y)
    if c0 < D_INNER:
        xs_ref[:, cols] = v
    elif c0 < D_INNER + BC_DIM:
        b_ref[:, c0 - D_INNER:c0 - D_INNER + LANES] = v
    else:
        c_ref[:, c0 - D_INNER - BC_DIM:c0 - D_INNER - BC_DIM + LANES] = v


def _project_conv(hn_s, wxbc_ref, xbufs, T, convw_ref, convb_ref, xs_ref, b_ref, c_ref):
    for jt in range(N_CONV_BUFS):
        slab0 = jt * SLABS_PER_BUF
        _store_rows(xbufs, CONV_PAD, _dot(hn_s[...], wxbc_ref[:, jt * MXU_COLS:(jt + 1) * MXU_COLS]), T, slab0=slab0)
        for j in range(slab0, slab0 + SLABS_PER_BUF):
            _conv_slab(xbufs, T, convw_ref, convb_ref, j, xs_ref, b_ref, c_ref)


def _gate_merge_out(x, hn_s, y_ref, bp, wz_ref, wgate_ref, snorm_ref, swo_ref, wo_ref, yn_s):
    for g in range(N_SSD_GROUPS):
        cols = slice(g * GROUP_WIDTH, (g + 1) * GROUP_WIDTH)
        z = _dot(hn_s[...], wz_ref[:, cols])
        yg = y_ref[:, cols] * _silu(z)
        yg = yg * lax.rsqrt(jnp.mean(yg * yg, axis=-1, keepdims=True) + EPS)
        yn_s[:, cols] = (yg * snorm_ref[:, cols]).astype(bf16)
    branch_ssd = _dot(yn_s[...], swo_ref[...])
    gate_pool = _sigmoid(_dot(hn_s[...], wgate_ref[:, 0:D_MODEL]))
    gate_ssd = _sigmoid(_dot(hn_s[...], wgate_ref[:, D_MODEL:2 * D_MODEL]))
    merged = (gate_pool * bp + gate_ssd * branch_ssd).astype(bf16)
    return x + _dot(merged, wo_ref[...])


def _ssd_chunk(rows, xs_s, b_s, c_s, dt_s, y_s, ht_s, alog_ref, dskip_ref, e_ref):
    L = CHUNK
    ri = lax.broadcasted_iota(jnp.int32, (L, L), 0)
    ci = lax.broadcasted_iota(jnp.int32, (L, L), 1)
    causal = ri >= ci
    tril = jnp.where(causal, 1.0, 0.0).astype(bf16)
    first_head = lax.broadcasted_iota(jnp.int32, (L, LANES), 1) < SSD_HEAD_DIM

    dt = dt_s[rows, :]
    dA = dt * (-jnp.exp(alog_ref[...]))
    hi, mid, lo = _split3(dA)
    acc = _dot(tril, jnp.concatenate([hi, mid, lo], axis=1))
    a = acc[:, 0:LANES] + acc[:, LANES:2 * LANES] + acc[:, 2 * LANES:3 * LANES]
    aT = a.T[0:N_SSD_HEADS]
    dtT = dt.T[0:N_SSD_HEADS]
    wT = jnp.exp(aT[:, L - 1:L] - aT) * dtT
    srcT = aT - jnp.log(dtT)
    end_decay = jnp.broadcast_to(jnp.exp(a[L - 1:L, :]), (2 * SUBLANES, HEAD_LANES))
    e_hi, e_mid, e_lo = (t.astype(f32) for t in _split3(end_decay))
    sel = lax.broadcasted_iota(jnp.int32, (2 * SUBLANES, HEAD_LANES), 0)
    stacked = jnp.where(sel == 0, e_hi, jnp.where(sel == 1, e_mid, jnp.where(sel == 2, e_lo, 0.0)))
    cdec = jnp.sum(_dot(stacked.astype(bf16), e_ref[...]), axis=0, keepdims=True)

    def head_mats(h, cb, bT):
        acol = jnp.broadcast_to(a[:, h:h + 1], (L, L))
        m = (cb * jnp.exp(jnp.where(causal, acol - srcT[h:h + 1, :], -jnp.inf))).astype(bf16)
        bw = (bT * wT[h:h + 1, :]).astype(bf16)
        return m, bw, jnp.exp(acol)

    pairs_per_group = HEADS_PER_GROUP // 2
    for q in range(N_SSD_HEADS // 2):
        g, qg = divmod(q, pairs_per_group)
        if qg == 0:
            gcols = slice(g * D_STATE, (g + 1) * D_STATE)
            b_f = b_s[rows, gcols]
            c_b = c_s[rows, gcols].astype(bf16)
            cb = lax.dot_general(c_b, b_f.astype(bf16), _NT, preferred_element_type=f32)
            bT = b_f.T
            ch = _dot(c_b, ht_s[:, g * GROUP_WIDTH:(g + 1) * GROUP_WIDTH].astype(bf16))
        cols = slice(q * LANES, (q + 1) * LANES)
        xq = xs_s[rows, cols]
        x2 = jnp.concatenate([jnp.where(first_head, xq, 0.0), jnp.where(first_head, 0.0, xq)], axis=0).astype(bf16)
        m_a, bw_a, ea_a = head_mats(2 * q, cb, bT)
        m_b, bw_b, ea_b = head_mats(2 * q + 1, cb, bT)
        y_off = ch[:, qg * LANES:(qg + 1) * LANES] * jnp.where(first_head, ea_a, ea_b)
        y_s[rows, cols] = _dot(jnp.concatenate([m_a, m_b], axis=1), x2) + y_off + dskip_ref[:, cols] * xq
        st = _dot(jnp.concatenate([bw_a, bw_b], axis=1), x2)
        ht_s[:, cols] = ht_s[:, cols] * cdec[:, cols] + st


def _mixer_prompt_kernel(x_ref, gmix_ref, wu_ref, wz_ref, wxbc_ref, wdt_ref, wgate_ref, pgw_ref, pscale_ref,
                         pwo_ref, convw_ref, convb_ref, dtb_ref, alog_ref, dskip_ref, snorm_ref, swo_ref,
                         wo_ref, e_ref,
                         o_ref, npool_ref, nconv_ref, nssm_ref,
                         hn_s, xs_s, b_s, c_s, dt_s, y_s, ht_s, yn_s, *window_bufs):
    T = PROMPT_TILE
    ti = pl.program_id(1)
    ubufs, xbufs = window_bufs[:N_POOL_BUFS], window_bufs[N_POOL_BUFS:]

    @pl.when(ti == 0)
    def _():
        for ref in ubufs:
            ref[:, :, 0:POOL_PAD, :] = jnp.zeros((SLABS_PER_BUF, 1, POOL_PAD, LANES), f32)
        for ref in xbufs:
            ref[:, :, 0:CONV_PAD, :] = jnp.zeros((SLABS_PER_BUF, 1, CONV_PAD, LANES), f32)
        ht_s[...] = jnp.zeros(ht_s.shape, f32)

    x = x_ref[0]
    hn_s[...] = _rms(x, gmix_ref[...]).astype(bf16)

    bp = _pool_branch(hn_s, wu_ref, ubufs, T, ti * T, pgw_ref, pscale_ref, pwo_ref)

    _project_conv(hn_s, wxbc_ref, xbufs, T, convw_ref, convb_ref, xs_s, b_s, c_s)
    dt_s[...] = _softplus(_dot(hn_s[...], wdt_ref[...]) + dtb_ref[...])

    def chunk(c, carry):
        rows = pl.ds(pl.multiple_of(c * CHUNK, CHUNK), CHUNK)
        _ssd_chunk(rows, xs_s, b_s, c_s, dt_s, y_s, ht_s, alog_ref, dskip_ref, e_ref)
        return carry

    lax.fori_loop(0, T // CHUNK, chunk, 0)

    o_ref[0] = _gate_merge_out(x, hn_s, y_s, bp, wz_ref, wgate_ref, snorm_ref, swo_ref, wo_ref, yn_s)

    @pl.when(ti == pl.num_programs(1) - 1)
    def _():
        for j in range(POOL_WIDTH // LANES):
            npool_ref[0, 0, :, j * LANES:(j + 1) * LANES] = _window(ubufs, j, POOL_PAD + T - POOL_HIST, POOL_HIST)
        for j in range(CONV_DIM // LANES):
            nconv_ref[0, 0, :, j * LANES:(j + 1) * LANES] = _window(
                xbufs, j, CONV_PAD + T - (CONV_WIDTH - 1), CONV_WIDTH - 1)
        for g in range(N_SSD_GROUPS):
            hg = ht_s[:, g * GROUP_WIDTH:(g + 1) * GROUP_WIDTH].T
            nssm_ref[0, 0, g * HEADS_PER_GROUP:(g + 1) * HEADS_PER_GROUP] = hg.reshape(
                HEADS_PER_GROUP, SSD_HEAD_DIM, D_STATE)

    for ref in ubufs:
        ref[:, :, 0:POOL_PAD, :] = ref[:, :, T:T + POOL_PAD, :]
    for ref in xbufs:
        ref[:, :, 0:CONV_PAD, :] = ref[:, :, T:T + CONV_PAD, :]


def _mixer_prompt(x1, w):
    B, S, _ = x1.shape
    T = PROMPT_TILE
    assert S % T == 0 and T % CHUNK == 0 and T >= POOL_PAD
    consts = [w["gmix"], w["wu"], w["wz"], w["wxbc"], w["wdt"], w["wgate"], w["pgw"], w["pscale"], w["pwo"],
              w["convw"], w["convb"], w["dtb"], w["alog"], w["dskip"], w["snorm"], w["swo"], w["wo"], w["expand"]]
    return pl.pallas_call(
        _mixer_prompt_kernel,
        grid=(B, S // T),
        in_specs=[pl.BlockSpec((1, T, D_MODEL), lambda b, t: (b, t, 0))] + [_const_spec(c.shape) for c in consts],
        out_specs=(pl.BlockSpec((1, T, D_MODEL), lambda b, t: (b, t, 0)),
                   pl.BlockSpec((1, 1, POOL_HIST, POOL_WIDTH), lambda b, t: (0, b, 0, 0)),
                   pl.BlockSpec((1, 1, CONV_WIDTH - 1, CONV_DIM), lambda b, t: (0, b, 0, 0)),
                   pl.BlockSpec((1, 1, N_SSD_HEADS, SSD_HEAD_DIM, D_STATE), lambda b, t: (0, b, 0, 0, 0))),
        out_shape=(jax.ShapeDtypeStruct((B, S, D_MODEL), f32),
                   jax.ShapeDtypeStruct((1, B, POOL_HIST, POOL_WIDTH), f32),
                   jax.ShapeDtypeStruct((1, B, CONV_WIDTH - 1, CONV_DIM), f32),
                   jax.ShapeDtypeStruct((1, B, N_SSD_HEADS, SSD_HEAD_DIM, D_STATE), f32)),
        scratch_shapes=[pltpu.VMEM((T, D_MODEL), bf16),
                        pltpu.VMEM((T, D_INNER), f32),
                        pltpu.VMEM((T, BC_DIM), f32),
                        pltpu.VMEM((T, BC_DIM), f32),
                        pltpu.VMEM((T, HEAD_LANES), f32),
                        pltpu.VMEM((T, D_INNER), f32),
                        pltpu.VMEM((D_STATE, D_INNER), f32),
                        pltpu.VMEM((T, D_INNER), bf16)]
                       + _window_scratch(N_POOL_BUFS, 1, POOL_PAD + T)
                       + _window_scratch(N_CONV_BUFS, 1, CONV_PAD + T),
        compiler_params=_params(2),
        name="mixer_prompt",
    )(x1, *consts)


def _sample_pre_kernel(x_ref, ph_ref, ch_ref, gmix_ref, wu_ref, wxbc_ref, wdt_ref, pgw_ref, pscale_ref, pwo_ref,
                       convw_ref, convb_ref, dtb_ref,
                       bp_ref, xs_ref, b_ref, c_ref, dt_ref, npool_ref, nconv_ref,
                       *window_bufs, T):
    ubufs, xbufs = window_bufs[:N_POOL_BUFS], window_bufs[N_POOL_BUFS:]
    hn = _rms(x_ref[...], gmix_ref[...]).astype(bf16)

    for j in range(POOL_WIDTH // LANES):
        ref, k = _slab(ubufs, j)
        ref[k, :, POOL_PAD - POOL_HIST:POOL_PAD, :] = ph_ref[0, :, :, j * LANES:(j + 1) * LANES]
    bp_ref[...] = _pool_branch(hn, wu_ref, ubufs, T, PAST_LEN, pgw_ref, pscale_ref, pwo_ref)
    for j in range(POOL_WIDTH // LANES):
        npool_ref[0, :, :, j * LANES:(j + 1) * LANES] = _window(ubufs, j, POOL_PAD + T - POOL_HIST, POOL_HIST)

    for j in range(CONV_DIM // LANES):
        ref, k = _slab(xbufs, j)
        ref[k, :, CONV_PAD - (CONV_WIDTH - 1):CONV_PAD, :] = ch_ref[0, :, :, j * LANES:(j + 1) * LANES]
    _project_conv(hn, wxbc_ref, xbufs, T, convw_ref, convb_ref, xs_ref, b_ref, c_ref)
    for j in range(CONV_DIM // LANES):
        nconv_ref[0, :, :, j * LANES:(j + 1) * LANES] = _window(
            xbufs, j, CONV_PAD + T - (CONV_WIDTH - 1), CONV_WIDTH - 1)
    dt_ref[...] = _softplus(_dot(hn, wdt_ref[...]) + dtb_ref[...])


def _sample_pre(x1, pool_hist, conv_hist, w, T):
    rows = x1.shape[0]
    nseq = rows // T
    S = SAMPLE_SEQS
    R = S * T
    assert nseq % S == 0 and T % SUBLANES == 0
    consts = [w["gmix"], w["wu"], w["wxbc"], w["wdt"], w["pgw"], w["pscale"], w["pwo"], w["convw"], w["convb"],
              w["dtb"]]
    row_spec = lambda width: pl.BlockSpec((R, width), lambda i: (i, 0))
    return pl.pallas_call(
        functools.partial(_sample_pre_kernel, T=T),
        grid=(nseq // S,),
        in_specs=[row_spec(D_MODEL),
                  pl.BlockSpec((1, S, POOL_HIST, POOL_WIDTH), lambda i: (0, i, 0, 0)),
                  pl.BlockSpec((1, S, CONV_WIDTH - 1, CONV_DIM), lambda i: (0, i, 0, 0))]
                 + [_const_spec(c.shape) for c in consts],
        out_specs=(row_spec(D_MODEL), row_spec(D_INNER), row_spec(BC_DIM), row_spec(BC_DIM), row_spec(HEAD_LANES),
                   pl.BlockSpec((1, S, POOL_HIST, POOL_WIDTH), lambda i: (0, i, 0, 0)),
                   pl.BlockSpec((1, S, CONV_WIDTH - 1, CONV_DIM), lambda i: (0, i, 0, 0))),
        out_shape=(jax.ShapeDtypeStruct((rows, D_MODEL), f32),
                   jax.ShapeDtypeStruct((rows, D_INNER), f32),
                   jax.ShapeDtypeStruct((rows, BC_DIM), f32),
                   jax.ShapeDtypeStruct((rows, BC_DIM), f32),
                   jax.ShapeDtypeStruct((rows, HEAD_LANES), f32),
                   jax.ShapeDtypeStruct((1, nseq, POOL_HIST, POOL_WIDTH), f32),
                   jax.ShapeDtypeStruct((1, nseq, CONV_WIDTH - 1, CONV_DIM), f32)),
        scratch_shapes=_window_scratch(N_POOL_BUFS, S, POOL_PAD + T) + _window_scratch(N_CONV_BUFS, S, CONV_PAD + T),
        compiler_params=_params(1),
        name="sample_pre",
    )(x1, pool_hist, conv_hist, *consts)


def _ssd_decode_kernel(xs_ref, b_ref, c_ref, dt_ref, h0_ref, alog_ref, dskip_ref, e_ref, y_ref, hn_ref, *, T):
    assert T == SUBLANES
    lane_group = lax.broadcasted_iota(jnp.int32, (T, HEAD_LANES), 1) // HEADS_PER_GROUP
    neg_a = -jnp.exp(alog_ref[...])

    def shift(v, d):
        if d == 0:
            return v
        r = lax.broadcasted_iota(jnp.int32, v.shape, 0)
        return jnp.where(r >= d, pltpu.roll(v, d, axis=0), 0.0)

    def per_seq(s, carry):
        dt = dt_ref[s]
        a = dt * neg_a
        d = 1
        while d < T:
            a = a + shift(a, d)
            d *= 2
        a_end = a[T - 1:T, :]
        x = xs_ref[s]
        bm = b_ref[s]
        cm = c_ref[s]
        terms = []
        for d in range(T):
            cbv = jnp.zeros((T, HEAD_LANES), f32)
            for g in range(N_SSD_GROUPS):
                gc = slice(g * D_STATE, (g + 1) * D_STATE)
                cb = jnp.sum(cm[:, gc] * shift(bm[:, gc], d), axis=-1, keepdims=True)
                cbv = jnp.where(lane_group == g, cb, cbv)
            terms.append(cbv * jnp.exp(a - shift(a, d)) * shift(dt, d))
        terms.append(jnp.exp(a))
        terms.append(jnp.exp(a_end - a) * dt)
        v = jnp.concatenate(terms, axis=0)
        v_hi = v.astype(bf16)
        v_lo = (v - v_hi.astype(f32)).astype(bf16)
        ex = _dot(jnp.concatenate([v_hi, v_lo], axis=0), e_ref[...])
        n = (T + 2) * T
        ex = ex[0:n] + ex[n:2 * n]

        y = dskip_ref[...] * x
        for d in range(T):
            y = y + ex[d * T:(d + 1) * T] * shift(x, d)
        ea = ex[T * T:(T + 1) * T]
        xw = (x * ex[(T + 1) * T:(T + 2) * T]).astype(bf16)
        cdec = jnp.exp(a_end)
        y_groups = []
        for g in range(N_SSD_GROUPS):
            gc = slice(g * D_STATE, (g + 1) * D_STATE)
            cols = slice(g * GROUP_WIDTH, (g + 1) * GROUP_WIDTH)
            heads = slice(g * HEADS_PER_GROUP, (g + 1) * HEADS_PER_GROUP)
            h0g = h0_ref[0, s, heads].reshape(GROUP_WIDTH, D_STATE)
            ch = lax.dot_general(cm[:, gc].astype(bf16), h0g.astype(bf16), _NT, preferred_element_type=f32)
            y_groups.append(y[:, cols] + ea[:, cols] * ch)
            st = lax.dot_general(xw[:, cols], bm[:, gc].astype(bf16), _TN, preferred_element_type=f32)
            for r in range(HEADS_PER_GROUP):
                h = g * HEADS_PER_GROUP + r
                hn_ref[0, s, h] = (h0_ref[0, s, h] * cdec[:, h:h + 1]
                                   + st[r * SSD_HEAD_DIM:(r + 1) * SSD_HEAD_DIM, :])
        y_ref[s] = jnp.concatenate(y_groups, axis=-1)
        return carry

    lax.fori_loop(0, DECODE_SEQS, per_seq, 0, unroll=DECODE_UNROLL)


def _ssd_decode(xs, bm, cm, dt, h0, w, T):
    nseq = xs.shape[0] // T
    Q = DECODE_SEQS
    assert nseq % Q == 0
    seq_spec = lambda width: pl.BlockSpec((Q, T, width), lambda i: (i, 0, 0))
    state_spec = pl.BlockSpec((1, Q, N_SSD_HEADS, SSD_HEAD_DIM, D_STATE), lambda i: (0, i, 0, 0, 0))
    consts = [w["alog"], w["dskip"], w["expand"]]
    return pl.pallas_call(
        functools.partial(_ssd_decode_kernel, T=T),
        grid=(nseq // Q,),
        in_specs=[seq_spec(D_INNER), seq_spec(BC_DIM), seq_spec(BC_DIM), seq_spec(HEAD_LANES), state_spec]
                 + [_const_spec(c.shape) for c in consts],
        out_specs=(seq_spec(D_INNER), state_spec),
        out_shape=(jax.ShapeDtypeStruct((nseq, T, D_INNER), f32), jax.ShapeDtypeStruct(h0.shape, f32)),
        compiler_params=_params(1),
        name="ssd_decode",
    )(xs.reshape(nseq, T, D_INNER), bm.reshape(nseq, T, BC_DIM), cm.reshape(nseq, T, BC_DIM),
      dt.reshape(nseq, T, HEAD_LANES), h0, *consts)


def _sample_post_kernel(x_ref, y_ref, bp_ref, gmix_ref, wz_ref, wgate_ref, snorm_ref, swo_ref, wo_ref,
                        o_ref, hn_s, yn_s):
    x = x_ref[...]
    hn_s[...] = _rms(x, gmix_ref[...]).astype(bf16)
    o_ref[...] = _gate_merge_out(x, hn_s, y_ref, bp_ref[...], wz_ref, wgate_ref, snorm_ref, swo_ref, wo_ref, yn_s)


def _sample_post(x1, y, bp, w, T):
    rows = x1.shape[0]
    R = SAMPLE_POST_SEQS * T
    assert rows % R == 0
    consts = [w["gmix"], w["wz"], w["wgate"], w["snorm"], w["swo"], w["wo"]]
    row_spec = lambda width: pl.BlockSpec((R, width), lambda i: (i, 0))
    return pl.pallas_call(
        _sample_post_kernel,
        grid=(rows // R,),
        in_specs=[row_spec(D_MODEL), row_spec(D_INNER), row_spec(D_MODEL)] + [_const_spec(c.shape) for c in consts],
        out_specs=row_spec(D_MODEL),
        out_shape=jax.ShapeDtypeStruct((rows, D_MODEL), f32),
        scratch_shapes=[pltpu.VMEM((R, D_MODEL), bf16), pltpu.VMEM((R, D_INNER), bf16)],
        compiler_params=_params(1),
        name="sample_post",
    )(x1, y, bp, *consts)


def _small_params(l, norm_mix, pool_scale, conv_w, conv_b, dt_bias, a_log, d_skip, ssd_norm):
    pad_heads = lambda v: jnp.pad(v, ((0, 0), (0, HEAD_LANES - N_SSD_HEADS)))
    head_of_lane = jnp.arange(D_INNER, dtype=jnp.int32) // SSD_HEAD_DIM
    expand = (jnp.arange(HEAD_LANES, dtype=jnp.int32)[:, None] == head_of_lane[None, :]).astype(bf16)
    return dict(
        gmix=norm_mix[l][None, :], pscale=pool_scale[l][None, :], convw=conv_w[l], convb=conv_b[l][None, :],
        dtb=pad_heads(dt_bias[l][None, :]), alog=pad_heads(a_log[l][None, :]),
        dskip=jnp.repeat(d_skip[l], SSD_HEAD_DIM)[None, :], snorm=ssd_norm[l][None, :], expand=expand)


def kernel(x_prompt, x_sample, state_pool, state_conv, state_ssm, norm_ffn1, ffn1_w_in, ffn1_w_out, norm_mix, w_in,
           pool_w_group, pool_scale, pool_w_out, conv_w, conv_b, dt_bias, a_log, d_skip, ssd_norm, ssd_w_out, w_o,
           norm_ffn2, ffn2_w_in, ffn2_w_out, norm_final):
    depth = w_in.shape[0]
    B, S, _ = x_prompt.shape
    DB, T, _ = x_sample.shape
    gfin = norm_final[None, :]
    xp = x_prompt.reshape(B * S, D_MODEL)
    xs = x_sample.reshape(DB * T, D_MODEL)
    outs = [[] for _ in range(6)]
    for l in range(depth):
        last = l == depth - 1
        f1_in, f1_out = ffn1_w_in[l].astype(bf16), ffn1_w_out[l].astype(bf16)
        g1, g2 = norm_ffn1[l][None, :], norm_ffn2[l][None, :]
        w = _small_params(l, norm_mix, pool_scale, conv_w, conv_b, dt_bias, a_log, d_skip, ssd_norm)
        n_pg = pool_w_group.shape[1]
        xp, xs, in_proj, casts = _ffn_and_casts(
            xp, xs, g1, f1_in, f1_out, gfin, w_in[l],
            [pool_w_group[l].reshape(n_pg * POOL_GROUP, POOL_GROUP), pool_w_out[l], ssd_w_out[l], w_o[l],
             ffn2_w_in[l], ffn2_w_out[l]])
        w.update(zip(("wu", "wz", "wxbc", "wdt", "wgate"), in_proj))
        w.update(pgw=casts[0].reshape(n_pg, POOL_GROUP, POOL_GROUP), pwo=casts[1], swo=casts[2], wo=casts[3])
        f2_in, f2_out = casts[4], casts[5]
        xp3, npool, nconv, nssm = _mixer_prompt(xp.reshape(B, S, D_MODEL), w)
        bp, cxs, cb, cc, cdt, spool, sconv = _sample_pre(xs, state_pool[l:l + 1], state_conv[l:l + 1], w, T)
        y, sssm = _ssd_decode(cxs, cb, cc, cdt, state_ssm[l:l + 1], w, T)
        xs = _sample_post(xs, y.reshape(DB * T, D_INNER), bp, w, T)
        xp, xs = _ffn(xp3.reshape(B * S, D_MODEL), xs, g2, f2_in, f2_out, gfin, final_norm=last)
        for acc, v in zip(outs, (npool, nconv, nssm, spool, sconv, sssm)):
            acc.append(v)
    stack = lambda vs: vs[0] if len(vs) == 1 else jnp.concatenate(vs, axis=0)
    return (xp.reshape(B, S, D_MODEL), xs.reshape(DB, T, D_MODEL),
            stack(outs[0]), stack(outs[1]), stack(outs[2]), stack(outs[3]), stack(outs[4]), stack(outs[5]))
```

```python
import functools

import jax
import jax.numpy as jnp
from jax import lax
from jax.experimental import pallas as pl
from jax.experimental.pallas import tpu as pltpu

f32 = jnp.float32
bf16 = jnp.bfloat16

D_MODEL = 1024
D_FF = 2816
POOL_WINDOWS = (2, 4, 8, 16)
POOL_WIDTH = D_MODEL
POOL_GROUP = POOL_WIDTH // len(POOL_WINDOWS)
POOL_HIST = max(POOL_WINDOWS) - 1
D_INNER = 2 * D_MODEL
SSD_HEAD_DIM = 64
N_SSD_HEADS = D_INNER // SSD_HEAD_DIM
N_SSD_GROUPS = 4
HEADS_PER_GROUP = N_SSD_HEADS // N_SSD_GROUPS
GROUP_WIDTH = D_INNER // N_SSD_GROUPS
D_STATE = 128
CONV_WIDTH = 4
BC_DIM = N_SSD_GROUPS * D_STATE
CONV_DIM = D_INNER + 2 * BC_DIM
CHUNK = 128
PAST_LEN = 16384
EPS = 1e-6

LANES = 128
SUBLANES = 8
MXU_COLS = 256
VMEM_LIMIT_BYTES = 56 * 1024 * 1024

FFN_ROWS = 512
FFN_CHUNK = 256
PROMPT_TILE = 256
SAMPLE_SEQS = 16
SAMPLE_POST_SEQS = 32
DECODE_SEQS = 8
DECODE_UNROLL = 4
POOL_PAD = 16
CONV_PAD = 8
HEAD_LANES = LANES
SLABS_PER_BUF = MXU_COLS // LANES
N_POOL_BUFS = POOL_WIDTH // MXU_COLS
N_CONV_BUFS = CONV_DIM // MXU_COLS

_NT = (((1,), (1,)), ((), ()))
_TN = (((0,), (0,)), ((), ()))


def _rms(x, g):
    return x * lax.rsqrt(jnp.mean(x * x, axis=-1, keepdims=True) + EPS) * g


def _sigmoid(v):
    return 0.5 * jnp.tanh(0.5 * v) + 0.5


def _silu(v):
    h = 0.5 * v
    return h * jnp.tanh(h) + h


def _softplus(v):
    return jnp.maximum(v, 0.0) + jnp.log1p(jnp.exp(-jnp.abs(v)))


def _dot(a, b):
    return jnp.dot(a, b, preferred_element_type=f32)


def _split3(v):
    hi = v.astype(bf16)
    r1 = v - hi.astype(f32)
    mid = r1.astype(bf16)
    lo = (r1 - mid.astype(f32)).astype(bf16)
    return hi, mid, lo


def _const_spec(shape):
    nd = len(shape)
    return pl.BlockSpec(shape, lambda *_: (0,) * nd, pipeline_mode=pl.Buffered(1))


def _params(n_grid):
    return pltpu.CompilerParams(dimension_semantics=("arbitrary",) * n_grid,
                                vmem_limit_bytes=VMEM_LIMIT_BYTES)


def _ffn_kernel(xa_ref, xb_ref, g_ref, win_ref, wout_ref, gfin_ref, oa_ref, ob_ref, hn_s, act_s, *, n_a, final_norm):
    def tile(x_ref, o_ref):
        hn_s[...] = _rms(x_ref[...], g_ref[...]).astype(bf16)
        for c in range(D_FF // FFN_CHUNK):
            lo = c * FFN_CHUNK
            gate = _dot(hn_s[...], win_ref[:, lo:lo + FFN_CHUNK])
            up = _dot(hn_s[...], win_ref[:, D_FF + lo:D_FF + lo + FFN_CHUNK])
            act_s[:, lo:lo + FFN_CHUNK] = (_silu(gate) * up).astype(bf16)
        out = x_ref[...] + 0.5 * _dot(act_s[...], wout_ref[...])
        if final_norm:
            out = _rms(out, gfin_ref[...])
        o_ref[...] = out

    on_a = pl.program_id(0) < n_a
    pl.when(on_a)(lambda: tile(xa_ref, oa_ref))
    pl.when(jnp.logical_not(on_a))(lambda: tile(xb_ref, ob_ref))


def _ffn_specs(rows_a, rows_b):
    assert rows_a % FFN_ROWS == 0 and rows_b % FFN_ROWS == 0
    n_a, n_b = rows_a // FFN_ROWS, rows_b // FFN_ROWS
    a_spec = pl.BlockSpec((FFN_ROWS, D_MODEL), lambda i: (jnp.minimum(i, n_a - 1), 0))
    b_spec = pl.BlockSpec((FFN_ROWS, D_MODEL), lambda i: (jnp.maximum(i - n_a, 0), 0))
    in_specs = [a_spec, b_spec,
                _const_spec((1, D_MODEL)),
                _const_spec((D_MODEL, 2 * D_FF)),
                _const_spec((D_FF, D_MODEL)),
                _const_spec((1, D_MODEL))]
    out_shapes = [jax.ShapeDtypeStruct((rows_a, D_MODEL), f32), jax.ShapeDtypeStruct((rows_b, D_MODEL), f32)]
    scratch = [pltpu.VMEM((FFN_ROWS, D_MODEL), bf16), pltpu.VMEM((FFN_ROWS, D_FF), bf16)]
    return n_a, n_b, in_specs, [a_spec, b_spec], out_shapes, scratch


def _ffn(xa, xb, g, w_in_b, w_out_b, gfin, *, final_norm):
    n_a, n_b, in_specs, out_specs, out_shapes, scratch = _ffn_specs(xa.shape[0], xb.shape[0])
    return pl.pallas_call(
        functools.partial(_ffn_kernel, n_a=n_a, final_norm=final_norm),
        grid=(n_a + n_b,),
        in_specs=in_specs, out_specs=tuple(out_specs), out_shape=tuple(out_shapes), scratch_shapes=scratch,
        compiler_params=_params(1),
        name="ffn_final" if final_norm else "ffn",
    )(xa, xb, g, w_in_b, w_out_b, gfin)


_IN_PROJ_SPLITS = (POOL_WIDTH, POOL_WIDTH + D_INNER, POOL_WIDTH + D_INNER + CONV_DIM,
                   POOL_WIDTH + D_INNER + CONV_DIM + N_SSD_HEADS)
BF16_SUBLANES = 2 * SUBLANES


def _cast_block(n_rows, n_steps):
    period = 1
    while (n_rows * period) % n_steps or ((n_rows * period) // n_steps) % BF16_SUBLANES:
        period *= 2
        assert period <= n_steps
    return (n_rows * period) // n_steps, period


_PIECE_TILES = (POOL_WIDTH // MXU_COLS, D_INNER // MXU_COLS, CONV_DIM // MXU_COLS, 2 * D_MODEL // MXU_COLS)
_GATE_ROW0 = _IN_PROJ_SPLITS[3]


def _store_when(cond, ref, val):
    @pl.when(cond)
    def _():
        ref[...] = val


def _ffn_cast_kernel(xa_ref, xb_ref, g_ref, win_ref, wout_ref, gfin_ref, wt_ref, wdt_f, wgt_ref, *rest,
                     n_a, n_plain):
    plain_f = rest[:n_plain]
    oa_ref, ob_ref = rest[n_plain:n_plain + 2]
    wu_o, wz_o, wxbc_o, wdt_o, wgate_o = rest[n_plain + 2:n_plain + 7]
    plain_o = rest[n_plain + 7:2 * n_plain + 7]
    hn_s, act_s = rest[2 * n_plain + 7:]
    _ffn_kernel(xa_ref, xb_ref, g_ref, win_ref, wout_ref, gfin_ref, oa_ref, ob_ref, hn_s, act_s,
                n_a=n_a, final_norm=False)
    i = pl.program_id(0)
    n_u, n_z, n_x, n_g = _PIECE_TILES
    front = n_u + n_z + n_x

    @pl.when(i < front)
    def _():
        t = wt_ref[...].T.astype(bf16)
        _store_when(i < n_u, wu_o, t)
        _store_when(jnp.logical_and(i >= n_u, i < n_u + n_z), wz_o, t)
        _store_when(i >= n_u + n_z, wxbc_o, t)

    @pl.when(jnp.logical_and(i >= front, i < front + n_g))
    def _():
        wgate_o[...] = wgt_ref[...].T.astype(bf16)

    @pl.when(i == 0)
    def _():
        t = wdt_f[...].T
        lane = lax.broadcasted_iota(jnp.int32, t.shape, 1)
        wdt_o[...] = jnp.where(lane < N_SSD_HEADS, t, 0.0).astype(bf16)

    @pl.when(i < n_a)
    def _():
        for src, dst in zip(plain_f, plain_o):
            dst[...] = src[...].astype(bf16)


def _ffn_and_casts(xa, xb, g, w_in_b, w_out_b, gfin, w_in_t, plain):
    n_a, n_b, in_specs, out_specs, out_shapes, scratch = _ffn_specs(xa.shape[0], xb.shape[0])
    n_u, n_z, n_x, n_g = _PIECE_TILES
    front = n_u + n_z + n_x
    assert n_a >= front + n_g and _IN_PROJ_SPLITS[2] == front * MXU_COLS and _IN_PROJ_SPLITS[2] % HEAD_LANES == 0

    def blocked(arr):
        blk, period = _cast_block(arr.shape[0], n_a)
        return (pl.BlockSpec((blk, arr.shape[1]), lambda i: (jnp.minimum(i, n_a - 1) // period, 0)),
                jax.ShapeDtypeStruct(arr.shape, bf16))

    def piece(first_step, n_tiles):
        return (pl.BlockSpec((D_MODEL, MXU_COLS), lambda i: (0, jnp.clip(i - first_step, 0, n_tiles - 1))),
                jax.ShapeDtypeStruct((D_MODEL, n_tiles * MXU_COLS), bf16))

    pieces = [piece(0, n_u), piece(n_u, n_z), piece(n_u + n_z, n_x),
              (pl.BlockSpec((D_MODEL, HEAD_LANES), lambda i: (0, 0)), jax.ShapeDtypeStruct((D_MODEL, HEAD_LANES), bf16)),
              piece(front, n_g)]
    in_proj_specs = [pl.BlockSpec((MXU_COLS, D_MODEL), lambda i: (jnp.minimum(i, front - 1), 0)),
                     pl.BlockSpec((HEAD_LANES, D_MODEL), lambda i: (_IN_PROJ_SPLITS[2] // HEAD_LANES, 0)),
                     pl.BlockSpec((MXU_COLS, D_MODEL), lambda i: (jnp.clip(i - front, 0, n_g - 1), 0))]
    plains = [blocked(p) for p in plain]
    res = pl.pallas_call(
        functools.partial(_ffn_cast_kernel, n_a=n_a, n_plain=len(plain)),
        grid=(n_a + n_b,),
        in_specs=in_specs + in_proj_specs + [s for s, _ in plains],
        out_specs=tuple(out_specs + [s for s, _ in pieces] + [s for s, _ in plains]),
        out_shape=tuple(out_shapes + [o for _, o in pieces] + [o for _, o in plains]),
        scratch_shapes=scratch,
        compiler_params=_params(1),
        name="ffn_casts",
    )(xa, xb, g, w_in_b, w_out_b, gfin, w_in_t, w_in_t, w_in_t[_GATE_ROW0:], *plain)
    return res[0], res[1], res[2:7], res[7:]


def _slab(bufs, j):
    return bufs[j // SLABS_PER_BUF], j % SLABS_PER_BUF


def _window_scratch(n_bufs, n_seqs, n_rows):
    return [pltpu.VMEM((SLABS_PER_BUF, n_seqs, n_rows, LANES), f32) for _ in range(n_bufs)]


def _store_rows(bufs, row0, val, T, slab0=0):
    for j in range(val.shape[1] // LANES):
        ref, k = _slab(bufs, slab0 + j)
        S = ref.shape[1]
        blk = val[:, j * LANES:(j + 1) * LANES]
        if S == 1:
            ref[k, 0, row0:row0 + T, :] = blk
        else:
            ref[k, :, row0:row0 + T, :] = blk.reshape(S, T, LANES)


def _window(bufs, j, row0, n):
    ref, k = _slab(bufs, j)
    S = ref.shape[1]
    rows = pl.ds(row0, n) if row0 % SUBLANES == 0 else pl.ds(row0, n, stride=1)
    if S == 1:
        return ref[k, 0, rows, :]
    v = ref[k, :, rows, :]
    return v.reshape(S * n, LANES) if n % SUBLANES == 0 else v


def _pool_branch(hn_s, wu_ref, ubufs, T, pos0, pgw_ref, pscale_ref, pwo_ref):
    S = ubufs[0].shape[1]
    assert T & (T - 1) == 0 and POOL_GROUP == MXU_COLS
    t_idx = jnp.bitwise_and(lax.broadcasted_iota(jnp.int32, (S * T, 1), 0), T - 1)
    n_seen = pos0 + 1 + t_idx
    _store_rows(ubufs, POOL_PAD, _dot(hn_s[...], wu_ref[...]), T)
    mixed = []
    for gi, w in enumerate(POOL_WINDOWS):
        cnt = jnp.minimum(n_seen, w).astype(f32)
        d = []
        for j in range(gi * SLABS_PER_BUF, (gi + 1) * SLABS_PER_BUF):
            cur = _window(ubufs, j, POOL_PAD, T)
            s = cur
            for k in range(1, w):
                s = s + _window(ubufs, j, POOL_PAD - k, T)
            d.append((s / cnt - cur).astype(bf16))
        mixed.append(_dot(jnp.concatenate(d, axis=-1), pgw_ref[gi]))
    y = jnp.concatenate(mixed, axis=-1) * pscale_ref[...]
    return _dot(y.astype(bf16), pwo_ref[...])


def _conv_slab(xbufs, T, convw_ref, convb_ref, j, xs_ref, b_ref, c_ref):
    c0 = j * LANES
    cols = slice(c0, c0 + LANES)
    y = convb_ref[:, cols]
    for k in range(CONV_WIDTH):
        y = y + _window(xbufs, j, CONV_PAD - (CONV_WIDTH - 1) + k, T) * convw_ref[k:k + 1, cols]
    v = _silu(y)
    if c0 < D_INNER:
        xs_ref[:, cols] = v
    elif c0 < D_INNER + BC_DIM:
        b_ref[:, c0 - D_INNER:c0 - D_INNER + LANES] = v
    else:
        c_ref[:, c0 - D_INNER - BC_DIM:c0 - D_INNER - BC_DIM + LANES] = v


def _project_conv(hn_s, wxbc_ref, xbufs, T, convw_ref, convb_ref, xs_ref, b_ref, c_ref):
    for jt in range(N_CONV_BUFS):
        slab0 = jt * SLABS_PER_BUF
        _store_rows(xbufs, CONV_PAD, _dot(hn_s[...], wxbc_ref[:, jt * MXU_COLS:(jt + 1) * MXU_COLS]), T, slab0=slab0)
        for j in range(slab0, slab0 + SLABS_PER_BUF):
            _conv_slab(xbufs, T, convw_ref, convb_ref, j, xs_ref, b_ref, c_ref)


def _gate_merge_out(x, hn_s, y_ref, bp, wz_ref, wgate_ref, snorm_ref, swo_ref, wo_ref, yn_s):
    for g in range(N_SSD_GROUPS):
        cols = slice(g * GROUP_WIDTH, (g + 1) * GROUP_WIDTH)
        z = _dot(hn_s[...], wz_ref[:, cols])
        yg = y_ref[:, cols] * _silu(z)
        yg = yg * lax.rsqrt(jnp.mean(yg * yg, axis=-1, keepdims=True) + EPS)
        yn_s[:, cols] = (yg * snorm_ref[:, cols]).astype(bf16)
    branch_ssd = _dot(yn_s[...], swo_ref[...])
    gate_pool = _sigmoid(_dot(hn_s[...], wgate_ref[:, 0:D_MODEL]))
    gate_ssd = _sigmoid(_dot(hn_s[...], wgate_ref[:, D_MODEL:2 * D_MODEL]))
    merged = (gate_pool * bp + gate_ssd * branch_ssd).astype(bf16)
    return x + _dot(merged, wo_ref[...])


def _ssd_chunk(rows, xs_s, b_s, c_s, dt_s, y_s, ht_s, alog_ref, dskip_ref, e_ref):
    L = CHUNK
    ri = lax.broadcasted_iota(jnp.int32, (L, L), 0)
    ci = lax.broadcasted_iota(jnp.int32, (L, L), 1)
    causal = ri >= ci
    tril = jnp.where(causal, 1.0, 0.0).astype(bf16)
    first_head = lax.broadcasted_iota(jnp.int32, (L, LANES), 1) < SSD_HEAD_DIM

    dt = dt_s[rows, :]
    dA = dt * (-jnp.exp(alog_ref[...]))
    hi, mid, lo = _split3(dA)
    acc = _dot(tril, jnp.concatenate([hi, mid, lo], axis=1))
    a = acc[:, 0:LANES] + acc[:, LANES:2 * LANES] + acc[:, 2 * LANES:3 * LANES]
    aT = a.T[0:N_SSD_HEADS]
    dtT = dt.T[0:N_SSD_HEADS]
    wT = jnp.exp(aT[:, L - 1:L] - aT) * dtT
    srcT = aT - jnp.log(dtT)
    end_decay = jnp.broadcast_to(jnp.exp(a[L - 1:L, :]), (2 * SUBLANES, HEAD_LANES))
    e_hi, e_mid, e_lo = (t.astype(f32) for t in _split3(end_decay))
    sel = lax.broadcasted_iota(jnp.int32, (2 * SUBLANES, HEAD_LANES), 0)
    stacked = jnp.where(sel == 0, e_hi, jnp.where(sel == 1, e_mid, jnp.where(sel == 2, e_lo, 0.0)))
    cdec = jnp.sum(_dot(stacked.astype(bf16), e_ref[...]), axis=0, keepdims=True)

    def head_mats(h, cb, bT):
        acol = jnp.broadcast_to(a[:, h:h + 1], (L, L))
        m = (cb * jnp.exp(jnp.where(causal, acol - srcT[h:h + 1, :], -jnp.inf))).astype(bf16)
        bw = (bT * wT[h:h + 1, :]).astype(bf16)
        return m, bw, jnp.exp(acol)

    pairs_per_group = HEADS_PER_GROUP // 2
    for q in range(N_SSD_HEADS // 2):
        g, qg = divmod(q, pairs_per_group)
        if qg == 0:
            gcols = slice(g * D_STATE, (g + 1) * D_STATE)
            b_f = b_s[rows, gcols]
            c_b = c_s[rows, gcols].astype(bf16)
            cb = lax.dot_general(c_b, b_f.astype(bf16), _NT, preferred_element_type=f32)
            bT = b_f.T
            ch = _dot(c_b, ht_s[:, g * GROUP_WIDTH:(g + 1) * GROUP_WIDTH].astype(bf16))
        cols = slice(q * LANES, (q + 1) * LANES)
        xq = xs_s[rows, cols]
        x2 = jnp.concatenate([jnp.where(first_head, xq, 0.0), jnp.where(first_head, 0.0, xq)], axis=0).astype(bf16)
        m_a, bw_a, ea_a = head_mats(2 * q, cb, bT)
        m_b, bw_b, ea_b = head_mats(2 * q + 1, cb, bT)
        y_off = ch[:, qg * LANES:(qg + 1) * LANES] * jnp.where(first_head, ea_a, ea_b)
        y_s[rows, cols] = _dot(jnp.concatenate([m_a, m_b], axis=1), x2) + y_off + dskip_ref[:, cols] * xq
        st = _dot(jnp.concatenate([bw_a, bw_b], axis=1), x2)
        ht_s[:, cols] = ht_s[:, cols] * cdec[:, cols] + st


def _mixer_prompt_kernel(x_ref, gmix_ref, wu_ref, wz_ref, wxbc_ref, wdt_ref, wgate_ref, pgw_ref, pscale_ref,
                         pwo_ref, convw_ref, convb_ref, dtb_ref, alog_ref, dskip_ref, snorm_ref, swo_ref,
                         wo_ref, e_ref,
                         o_ref, npool_ref, nconv_ref, nssm_ref,
                         hn_s, xs_s, b_s, c_s, dt_s, y_s, ht_s, yn_s, *window_bufs):
    T = PROMPT_TILE
    ti = pl.program_id(1)
    ubufs, xbufs = window_bufs[:N_POOL_BUFS], window_bufs[N_POOL_BUFS:]

    @pl.when(ti == 0)
    def _():
        for ref in ubufs:
            ref[:, :, 0:POOL_PAD, :] = jnp.zeros((SLABS_PER_BUF, 1, POOL_PAD, LANES), f32)
        for ref in xbufs:
            ref[:, :, 0:CONV_PAD, :] = jnp.zeros((SLABS_PER_BUF, 1, CONV_PAD, LANES), f32)
        ht_s[...] = jnp.zeros(ht_s.shape, f32)

    x = x_ref[0]
    hn_s[...] = _rms(x, gmix_ref[...]).astype(bf16)

    bp = _pool_branch(hn_s, wu_ref, ubufs, T, ti * T, pgw_ref, pscale_ref, pwo_ref)

    _project_conv(hn_s, wxbc_ref, xbufs, T, convw_ref, convb_ref, xs_s, b_s, c_s)
    dt_s[...] = _softplus(_dot(hn_s[...], wdt_ref[...]) + dtb_ref[...])

    def chunk(c, carry):
        rows = pl.ds(pl.multiple_of(c * CHUNK, CHUNK), CHUNK)
        _ssd_chunk(rows, xs_s, b_s, c_s, dt_s, y_s, ht_s, alog_ref, dskip_ref, e_ref)
        return carry

    lax.fori_loop(0, T // CHUNK, chunk, 0)

    o_ref[0] = _gate_merge_out(x, hn_s, y_s, bp, wz_ref, wgate_ref, snorm_ref, swo_ref, wo_ref, yn_s)

    @pl.when(ti == pl.num_programs(1) - 1)
    def _():
        for j in range(POOL_WIDTH // LANES):
            npool_ref[0, 0, :, j * LANES:(j + 1) * LANES] = _window(ubufs, j, POOL_PAD + T - POOL_HIST, POOL_HIST)
        for j in range(CONV_DIM // LANES):
            nconv_ref[0, 0, :, j * LANES:(j + 1) * LANES] = _window(
                xbufs, j, CONV_PAD + T - (CONV_WIDTH - 1), CONV_WIDTH - 1)
        for g in range(N_SSD_GROUPS):
            hg = ht_s[:, g * GROUP_WIDTH:(g + 1) * GROUP_WIDTH].T
            nssm_ref[0, 0, g * HEADS_PER_GROUP:(g + 1) * HEADS_PER_GROUP] = hg.reshape(
                HEADS_PER_GROUP, SSD_HEAD_DIM, D_STATE)

    for ref in ubufs:
        ref[:, :, 0:POOL_PAD, :] = ref[:, :, T:T + POOL_PAD, :]
    for ref in xbufs:
        ref[:, :, 0:CONV_PAD, :] = ref[:, :, T:T + CONV_PAD, :]


def _mixer_prompt(x1, w):
    B, S, _ = x1.shape
    T = PROMPT_TILE
    assert S % T == 0 and T % CHUNK == 0 and T >= POOL_PAD
    consts = [w["gmix"], w["wu"], w["wz"], w["wxbc"], w["wdt"], w["wgate"], w["pgw"], w["pscale"], w["pwo"],
              w["convw"], w["convb"], w["dtb"], w["alog"], w["dskip"], w["snorm"], w["swo"], w["wo"], w["expand"]]
    return pl.pallas_call(
        _mixer_prompt_kernel,
        grid=(B, S // T),
        in_specs=[pl.BlockSpec((1, T, D_MODEL), lambda b, t: (b, t, 0))] + [_const_spec(c.shape) for c in consts],
        out_specs=(pl.BlockSpec((1, T, D_MODEL), lambda b, t: (b, t, 0)),
                   pl.BlockSpec((1, 1, POOL_HIST, POOL_WIDTH), lambda b, t: (0, b, 0, 0)),
                   pl.BlockSpec((1, 1, CONV_WIDTH - 1, CONV_DIM), lambda b, t: (0, b, 0, 0)),
                   pl.BlockSpec((1, 1, N_SSD_HEADS, SSD_HEAD_DIM, D_STATE), lambda b, t: (0, b, 0, 0, 0))),
        out_shape=(jax.ShapeDtypeStruct((B, S, D_MODEL), f32),
                   jax.ShapeDtypeStruct((1, B, POOL_HIST, POOL_WIDTH), f32),
                   jax.ShapeDtypeStruct((1, B, CONV_WIDTH - 1, CONV_DIM), f32),
                   jax.ShapeDtypeStruct((1, B, N_SSD_HEADS, SSD_HEAD_DIM, D_STATE), f32)),
        scratch_shapes=[pltpu.VMEM((T, D_MODEL), bf16),
                        pltpu.VMEM((T, D_INNER), f32),
                        pltpu.VMEM((T, BC_DIM), f32),
                        pltpu.VMEM((T, BC_DIM), f32),
                        pltpu.VMEM((T, HEAD_LANES), f32),
                        pltpu.VMEM((T, D_INNER), f32),
                        pltpu.VMEM((D_STATE, D_INNER), f32),
                        pltpu.VMEM((T, D_INNER), bf16)]
                       + _window_scratch(N_POOL_BUFS, 1, POOL_PAD + T)
                       + _window_scratch(N_CONV_BUFS, 1, CONV_PAD + T),
        compiler_params=_params(2),
        name="mixer_prompt",
    )(x1, *consts)


def _sample_pre_kernel(x_ref, ph_ref, ch_ref, gmix_ref, wu_ref, wxbc_ref, wdt_ref, pgw_ref, pscale_ref, pwo_ref,
                       convw_ref, convb_ref, dtb_ref,
                       bp_ref, xs_ref, b_ref, c_ref, dt_ref, npool_ref, nconv_ref,
                       *window_bufs, T):
    ubufs, xbufs = window_bufs[:N_POOL_BUFS], window_bufs[N_POOL_BUFS:]
    hn = _rms(x_ref[...], gmix_ref[...]).astype(bf16)

    for j in range(POOL_WIDTH // LANES):
        ref, k = _slab(ubufs, j)
        ref[k, :, POOL_PAD - POOL_HIST:POOL_PAD, :] = ph_ref[0, :, :, j * LANES:(j + 1) * LANES]
    bp_ref[...] = _pool_branch(hn, wu_ref, ubufs, T, PAST_LEN, pgw_ref, pscale_ref, pwo_ref)
    for j in range(POOL_WIDTH // LANES):
        npool_ref[0, :, :, j * LANES:(j + 1) * LANES] = _window(ubufs, j, POOL_PAD + T - POOL_HIST, POOL_HIST)

    for j in range(CONV_DIM // LANES):
        ref, k = _slab(xbufs, j)
        ref[k, :, CONV_PAD - (CONV_WIDTH - 1):CONV_PAD, :] = ch_ref[0, :, :, j * LANES:(j + 1) * LANES]
    _project_conv(hn, wxbc_ref, xbufs, T, convw_ref, convb_ref, xs_ref, b_ref, c_ref)
    for j in range(CONV_DIM // LANES):
        nconv_ref[0, :, :, j * LANES:(j + 1) * LANES] = _window(
            xbufs, j, CONV_PAD + T - (CONV_WIDTH - 1), CONV_WIDTH - 1)
    dt_ref[...] = _softplus(_dot(hn, wdt_ref[...]) + dtb_ref[...])


def _sample_pre(x1, pool_hist, conv_hist, w, T):
    rows = x1.shape[0]
    nseq = rows // T
    S = SAMPLE_SEQS
    R = S * T
    assert nseq % S == 0 and T % SUBLANES == 0
    consts = [w["gmix"], w["wu"], w["wxbc"], w["wdt"], w["pgw"], w["pscale"], w["pwo"], w["convw"], w["convb"],
              w["dtb"]]
    row_spec = lambda width: pl.BlockSpec((R, width), lambda i: (i, 0))
    return pl.pallas_call(
        functools.partial(_sample_pre_kernel, T=T),
        grid=(nseq // S,),
        in_specs=[row_spec(D_MODEL),
                  pl.BlockSpec((1, S, POOL_HIST, POOL_WIDTH), lambda i: (0, i, 0, 0)),
                  pl.BlockSpec((1, S, CONV_WIDTH - 1, CONV_DIM), lambda i: (0, i, 0, 0))]
                 + [_const_spec(c.shape) for c in consts],
        out_specs=(row_spec(D_MODEL), row_spec(D_INNER), row_spec(BC_DIM), row_spec(BC_DIM), row_spec(HEAD_LANES),
                   pl.BlockSpec((1, S, POOL_HIST, POOL_WIDTH), lambda i: (0, i, 0, 0)),
                   pl.BlockSpec((1, S, CONV_WIDTH - 1, CONV_DIM), lambda i: (0, i, 0, 0))),
        out_shape=(jax.ShapeDtypeStruct((rows, D_MODEL), f32),
                   jax.ShapeDtypeStruct((rows, D_INNER), f32),
                   jax.ShapeDtypeStruct((rows, BC_DIM), f32),
                   jax.ShapeDtypeStruct((rows, BC_DIM), f32),
                   jax.ShapeDtypeStruct((rows, HEAD_LANES), f32),
                   jax.ShapeDtypeStruct((1, nseq, POOL_HIST, POOL_WIDTH), f32),
                   jax.ShapeDtypeStruct((1, nseq, CONV_WIDTH - 1, CONV_DIM), f32)),
        scratch_shapes=_window_scratch(N_POOL_BUFS, S, POOL_PAD + T) + _window_scratch(N_CONV_BUFS, S, CONV_PAD + T),
        compiler_params=_params(1),
        name="sample_pre",
    )(x1, pool_hist, conv_hist, *consts)


def _ssd_decode_kernel(xs_ref, b_ref, c_ref, dt_ref, h0_ref, alog_ref, dskip_ref, e_ref, y_ref, hn_ref, *, T):
    assert T == SUBLANES
    lane_group = lax.broadcasted_iota(jnp.int32, (T, HEAD_LANES), 1) // HEADS_PER_GROUP
    neg_a = -jnp.exp(alog_ref[...])

    def shift(v, d):
        if d == 0:
            return v
        r = lax.broadcasted_iota(jnp.int32, v.shape, 0)
        return jnp.where(r >= d, pltpu.roll(v, d, axis=0), 0.0)

    def per_seq(s, carry):
        dt = dt_ref[s]
        a = dt * neg_a
        d = 1
        while d < T:
            a = a + shift(a, d)
            d *= 2
        a_end = a[T - 1:T, :]
        x = xs_ref[s]
        bm = b_ref[s]
        cm = c_ref[s]
        terms = []
        for d in range(T):
            cbv = jnp.zeros((T, HEAD_LANES), f32)
            for g in range(N_SSD_GROUPS):
                gc = slice(g * D_STATE, (g + 1) * D_STATE)
                cb = jnp.sum(cm[:, gc] * shift(bm[:, gc], d), axis=-1, keepdims=True)
                cbv = jnp.where(lane_group == g, cb, cbv)
            terms.append(cbv * jnp.exp(a - shift(a, d)) * shift(dt, d))
        terms.append(jnp.exp(a))
        terms.append(jnp.exp(a_end - a) * dt)
        v = jnp.concatenate(terms, axis=0)
        v_hi = v.astype(bf16)
        v_lo = (v - v_hi.astype(f32)).astype(bf16)
        ex = _dot(jnp.concatenate([v_hi, v_lo], axis=0), e_ref[...])
        n = (T + 2) * T
        ex = ex[0:n] + ex[n:2 * n]

        y = dskip_ref[...] * x
        for d in range(T):
            y = y + ex[d * T:(d + 1) * T] * shift(x, d)
        ea = ex[T * T:(T + 1) * T]
        xw = (x * ex[(T + 1) * T:(T + 2) * T]).astype(bf16)
        cdec = jnp.exp(a_end)
        y_groups = []
        for g in range(N_SSD_GROUPS):
            gc = slice(g * D_STATE, (g + 1) * D_STATE)
            cols = slice(g * GROUP_WIDTH, (g + 1) * GROUP_WIDTH)
            heads = slice(g * HEADS_PER_GROUP, (g + 1) * HEADS_PER_GROUP)
            h0g = h0_ref[0, s, heads].reshape(GROUP_WIDTH, D_STATE)
            ch = lax.dot_general(cm[:, gc].astype(bf16), h0g.astype(bf16), _NT, preferred_element_type=f32)
            y_groups.append(y[:, cols] + ea[:, cols] * ch)
            st = lax.dot_general(xw[:, cols], bm[:, gc].astype(bf16), _TN, preferred_element_type=f32)
            for r in range(HEADS_PER_GROUP):
                h = g * HEADS_PER_GROUP + r
                hn_ref[0, s, h] = (h0_ref[0, s, h] * cdec[:, h:h + 1]
                                   + st[r * SSD_HEAD_DIM:(r + 1) * SSD_HEAD_DIM, :])
        y_ref[s] = jnp.concatenate(y_groups, axis=-1)
        return carry

    lax.fori_loop(0, DECODE_SEQS, per_seq, 0, unroll=DECODE_UNROLL)


def _ssd_decode(xs, bm, cm, dt, h0, w, T):
    nseq = xs.shape[0] // T
    Q = DECODE_SEQS
    assert nseq % Q == 0
    seq_spec = lambda width: pl.BlockSpec((Q, T, width), lambda i: (i, 0, 0))
    state_spec = pl.BlockSpec((1, Q, N_SSD_HEADS, SSD_HEAD_DIM, D_STATE), lambda i: (0, i, 0, 0, 0))
    consts = [w["alog"], w["dskip"], w["expand"]]
    return pl.pallas_call(
        functools.partial(_ssd_decode_kernel, T=T),
        grid=(nseq // Q,),
        in_specs=[seq_spec(D_INNER), seq_spec(BC_DIM), seq_spec(BC_DIM), seq_spec(HEAD_LANES), state_spec]
                 + [_const_spec(c.shape) for c in consts],
        out_specs=(seq_spec(D_INNER), state_spec),
        out_shape=(jax.ShapeDtypeStruct((nseq, T, D_INNER), f32), jax.ShapeDtypeStruct(h0.shape, f32)),
        compiler_params=_params(1),
        name="ssd_decode",
    )(xs.reshape(nseq, T, D_INNER), bm.reshape(nseq, T, BC_DIM), cm.reshape(nseq, T, BC_DIM),
      dt.reshape(nseq, T, HEAD_LANES), h0, *consts)


def _sample_post_kernel(x_ref, y_ref, bp_ref, gmix_ref, wz_ref, wgate_ref, snorm_ref, swo_ref, wo_ref,
                        o_ref, hn_s, yn_s):
    x = x_ref[...]
    hn_s[...] = _rms(x, gmix_ref[...]).astype(bf16)
    o_ref[...] = _gate_merge_out(x, hn_s, y_ref, bp_ref[...], wz_ref, wgate_ref, snorm_ref, swo_ref, wo_ref, yn_s)


def _sample_post(x1, y, bp, w, T):
    rows = x1.shape[0]
    R = SAMPLE_POST_SEQS * T
    assert rows % R == 0
    consts = [w["gmix"], w["wz"], w["wgate"], w["snorm"], w["swo"], w["wo"]]
    row_spec = lambda width: pl.BlockSpec((R, width), lambda i: (i, 0))
    return pl.pallas_call(
        _sample_post_kernel,
        grid=(rows // R,),
        in_specs=[row_spec(D_MODEL), row_spec(D_INNER), row_spec(D_MODEL)] + [_const_spec(c.shape) for c in consts],
        out_specs=row_spec(D_MODEL),
        out_shape=jax.ShapeDtypeStruct((rows, D_MODEL), f32),
        scratch_shapes=[pltpu.VMEM((R, D_MODEL), bf16), pltpu.VMEM((R, D_INNER), bf16)],
        compiler_params=_params(1),
        name="sample_post",
    )(x1, y, bp, *consts)


def _small_params(l, norm_mix, pool_scale, conv_w, conv_b, dt_bias, a_log, d_skip, ssd_norm):
    pad_heads = lambda v: jnp.pad(v, ((0, 0), (0, HEAD_LANES - N_SSD_HEADS)))
    head_of_lane = jnp.arange(D_INNER, dtype=jnp.int32) // SSD_HEAD_DIM
    expand = (jnp.arange(HEAD_LANES, dtype=jnp.int32)[:, None] == head_of_lane[None, :]).astype(bf16)
    return dict(
        gmix=norm_mix[l][None, :], pscale=pool_scale[l][None, :], convw=conv_w[l], convb=conv_b[l][None, :],
        dtb=pad_heads(dt_bias[l][None, :]), alog=pad_heads(a_log[l][None, :]),
        dskip=jnp.repeat(d_skip[l], SSD_HEAD_DIM)[None, :], snorm=ssd_norm[l][None, :], expand=expand)


def kernel(x_prompt, x_sample, state_pool, state_conv, state_ssm, norm_ffn1, ffn1_w_in, ffn1_w_out, norm_mix, w_in,
           pool_w_group, pool_scale, pool_w_out, conv_w, conv_b, dt_bias, a_log, d_skip, ssd_norm, ssd_w_out, w_o,
           norm_ffn2, ffn2_w_in, ffn2_w_out, norm_final):
    depth = w_in.shape[0]
    B, S, _ = x_prompt.shape
    DB, T, _ = x_sample.shape
    gfin = norm_final[None, :]
    xp = x_prompt.reshape(B * S, D_MODEL)
    xs = x_sample.reshape(DB * T, D_MODEL)
    outs = [[] for _ in range(6)]
    for l in range(depth):
        last = l == depth - 1
        f1_in, f1_out = ffn1_w_in[l].astype(bf16), ffn1_w_out[l].astype(bf16)
        g1, g2 = norm_ffn1[l][None, :], norm_ffn2[l][None, :]
        w = _small_params(l, norm_mix, pool_scale, conv_w, conv_b, dt_bias, a_log, d_skip, ssd_norm)
        n_pg = pool_w_group.shape[1]
        xp, xs, in_proj, casts = _ffn_and_casts(
            xp, xs, g1, f1_in, f1_out, gfin, w_in[l].T,
            [pool_w_group[l].reshape(n_pg * POOL_GROUP, POOL_GROUP), pool_w_out[l], ssd_w_out[l], w_o[l],
             ffn2_w_in[l], ffn2_w_out[l]])
        w.update(zip(("wu", "wz", "wxbc", "wdt", "wgate"), in_proj))
        w.update(pgw=casts[0].reshape(n_pg, POOL_GROUP, POOL_GROUP), pwo=casts[1], swo=casts[2], wo=casts[3])
        f2_in, f2_out = casts[4], casts[5]
        xp3, npool, nconv, nssm = _mixer_prompt(xp.reshape(B, S, D_MODEL), w)
        bp, cxs, cb, cc, cdt, spool, sconv = _sample_pre(xs, state_pool[l:l + 1], state_conv[l:l + 1], w, T)
        y, sssm = _ssd_decode(cxs, cb, cc, cdt, state_ssm[l:l + 1], w, T)
        xs = _sample_post(xs, y.reshape(DB * T, D_INNER), bp, w, T)
        xp, xs = _ffn(xp3.reshape(B * S, D_MODEL), xs, g2, f2_in, f2_out, gfin, final_norm=last)
        for acc, v in zip(outs, (npool, nconv, nssm, spool, sconv, sssm)):
            acc.append(v)
    stack = lambda vs: vs[0] if len(vs) == 1 else jnp.concatenate(vs, axis=0)
    return (xp.reshape(B, S, D_MODEL), xs.reshape(DB, T, D_MODEL),
            stack(outs[0]), stack(outs[1]), stack(outs[2]), stack(outs[3]), stack(outs[4]), stack(outs[5]))
```

```python
import functools

import jax
import jax.numpy as jnp
from jax import lax
from jax.experimental import pallas as pl
from jax.experimental.pallas import tpu as pltpu

f32 = jnp.float32
bf16 = jnp.bfloat16

D_MODEL = 1024
D_FF = 2816
POOL_WINDOWS = (2, 4, 8, 16)
POOL_WIDTH = D_MODEL
POOL_GROUP = POOL_WIDTH // len(POOL_WINDOWS)
POOL_HIST = max(POOL_WINDOWS) - 1
D_INNER = 2 * D_MODEL
SSD_HEAD_DIM = 64
N_SSD_HEADS = D_INNER // SSD_HEAD_DIM
N_SSD_GROUPS = 4
HEADS_PER_GROUP = N_SSD_HEADS // N_SSD_GROUPS
GROUP_WIDTH = D_INNER // N_SSD_GROUPS
D_STATE = 128
CONV_WIDTH = 4
BC_DIM = N_SSD_GROUPS * D_STATE
CONV_DIM = D_INNER + 2 * BC_DIM
CHUNK = 128
PAST_LEN = 16384
EPS = 1e-6
LOG2_E = 1.4426950408889634

LANES = 128
SUBLANES = 8
MXU_COLS = 256
VMEM_LIMIT_BYTES = 56 * 1024 * 1024

FFN_ROWS = 512
FFN_CHUNK = 256
PROMPT_TILE = 256
SAMPLE_SEQS = 16
SAMPLE_POST_SEQS = 32
DECODE_SEQS = 8
DECODE_UNROLL = 4
POOL_PAD = 16
CONV_PAD = 8
HEAD_LANES = LANES
SLABS_PER_BUF = MXU_COLS // LANES
N_POOL_BUFS = POOL_WIDTH // MXU_COLS
N_CONV_BUFS = CONV_DIM // MXU_COLS

_NT = (((1,), (1,)), ((), ()))
_TN = (((0,), (0,)), ((), ()))


def _rms(x, g):
    return x * lax.rsqrt(jnp.mean(x * x, axis=-1, keepdims=True) + EPS) * g


def _sigmoid(v):
    return 0.5 * jnp.tanh(0.5 * v) + 0.5


def _silu(v):
    h = 0.5 * v
    return h * jnp.tanh(h) + h


def _softplus(v):
    return jnp.maximum(v, 0.0) + jnp.log1p(jnp.exp(-jnp.abs(v)))


def _dot(a, b):
    return jnp.dot(a, b, preferred_element_type=f32)


def _split3(v):
    hi = v.astype(bf16)
    r1 = v - hi.astype(f32)
    mid = r1.astype(bf16)
    lo = (r1 - mid.astype(f32)).astype(bf16)
    return hi, mid, lo


def _const_spec(shape):
    nd = len(shape)
    return pl.BlockSpec(shape, lambda *_: (0,) * nd, pipeline_mode=pl.Buffered(1))


def _params(n_grid):
    return pltpu.CompilerParams(dimension_semantics=("arbitrary",) * n_grid,
                                vmem_limit_bytes=VMEM_LIMIT_BYTES)


def _ffn_kernel(xa_ref, xb_ref, g_ref, win_ref, wout_ref, gfin_ref, oa_ref, ob_ref, hn_s, act_s, *, n_a, final_norm):
    def tile(x_ref, o_ref):
        hn_s[...] = _rms(x_ref[...], g_ref[...]).astype(bf16)
        for c in range(D_FF // FFN_CHUNK):
            lo = c * FFN_CHUNK
            gate = _dot(hn_s[...], win_ref[:, lo:lo + FFN_CHUNK])
            up = _dot(hn_s[...], win_ref[:, D_FF + lo:D_FF + lo + FFN_CHUNK])
            act_s[:, lo:lo + FFN_CHUNK] = (_silu(gate) * up).astype(bf16)
        out = x_ref[...] + 0.5 * _dot(act_s[...], wout_ref[...])
        if final_norm:
            out = _rms(out, gfin_ref[...])
        o_ref[...] = out

    on_a = pl.program_id(0) < n_a
    pl.when(on_a)(lambda: tile(xa_ref, oa_ref))
    pl.when(jnp.logical_not(on_a))(lambda: tile(xb_ref, ob_ref))


def _ffn_specs(rows_a, rows_b):
    assert rows_a % FFN_ROWS == 0 and rows_b % FFN_ROWS == 0
    n_a, n_b = rows_a // FFN_ROWS, rows_b // FFN_ROWS
    a_spec = pl.BlockSpec((FFN_ROWS, D_MODEL), lambda i: (jnp.minimum(i, n_a - 1), 0))
    b_spec = pl.BlockSpec((FFN_ROWS, D_MODEL), lambda i: (jnp.maximum(i - n_a, 0), 0))
    in_specs = [a_spec, b_spec,
                _const_spec((1, D_MODEL)),
                _const_spec((D_MODEL, 2 * D_FF)),
                _const_spec((D_FF, D_MODEL)),
                _const_spec((1, D_MODEL))]
    out_shapes = [jax.ShapeDtypeStruct((rows_a, D_MODEL), f32), jax.ShapeDtypeStruct((rows_b, D_MODEL), f32)]
    scratch = [pltpu.VMEM((FFN_ROWS, D_MODEL), bf16), pltpu.VMEM((FFN_ROWS, D_FF), bf16)]
    return n_a, n_b, in_specs, [a_spec, b_spec], out_shapes, scratch


def _ffn(xa, xb, g, w_in_b, w_out_b, gfin, *, final_norm):
    n_a, n_b, in_specs, out_specs, out_shapes, scratch = _ffn_specs(xa.shape[0], xb.shape[0])
    return pl.pallas_call(
        functools.partial(_ffn_kernel, n_a=n_a, final_norm=final_norm),
        grid=(n_a + n_b,),
        in_specs=in_specs, out_specs=tuple(out_specs), out_shape=tuple(out_shapes), scratch_shapes=scratch,
        compiler_params=_params(1),
        name="ffn_final" if final_norm else "ffn",
    )(xa, xb, g, w_in_b, w_out_b, gfin)


_IN_PROJ_SPLITS = (POOL_WIDTH, POOL_WIDTH + D_INNER, POOL_WIDTH + D_INNER + CONV_DIM,
                   POOL_WIDTH + D_INNER + CONV_DIM + N_SSD_HEADS)
BF16_SUBLANES = 2 * SUBLANES


def _cast_block(n_rows, n_steps):
    period = 1
    while (n_rows * period) % n_steps or ((n_rows * period) // n_steps) % BF16_SUBLANES:
        period *= 2
        assert period <= n_steps
    return (n_rows * period) // n_steps, period


_PIECE_TILES = (POOL_WIDTH // MXU_COLS, D_INNER // MXU_COLS, CONV_DIM // MXU_COLS, 2 * D_MODEL // MXU_COLS)
_GATE_SKEW = _IN_PROJ_SPLITS[3] - _IN_PROJ_SPLITS[2]


def _store_when(cond, ref, val):
    @pl.when(cond)
    def _():
        ref[...] = val


def _ffn_cast_kernel(xa_ref, xb_ref, g_ref, win_ref, wout_ref, gfin_ref, wt_ref, wdt_f, wg_lo, wg_hi, *rest,
                     n_a, n_plain):
    plain_f = rest[:n_plain]
    oa_ref, ob_ref = rest[n_plain:n_plain + 2]
    wu_o, wz_o, wxbc_o, wdt_o, wgate_o = rest[n_plain + 2:n_plain + 7]
    plain_o = rest[n_plain + 7:2 * n_plain + 7]
    hn_s, act_s = rest[2 * n_plain + 7:]
    _ffn_kernel(xa_ref, xb_ref, g_ref, win_ref, wout_ref, gfin_ref, oa_ref, ob_ref, hn_s, act_s,
                n_a=n_a, final_norm=False)
    i = pl.program_id(0)
    n_u, n_z, n_x, n_g = _PIECE_TILES
    front = n_u + n_z + n_x

    @pl.when(i < front)
    def _():
        t = wt_ref[...].T.astype(bf16)
        _store_when(i < n_u, wu_o, t)
        _store_when(jnp.logical_and(i >= n_u, i < n_u + n_z), wz_o, t)
        _store_when(i >= n_u + n_z, wxbc_o, t)

    @pl.when(jnp.logical_and(i >= front, i < front + n_g))
    def _():
        rows = jnp.concatenate([wg_lo[_GATE_SKEW:, :], wg_hi[0:_GATE_SKEW, :]], axis=0)
        wgate_o[...] = rows.T.astype(bf16)

    @pl.when(i == 0)
    def _():
        t = wdt_f[...].T
        lane = lax.broadcasted_iota(jnp.int32, t.shape, 1)
        wdt_o[...] = jnp.where(lane < N_SSD_HEADS, t, 0.0).astype(bf16)

    @pl.when(i < n_a)
    def _():
        for src, dst in zip(plain_f, plain_o):
            dst[...] = src[...].astype(bf16)


def _ffn_and_casts(xa, xb, g, w_in_b, w_out_b, gfin, w_in_t, plain):
    n_a, n_b, in_specs, out_specs, out_shapes, scratch = _ffn_specs(xa.shape[0], xb.shape[0])
    n_u, n_z, n_x, n_g = _PIECE_TILES
    front = n_u + n_z + n_x
    assert n_a >= front + n_g and _IN_PROJ_SPLITS[2] == front * MXU_COLS and _IN_PROJ_SPLITS[2] % HEAD_LANES == 0

    def blocked(arr):
        blk, period = _cast_block(arr.shape[0], n_a)
        return (pl.BlockSpec((blk, arr.shape[1]), lambda i: (jnp.minimum(i, n_a - 1) // period, 0)),
                jax.ShapeDtypeStruct(arr.shape, bf16))

    def piece(first_step, n_tiles):
        return (pl.BlockSpec((D_MODEL, MXU_COLS), lambda i: (0, jnp.clip(i - first_step, 0, n_tiles - 1))),
                jax.ShapeDtypeStruct((D_MODEL, n_tiles * MXU_COLS), bf16))

    pieces = [piece(0, n_u), piece(n_u, n_z), piece(n_u + n_z, n_x),
              (pl.BlockSpec((D_MODEL, HEAD_LANES), lambda i: (0, 0)), jax.ShapeDtypeStruct((D_MODEL, HEAD_LANES), bf16)),
              piece(front, n_g)]
    in_proj_specs = [pl.BlockSpec((MXU_COLS, D_MODEL), lambda i: (jnp.minimum(i, front - 1), 0)),
                     pl.BlockSpec((HEAD_LANES, D_MODEL), lambda i: (_IN_PROJ_SPLITS[2] // HEAD_LANES, 0)),
                     pl.BlockSpec((MXU_COLS, D_MODEL), lambda i: (front + jnp.clip(i - front, 0, n_g - 1), 0)),
                     pl.BlockSpec((MXU_COLS, D_MODEL), lambda i: (front + 1 + jnp.clip(i - front, 0, n_g - 1), 0))]
    plains = [blocked(p) for p in plain]
    res = pl.pallas_call(
        functools.partial(_ffn_cast_kernel, n_a=n_a, n_plain=len(plain)),
        grid=(n_a + n_b,),
        in_specs=in_specs + in_proj_specs + [s for s, _ in plains],
        out_specs=tuple(out_specs + [s for s, _ in pieces] + [s for s, _ in plains]),
        out_shape=tuple(out_shapes + [o for _, o in pieces] + [o for _, o in plains]),
        scratch_shapes=scratch,
        compiler_params=_params(1),
        name="ffn_casts",
    )(xa, xb, g, w_in_b, w_out_b, gfin, w_in_t, w_in_t, w_in_t, w_in_t, *plain)
    return res[0], res[1], res[2:7], res[7:]


def _slab(bufs, j):
    return bufs[j // SLABS_PER_BUF], j % SLABS_PER_BUF


def _window_scratch(n_bufs, n_seqs, n_rows):
    return [pltpu.VMEM((SLABS_PER_BUF, n_seqs, n_rows, LANES), f32) for _ in range(n_bufs)]


def _store_rows(bufs, row0, val, T, slab0=0):
    for j in range(val.shape[1] // LANES):
        ref, k = _slab(bufs, slab0 + j)
        S = ref.shape[1]
        blk = val[:, j * LANES:(j + 1) * LANES]
        if S == 1:
            ref[k, 0, row0:row0 + T, :] = blk
        else:
            ref[k, :, row0:row0 + T, :] = blk.reshape(S, T, LANES)


def _window(bufs, j, row0, n):
    ref, k = _slab(bufs, j)
    S = ref.shape[1]
    rows = pl.ds(row0, n) if row0 % SUBLANES == 0 else pl.ds(row0, n, stride=1)
    if S == 1:
        return ref[k, 0, rows, :]
    v = ref[k, :, rows, :]
    return v.reshape(S * n, LANES) if n % SUBLANES == 0 else v


def _pool_branch(hn_s, wu_ref, ubufs, T, pos0, pgw_ref, pscale_ref, pwo_ref):
    S = ubufs[0].shape[1]
    assert T & (T - 1) == 0 and POOL_GROUP == MXU_COLS
    t_idx = jnp.bitwise_and(lax.broadcasted_iota(jnp.int32, (S * T, 1), 0), T - 1)
    n_seen = pos0 + 1 + t_idx
    _store_rows(ubufs, POOL_PAD, _dot(hn_s[...], wu_ref[...]), T)
    mixed = []
    for gi, w in enumerate(POOL_WINDOWS):
        cnt = jnp.minimum(n_seen, w).astype(f32)
        d = []
        for j in range(gi * SLABS_PER_BUF, (gi + 1) * SLABS_PER_BUF):
            cur = _window(ubufs, j, POOL_PAD, T)
            s = cur
            for k in range(1, w):
                s = s + _window(ubufs, j, POOL_PAD - k, T)
            d.append((s / cnt - cur).astype(bf16))
        mixed.append(_dot(jnp.concatenate(d, axis=-1), pgw_ref[gi]))
    y = jnp.concatenate(mixed, axis=-1) * pscale_ref[...]
    return _dot(y.astype(bf16), pwo_ref[...])


def _conv_slab(xbufs, T, convw_ref, convb_ref, j, xs_ref, b_ref, c_ref):
    c0 = j * LANES
    cols = slice(c0, c0 + LANES)
    y = convb_ref[:, cols]
    for k in range(CONV_WIDTH):
        y = y + _window(xbufs, j, CONV_PAD - (CONV_WIDTH - 1) + k, T) * convw_ref[k:k + 1, cols]
    v = _silu(y)
    if c0 < D_INNER:
        xs_ref[:, cols] = v
    elif c0 < D_INNER + BC_DIM:
        b_ref[:, c0 - D_INNER:c0 - D_INNER + LANES] = v
    else:
        c_ref[:, c0 - D_INNER - BC_DIM:c0 - D_INNER - BC_DIM + LANES] = v


def _project_conv(hn_s, wxbc_ref, xbufs, T, convw_ref, convb_ref, xs_ref, b_ref, c_ref):
    for jt in range(N_CONV_BUFS):
        slab0 = jt * SLABS_PER_BUF
        _store_rows(xbufs, CONV_PAD, _dot(hn_s[...], wxbc_ref[:, jt * MXU_COLS:(jt + 1) * MXU_COLS]), T, slab0=slab0)
        for j in range(slab0, slab0 + SLABS_PER_BUF):
            _conv_slab(xbufs, T, convw_ref, convb_ref, j, xs_ref, b_ref, c_ref)


def _gate_merge_out(x, hn_s, y_ref, bp, wz_ref, wgate_ref, snorm_ref, swo_ref, wo_ref, yn_s):
    for g in range(N_SSD_GROUPS):
        cols = slice(g * GROUP_WIDTH, (g + 1) * GROUP_WIDTH)
        z = _dot(hn_s[...], wz_ref[:, cols])
        yg = y_ref[:, cols] * _silu(z)
        yg = yg * lax.rsqrt(jnp.mean(yg * yg, axis=-1, keepdims=True) + EPS)
        yn_s[:, cols] = (yg * snorm_ref[:, cols]).astype(bf16)
    branch_ssd = _dot(yn_s[...], swo_ref[...])
    gate_pool = _sigmoid(_dot(hn_s[...], wgate_ref[:, 0:D_MODEL]))
    gate_ssd = _sigmoid(_dot(hn_s[...], wgate_ref[:, D_MODEL:2 * D_MODEL]))
    merged = (gate_pool * bp + gate_ssd * branch_ssd).astype(bf16)
    return x + _dot(merged, wo_ref[...])


def _ssd_chunk(rows, xs_s, b_s, c_s, dt_s, y_s, ht_s, alog_ref, dskip_ref, e_ref):
    L = CHUNK
    ri = lax.broadcasted_iota(jnp.int32, (L, L), 0)
    ci = lax.broadcasted_iota(jnp.int32, (L, L), 1)
    causal = ri >= ci
    tril = jnp.where(causal, 1.0, 0.0).astype(bf16)
    first_head = lax.broadcasted_iota(jnp.int32, (L, LANES), 1) < SSD_HEAD_DIM

    dt = dt_s[rows, :]
    dA = dt * (-jnp.exp(alog_ref[...]))
    hi, mid, lo = _split3(dA)
    acc = _dot(tril, jnp.concatenate([hi, mid, lo], axis=1))
    a = (acc[:, 0:LANES] + acc[:, LANES:2 * LANES] + acc[:, 2 * LANES:3 * LANES]) * LOG2_E
    aT = a.T[0:N_SSD_HEADS]
    dtT = dt.T[0:N_SSD_HEADS]
    wT = jnp.exp2(aT[:, L - 1:L] - aT) * dtT
    srcT = aT - jnp.log2(dtT)
    end_decay = jnp.broadcast_to(jnp.exp2(a[L - 1:L, :]), (2 * SUBLANES, HEAD_LANES))
    e_hi, e_mid, e_lo = (t.astype(f32) for t in _split3(end_decay))
    sel = lax.broadcasted_iota(jnp.int32, (2 * SUBLANES, HEAD_LANES), 0)
    stacked = jnp.where(sel == 0, e_hi, jnp.where(sel == 1, e_mid, jnp.where(sel == 2, e_lo, 0.0)))
    cdec = jnp.sum(_dot(stacked.astype(bf16), e_ref[...]), axis=0, keepdims=True)

    def head_mats(h, cb, bT):
        acol = jnp.broadcast_to(a[:, h:h + 1], (L, L))
        m = (cb * jnp.exp2(jnp.where(causal, acol - srcT[h:h + 1, :], -jnp.inf))).astype(bf16)
        bw = (bT * wT[h:h + 1, :]).astype(bf16)
        return m, bw, jnp.exp2(acol)

    pairs_per_group = HEADS_PER_GROUP // 2
    for q in range(N_SSD_HEADS // 2):
        g, qg = divmod(q, pairs_per_group)
        if qg == 0:
            gcols = slice(g * D_STATE, (g + 1) * D_STATE)
            b_f = b_s[rows, gcols]
            c_b = c_s[rows, gcols].astype(bf16)
            cb = lax.dot_general(c_b, b_f.astype(bf16), _NT, preferred_element_type=f32)
            bT = b_f.T
            ch = _dot(c_b, ht_s[:, g * GROUP_WIDTH:(g + 1) * GROUP_WIDTH].astype(bf16))
        cols = slice(q * LANES, (q + 1) * LANES)
        xq = xs_s[rows, cols]
        x2 = jnp.concatenate([jnp.where(first_head, xq, 0.0), jnp.where(first_head, 0.0, xq)], axis=0).astype(bf16)
        m_a, bw_a, ea_a = head_mats(2 * q, cb, bT)
        m_b, bw_b, ea_b = head_mats(2 * q + 1, cb, bT)
        y_off = ch[:, qg * LANES:(qg + 1) * LANES] * jnp.where(first_head, ea_a, ea_b)
        y_s[rows, cols] = _dot(jnp.concatenate([m_a, m_b], axis=1), x2) + y_off + dskip_ref[:, cols] * xq
        st = _dot(jnp.concatenate([bw_a, bw_b], axis=1), x2)
        ht_s[:, cols] = ht_s[:, cols] * cdec[:, cols] + st


def _mixer_prompt_kernel(x_ref, gmix_ref, wu_ref, wz_ref, wxbc_ref, wdt_ref, wgate_ref, pgw_ref, pscale_ref,
                         pwo_ref, convw_ref, convb_ref, dtb_ref, alog_ref, dskip_ref, snorm_ref, swo_ref,
                         wo_ref, e_ref,
                         o_ref, npool_ref, nconv_ref, nssm_ref,
                         hn_s, xs_s, b_s, c_s, dt_s, y_s, ht_s, yn_s, *window_bufs):
    T = PROMPT_TILE
    ti = pl.program_id(1)
    ubufs, xbufs = window_bufs[:N_POOL_BUFS], window_bufs[N_POOL_BUFS:]

    @pl.when(ti == 0)
    def _():
        for ref in ubufs:
            ref[:, :, 0:POOL_PAD, :] = jnp.zeros((SLABS_PER_BUF, 1, POOL_PAD, LANES), f32)
        for ref in xbufs:
            ref[:, :, 0:CONV_PAD, :] = jnp.zeros((SLABS_PER_BUF, 1, CONV_PAD, LANES), f32)
        ht_s[...] = jnp.zeros(ht_s.shape, f32)

    x = x_ref[0]
    hn_s[...] = _rms(x, gmix_ref[...]).astype(bf16)

    bp = _pool_branch(hn_s, wu_ref, ubufs, T, ti * T, pgw_ref, pscale_ref, pwo_ref)

    _project_conv(hn_s, wxbc_ref, xbufs, T, convw_ref, convb_ref, xs_s, b_s, c_s)
    dt_s[...] = _softplus(_dot(hn_s[...], wdt_ref[...]) + dtb_ref[...])

    def chunk(c, carry):
        rows = pl.ds(pl.multiple_of(c * CHUNK, CHUNK), CHUNK)
        _ssd_chunk(rows, xs_s, b_s, c_s, dt_s, y_s, ht_s, alog_ref, dskip_ref, e_ref)
        return carry

    lax.fori_loop(0, T // CHUNK, chunk, 0)

    o_ref[0] = _gate_merge_out(x, hn_s, y_s, bp, wz_ref, wgate_ref, snorm_ref, swo_ref, wo_ref, yn_s)

    @pl.when(ti == pl.num_programs(1) - 1)
    def _():
        for j in range(POOL_WIDTH // LANES):
            npool_ref[0, 0, :, j * LANES:(j + 1) * LANES] = _window(ubufs, j, POOL_PAD + T - POOL_HIST, POOL_HIST)
        for j in range(CONV_DIM // LANES):
            nconv_ref[0, 0, :, j * LANES:(j + 1) * LANES] = _window(
                xbufs, j, CONV_PAD + T - (CONV_WIDTH - 1), CONV_WIDTH - 1)
        for g in range(N_SSD_GROUPS):
            hg = ht_s[:, g * GROUP_WIDTH:(g + 1) * GROUP_WIDTH].T
            nssm_ref[0, 0, g * HEADS_PER_GROUP:(g + 1) * HEADS_PER_GROUP] = hg.reshape(
                HEADS_PER_GROUP, SSD_HEAD_DIM, D_STATE)

    for ref in ubufs:
        ref[:, :, 0:POOL_PAD, :] = ref[:, :, T:T + POOL_PAD, :]
    for ref in xbufs:
        ref[:, :, 0:CONV_PAD, :] = ref[:, :, T:T + CONV_PAD, :]


def _mixer_prompt(x1, w):
    B, S, _ = x1.shape
    T = PROMPT_TILE
    assert S % T == 0 and T % CHUNK == 0 and T >= POOL_PAD
    consts = [w["gmix"], w["wu"], w["wz"], w["wxbc"], w["wdt"], w["wgate"], w["pgw"], w["pscale"], w["pwo"],
              w["convw"], w["convb"], w["dtb"], w["alog"], w["dskip"], w["snorm"], w["swo"], w["wo"], w["expand"]]
    return pl.pallas_call(
        _mixer_prompt_kernel,
        grid=(B, S // T),
        in_specs=[pl.BlockSpec((1, T, D_MODEL), lambda b, t: (b, t, 0))] + [_const_spec(c.shape) for c in consts],
        out_specs=(pl.BlockSpec((1, T, D_MODEL), lambda b, t: (b, t, 0)),
                   pl.BlockSpec((1, 1, POOL_HIST, POOL_WIDTH), lambda b, t: (0, b, 0, 0)),
                   pl.BlockSpec((1, 1, CONV_WIDTH - 1, CONV_DIM), lambda b, t: (0, b, 0, 0)),
                   pl.BlockSpec((1, 1, N_SSD_HEADS, SSD_HEAD_DIM, D_STATE), lambda b, t: (0, b, 0, 0, 0))),
        out_shape=(jax.ShapeDtypeStruct((B, S, D_MODEL), f32),
                   jax.ShapeDtypeStruct((1, B, POOL_HIST, POOL_WIDTH), f32),
                   jax.ShapeDtypeStruct((1, B, CONV_WIDTH - 1, CONV_DIM), f32),
                   jax.ShapeDtypeStruct((1, B, N_SSD_HEADS, SSD_HEAD_DIM, D_STATE), f32)),
        scratch_shapes=[pltpu.VMEM((T, D_MODEL), bf16),
                        pltpu.VMEM((T, D_INNER), f32),
                        pltpu.VMEM((T, BC_DIM), f32),
                        pltpu.VMEM((T, BC_DIM), f32),
                        pltpu.VMEM((T, HEAD_LANES), f32),
                        pltpu.VMEM((T, D_INNER), f32),
                        pltpu.VMEM((D_STATE, D_INNER), f32),
                        pltpu.VMEM((T, D_INNER), bf16)]
                       + _window_scratch(N_POOL_BUFS, 1, POOL_PAD + T)
                       + _window_scratch(N_CONV_BUFS, 1, CONV_PAD + T),
        compiler_params=_params(2),
        name="mixer_prompt",
    )(x1, *consts)


def _sample_pre_kernel(x_ref, ph_ref, ch_ref, gmix_ref, wu_ref, wxbc_ref, wdt_ref, pgw_ref, pscale_ref, pwo_ref,
                       convw_ref, convb_ref, dtb_ref,
                       bp_ref, xs_ref, b_ref, c_ref, dt_ref, npool_ref, nconv_ref,
                       *window_bufs, T):
    ubufs, xbufs = window_bufs[:N_POOL_BUFS], window_bufs[N_POOL_BUFS:]
    hn = _rms(x_ref[...], gmix_ref[...]).astype(bf16)

    for j in range(POOL_WIDTH // LANES):
        ref, k = _slab(ubufs, j)
        ref[k, :, POOL_PAD - POOL_HIST:POOL_PAD, :] = ph_ref[0, :, :, j * LANES:(j + 1) * LANES]
    bp_ref[...] = _pool_branch(hn, wu_ref, ubufs, T, PAST_LEN, pgw_ref, pscale_ref, pwo_ref)
    for j in range(POOL_WIDTH // LANES):
        npool_ref[0, :, :, j * LANES:(j + 1) * LANES] = _window(ubufs, j, POOL_PAD + T - POOL_HIST, POOL_HIST)

    for j in range(CONV_DIM // LANES):
        ref, k = _slab(xbufs, j)
        ref[k, :, CONV_PAD - (CONV_WIDTH - 1):CONV_PAD, :] = ch_ref[0, :, :, j * LANES:(j + 1) * LANES]
    _project_conv(hn, wxbc_ref, xbufs, T, convw_ref, convb_ref, xs_ref, b_ref, c_ref)
    for j in range(CONV_DIM // LANES):
        nconv_ref[0, :, :, j * LANES:(j + 1) * LANES] = _window(
            xbufs, j, CONV_PAD + T - (CONV_WIDTH - 1), CONV_WIDTH - 1)
    dt_ref[...] = _softplus(_dot(hn, wdt_ref[...]) + dtb_ref[...])


def _sample_pre(x1, pool_hist, conv_hist, w, T):
    rows = x1.shape[0]
    nseq = rows // T
    S = SAMPLE_SEQS
    R = S * T
    assert nseq % S == 0 and T % SUBLANES == 0
    consts = [w["gmix"], w["wu"], w["wxbc"], w["wdt"], w["pgw"], w["pscale"], w["pwo"], w["convw"], w["convb"],
              w["dtb"]]
    row_spec = lambda width: pl.BlockSpec((R, width), lambda i: (i, 0))
    return pl.pallas_call(
        functools.partial(_sample_pre_kernel, T=T),
        grid=(nseq // S,),
        in_specs=[row_spec(D_MODEL),
                  pl.BlockSpec((1, S, POOL_HIST, POOL_WIDTH), lambda i: (0, i, 0, 0)),
                  pl.BlockSpec((1, S, CONV_WIDTH - 1, CONV_DIM), lambda i: (0, i, 0, 0))]
                 + [_const_spec(c.shape) for c in consts],
        out_specs=(row_spec(D_MODEL), row_spec(D_INNER), row_spec(BC_DIM), row_spec(BC_DIM), row_spec(HEAD_LANES),
                   pl.BlockSpec((1, S, POOL_HIST, POOL_WIDTH), lambda i: (0, i, 0, 0)),
                   pl.BlockSpec((1, S, CONV_WIDTH - 1, CONV_DIM), lambda i: (0, i, 0, 0))),
        out_shape=(jax.ShapeDtypeStruct((rows, D_MODEL), f32),
                   jax.ShapeDtypeStruct((rows, D_INNER), f32),
                   jax.ShapeDtypeStruct((rows, BC_DIM), f32),
                   jax.ShapeDtypeStruct((rows, BC_DIM), f32),
                   jax.ShapeDtypeStruct((rows, HEAD_LANES), f32),
                   jax.ShapeDtypeStruct((1, nseq, POOL_HIST, POOL_WIDTH), f32),
                   jax.ShapeDtypeStruct((1, nseq, CONV_WIDTH - 1, CONV_DIM), f32)),
        scratch_shapes=_window_scratch(N_POOL_BUFS, S, POOL_PAD + T) + _window_scratch(N_CONV_BUFS, S, CONV_PAD + T),
        compiler_params=_params(1),
        name="sample_pre",
    )(x1, pool_hist, conv_hist, *consts)


def _ssd_decode_kernel(xs_ref, b_ref, c_ref, dt_ref, h0_ref, alog_ref, dskip_ref, e_ref, y_ref, hn_ref, *, T):
    assert T == SUBLANES
    lane_group = lax.broadcasted_iota(jnp.int32, (T, HEAD_LANES), 1) // HEADS_PER_GROUP
    neg_a = -jnp.exp(alog_ref[...])

    def shift(v, d):
        if d == 0:
            return v
        r = lax.broadcasted_iota(jnp.int32, v.shape, 0)
        return jnp.where(r >= d, pltpu.roll(v, d, axis=0), 0.0)

    def per_seq(s, carry):
        dt = dt_ref[s]
        a = dt * neg_a
        d = 1
        while d < T:
            a = a + shift(a, d)
            d *= 2
        a_end = a[T - 1:T, :]
        x = xs_ref[s]
        bm = b_ref[s]
        cm = c_ref[s]
        terms = []
        for d in range(T):
            cbv = jnp.zeros((T, HEAD_LANES), f32)
            for g in range(N_SSD_GROUPS):
                gc = slice(g * D_STATE, (g + 1) * D_STATE)
                cb = jnp.sum(cm[:, gc] * shift(bm[:, gc], d), axis=-1, keepdims=True)
                cbv = jnp.where(lane_group == g, cb, cbv)
            terms.append(cbv * jnp.exp(a - shift(a, d)) * shift(dt, d))
        terms.append(jnp.exp(a))
        terms.append(jnp.exp(a_end - a) * dt)
        v = jnp.concatenate(terms, axis=0)
        v_hi = v.astype(bf16)
        v_lo = (v - v_hi.astype(f32)).astype(bf16)
        ex = _dot(jnp.concatenate([v_hi, v_lo], axis=0), e_ref[...])
        n = (T + 2) * T
        ex = ex[0:n] + ex[n:2 * n]

        y = dskip_ref[...] * x
        for d in range(T):
            y = y + ex[d * T:(d + 1) * T] * shift(x, d)
        ea = ex[T * T:(T + 1) * T]
        xw = (x * ex[(T + 1) * T:(T + 2) * T]).astype(bf16)
        cdec = jnp.exp(a_end)
        y_groups = []
        for g in range(N_SSD_GROUPS):
            gc = slice(g * D_STATE, (g + 1) * D_STATE)
            cols = slice(g * GROUP_WIDTH, (g + 1) * GROUP_WIDTH)
            heads = slice(g * HEADS_PER_GROUP, (g + 1) * HEADS_PER_GROUP)
            h0g = h0_ref[0, s, heads].reshape(GROUP_WIDTH, D_STATE)
            ch = lax.dot_general(cm[:, gc].astype(bf16), h0g.astype(bf16), _NT, preferred_element_type=f32)
            y_groups.append(y[:, cols] + ea[:, cols] * ch)
            st = lax.dot_general(xw[:, cols], bm[:, gc].astype(bf16), _TN, preferred_element_type=f32)
            for r in range(HEADS_PER_GROUP):
                h = g * HEADS_PER_GROUP + r
                hn_ref[0, s, h] = (h0_ref[0, s, h] * cdec[:, h:h + 1]
                                   + st[r * SSD_HEAD_DIM:(r + 1) * SSD_HEAD_DIM, :])
        y_ref[s] = jnp.concatenate(y_groups, axis=-1)
        return carry

    lax.fori_loop(0, DECODE_SEQS, per_seq, 0, unroll=DECODE_UNROLL)


def _ssd_decode(xs, bm, cm, dt, h0, w, T):
    nseq = xs.shape[0] // T
    Q = DECODE_SEQS
    assert nseq % Q == 0
    seq_spec = lambda width: pl.BlockSpec((Q, T, width), lambda i: (i, 0, 0))
    state_spec = pl.BlockSpec((1, Q, N_SSD_HEADS, SSD_HEAD_DIM, D_STATE), lambda i: (0, i, 0, 0, 0))
    consts = [w["alog"], w["dskip"], w["expand"]]
    return pl.pallas_call(
        functools.partial(_ssd_decode_kernel, T=T),
        grid=(nseq // Q,),
        in_specs=[seq_spec(D_INNER), seq_spec(BC_DIM), seq_spec(BC_DIM), seq_spec(HEAD_LANES), state_spec]
                 + [_const_spec(c.shape) for c in consts],
        out_specs=(seq_spec(D_INNER), state_spec),
        out_shape=(jax.ShapeDtypeStruct((nseq, T, D_INNER), f32), jax.ShapeDtypeStruct(h0.shape, f32)),
        compiler_params=_params(1),
        name="ssd_decode",
    )(xs.reshape(nseq, T, D_INNER), bm.reshape(nseq, T, BC_DIM), cm.reshape(nseq, T, BC_DIM),
      dt.reshape(nseq, T, HEAD_LANES), h0, *consts)


def _sample_post_kernel(x_ref, y_ref, bp_ref, gmix_ref, wz_ref, wgate_ref, snorm_ref, swo_ref, wo_ref,
                        o_ref, hn_s, yn_s):
    x = x_ref[...]
    hn_s[...] = _rms(x, gmix_ref[...]).astype(bf16)
    o_ref[...] = _gate_merge_out(x, hn_s, y_ref, bp_ref[...], wz_ref, wgate_ref, snorm_ref, swo_ref, wo_ref, yn_s)


def _sample_post(x1, y, bp, w, T):
    rows = x1.shape[0]
    R = SAMPLE_POST_SEQS * T
    assert rows % R == 0
    consts = [w["gmix"], w["wz"], w["wgate"], w["snorm"], w["swo"], w["wo"]]
    row_spec = lambda width: pl.BlockSpec((R, width), lambda i: (i, 0))
    return pl.pallas_call(
        _sample_post_kernel,
        grid=(rows // R,),
        in_specs=[row_spec(D_MODEL), row_spec(D_INNER), row_spec(D_MODEL)] + [_const_spec(c.shape) for c in consts],
        out_specs=row_spec(D_MODEL),
        out_shape=jax.ShapeDtypeStruct((rows, D_MODEL), f32),
        scratch_shapes=[pltpu.VMEM((R, D_MODEL), bf16), pltpu.VMEM((R, D_INNER), bf16)],
        compiler_params=_params(1),
        name="sample_post",
    )(x1, y, bp, *consts)


def _small_params(l, norm_mix, pool_scale, conv_w, conv_b, dt_bias, a_log, d_skip, ssd_norm):
    pad_heads = lambda v: jnp.pad(v, ((0, 0), (0, HEAD_LANES - N_SSD_HEADS)))
    head_of_lane = jnp.arange(D_INNER, dtype=jnp.int32) // SSD_HEAD_DIM
    expand = (jnp.arange(HEAD_LANES, dtype=jnp.int32)[:, None] == head_of_lane[None, :]).astype(bf16)
    return dict(
        gmix=norm_mix[l][None, :], pscale=pool_scale[l][None, :], convw=conv_w[l], convb=conv_b[l][None, :],
        dtb=pad_heads(dt_bias[l][None, :]), alog=pad_heads(a_log[l][None, :]),
        dskip=jnp.repeat(d_skip[l], SSD_HEAD_DIM)[None, :], snorm=ssd_norm[l][None, :], expand=expand)


def kernel(x_prompt, x_sample, state_pool, state_conv, state_ssm, norm_ffn1, ffn1_w_in, ffn1_w_out, norm_mix, w_in,
           pool_w_group, pool_scale, pool_w_out, conv_w, conv_b, dt_bias, a_log, d_skip, ssd_norm, ssd_w_out, w_o,
           norm_ffn2, ffn2_w_in, ffn2_w_out, norm_final):
    depth = w_in.shape[0]
    B, S, _ = x_prompt.shape
    DB, T, _ = x_sample.shape
    gfin = norm_final[None, :]
    xp = x_prompt.reshape(B * S, D_MODEL)
    xs = x_sample.reshape(DB * T, D_MODEL)
    outs = [[] for _ in range(6)]
    for l in range(depth):
        last = l == depth - 1
        f1_in, f1_out = ffn1_w_in[l].astype(bf16), ffn1_w_out[l].astype(bf16)
        g1, g2 = norm_ffn1[l][None, :], norm_ffn2[l][None, :]
        w = _small_params(l, norm_mix, pool_scale, conv_w, conv_b, dt_bias, a_log, d_skip, ssd_norm)
        n_pg = pool_w_group.shape[1]
        xp, xs, in_proj, casts = _ffn_and_casts(
            xp, xs, g1, f1_in, f1_out, gfin, w_in[l].T,
            [pool_w_group[l].reshape(n_pg * POOL_GROUP, POOL_GROUP), pool_w_out[l], ssd_w_out[l], w_o[l],
             ffn2_w_in[l], ffn2_w_out[l]])
        w.update(zip(("wu", "wz", "wxbc", "wdt", "wgate"), in_proj))
        w.update(pgw=casts[0].reshape(n_pg, POOL_GROUP, POOL_GROUP), pwo=casts[1], swo=casts[2], wo=casts[3])
        f2_in, f2_out = casts[4], casts[5]
        xp3, npool, nconv, nssm = _mixer_prompt(xp.reshape(B, S, D_MODEL), w)
        bp, cxs, cb, cc, cdt, spool, sconv = _sample_pre(xs, state_pool[l:l + 1], state_conv[l:l + 1], w, T)
        y, sssm = _ssd_decode(cxs, cb, cc, cdt, state_ssm[l:l + 1], w, T)
        xs = _sample_post(xs, y.reshape(DB * T, D_INNER), bp, w, T)
        xp, xs = _ffn(xp3.reshape(B * S, D_MODEL), xs, g2, f2_in, f2_out, gfin, final_norm=last)
        for acc, v in zip(outs, (npool, nconv, nssm, spool, sconv, sssm)):
            acc.append(v)
    stack = lambda vs: vs[0] if len(vs) == 1 else jnp.concatenate(vs, axis=0)
    return (xp.reshape(B, S, D_MODEL), xs.reshape(DB, T, D_MODEL),
            stack(outs[0]), stack(outs[1]), stack(outs[2]), stack(outs[3]), stack(outs[4]), stack(outs[5]))
```

```python
import functools

import jax
import jax.numpy as jnp
from jax import lax
from jax.experimental import pallas as pl
from jax.experimental.pallas import tpu as pltpu

f32 = jnp.float32
bf16 = jnp.bfloat16

D_MODEL = 1024
D_FF = 2816
POOL_WINDOWS = (2, 4, 8, 16)
POOL_WIDTH = D_MODEL
POOL_GROUP = POOL_WIDTH // len(POOL_WINDOWS)
POOL_HIST = max(POOL_WINDOWS) - 1
D_INNER = 2 * D_MODEL
SSD_HEAD_DIM = 64
N_SSD_HEADS = D_INNER // SSD_HEAD_DIM
N_SSD_GROUPS = 4
HEADS_PER_GROUP = N_SSD_HEADS // N_SSD_GROUPS
GROUP_WIDTH = D_INNER // N_SSD_GROUPS
D_STATE = 128
CONV_WIDTH = 4
BC_DIM = N_SSD_GROUPS * D_STATE
CONV_DIM = D_INNER + 2 * BC_DIM
CHUNK = 128
PAST_LEN = 16384
EPS = 1e-6
LOG2_E = 1.4426950408889634

LANES = 128
SUBLANES = 8
MXU_COLS = 256
VMEM_LIMIT_BYTES = 56 * 1024 * 1024

FFN_ROWS = 512
FFN_CHUNK = 256
PROMPT_TILE = 256
SAMPLE_SEQS = 32
SAMPLE_POST_SEQS = 32
DECODE_SEQS = 8
DECODE_UNROLL = 4
POOL_PAD = 16
CONV_PAD = 8
HEAD_LANES = LANES
SLABS_PER_BUF = MXU_COLS // LANES
N_POOL_BUFS = POOL_WIDTH // MXU_COLS
N_CONV_BUFS = CONV_DIM // MXU_COLS

_NT = (((1,), (1,)), ((), ()))
_TN = (((0,), (0,)), ((), ()))


def _rms(x, g):
    return x * lax.rsqrt(jnp.mean(x * x, axis=-1, keepdims=True) + EPS) * g


def _sigmoid(v):
    return 0.5 * jnp.tanh(0.5 * v) + 0.5


def _silu(v):
    h = 0.5 * v
    return h * jnp.tanh(h) + h


def _softplus(v):
    return jnp.maximum(v, 0.0) + jnp.log1p(jnp.exp(-jnp.abs(v)))


def _dot(a, b):
    return jnp.dot(a, b, preferred_element_type=f32)


def _split3(v):
    hi = v.astype(bf16)
    r1 = v - hi.astype(f32)
    mid = r1.astype(bf16)
    lo = (r1 - mid.astype(f32)).astype(bf16)
    return hi, mid, lo


def _const_spec(shape):
    nd = len(shape)
    return pl.BlockSpec(shape, lambda *_: (0,) * nd, pipeline_mode=pl.Buffered(1))


def _params(n_grid):
    return pltpu.CompilerParams(dimension_semantics=("arbitrary",) * n_grid,
                                vmem_limit_bytes=VMEM_LIMIT_BYTES)


def _ffn_kernel(xa_ref, xb_ref, g_ref, win_ref, wout_ref, gfin_ref, oa_ref, ob_ref, hn_s, act_s, *, n_a, final_norm):
    def tile(x_ref, o_ref):
        hn_s[...] = _rms(x_ref[...], g_ref[...]).astype(bf16)
        for c in range(D_FF // FFN_CHUNK):
            lo = c * FFN_CHUNK
            gate = _dot(hn_s[...], win_ref[:, lo:lo + FFN_CHUNK])
            up = _dot(hn_s[...], win_ref[:, D_FF + lo:D_FF + lo + FFN_CHUNK])
            act_s[:, lo:lo + FFN_CHUNK] = (_silu(gate) * up).astype(bf16)
        out = x_ref[...] + 0.5 * _dot(act_s[...], wout_ref[...])
        if final_norm:
            out = _rms(out, gfin_ref[...])
        o_ref[...] = out

    on_a = pl.program_id(0) < n_a
    pl.when(on_a)(lambda: tile(xa_ref, oa_ref))
    pl.when(jnp.logical_not(on_a))(lambda: tile(xb_ref, ob_ref))


def _ffn_specs(rows_a, rows_b):
    assert rows_a % FFN_ROWS == 0 and rows_b % FFN_ROWS == 0
    n_a, n_b = rows_a // FFN_ROWS, rows_b // FFN_ROWS
    a_spec = pl.BlockSpec((FFN_ROWS, D_MODEL), lambda i: (jnp.minimum(i, n_a - 1), 0))
    b_spec = pl.BlockSpec((FFN_ROWS, D_MODEL), lambda i: (jnp.maximum(i - n_a, 0), 0))
    in_specs = [a_spec, b_spec,
                _const_spec((1, D_MODEL)),
                _const_spec((D_MODEL, 2 * D_FF)),
                _const_spec((D_FF, D_MODEL)),
                _const_spec((1, D_MODEL))]
    out_shapes = [jax.ShapeDtypeStruct((rows_a, D_MODEL), f32), jax.ShapeDtypeStruct((rows_b, D_MODEL), f32)]
    scratch = [pltpu.VMEM((FFN_ROWS, D_MODEL), bf16), pltpu.VMEM((FFN_ROWS, D_FF), bf16)]
    return n_a, n_b, in_specs, [a_spec, b_spec], out_shapes, scratch


def _ffn(xa, xb, g, w_in_b, w_out_b, gfin, *, final_norm):
    n_a, n_b, in_specs, out_specs, out_shapes, scratch = _ffn_specs(xa.shape[0], xb.shape[0])
    return pl.pallas_call(
        functools.partial(_ffn_kernel, n_a=n_a, final_norm=final_norm),
        grid=(n_a + n_b,),
        in_specs=in_specs, out_specs=tuple(out_specs), out_shape=tuple(out_shapes), scratch_shapes=scratch,
        compiler_params=_params(1),
        name="ffn_final" if final_norm else "ffn",
    )(xa, xb, g, w_in_b, w_out_b, gfin)


_IN_PROJ_SPLITS = (POOL_WIDTH, POOL_WIDTH + D_INNER, POOL_WIDTH + D_INNER + CONV_DIM,
                   POOL_WIDTH + D_INNER + CONV_DIM + N_SSD_HEADS)
BF16_SUBLANES = 2 * SUBLANES


def _cast_block(n_rows, n_steps):
    period = 1
    while (n_rows * period) % n_steps or ((n_rows * period) // n_steps) % BF16_SUBLANES:
        period *= 2
        assert period <= n_steps
    return (n_rows * period) // n_steps, period


_PIECE_TILES = (POOL_WIDTH // MXU_COLS, D_INNER // MXU_COLS, CONV_DIM // MXU_COLS, 2 * D_MODEL // MXU_COLS)
_GATE_SKEW = _IN_PROJ_SPLITS[3] - _IN_PROJ_SPLITS[2]


def _store_when(cond, ref, val):
    @pl.when(cond)
    def _():
        ref[...] = val


def _ffn_cast_kernel(xa_ref, xb_ref, g_ref, win_ref, wout_ref, gfin_ref, wt_ref, wdt_f, wg_lo, wg_hi, *rest,
                     n_a, n_plain):
    plain_f = rest[:n_plain]
    oa_ref, ob_ref = rest[n_plain:n_plain + 2]
    wu_o, wz_o, wxbc_o, wdt_o, wgate_o = rest[n_plain + 2:n_plain + 7]
    plain_o = rest[n_plain + 7:2 * n_plain + 7]
    hn_s, act_s = rest[2 * n_plain + 7:]
    _ffn_kernel(xa_ref, xb_ref, g_ref, win_ref, wout_ref, gfin_ref, oa_ref, ob_ref, hn_s, act_s,
                n_a=n_a, final_norm=False)
    i = pl.program_id(0)
    n_u, n_z, n_x, n_g = _PIECE_TILES
    front = n_u + n_z + n_x

    @pl.when(i < front)
    def _():
        t = wt_ref[...].T.astype(bf16)
        _store_when(i < n_u, wu_o, t)
        _store_when(jnp.logical_and(i >= n_u, i < n_u + n_z), wz_o, t)
        _store_when(i >= n_u + n_z, wxbc_o, t)

    @pl.when(jnp.logical_and(i >= front, i < front + n_g))
    def _():
        rows = jnp.concatenate([wg_lo[_GATE_SKEW:, :], wg_hi[0:_GATE_SKEW, :]], axis=0)
        wgate_o[...] = rows.T.astype(bf16)

    @pl.when(i == 0)
    def _():
        t = wdt_f[...].T
        lane = lax.broadcasted_iota(jnp.int32, t.shape, 1)
        wdt_o[...] = jnp.where(lane < N_SSD_HEADS, t, 0.0).astype(bf16)

    @pl.when(i < n_a)
    def _():
        for src, dst in zip(plain_f, plain_o):
            dst[...] = src[...].astype(bf16)


def _ffn_and_casts(xa, xb, g, w_in_b, w_out_b, gfin, w_in_t, plain):
    n_a, n_b, in_specs, out_specs, out_shapes, scratch = _ffn_specs(xa.shape[0], xb.shape[0])
    n_u, n_z, n_x, n_g = _PIECE_TILES
    front = n_u + n_z + n_x
    assert n_a >= front + n_g and _IN_PROJ_SPLITS[2] == front * MXU_COLS and _IN_PROJ_SPLITS[2] % HEAD_LANES == 0

    def blocked(arr):
        blk, period = _cast_block(arr.shape[0], n_a)
        return (pl.BlockSpec((blk, arr.shape[1]), lambda i: (jnp.minimum(i, n_a - 1) // period, 0)),
                jax.ShapeDtypeStruct(arr.shape, bf16))

    def piece(first_step, n_tiles):
        return (pl.BlockSpec((D_MODEL, MXU_COLS), lambda i: (0, jnp.clip(i - first_step, 0, n_tiles - 1))),
                jax.ShapeDtypeStruct((D_MODEL, n_tiles * MXU_COLS), bf16))

    pieces = [piece(0, n_u), piece(n_u, n_z), piece(n_u + n_z, n_x),
              (pl.BlockSpec((D_MODEL, HEAD_LANES), lambda i: (0, 0)), jax.ShapeDtypeStruct((D_MODEL, HEAD_LANES), bf16)),
              piece(front, n_g)]
    in_proj_specs = [pl.BlockSpec((MXU_COLS, D_MODEL), lambda i: (jnp.minimum(i, front - 1), 0)),
                     pl.BlockSpec((HEAD_LANES, D_MODEL), lambda i: (_IN_PROJ_SPLITS[2] // HEAD_LANES, 0)),
                     pl.BlockSpec((MXU_COLS, D_MODEL), lambda i: (front + jnp.clip(i - front, 0, n_g - 1), 0)),
                     pl.BlockSpec((MXU_COLS, D_MODEL), lambda i: (front + 1 + jnp.clip(i - front, 0, n_g - 1), 0))]
    plains = [blocked(p) for p in plain]
    res = pl.pallas_call(
        functools.partial(_ffn_cast_kernel, n_a=n_a, n_plain=len(plain)),
        grid=(n_a + n_b,),
        in_specs=in_specs + in_proj_specs + [s for s, _ in plains],
        out_specs=tuple(out_specs + [s for s, _ in pieces] + [s for s, _ in plains]),
        out_shape=tuple(out_shapes + [o for _, o in pieces] + [o for _, o in plains]),
        scratch_shapes=scratch,
        compiler_params=_params(1),
        name="ffn_casts",
    )(xa, xb, g, w_in_b, w_out_b, gfin, w_in_t, w_in_t, w_in_t, w_in_t, *plain)
    return res[0], res[1], res[2:7], res[7:]


def _slab(bufs, j):
    return bufs[j // SLABS_PER_BUF], j % SLABS_PER_BUF


def _window_scratch(n_bufs, n_seqs, n_rows):
    return [pltpu.VMEM((SLABS_PER_BUF, n_seqs, n_rows, LANES), f32) for _ in range(n_bufs)]


def _store_rows(bufs, row0, val, T, slab0=0):
    for j in range(val.shape[1] // LANES):
        ref, k = _slab(bufs, slab0 + j)
        S = ref.shape[1]
        blk = val[:, j * LANES:(j + 1) * LANES]
        if S == 1:
            ref[k, 0, row0:row0 + T, :] = blk
        else:
            ref[k, :, row0:row0 + T, :] = blk.reshape(S, T, LANES)


def _window(bufs, j, row0, n):
    ref, k = _slab(bufs, j)
    S = ref.shape[1]
    rows = pl.ds(row0, n) if row0 % SUBLANES == 0 else pl.ds(row0, n, stride=1)
    if S == 1:
        return ref[k, 0, rows, :]
    v = ref[k, :, rows, :]
    return v.reshape(S * n, LANES) if n % SUBLANES == 0 else v


def _pool_branch(hn_s, wu_ref, ubufs, T, pos0, pgw_ref, pscale_ref, pwo_ref):
    S = ubufs[0].shape[1]
    assert T & (T - 1) == 0 and POOL_GROUP == MXU_COLS
    t_idx = jnp.bitwise_and(lax.broadcasted_iota(jnp.int32, (S * T, 1), 0), T - 1)
    n_seen = pos0 + 1 + t_idx
    _store_rows(ubufs, POOL_PAD, _dot(hn_s[...], wu_ref[...]), T)
    mixed = []
    for gi, w in enumerate(POOL_WINDOWS):
        cnt = jnp.minimum(n_seen, w).astype(f32)
        d = []
        for j in range(gi * SLABS_PER_BUF, (gi + 1) * SLABS_PER_BUF):
            cur = _window(ubufs, j, POOL_PAD, T)
            s = cur
            for k in range(1, w):
                s = s + _window(ubufs, j, POOL_PAD - k, T)
            d.append((s / cnt - cur).astype(bf16))
        mixed.append(_dot(jnp.concatenate(d, axis=-1), pgw_ref[gi]))
    y = jnp.concatenate(mixed, axis=-1) * pscale_ref[...]
    return _dot(y.astype(bf16), pwo_ref[...])


def _conv_slab(xbufs, T, convw_ref, convb_ref, j, xs_ref, b_ref, c_ref):
    c0 = j * LANES
    cols = slice(c0, c0 + LANES)
    y = convb_ref[:, cols]
    for k in range(CONV_WIDTH):
        y = y + _window(xbufs, j, CONV_PAD - (CONV_WIDTH - 1) + k, T) * convw_ref[k:k + 1, cols]
    v = _silu(y)
    if c0 < D_INNER:
        xs_ref[:, cols] = v
    elif c0 < D_INNER + BC_DIM:
        b_ref[:, c0 - D_INNER:c0 - D_INNER + LANES] = v
    else:
        c_ref[:, c0 - D_INNER - BC_DIM:c0 - D_INNER - BC_DIM + LANES] = v


def _project_conv(hn_s, wxbc_ref, xbufs, T, convw_ref, convb_ref, xs_ref, b_ref, c_ref):
    for jt in range(N_CONV_BUFS):
        slab0 = jt * SLABS_PER_BUF
        _store_rows(xbufs, CONV_PAD, _dot(hn_s[...], wxbc_ref[:, jt * MXU_COLS:(jt + 1) * MXU_COLS]), T, slab0=slab0)
        for j in range(slab0, slab0 + SLABS_PER_BUF):
            _conv_slab(xbufs, T, convw_ref, convb_ref, j, xs_ref, b_ref, c_ref)


def _gate_merge_out(x, hn_s, y_ref, bp, wz_ref, wgate_ref, snorm_ref, swo_ref, wo_ref, yn_s):
    for g in range(N_SSD_GROUPS):
        cols = slice(g * GROUP_WIDTH, (g + 1) * GROUP_WIDTH)
        z = _dot(hn_s[...], wz_ref[:, cols])
        yg = y_ref[:, cols] * _silu(z)
        yg = yg * lax.rsqrt(jnp.mean(yg * yg, axis=-1, keepdims=True) + EPS)
        yn_s[:, cols] = (yg * snorm_ref[:, cols]).astype(bf16)
    branch_ssd = _dot(yn_s[...], swo_ref[...])
    gate_pool = _sigmoid(_dot(hn_s[...], wgate_ref[:, 0:D_MODEL]))
    gate_ssd = _sigmoid(_dot(hn_s[...], wgate_ref[:, D_MODEL:2 * D_MODEL]))
    merged = (gate_pool * bp + gate_ssd * branch_ssd).astype(bf16)
    return x + _dot(merged, wo_ref[...])


def _ssd_chunk(rows, xs_s, b_s, c_s, dt_s, y_s, ht_s, alog_ref, dskip_ref, e_ref):
    L = CHUNK
    ri = lax.broadcasted_iota(jnp.int32, (L, L), 0)
    ci = lax.broadcasted_iota(jnp.int32, (L, L), 1)
    causal = ri >= ci
    tril = jnp.where(causal, 1.0, 0.0).astype(bf16)
    first_head = lax.broadcasted_iota(jnp.int32, (L, LANES), 1) < SSD_HEAD_DIM

    dt = dt_s[rows, :]
    dA = dt * (-jnp.exp(alog_ref[...]))
    hi, mid, lo = _split3(dA)
    acc = _dot(tril, jnp.concatenate([hi, mid, lo], axis=1))
    a = (acc[:, 0:LANES] + acc[:, LANES:2 * LANES] + acc[:, 2 * LANES:3 * LANES]) * LOG2_E
    aT = a.T[0:N_SSD_HEADS]
    dtT = dt.T[0:N_SSD_HEADS]
    wT = jnp.exp2(aT[:, L - 1:L] - aT) * dtT
    srcT = aT - jnp.log2(dtT)
    end_decay = jnp.broadcast_to(jnp.exp2(a[L - 1:L, :]), (2 * SUBLANES, HEAD_LANES))
    e_hi, e_mid, e_lo = (t.astype(f32) for t in _split3(end_decay))
    sel = lax.broadcasted_iota(jnp.int32, (2 * SUBLANES, HEAD_LANES), 0)
    stacked = jnp.where(sel == 0, e_hi, jnp.where(sel == 1, e_mid, jnp.where(sel == 2, e_lo, 0.0)))
    cdec = jnp.sum(_dot(stacked.astype(bf16), e_ref[...]), axis=0, keepdims=True)

    def head_mats(h, cb, bT):
        acol = jnp.broadcast_to(a[:, h:h + 1], (L, L))
        m = (cb * jnp.exp2(jnp.where(causal, acol - srcT[h:h + 1, :], -jnp.inf))).astype(bf16)
        bw = (bT * wT[h:h + 1, :]).astype(bf16)
        return m, bw, jnp.exp2(acol)

    pairs_per_group = HEADS_PER_GROUP // 2
    for q in range(N_SSD_HEADS // 2):
        g, qg = divmod(q, pairs_per_group)
        if qg == 0:
            gcols = slice(g * D_STATE, (g + 1) * D_STATE)
            b_f = b_s[rows, gcols]
            c_b = c_s[rows, gcols].astype(bf16)
            cb = lax.dot_general(c_b, b_f.astype(bf16), _NT, preferred_element_type=f32)
            bT = b_f.T
            ch = _dot(c_b, ht_s[:, g * GROUP_WIDTH:(g + 1) * GROUP_WIDTH].astype(bf16))
        cols = slice(q * LANES, (q + 1) * LANES)
        xq = xs_s[rows, cols]
        x2 = jnp.concatenate([jnp.where(first_head, xq, 0.0), jnp.where(first_head, 0.0, xq)], axis=0).astype(bf16)
        m_a, bw_a, ea_a = head_mats(2 * q, cb, bT)
        m_b, bw_b, ea_b = head_mats(2 * q + 1, cb, bT)
        y_off = ch[:, qg * LANES:(qg + 1) * LANES] * jnp.where(first_head, ea_a, ea_b)
        y_s[rows, cols] = _dot(jnp.concatenate([m_a, m_b], axis=1), x2) + y_off + dskip_ref[:, cols] * xq
        st = _dot(jnp.concatenate([bw_a, bw_b], axis=1), x2)
        ht_s[:, cols] = ht_s[:, cols] * cdec[:, cols] + st


def _mixer_prompt_kernel(x_ref, gmix_ref, wu_ref, wz_ref, wxbc_ref, wdt_ref, wgate_ref, pgw_ref, pscale_ref,
                         pwo_ref, convw_ref, convb_ref, dtb_ref, alog_ref, dskip_ref, snorm_ref, swo_ref,
                         wo_ref, e_ref,
                         o_ref, npool_ref, nconv_ref, nssm_ref,
                         hn_s, xs_s, b_s, c_s, dt_s, y_s, ht_s, yn_s, *window_bufs):
    T = PROMPT_TILE
    ti = pl.program_id(1)
    ubufs, xbufs = window_bufs[:N_POOL_BUFS], window_bufs[N_POOL_BUFS:]

    @pl.when(ti == 0)
    def _():
        for ref in ubufs:
            ref[:, :, 0:POOL_PAD, :] = jnp.zeros((SLABS_PER_BUF, 1, POOL_PAD, LANES), f32)
        for ref in xbufs:
            ref[:, :, 0:CONV_PAD, :] = jnp.zeros((SLABS_PER_BUF, 1, CONV_PAD, LANES), f32)
        ht_s[...] = jnp.zeros(ht_s.shape, f32)

    x = x_ref[0]
    hn_s[...] = _rms(x, gmix_ref[...]).astype(bf16)

    bp = _pool_branch(hn_s, wu_ref, ubufs, T, ti * T, pgw_ref, pscale_ref, pwo_ref)

    _project_conv(hn_s, wxbc_ref, xbufs, T, convw_ref, convb_ref, xs_s, b_s, c_s)
    dt_s[...] = _softplus(_dot(hn_s[...], wdt_ref[...]) + dtb_ref[...])

    def chunk(c, carry):
        rows = pl.ds(pl.multiple_of(c * CHUNK, CHUNK), CHUNK)
        _ssd_chunk(rows, xs_s, b_s, c_s, dt_s, y_s, ht_s, alog_ref, dskip_ref, e_ref)
        return carry

    lax.fori_loop(0, T // CHUNK, chunk, 0)

    o_ref[0] = _gate_merge_out(x, hn_s, y_s, bp, wz_ref, wgate_ref, snorm_ref, swo_ref, wo_ref, yn_s)

    @pl.when(ti == pl.num_programs(1) - 1)
    def _():
        for j in range(POOL_WIDTH // LANES):
            npool_ref[0, 0, :, j * LANES:(j + 1) * LANES] = _window(ubufs, j, POOL_PAD + T - POOL_HIST, POOL_HIST)
        for j in range(CONV_DIM // LANES):
            nconv_ref[0, 0, :, j * LANES:(j + 1) * LANES] = _window(
                xbufs, j, CONV_PAD + T - (CONV_WIDTH - 1), CONV_WIDTH - 1)
        for g in range(N_SSD_GROUPS):
            hg = ht_s[:, g * GROUP_WIDTH:(g + 1) * GROUP_WIDTH].T
            nssm_ref[0, 0, g * HEADS_PER_GROUP:(g + 1) * HEADS_PER_GROUP] = hg.reshape(
                HEADS_PER_GROUP, SSD_HEAD_DIM, D_STATE)

    for ref in ubufs:
        ref[:, :, 0:POOL_PAD, :] = ref[:, :, T:T + POOL_PAD, :]
    for ref in xbufs:
        ref[:, :, 0:CONV_PAD, :] = ref[:, :, T:T + CONV_PAD, :]


def _mixer_prompt(x1, w):
    B, S, _ = x1.shape
    T = PROMPT_TILE
    assert S % T == 0 and T % CHUNK == 0 and T >= POOL_PAD
    consts = [w["gmix"], w["wu"], w["wz"], w["wxbc"], w["wdt"], w["wgate"], w["pgw"], w["pscale"], w["pwo"],
              w["convw"], w["convb"], w["dtb"], w["alog"], w["dskip"], w["snorm"], w["swo"], w["wo"], w["expand"]]
    return pl.pallas_call(
        _mixer_prompt_kernel,
        grid=(B, S // T),
        in_specs=[pl.BlockSpec((1, T, D_MODEL), lambda b, t: (b, t, 0))] + [_const_spec(c.shape) for c in consts],
        out_specs=(pl.BlockSpec((1, T, D_MODEL), lambda b, t: (b, t, 0)),
                   pl.BlockSpec((1, 1, POOL_HIST, POOL_WIDTH), lambda b, t: (0, b, 0, 0)),
                   pl.BlockSpec((1, 1, CONV_WIDTH - 1, CONV_DIM), lambda b, t: (0, b, 0, 0)),
                   pl.BlockSpec((1, 1, N_SSD_HEADS, SSD_HEAD_DIM, D_STATE), lambda b, t: (0, b, 0, 0, 0))),
        out_shape=(jax.ShapeDtypeStruct((B, S, D_MODEL), f32),
                   jax.ShapeDtypeStruct((1, B, POOL_HIST, POOL_WIDTH), f32),
                   jax.ShapeDtypeStruct((1, B, CONV_WIDTH - 1, CONV_DIM), f32),
                   jax.ShapeDtypeStruct((1, B, N_SSD_HEADS, SSD_HEAD_DIM, D_STATE), f32)),
        scratch_shapes=[pltpu.VMEM((T, D_MODEL), bf16),
                        pltpu.VMEM((T, D_INNER), f32),
                        pltpu.VMEM((T, BC_DIM), f32),
                        pltpu.VMEM((T, BC_DIM), f32),
                        pltpu.VMEM((T, HEAD_LANES), f32),
                        pltpu.VMEM((T, D_INNER), f32),
                        pltpu.VMEM((D_STATE, D_INNER), f32),
                        pltpu.VMEM((T, D_INNER), bf16)]
                       + _window_scratch(N_POOL_BUFS, 1, POOL_PAD + T)
                       + _window_scratch(N_CONV_BUFS, 1, CONV_PAD + T),
        compiler_params=_params(2),
        name="mixer_prompt",
    )(x1, *consts)


def _load_history(bufs, hist_ref, row0):
    for j in range(hist_ref.shape[2] // LANES):
        ref, k = _slab(bufs, j)
        for r in range(hist_ref.shape[0]):
            ref[k, :, row0 + r, :] = hist_ref[r, :, j * LANES:(j + 1) * LANES]


def _store_history(hist_ref, bufs, row0):
    for j in range(hist_ref.shape[2] // LANES):
        ref, k = _slab(bufs, j)
        for r in range(hist_ref.shape[0]):
            hist_ref[r, :, j * LANES:(j + 1) * LANES] = ref[k, :, row0 + r, :]


def _sample_pre_kernel(x_ref, ph_ref, ch_ref, gmix_ref, wu_ref, wxbc_ref, wdt_ref, pgw_ref, pscale_ref, pwo_ref,
                       convw_ref, convb_ref, dtb_ref,
                       bp_ref, xs_ref, b_ref, c_ref, dt_ref, npool_ref, nconv_ref,
                       *window_bufs, T):
    ubufs, xbufs = window_bufs[:N_POOL_BUFS], window_bufs[N_POOL_BUFS:]
    hn = _rms(x_ref[...], gmix_ref[...]).astype(bf16)

    _load_history(ubufs, ph_ref, POOL_PAD - POOL_HIST)
    bp_ref[...] = _pool_branch(hn, wu_ref, ubufs, T, PAST_LEN, pgw_ref, pscale_ref, pwo_ref)
    _store_history(npool_ref, ubufs, POOL_PAD + T - POOL_HIST)

    _load_history(xbufs, ch_ref, CONV_PAD - (CONV_WIDTH - 1))
    _project_conv(hn, wxbc_ref, xbufs, T, convw_ref, convb_ref, xs_ref, b_ref, c_ref)
    _store_history(nconv_ref, xbufs, CONV_PAD + T - (CONV_WIDTH - 1))
    dt_ref[...] = _softplus(_dot(hn, wdt_ref[...]) + dtb_ref[...])


def _sample_pre(x1, pool_hist, conv_hist, w, T):
    rows = x1.shape[0]
    nseq = rows // T
    S = SAMPLE_SEQS
    R = S * T
    assert nseq % S == 0 and T % SUBLANES == 0
    consts = [w["gmix"], w["wu"], w["wxbc"], w["wdt"], w["pgw"], w["pscale"], w["pwo"], w["convw"], w["convb"],
              w["dtb"]]
    row_spec = lambda width: pl.BlockSpec((R, width), lambda i: (i, 0))
    return pl.pallas_call(
        functools.partial(_sample_pre_kernel, T=T),
        grid=(nseq // S,),
        in_specs=[row_spec(D_MODEL),
                  pl.BlockSpec((POOL_HIST, S, POOL_WIDTH), lambda i: (0, i, 0)),
                  pl.BlockSpec((CONV_WIDTH - 1, S, CONV_DIM), lambda i: (0, i, 0))]
                 + [_const_spec(c.shape) for c in consts],
        out_specs=(row_spec(D_MODEL), row_spec(D_INNER), row_spec(BC_DIM), row_spec(BC_DIM), row_spec(HEAD_LANES),
                   pl.BlockSpec((POOL_HIST, S, POOL_WIDTH), lambda i: (0, i, 0)),
                   pl.BlockSpec((CONV_WIDTH - 1, S, CONV_DIM), lambda i: (0, i, 0))),
        out_shape=(jax.ShapeDtypeStruct((rows, D_MODEL), f32),
                   jax.ShapeDtypeStruct((rows, D_INNER), f32),
                   jax.ShapeDtypeStruct((rows, BC_DIM), f32),
                   jax.ShapeDtypeStruct((rows, BC_DIM), f32),
                   jax.ShapeDtypeStruct((rows, HEAD_LANES), f32),
                   jax.ShapeDtypeStruct((POOL_HIST, nseq, POOL_WIDTH), f32),
                   jax.ShapeDtypeStruct((CONV_WIDTH - 1, nseq, CONV_DIM), f32)),
        scratch_shapes=_window_scratch(N_POOL_BUFS, S, POOL_PAD + T) + _window_scratch(N_CONV_BUFS, S, CONV_PAD + T),
        compiler_params=_params(1),
        name="sample_pre",
    )(x1, pool_hist, conv_hist, *consts)


def _ssd_decode_kernel(xs_ref, b_ref, c_ref, dt_ref, h0_ref, alog_ref, dskip_ref, e_ref, y_ref, hn_ref, *, T):
    assert T == SUBLANES
    lane_group = lax.broadcasted_iota(jnp.int32, (T, HEAD_LANES), 1) // HEADS_PER_GROUP
    neg_a = -jnp.exp(alog_ref[...])

    def shift(v, d):
        if d == 0:
            return v
        r = lax.broadcasted_iota(jnp.int32, v.shape, 0)
        return jnp.where(r >= d, pltpu.roll(v, d, axis=0), 0.0)

    def per_seq(s, carry):
        dt = dt_ref[s]
        a = dt * neg_a
        d = 1
        while d < T:
            a = a + shift(a, d)
            d *= 2
        a_end = a[T - 1:T, :]
        x = xs_ref[s]
        bm = b_ref[s]
        cm = c_ref[s]
        terms = []
        for d in range(T):
            cbv = jnp.zeros((T, HEAD_LANES), f32)
            for g in range(N_SSD_GROUPS):
                gc = slice(g * D_STATE, (g + 1) * D_STATE)
                cb = jnp.sum(cm[:, gc] * shift(bm[:, gc], d), axis=-1, keepdims=True)
                cbv = jnp.where(lane_group == g, cb, cbv)
            terms.append(cbv * jnp.exp(a - shift(a, d)) * shift(dt, d))
        terms.append(jnp.exp(a))
        terms.append(jnp.exp(a_end - a) * dt)
        v = jnp.concatenate(terms, axis=0)
        v_hi = v.astype(bf16)
        v_lo = (v - v_hi.astype(f32)).astype(bf16)
        ex = _dot(jnp.concatenate([v_hi, v_lo], axis=0), e_ref[...])
        n = (T + 2) * T
        ex = ex[0:n] + ex[n:2 * n]

        y = dskip_ref[...] * x
        for d in range(T):
            y = y + ex[d * T:(d + 1) * T] * shift(x, d)
        ea = ex[T * T:(T + 1) * T]
        xw = (x * ex[(T + 1) * T:(T + 2) * T]).astype(bf16)
        cdec = jnp.exp(a_end)
        y_groups = []
        for g in range(N_SSD_GROUPS):
            gc = slice(g * D_STATE, (g + 1) * D_STATE)
            cols = slice(g * GROUP_WIDTH, (g + 1) * GROUP_WIDTH)
            heads = slice(g * HEADS_PER_GROUP, (g + 1) * HEADS_PER_GROUP)
            h0g = h0_ref[0, s, heads].reshape(GROUP_WIDTH, D_STATE)
            ch = lax.dot_general(cm[:, gc].astype(bf16), h0g.astype(bf16), _NT, preferred_element_type=f32)
            y_groups.append(y[:, cols] + ea[:, cols] * ch)
            st = lax.dot_general(xw[:, cols], bm[:, gc].astype(bf16), _TN, preferred_element_type=f32)
            for r in range(HEADS_PER_GROUP):
                h = g * HEADS_PER_GROUP + r
                hn_ref[0, s, h] = (h0_ref[0, s, h] * cdec[:, h:h + 1]
                                   + st[r * SSD_HEAD_DIM:(r + 1) * SSD_HEAD_DIM, :])
        y_ref[s] = jnp.concatenate(y_groups, axis=-1)
        return carry

    lax.fori_loop(0, DECODE_SEQS, per_seq, 0, unroll=DECODE_UNROLL)


def _ssd_decode(xs, bm, cm, dt, h0, w, T):
    nseq = xs.shape[0] // T
    Q = DECODE_SEQS
    assert nseq % Q == 0
    seq_spec = lambda width: pl.BlockSpec((Q, T, width), lambda i: (i, 0, 0))
    state_spec = pl.BlockSpec((1, Q, N_SSD_HEADS, SSD_HEAD_DIM, D_STATE), lambda i: (0, i, 0, 0, 0))
    consts = [w["alog"], w["dskip"], w["expand"]]
    return pl.pallas_call(
        functools.partial(_ssd_decode_kernel, T=T),
        grid=(nseq // Q,),
        in_specs=[seq_spec(D_INNER), seq_spec(BC_DIM), seq_spec(BC_DIM), seq_spec(HEAD_LANES), state_spec]
                 + [_const_spec(c.shape) for c in consts],
        out_specs=(seq_spec(D_INNER), state_spec),
        out_shape=(jax.ShapeDtypeStruct((nseq, T, D_INNER), f32), jax.ShapeDtypeStruct(h0.shape, f32)),
        compiler_params=_params(1),
        name="ssd_decode",
    )(xs.reshape(nseq, T, D_INNER), bm.reshape(nseq, T, BC_DIM), cm.reshape(nseq, T, BC_DIM),
      dt.reshape(nseq, T, HEAD_LANES), h0, *consts)


def _sample_post_kernel(x_ref, y_ref, bp_ref, gmix_ref, wz_ref, wgate_ref, snorm_ref, swo_ref, wo_ref,
                        o_ref, hn_s, yn_s):
    x = x_ref[...]
    hn_s[...] = _rms(x, gmix_ref[...]).astype(bf16)
    o_ref[...] = _gate_merge_out(x, hn_s, y_ref, bp_ref[...], wz_ref, wgate_ref, snorm_ref, swo_ref, wo_ref, yn_s)


def _sample_post(x1, y, bp, w, T):
    rows = x1.shape[0]
    R = SAMPLE_POST_SEQS * T
    assert rows % R == 0
    consts = [w["gmix"], w["wz"], w["wgate"], w["snorm"], w["swo"], w["wo"]]
    row_spec = lambda width: pl.BlockSpec((R, width), lambda i: (i, 0))
    return pl.pallas_call(
        _sample_post_kernel,
        grid=(rows // R,),
        in_specs=[row_spec(D_MODEL), row_spec(D_INNER), row_spec(D_MODEL)] + [_const_spec(c.shape) for c in consts],
        out_specs=row_spec(D_MODEL),
        out_shape=jax.ShapeDtypeStruct((rows, D_MODEL), f32),
        scratch_shapes=[pltpu.VMEM((R, D_MODEL), bf16), pltpu.VMEM((R, D_INNER), bf16)],
        compiler_params=_params(1),
        name="sample_post",
    )(x1, y, bp, *consts)


def _small_params(l, norm_mix, pool_scale, conv_w, conv_b, dt_bias, a_log, d_skip, ssd_norm):
    pad_heads = lambda v: jnp.pad(v, ((0, 0), (0, HEAD_LANES - N_SSD_HEADS)))
    head_of_lane = jnp.arange(D_INNER, dtype=jnp.int32) // SSD_HEAD_DIM
    expand = (jnp.arange(HEAD_LANES, dtype=jnp.int32)[:, None] == head_of_lane[None, :]).astype(bf16)
    return dict(
        gmix=norm_mix[l][None, :], pscale=pool_scale[l][None, :], convw=conv_w[l], convb=conv_b[l][None, :],
        dtb=pad_heads(dt_bias[l][None, :]), alog=pad_heads(a_log[l][None, :]),
        dskip=jnp.repeat(d_skip[l], SSD_HEAD_DIM)[None, :], snorm=ssd_norm[l][None, :], expand=expand)


def kernel(x_prompt, x_sample, state_pool, state_conv, state_ssm, norm_ffn1, ffn1_w_in, ffn1_w_out, norm_mix, w_in,
           pool_w_group, pool_scale, pool_w_out, conv_w, conv_b, dt_bias, a_log, d_skip, ssd_norm, ssd_w_out, w_o,
           norm_ffn2, ffn2_w_in, ffn2_w_out, norm_final):
    depth = w_in.shape[0]
    B, S, _ = x_prompt.shape
    DB, T, _ = x_sample.shape
    gfin = norm_final[None, :]
    xp = x_prompt.reshape(B * S, D_MODEL)
    xs = x_sample.reshape(DB * T, D_MODEL)
    outs = [[] for _ in range(6)]
    for l in range(depth):
        last = l == depth - 1
        f1_in, f1_out = ffn1_w_in[l].astype(bf16), ffn1_w_out[l].astype(bf16)
        g1, g2 = norm_ffn1[l][None, :], norm_ffn2[l][None, :]
        w = _small_params(l, norm_mix, pool_scale, conv_w, conv_b, dt_bias, a_log, d_skip, ssd_norm)
        n_pg = pool_w_group.shape[1]
        xp, xs, in_proj, casts = _ffn_and_casts(
            xp, xs, g1, f1_in, f1_out, gfin, w_in[l].T,
            [pool_w_group[l].reshape(n_pg * POOL_GROUP, POOL_GROUP), pool_w_out[l], ssd_w_out[l], w_o[l],
             ffn2_w_in[l], ffn2_w_out[l]])
        w.update(zip(("wu", "wz", "wxbc", "wdt", "wgate"), in_proj))
        w.update(pgw=casts[0].reshape(n_pg, POOL_GROUP, POOL_GROUP), pwo=casts[1], swo=casts[2], wo=casts[3])
        f2_in, f2_out = casts[4], casts[5]
        xp3, npool, nconv, nssm = _mixer_prompt(xp.reshape(B, S, D_MODEL), w)
        hist_major = lambda v: jnp.transpose(v, (1, 0, 2))
        bp, cxs, cb, cc, cdt, spool, sconv = _sample_pre(xs, hist_major(state_pool[l]), hist_major(state_conv[l]), w, T)
        spool, sconv = hist_major(spool)[None], hist_major(sconv)[None]
        y, sssm = _ssd_decode(cxs, cb, cc, cdt, state_ssm[l:l + 1], w, T)
        xs = _sample_post(xs, y.reshape(DB * T, D_INNER), bp, w, T)
        xp, xs = _ffn(xp3.reshape(B * S, D_MODEL), xs, g2, f2_in, f2_out, gfin, final_norm=last)
        for acc, v in zip(outs, (npool, nconv, nssm, spool, sconv, sssm)):
            acc.append(v)
    stack = lambda vs: vs[0] if len(vs) == 1 else jnp.concatenate(vs, axis=0)
    return (xp.reshape(B, S, D_MODEL), xs.reshape(DB, T, D_MODEL),
            stack(outs[0]), stack(outs[1]), stack(outs[2]), stack(outs[3]), stack(outs[4]), stack(outs[5]))
```

```python
import functools

import jax
import jax.numpy as jnp
from jax import lax
from jax.experimental import pallas as pl
from jax.experimental.pallas import tpu as pltpu

f32 = jnp.float32
bf16 = jnp.bfloat16

D_MODEL = 1024
D_FF = 2816
POOL_WINDOWS = (2, 4, 8, 16)
POOL_WIDTH = D_MODEL
POOL_GROUP = POOL_WIDTH // len(POOL_WINDOWS)
POOL_HIST = max(POOL_WINDOWS) - 1
D_INNER = 2 * D_MODEL
SSD_HEAD_DIM = 64
N_SSD_HEADS = D_INNER // SSD_HEAD_DIM
N_SSD_GROUPS = 4
HEADS_PER_GROUP = N_SSD_HEADS // N_SSD_GROUPS
GROUP_WIDTH = D_INNER // N_SSD_GROUPS
D_STATE = 128
CONV_WIDTH = 4
BC_DIM = N_SSD_GROUPS * D_STATE
CONV_DIM = D_INNER + 2 * BC_DIM
CHUNK = 128
PAST_LEN = 16384
EPS = 1e-6
LOG2_E = 1.4426950408889634

LANES = 128
SUBLANES = 8
MXU_COLS = 256
VMEM_LIMIT_BYTES = 56 * 1024 * 1024

FFN_ROWS = 512
FFN_CHUNK = 256
PROMPT_TILE = 256
SAMPLE_SEQS = 32
SAMPLE_POST_SEQS = 32
DECODE_SEQS = 8
DECODE_UNROLL = 4
POOL_PAD = 16
CONV_PAD = 8
HEAD_LANES = LANES
SLABS_PER_BUF = MXU_COLS // LANES
N_POOL_BUFS = POOL_WIDTH // MXU_COLS
N_CONV_BUFS = CONV_DIM // MXU_COLS

_NT = (((1,), (1,)), ((), ()))
_TN = (((0,), (0,)), ((), ()))


def _rms(x, g):
    return x * lax.rsqrt(jnp.mean(x * x, axis=-1, keepdims=True) + EPS) * g


def _sigmoid(v):
    return 0.5 * jnp.tanh(0.5 * v) + 0.5


def _silu(v):
    h = 0.5 * v
    return h * jnp.tanh(h) + h


def _softplus(v):
    return jnp.maximum(v, 0.0) + jnp.log1p(jnp.exp(-jnp.abs(v)))


def _dot(a, b):
    return jnp.dot(a, b, preferred_element_type=f32)


def _split3(v):
    hi = v.astype(bf16)
    r1 = v - hi.astype(f32)
    mid = r1.astype(bf16)
    lo = (r1 - mid.astype(f32)).astype(bf16)
    return hi, mid, lo


def _const_spec(shape):
    nd = len(shape)
    return pl.BlockSpec(shape, lambda *_: (0,) * nd, pipeline_mode=pl.Buffered(1))


def _params(n_grid):
    return pltpu.CompilerParams(dimension_semantics=("arbitrary",) * n_grid,
                                vmem_limit_bytes=VMEM_LIMIT_BYTES)


def _ffn_kernel(xa_ref, xb_ref, g_ref, win_ref, wout_ref, gfin_ref, oa_ref, ob_ref, hn_s, act_s, *, n_a, final_norm):
    def tile(x_ref, o_ref):
        hn_s[...] = _rms(x_ref[...], g_ref[...]).astype(bf16)
        for c in range(D_FF // FFN_CHUNK):
            lo = c * FFN_CHUNK
            gate = _dot(hn_s[...], win_ref[:, lo:lo + FFN_CHUNK])
            up = _dot(hn_s[...], win_ref[:, D_FF + lo:D_FF + lo + FFN_CHUNK])
            act_s[:, lo:lo + FFN_CHUNK] = (_silu(gate) * up).astype(bf16)
        out = x_ref[...] + 0.5 * _dot(act_s[...], wout_ref[...])
        if final_norm:
            out = _rms(out, gfin_ref[...])
        o_ref[...] = out

    on_a = pl.program_id(0) < n_a
    pl.when(on_a)(lambda: tile(xa_ref, oa_ref))
    pl.when(jnp.logical_not(on_a))(lambda: tile(xb_ref, ob_ref))


def _ffn_specs(rows_a, rows_b):
    assert rows_a % FFN_ROWS == 0 and rows_b % FFN_ROWS == 0
    n_a, n_b = rows_a // FFN_ROWS, rows_b // FFN_ROWS
    a_spec = pl.BlockSpec((FFN_ROWS, D_MODEL), lambda i: (jnp.minimum(i, n_a - 1), 0))
    b_spec = pl.BlockSpec((FFN_ROWS, D_MODEL), lambda i: (jnp.maximum(i - n_a, 0), 0))
    in_specs = [a_spec, b_spec,
                _const_spec((1, D_MODEL)),
                _const_spec((D_MODEL, 2 * D_FF)),
                _const_spec((D_FF, D_MODEL)),
                _const_spec((1, D_MODEL))]
    out_shapes = [jax.ShapeDtypeStruct((rows_a, D_MODEL), f32), jax.ShapeDtypeStruct((rows_b, D_MODEL), f32)]
    scratch = [pltpu.VMEM((FFN_ROWS, D_MODEL), bf16), pltpu.VMEM((FFN_ROWS, D_FF), bf16)]
    return n_a, n_b, in_specs, [a_spec, b_spec], out_shapes, scratch


def _ffn(xa, xb, g, w_in_b, w_out_b, gfin, *, final_norm):
    n_a, n_b, in_specs, out_specs, out_shapes, scratch = _ffn_specs(xa.shape[0], xb.shape[0])
    return pl.pallas_call(
        functools.partial(_ffn_kernel, n_a=n_a, final_norm=final_norm),
        grid=(n_a + n_b,),
        in_specs=in_specs, out_specs=tuple(out_specs), out_shape=tuple(out_shapes), scratch_shapes=scratch,
        compiler_params=_params(1),
        name="ffn_final" if final_norm else "ffn",
    )(xa, xb, g, w_in_b, w_out_b, gfin)


_IN_PROJ_SPLITS = (POOL_WIDTH, POOL_WIDTH + D_INNER, POOL_WIDTH + D_INNER + CONV_DIM,
                   POOL_WIDTH + D_INNER + CONV_DIM + N_SSD_HEADS)
BF16_SUBLANES = 2 * SUBLANES


def _cast_block(n_rows, n_steps):
    period = 1
    while (n_rows * period) % n_steps or ((n_rows * period) // n_steps) % BF16_SUBLANES:
        period *= 2
        assert period <= n_steps
    return (n_rows * period) // n_steps, period


_PIECE_TILES = (POOL_WIDTH // MXU_COLS, D_INNER // MXU_COLS, CONV_DIM // MXU_COLS, 2 * D_MODEL // MXU_COLS)
_GATE_SKEW = _IN_PROJ_SPLITS[3] - _IN_PROJ_SPLITS[2]


def _store_when(cond, ref, val):
    @pl.when(cond)
    def _():
        ref[...] = val


def _ffn_cast_kernel(xa_ref, xb_ref, g_ref, win_ref, wout_ref, gfin_ref, wt_ref, wdt_f, wg_lo, wg_hi, *rest,
                     n_a, n_plain):
    plain_f = rest[:n_plain]
    oa_ref, ob_ref = rest[n_plain:n_plain + 2]
    wu_o, wz_o, wxbc_o, wdt_o, wgate_o = rest[n_plain + 2:n_plain + 7]
    plain_o = rest[n_plain + 7:2 * n_plain + 7]
    hn_s, act_s = rest[2 * n_plain + 7:]
    _ffn_kernel(xa_ref, xb_ref, g_ref, win_ref, wout_ref, gfin_ref, oa_ref, ob_ref, hn_s, act_s,
                n_a=n_a, final_norm=False)
    i = pl.program_id(0)
    n_u, n_z, n_x, n_g = _PIECE_TILES
    front = n_u + n_z + n_x

    @pl.when(i < front)
    def _():
        t = wt_ref[...].T.astype(bf16)
        _store_when(i < n_u, wu_o, t)
        _store_when(jnp.logical_and(i >= n_u, i < n_u + n_z), wz_o, t)
        _store_when(i >= n_u + n_z, wxbc_o, t)

    @pl.when(jnp.logical_and(i >= front, i < front + n_g))
    def _():
        rows = jnp.concatenate([wg_lo[_GATE_SKEW:, :], wg_hi[0:_GATE_SKEW, :]], axis=0)
        wgate_o[...] = rows.T.astype(bf16)

    @pl.when(i == 0)
    def _():
        t = wdt_f[...].T
        lane = lax.broadcasted_iota(jnp.int32, t.shape, 1)
        wdt_o[...] = jnp.where(lane < N_SSD_HEADS, t, 0.0).astype(bf16)

    @pl.when(i < n_a)
    def _():
        for src, dst in zip(plain_f, plain_o):
            dst[...] = src[...].astype(bf16)


def _ffn_and_casts(xa, xb, g, w_in_b, w_out_b, gfin, w_in_t, plain):
    n_a, n_b, in_specs, out_specs, out_shapes, scratch = _ffn_specs(xa.shape[0], xb.shape[0])
    n_u, n_z, n_x, n_g = _PIECE_TILES
    front = n_u + n_z + n_x
    assert n_a >= front + n_g and _IN_PROJ_SPLITS[2] == front * MXU_COLS and _IN_PROJ_SPLITS[2] % HEAD_LANES == 0

    def blocked(arr):
        blk, period = _cast_block(arr.shape[0], n_a)
        return (pl.BlockSpec((blk, arr.shape[1]), lambda i: (jnp.minimum(i, n_a - 1) // period, 0)),
                jax.ShapeDtypeStruct(arr.shape, bf16))

    def piece(first_step, n_tiles):
        return (pl.BlockSpec((D_MODEL, MXU_COLS), lambda i: (0, jnp.clip(i - first_step, 0, n_tiles - 1))),
                jax.ShapeDtypeStruct((D_MODEL, n_tiles * MXU_COLS), bf16))

    pieces = [piece(0, n_u), piece(n_u, n_z), piece(n_u + n_z, n_x),
              (pl.BlockSpec((D_MODEL, HEAD_LANES), lambda i: (0, 0)), jax.ShapeDtypeStruct((D_MODEL, HEAD_LANES), bf16)),
              piece(front, n_g)]
    in_proj_specs = [pl.BlockSpec((MXU_COLS, D_MODEL), lambda i: (jnp.minimum(i, front - 1), 0)),
                     pl.BlockSpec((HEAD_LANES, D_MODEL), lambda i: (_IN_PROJ_SPLITS[2] // HEAD_LANES, 0)),
                     pl.BlockSpec((MXU_COLS, D_MODEL), lambda i: (front + jnp.clip(i - front, 0, n_g - 1), 0)),
                     pl.BlockSpec((MXU_COLS, D_MODEL), lambda i: (front + 1 + jnp.clip(i - front, 0, n_g - 1), 0))]
    plains = [blocked(p) for p in plain]
    res = pl.pallas_call(
        functools.partial(_ffn_cast_kernel, n_a=n_a, n_plain=len(plain)),
        grid=(n_a + n_b,),
        in_specs=in_specs + in_proj_specs + [s for s, _ in plains],
        out_specs=tuple(out_specs + [s for s, _ in pieces] + [s for s, _ in plains]),
        out_shape=tuple(out_shapes + [o for _, o in pieces] + [o for _, o in plains]),
        scratch_shapes=scratch,
        compiler_params=_params(1),
        name="ffn_casts",
    )(xa, xb, g, w_in_b, w_out_b, gfin, w_in_t, w_in_t, w_in_t, w_in_t, *plain)
    return res[0], res[1], res[2:7], res[7:]


def _slab(bufs, j):
    return bufs[j // SLABS_PER_BUF], j % SLABS_PER_BUF


def _window_scratch(n_bufs, n_seqs, n_rows):
    return [pltpu.VMEM((SLABS_PER_BUF, n_seqs, n_rows, LANES), f32) for _ in range(n_bufs)]


def _store_rows(bufs, row0, val, T, slab0=0):
    for j in range(val.shape[1] // LANES):
        ref, k = _slab(bufs, slab0 + j)
        S = ref.shape[1]
        blk = val[:, j * LANES:(j + 1) * LANES]
        if S == 1:
            ref[k, 0, row0:row0 + T, :] = blk
        else:
            ref[k, :, row0:row0 + T, :] = blk.reshape(S, T, LANES)


def _window(bufs, j, row0, n):
    ref, k = _slab(bufs, j)
    S = ref.shape[1]
    rows = pl.ds(row0, n) if row0 % SUBLANES == 0 else pl.ds(row0, n, stride=1)
    if S == 1:
        return ref[k, 0, rows, :]
    v = ref[k, :, rows, :]
    return v.reshape(S * n, LANES) if n % SUBLANES == 0 else v


def _pool_branch(hn_s, wu_ref, ubufs, T, pos0, pgw_ref, pscale_ref, pwo_ref):
    S = ubufs[0].shape[1]
    assert T & (T - 1) == 0 and POOL_GROUP == MXU_COLS
    t_idx = jnp.bitwise_and(lax.broadcasted_iota(jnp.int32, (S * T, 1), 0), T - 1)
    n_seen = pos0 + 1 + t_idx
    _store_rows(ubufs, POOL_PAD, _dot(hn_s[...], wu_ref[...]), T)
    mixed = []
    for gi, w in enumerate(POOL_WINDOWS):
        cnt = jnp.minimum(n_seen, w).astype(f32)
        d = []
        for j in range(gi * SLABS_PER_BUF, (gi + 1) * SLABS_PER_BUF):
            cur = _window(ubufs, j, POOL_PAD, T)
            s = cur
            for k in range(1, w):
                s = s + _window(ubufs, j, POOL_PAD - k, T)
            d.append((s / cnt - cur).astype(bf16))
        mixed.append(_dot(jnp.concatenate(d, axis=-1), pgw_ref[gi]))
    y = jnp.concatenate(mixed, axis=-1) * pscale_ref[...]
    return _dot(y.astype(bf16), pwo_ref[...])


def _conv_slab(xbufs, T, convw_ref, convb_ref, j, xs_ref, b_ref, c_ref):
    c0 = j * LANES
    cols = slice(c0, c0 + LANES)
    y = convb_ref[:, cols]
    for k in range(CONV_WIDTH):
        y = y + _window(xbufs, j, CONV_PAD - (CONV_WIDTH - 1) + k, T) * convw_ref[k:k + 1, cols]
    v = _silu(y)
    if c0 < D_INNER:
        xs_ref[:, cols] = v
    elif c0 < D_INNER + BC_DIM:
        b_ref[:, c0 - D_INNER:c0 - D_INNER + LANES] = v
    else:
        c_ref[:, c0 - D_INNER - BC_DIM:c0 - D_INNER - BC_DIM + LANES] = v


def _project_conv(hn_s, wxbc_ref, xbufs, T, convw_ref, convb_ref, xs_ref, b_ref, c_ref):
    for jt in range(N_CONV_BUFS):
        slab0 = jt * SLABS_PER_BUF
        _store_rows(xbufs, CONV_PAD, _dot(hn_s[...], wxbc_ref[:, jt * MXU_COLS:(jt + 1) * MXU_COLS]), T, slab0=slab0)
        for j in range(slab0, slab0 + SLABS_PER_BUF):
            _conv_slab(xbufs, T, convw_ref, convb_ref, j, xs_ref, b_ref, c_ref)


def _gate_merge_out(x, hn_s, y_ref, bp, wz_ref, wgate_ref, snorm_ref, swo_ref, wo_ref, yn_s):
    for g in range(N_SSD_GROUPS):
        cols = slice(g * GROUP_WIDTH, (g + 1) * GROUP_WIDTH)
        z = _dot(hn_s[...], wz_ref[:, cols])
        yg = y_ref[:, cols] * _silu(z)
        yg = yg * lax.rsqrt(jnp.mean(yg * yg, axis=-1, keepdims=True) + EPS)
        yn_s[:, cols] = (yg * snorm_ref[:, cols]).astype(bf16)
    branch_ssd = _dot(yn_s[...], swo_ref[...])
    gate_pool = _sigmoid(_dot(hn_s[...], wgate_ref[:, 0:D_MODEL]))
    gate_ssd = _sigmoid(_dot(hn_s[...], wgate_ref[:, D_MODEL:2 * D_MODEL]))
    merged = (gate_pool * bp + gate_ssd * branch_ssd).astype(bf16)
    return x + _dot(merged, wo_ref[...])


def _ssd_chunk(rows, xs_s, b_s, c_s, dt_s, y_s, ht_s, alog_ref, dskip_ref, e_ref):
    L = CHUNK
    ri = lax.broadcasted_iota(jnp.int32, (L, L), 0)
    ci = lax.broadcasted_iota(jnp.int32, (L, L), 1)
    causal = ri >= ci
    tril = jnp.where(causal, 1.0, 0.0).astype(bf16)
    first_head = lax.broadcasted_iota(jnp.int32, (L, LANES), 1) < SSD_HEAD_DIM

    dt = dt_s[rows, :]
    dA = dt * (-jnp.exp(alog_ref[...]))
    hi, mid, lo = _split3(dA)
    acc = _dot(tril, jnp.concatenate([hi, mid, lo], axis=1))
    a = (acc[:, 0:LANES] + acc[:, LANES:2 * LANES] + acc[:, 2 * LANES:3 * LANES]) * LOG2_E
    aT = a.T[0:N_SSD_HEADS]
    dtT = dt.T[0:N_SSD_HEADS]
    wT = jnp.exp2(aT[:, L - 1:L] - aT) * dtT
    srcT = aT - jnp.log2(dtT)
    end_decay = jnp.broadcast_to(jnp.exp2(a[L - 1:L, :]), (2 * SUBLANES, HEAD_LANES))
    e_hi, e_mid, e_lo = (t.astype(f32) for t in _split3(end_decay))
    sel = lax.broadcasted_iota(jnp.int32, (2 * SUBLANES, HEAD_LANES), 0)
    stacked = jnp.where(sel == 0, e_hi, jnp.where(sel == 1, e_mid, jnp.where(sel == 2, e_lo, 0.0)))
    cdec = jnp.sum(_dot(stacked.astype(bf16), e_ref[...]), axis=0, keepdims=True)

    def head_mats(h, cb, bT):
        acol = jnp.broadcast_to(a[:, h:h + 1], (L, L))
        m = (cb * jnp.exp2(jnp.where(causal, acol - srcT[h:h + 1, :], -jnp.inf))).astype(bf16)
        bw = (bT * wT[h:h + 1, :]).astype(bf16)
        return m, bw, jnp.exp2(acol)

    pairs_per_group = HEADS_PER_GROUP // 2
    for q in range(N_SSD_HEADS // 2):
        g, qg = divmod(q, pairs_per_group)
        if qg == 0:
            gcols = slice(g * D_STATE, (g + 1) * D_STATE)
            b_f = b_s[rows, gcols]
            c_b = c_s[rows, gcols].astype(bf16)
            cb = lax.dot_general(c_b, b_f.astype(bf16), _NT, preferred_element_type=f32)
            bT = b_f.T
            ch = _dot(c_b, ht_s[:, g * GROUP_WIDTH:(g + 1) * GROUP_WIDTH].astype(bf16))
        cols = slice(q * LANES, (q + 1) * LANES)
        xq = xs_s[rows, cols]
        x2 = jnp.concatenate([jnp.where(first_head, xq, 0.0), jnp.where(first_head, 0.0, xq)], axis=0).astype(bf16)
        m_a, bw_a, ea_a = head_mats(2 * q, cb, bT)
        m_b, bw_b, ea_b = head_mats(2 * q + 1, cb, bT)
        y_off = ch[:, qg * LANES:(qg + 1) * LANES] * jnp.where(first_head, ea_a, ea_b)
        y_s[rows, cols] = _dot(jnp.concatenate([m_a, m_b], axis=1), x2) + y_off + dskip_ref[:, cols] * xq
        st = _dot(jnp.concatenate([bw_a, bw_b], axis=1), x2)
        ht_s[:, cols] = ht_s[:, cols] * cdec[:, cols] + st


def _mixer_prompt_kernel(x_ref, gmix_ref, wu_ref, wz_ref, wxbc_ref, wdt_ref, wgate_ref, pgw_ref, pscale_ref,
                         pwo_ref, convw_ref, convb_ref, dtb_ref, alog_ref, dskip_ref, snorm_ref, swo_ref,
                         wo_ref, e_ref,
                         o_ref, npool_ref, nconv_ref, nssm_ref,
                         hn_s, xs_s, b_s, c_s, dt_s, y_s, ht_s, yn_s, *window_bufs):
    T = PROMPT_TILE
    ti = pl.program_id(1)
    ubufs, xbufs = window_bufs[:N_POOL_BUFS], window_bufs[N_POOL_BUFS:]

    @pl.when(ti == 0)
    def _():
        for ref in ubufs:
            ref[:, :, 0:POOL_PAD, :] = jnp.zeros((SLABS_PER_BUF, 1, POOL_PAD, LANES), f32)
        for ref in xbufs:
            ref[:, :, 0:CONV_PAD, :] = jnp.zeros((SLABS_PER_BUF, 1, CONV_PAD, LANES), f32)
        ht_s[...] = jnp.zeros(ht_s.shape, f32)

    x = x_ref[0]
    hn_s[...] = _rms(x, gmix_ref[...]).astype(bf16)

    bp = _pool_branch(hn_s, wu_ref, ubufs, T, ti * T, pgw_ref, pscale_ref, pwo_ref)

    _project_conv(hn_s, wxbc_ref, xbufs, T, convw_ref, convb_ref, xs_s, b_s, c_s)
    dt_s[...] = _softplus(_dot(hn_s[...], wdt_ref[...]) + dtb_ref[...])

    def chunk(c, carry):
        rows = pl.ds(pl.multiple_of(c * CHUNK, CHUNK), CHUNK)
        _ssd_chunk(rows, xs_s, b_s, c_s, dt_s, y_s, ht_s, alog_ref, dskip_ref, e_ref)
        return carry

    lax.fori_loop(0, T // CHUNK, chunk, 0)

    o_ref[0] = _gate_merge_out(x, hn_s, y_s, bp, wz_ref, wgate_ref, snorm_ref, swo_ref, wo_ref, yn_s)

    @pl.when(ti == pl.num_programs(1) - 1)
    def _():
        for j in range(POOL_WIDTH // LANES):
            npool_ref[0, 0, :, j * LANES:(j + 1) * LANES] = _window(ubufs, j, POOL_PAD + T - POOL_HIST, POOL_HIST)
        for j in range(CONV_DIM // LANES):
            nconv_ref[0, 0, :, j * LANES:(j + 1) * LANES] = _window(
                xbufs, j, CONV_PAD + T - (CONV_WIDTH - 1), CONV_WIDTH - 1)
        for g in range(N_SSD_GROUPS):
            hg = ht_s[:, g * GROUP_WIDTH:(g + 1) * GROUP_WIDTH].T
            nssm_ref[0, 0, g * HEADS_PER_GROUP:(g + 1) * HEADS_PER_GROUP] = hg.reshape(
                HEADS_PER_GROUP, SSD_HEAD_DIM, D_STATE)

    for ref in ubufs:
        ref[:, :, 0:POOL_PAD, :] = ref[:, :, T:T + POOL_PAD, :]
    for ref in xbufs:
        ref[:, :, 0:CONV_PAD, :] = ref[:, :, T:T + CONV_PAD, :]


def _mixer_prompt(x1, w):
    B, S, _ = x1.shape
    T = PROMPT_TILE
    assert S % T == 0 and T % CHUNK == 0 and T >= POOL_PAD
    consts = [w["gmix"], w["wu"], w["wz"], w["wxbc"], w["wdt"], w["wgate"], w["pgw"], w["pscale"], w["pwo"],
              w["convw"], w["convb"], w["dtb"], w["alog"], w["dskip"], w["snorm"], w["swo"], w["wo"], w["expand"]]
    return pl.pallas_call(
        _mixer_prompt_kernel,
        grid=(B, S // T),
        in_specs=[pl.BlockSpec((1, T, D_MODEL), lambda b, t: (b, t, 0))] + [_const_spec(c.shape) for c in consts],
        out_specs=(pl.BlockSpec((1, T, D_MODEL), lambda b, t: (b, t, 0)),
                   pl.BlockSpec((1, 1, POOL_HIST, POOL_WIDTH), lambda b, t: (0, b, 0, 0)),
                   pl.BlockSpec((1, 1, CONV_WIDTH - 1, CONV_DIM), lambda b, t: (0, b, 0, 0)),
                   pl.BlockSpec((1, 1, N_SSD_HEADS, SSD_HEAD_DIM, D_STATE), lambda b, t: (0, b, 0, 0, 0))),
        out_shape=(jax.ShapeDtypeStruct((B, S, D_MODEL), f32),
                   jax.ShapeDtypeStruct((1, B, POOL_HIST, POOL_WIDTH), f32),
                   jax.ShapeDtypeStruct((1, B, CONV_WIDTH - 1, CONV_DIM), f32),
                   jax.ShapeDtypeStruct((1, B, N_SSD_HEADS, SSD_HEAD_DIM, D_STATE), f32)),
        scratch_shapes=[pltpu.VMEM((T, D_MODEL), bf16),
                        pltpu.VMEM((T, D_INNER), f32),
                        pltpu.VMEM((T, BC_DIM), f32),
                        pltpu.VMEM((T, BC_DIM), f32),
                        pltpu.VMEM((T, HEAD_LANES), f32),
                        pltpu.VMEM((T, D_INNER), f32),
                        pltpu.VMEM((D_STATE, D_INNER), f32),
                        pltpu.VMEM((T, D_INNER), bf16)]
                       + _window_scratch(N_POOL_BUFS, 1, POOL_PAD + T)
                       + _window_scratch(N_CONV_BUFS, 1, CONV_PAD + T),
        compiler_params=_params(2),
        name="mixer_prompt",
    )(x1, *consts)


def _load_history(bufs, hist_ref, row0):
    for j in range(hist_ref.shape[2] // LANES):
        ref, k = _slab(bufs, j)
        n_slabs, S, n_rows, _ = ref.shape
        flat = ref.reshape(n_slabs, S * n_rows, LANES)
        for r in range(hist_ref.shape[0]):
            flat[k, pl.ds(row0 + r, S, stride=n_rows), :] = hist_ref[r, :, j * LANES:(j + 1) * LANES]


def _store_history(hist_ref, bufs, row0):
    for j in range(hist_ref.shape[2] // LANES):
        ref, k = _slab(bufs, j)
        n_slabs, S, n_rows, _ = ref.shape
        flat = ref.reshape(n_slabs, S * n_rows, LANES)
        for r in range(hist_ref.shape[0]):
            hist_ref[r, :, j * LANES:(j + 1) * LANES] = flat[k, pl.ds(row0 + r, S, stride=n_rows), :]


def _sample_pre_kernel(x_ref, ph_ref, ch_ref, gmix_ref, wu_ref, wxbc_ref, wdt_ref, pgw_ref, pscale_ref, pwo_ref,
                       convw_ref, convb_ref, dtb_ref,
                       bp_ref, xs_ref, b_ref, c_ref, dt_ref, npool_ref, nconv_ref,
                       *window_bufs, T):
    ubufs, xbufs = window_bufs[:N_POOL_BUFS], window_bufs[N_POOL_BUFS:]
    hn = _rms(x_ref[...], gmix_ref[...]).astype(bf16)

    _load_history(ubufs, ph_ref, POOL_PAD - POOL_HIST)
    bp_ref[...] = _pool_branch(hn, wu_ref, ubufs, T, PAST_LEN, pgw_ref, pscale_ref, pwo_ref)
    _store_history(npool_ref, ubufs, POOL_PAD + T - POOL_HIST)

    _load_history(xbufs, ch_ref, CONV_PAD - (CONV_WIDTH - 1))
    _project_conv(hn, wxbc_ref, xbufs, T, convw_ref, convb_ref, xs_ref, b_ref, c_ref)
    _store_history(nconv_ref, xbufs, CONV_PAD + T - (CONV_WIDTH - 1))
    dt_ref[...] = _softplus(_dot(hn, wdt_ref[...]) + dtb_ref[...])


def _sample_pre(x1, pool_hist, conv_hist, w, T):
    rows = x1.shape[0]
    nseq = rows // T
    S = SAMPLE_SEQS
    R = S * T
    assert nseq % S == 0 and T % SUBLANES == 0
    consts = [w["gmix"], w["wu"], w["wxbc"], w["wdt"], w["pgw"], w["pscale"], w["pwo"], w["convw"], w["convb"],
              w["dtb"]]
    row_spec = lambda width: pl.BlockSpec((R, width), lambda i: (i, 0))
    return pl.pallas_call(
        functools.partial(_sample_pre_kernel, T=T),
        grid=(nseq // S,),
        in_specs=[row_spec(D_MODEL),
                  pl.BlockSpec((POOL_HIST, S, POOL_WIDTH), lambda i: (0, i, 0)),
                  pl.BlockSpec((CONV_WIDTH - 1, S, CONV_DIM), lambda i: (0, i, 0))]
                 + [_const_spec(c.shape) for c in consts],
        out_specs=(row_spec(D_MODEL), row_spec(D_INNER), row_spec(BC_DIM), row_spec(BC_DIM), row_spec(HEAD_LANES),
                   pl.BlockSpec((POOL_HIST, S, POOL_WIDTH), lambda i: (0, i, 0)),
                   pl.BlockSpec((CONV_WIDTH - 1, S, CONV_DIM), lambda i: (0, i, 0))),
        out_shape=(jax.ShapeDtypeStruct((rows, D_MODEL), f32),
                   jax.ShapeDtypeStruct((rows, D_INNER), f32),
                   jax.ShapeDtypeStruct((rows, BC_DIM), f32),
                   jax.ShapeDtypeStruct((rows, BC_DIM), f32),
                   jax.ShapeDtypeStruct((rows, HEAD_LANES), f32),
                   jax.ShapeDtypeStruct((POOL_HIST, nseq, POOL_WIDTH), f32),
                   jax.ShapeDtypeStruct((CONV_WIDTH - 1, nseq, CONV_DIM), f32)),
        scratch_shapes=_window_scratch(N_POOL_BUFS, S, POOL_PAD + T) + _window_scratch(N_CONV_BUFS, S, CONV_PAD + T),
        compiler_params=_params(1),
        name="sample_pre",
    )(x1, pool_hist, conv_hist, *consts)


def _ssd_decode_kernel(xs_ref, b_ref, c_ref, dt_ref, h0_ref, alog_ref, dskip_ref, e_ref, y_ref, hn_ref, *, T):
    assert T == SUBLANES
    lane_group = lax.broadcasted_iota(jnp.int32, (T, HEAD_LANES), 1) // HEADS_PER_GROUP
    neg_a = -jnp.exp(alog_ref[...])

    def shift(v, d):
        if d == 0:
            return v
        r = lax.broadcasted_iota(jnp.int32, v.shape, 0)
        return jnp.where(r >= d, pltpu.roll(v, d, axis=0), 0.0)

    def per_seq(s, carry):
        dt = dt_ref[s]
        a = dt * neg_a
        d = 1
        while d < T:
            a = a + shift(a, d)
            d *= 2
        a_end = a[T - 1:T, :]
        x = xs_ref[s]
        bm = b_ref[s]
        cm = c_ref[s]
        terms = []
        for d in range(T):
            cbv = jnp.zeros((T, HEAD_LANES), f32)
            for g in range(N_SSD_GROUPS):
                gc = slice(g * D_STATE, (g + 1) * D_STATE)
                cb = jnp.sum(cm[:, gc] * shift(bm[:, gc], d), axis=-1, keepdims=True)
                cbv = jnp.where(lane_group == g, cb, cbv)
            terms.append(cbv * jnp.exp(a - shift(a, d)) * shift(dt, d))
        terms.append(jnp.exp(a))
        terms.append(jnp.exp(a_end - a) * dt)
        v = jnp.concatenate(terms, axis=0)
        v_hi = v.astype(bf16)
        v_lo = (v - v_hi.astype(f32)).astype(bf16)
        ex = _dot(jnp.concatenate([v_hi, v_lo], axis=0), e_ref[...])
        n = (T + 2) * T
        ex = ex[0:n] + ex[n:2 * n]

        y = dskip_ref[...] * x
        for d in range(T):
            y = y + ex[d * T:(d + 1) * T] * shift(x, d)
        ea = ex[T * T:(T + 1) * T]
        xw = (x * ex[(T + 1) * T:(T + 2) * T]).astype(bf16)
        cdec = jnp.exp(a_end)
        y_groups = []
        for g in range(N_SSD_GROUPS):
            gc = slice(g * D_STATE, (g + 1) * D_STATE)
            cols = slice(g * GROUP_WIDTH, (g + 1) * GROUP_WIDTH)
            heads = slice(g * HEADS_PER_GROUP, (g + 1) * HEADS_PER_GROUP)
            h0g = h0_ref[0, s, heads].reshape(GROUP_WIDTH, D_STATE)
            ch = lax.dot_general(cm[:, gc].astype(bf16), h0g.astype(bf16), _NT, preferred_element_type=f32)
            y_groups.append(y[:, cols] + ea[:, cols] * ch)
            st = lax.dot_general(xw[:, cols], bm[:, gc].astype(bf16), _TN, preferred_element_type=f32)
            for r in range(HEADS_PER_GROUP):
                h = g * HEADS_PER_GROUP + r
                hn_ref[0, s, h] = (h0_ref[0, s, h] * cdec[:, h:h + 1]
                                   + st[r * SSD_HEAD_DIM:(r + 1) * SSD_HEAD_DIM, :])
        y_ref[s] = jnp.concatenate(y_groups, axis=-1)
        return carry

    lax.fori_loop(0, DECODE_SEQS, per_seq, 0, unroll=DECODE_UNROLL)


def _ssd_decode(xs, bm, cm, dt, h0, w, T):
    nseq = xs.shape[0] // T
    Q = DECODE_SEQS
    assert nseq % Q == 0
    seq_spec = lambda width: pl.BlockSpec((Q, T, width), lambda i: (i, 0, 0))
    state_spec = pl.BlockSpec((1, Q, N_SSD_HEADS, SSD_HEAD_DIM, D_STATE), lambda i: (0, i, 0, 0, 0))
    consts = [w["alog"], w["dskip"], w["expand"]]
    return pl.pallas_call(
        functools.partial(_ssd_decode_kernel, T=T),
        grid=(nseq // Q,),
        in_specs=[seq_spec(D_INNER), seq_spec(BC_DIM), seq_spec(BC_DIM), seq_spec(HEAD_LANES), state_spec]
                 + [_const_spec(c.shape) for c in consts],
        out_specs=(seq_spec(D_INNER), state_spec),
        out_shape=(jax.ShapeDtypeStruct((nseq, T, D_INNER), f32), jax.ShapeDtypeStruct(h0.shape, f32)),
        compiler_params=_params(1),
        name="ssd_decode",
    )(xs.reshape(nseq, T, D_INNER), bm.reshape(nseq, T, BC_DIM), cm.reshape(nseq, T, BC_DIM),
      dt.reshape(nseq, T, HEAD_LANES), h0, *consts)


def _sample_post_kernel(x_ref, y_ref, bp_ref, gmix_ref, wz_ref, wgate_ref, snorm_ref, swo_ref, wo_ref,
                        o_ref, hn_s, yn_s):
    x = x_ref[...]
    hn_s[...] = _rms(x, gmix_ref[...]).astype(bf16)
    o_ref[...] = _gate_merge_out(x, hn_s, y_ref, bp_ref[...], wz_ref, wgate_ref, snorm_ref, swo_ref, wo_ref, yn_s)


def _sample_post(x1, y, bp, w, T):
    rows = x1.shape[0]
    R = SAMPLE_POST_SEQS * T
    assert rows % R == 0
    consts = [w["gmix"], w["wz"], w["wgate"], w["snorm"], w["swo"], w["wo"]]
    row_spec = lambda width: pl.BlockSpec((R, width), lambda i: (i, 0))
    return pl.pallas_call(
        _sample_post_kernel,
        grid=(rows // R,),
        in_specs=[row_spec(D_MODEL), row_spec(D_INNER), row_spec(D_MODEL)] + [_const_spec(c.shape) for c in consts],
        out_specs=row_spec(D_MODEL),
        out_shape=jax.ShapeDtypeStruct((rows, D_MODEL), f32),
        scratch_shapes=[pltpu.VMEM((R, D_MODEL), bf16), pltpu.VMEM((R, D_INNER), bf16)],
        compiler_params=_params(1),
        name="sample_post",
    )(x1, y, bp, *consts)


def _small_params(l, norm_mix, pool_scale, conv_w, conv_b, dt_bias, a_log, d_skip, ssd_norm):
    pad_heads = lambda v: jnp.pad(v, ((0, 0), (0, HEAD_LANES - N_SSD_HEADS)))
    head_of_lane = jnp.arange(D_INNER, dtype=jnp.int32) // SSD_HEAD_DIM
    expand = (jnp.arange(HEAD_LANES, dtype=jnp.int32)[:, None] == head_of_lane[None, :]).astype(bf16)
    return dict(
        gmix=norm_mix[l][None, :], pscale=pool_scale[l][None, :], convw=conv_w[l], convb=conv_b[l][None, :],
        dtb=pad_heads(dt_bias[l][None, :]), alog=pad_heads(a_log[l][None, :]),
        dskip=jnp.repeat(d_skip[l], SSD_HEAD_DIM)[None, :], snorm=ssd_norm[l][None, :], expand=expand)


def kernel(x_prompt, x_sample, state_pool, state_conv, state_ssm, norm_ffn1, ffn1_w_in, ffn1_w_out, norm_mix, w_in,
           pool_w_group, pool_scale, pool_w_out, conv_w, conv_b, dt_bias, a_log, d_skip, ssd_norm, ssd_w_out, w_o,
           norm_ffn2, ffn2_w_in, ffn2_w_out, norm_final):
    depth = w_in.shape[0]
    B, S, _ = x_prompt.shape
    DB, T, _ = x_sample.shape
    gfin = norm_final[None, :]
    xp = x_prompt.reshape(B * S, D_MODEL)
    xs = x_sample.reshape(DB * T, D_MODEL)
    outs = [[] for _ in range(6)]
    for l in range(depth):
        last = l == depth - 1
        f1_in, f1_out = ffn1_w_in[l].astype(bf16), ffn1_w_out[l].astype(bf16)
        g1, g2 = norm_ffn1[l][None, :], norm_ffn2[l][None, :]
        w = _small_params(l, norm_mix, pool_scale, conv_w, conv_b, dt_bias, a_log, d_skip, ssd_norm)
        n_pg = pool_w_group.shape[1]
        xp, xs, in_proj, casts = _ffn_and_casts(
            xp, xs, g1, f1_in, f1_out, gfin, w_in[l].T,
            [pool_w_group[l].reshape(n_pg * POOL_GROUP, POOL_GROUP), pool_w_out[l], ssd_w_out[l], w_o[l],
             ffn2_w_in[l], ffn2_w_out[l]])
        w.update(zip(("wu", "wz", "wxbc", "wdt", "wgate"), in_proj))
        w.update(pgw=casts[0].reshape(n_pg, POOL_GROUP, POOL_GROUP), pwo=casts[1], swo=casts[2], wo=casts[3])
        f2_in, f2_out = casts[4], casts[5]
        xp3, npool, nconv, nssm = _mixer_prompt(xp.reshape(B, S, D_MODEL), w)
        hist_major = lambda v: jnp.transpose(v, (1, 0, 2))
        bp, cxs, cb, cc, cdt, spool, sconv = _sample_pre(xs, hist_major(state_pool[l]), hist_major(state_conv[l]), w, T)
        spool, sconv = hist_major(spool)[None], hist_major(sconv)[None]
        y, sssm = _ssd_decode(cxs, cb, cc, cdt, state_ssm[l:l + 1], w, T)
        xs = _sample_post(xs, y.reshape(DB * T, D_INNER), bp, w, T)
        xp, xs = _ffn(xp3.reshape(B * S, D_MODEL), xs, g2, f2_in, f2_out, gfin, final_norm=last)
        for acc, v in zip(outs, (npool, nconv, nssm, spool, sconv, sssm)):
            acc.append(v)
    stack = lambda vs: vs[0] if len(vs) == 1 else jnp.concatenate(vs, axis=0)
    return (xp.reshape(B, S, D_MODEL), xs.reshape(DB, T, D_MODEL),
            stack(outs[0]), stack(outs[1]), stack(outs[2]), stack(outs[3]), stack(outs[4]), stack(outs[5]))
```

```python
import functools

import jax
import jax.numpy as jnp
from jax import lax
from jax.experimental import pallas as pl
from jax.experimental.pallas import tpu as pltpu

f32 = jnp.float32
bf16 = jnp.bfloat16

D_MODEL = 1024
D_FF = 2816
POOL_WINDOWS = (2, 4, 8, 16)
POOL_WIDTH = D_MODEL
POOL_GROUP = POOL_WIDTH // len(POOL_WINDOWS)
POOL_HIST = max(POOL_WINDOWS) - 1
D_INNER = 2 * D_MODEL
SSD_HEAD_DIM = 64
N_SSD_HEADS = D_INNER // SSD_HEAD_DIM
N_SSD_GROUPS = 4
HEADS_PER_GROUP = N_SSD_HEADS // N_SSD_GROUPS
GROUP_WIDTH = D_INNER // N_SSD_GROUPS
D_STATE = 128
CONV_WIDTH = 4
BC_DIM = N_SSD_GROUPS * D_STATE
CONV_DIM = D_INNER + 2 * BC_DIM
CHUNK = 128
PAST_LEN = 16384
EPS = 1e-6
LOG2_E = 1.4426950408889634

LANES = 128
SUBLANES = 8
MXU_COLS = 256
VMEM_LIMIT_BYTES = 56 * 1024 * 1024

FFN_ROWS = 512
FFN_CHUNK = 256
PROMPT_TILE = 256
SAMPLE_SEQS = 32
SAMPLE_POST_SEQS = 32
DECODE_SEQS = 8
DECODE_UNROLL = 4
POOL_PAD = 16
CONV_PAD = 8
HEAD_LANES = LANES
SLABS_PER_BUF = MXU_COLS // LANES
N_POOL_BUFS = POOL_WIDTH // MXU_COLS
N_CONV_BUFS = CONV_DIM // MXU_COLS

_NT = (((1,), (1,)), ((), ()))
_TN = (((0,), (0,)), ((), ()))


def _rms(x, g):
    return x * lax.rsqrt(jnp.mean(x * x, axis=-1, keepdims=True) + EPS) * g


def _sigmoid(v):
    return 0.5 * jnp.tanh(0.5 * v) + 0.5


def _silu(v):
    h = 0.5 * v
    return h * jnp.tanh(h) + h


def _softplus(v):
    return jnp.maximum(v, 0.0) + jnp.log1p(jnp.exp(-jnp.abs(v)))


def _dot(a, b):
    return jnp.dot(a, b, preferred_element_type=f32)


def _split3(v):
    hi = v.astype(bf16)
    r1 = v - hi.astype(f32)
    mid = r1.astype(bf16)
    lo = (r1 - mid.astype(f32)).astype(bf16)
    return hi, mid, lo


def _const_spec(shape):
    nd = len(shape)
    return pl.BlockSpec(shape, lambda *_: (0,) * nd, pipeline_mode=pl.Buffered(1))


def _params(n_grid):
    return pltpu.CompilerParams(dimension_semantics=("arbitrary",) * n_grid,
                                vmem_limit_bytes=VMEM_LIMIT_BYTES)


def _ffn_kernel(xa_ref, xb_ref, g_ref, win_ref, wout_ref, gfin_ref, oa_ref, ob_ref, hn_s, act_s, *, n_a, final_norm,
                side_work=None):
    def tile(x_ref, o_ref, extra=None):
        commit = extra() if extra is not None else None
        hn_s[...] = _rms(x_ref[...], g_ref[...]).astype(bf16)
        for c in range(D_FF // FFN_CHUNK):
            lo = c * FFN_CHUNK
            gate = _dot(hn_s[...], win_ref[:, lo:lo + FFN_CHUNK])
            up = _dot(hn_s[...], win_ref[:, D_FF + lo:D_FF + lo + FFN_CHUNK])
            act_s[:, lo:lo + FFN_CHUNK] = (_silu(gate) * up).astype(bf16)
        out = x_ref[...] + 0.5 * _dot(act_s[...], wout_ref[...])
        if final_norm:
            out = _rms(out, gfin_ref[...])
        o_ref[...] = out
        if commit is not None:
            commit()

    on_a = pl.program_id(0) < n_a
    pl.when(on_a)(lambda: tile(xa_ref, oa_ref, side_work))
    pl.when(jnp.logical_not(on_a))(lambda: tile(xb_ref, ob_ref))


def _ffn_specs(rows_a, rows_b):
    assert rows_a % FFN_ROWS == 0 and rows_b % FFN_ROWS == 0
    n_a, n_b = rows_a // FFN_ROWS, rows_b // FFN_ROWS
    a_spec = pl.BlockSpec((FFN_ROWS, D_MODEL), lambda i: (jnp.minimum(i, n_a - 1), 0))
    b_spec = pl.BlockSpec((FFN_ROWS, D_MODEL), lambda i: (jnp.maximum(i - n_a, 0), 0))
    in_specs = [a_spec, b_spec,
                _const_spec((1, D_MODEL)),
                _const_spec((D_MODEL, 2 * D_FF)),
                _const_spec((D_FF, D_MODEL)),
                _const_spec((1, D_MODEL))]
    out_shapes = [jax.ShapeDtypeStruct((rows_a, D_MODEL), f32), jax.ShapeDtypeStruct((rows_b, D_MODEL), f32)]
    scratch = [pltpu.VMEM((FFN_ROWS, D_MODEL), bf16), pltpu.VMEM((FFN_ROWS, D_FF), bf16)]
    return n_a, n_b, in_specs, [a_spec, b_spec], out_shapes, scratch


def _ffn(xa, xb, g, w_in_b, w_out_b, gfin, *, final_norm):
    n_a, n_b, in_specs, out_specs, out_shapes, scratch = _ffn_specs(xa.shape[0], xb.shape[0])
    return pl.pallas_call(
        functools.partial(_ffn_kernel, n_a=n_a, final_norm=final_norm),
        grid=(n_a + n_b,),
        in_specs=in_specs, out_specs=tuple(out_specs), out_shape=tuple(out_shapes), scratch_shapes=scratch,
        compiler_params=_params(1),
        name="ffn_final" if final_norm else "ffn",
    )(xa, xb, g, w_in_b, w_out_b, gfin)


_IN_PROJ_SPLITS = (POOL_WIDTH, POOL_WIDTH + D_INNER, POOL_WIDTH + D_INNER + CONV_DIM,
                   POOL_WIDTH + D_INNER + CONV_DIM + N_SSD_HEADS)
BF16_SUBLANES = 2 * SUBLANES


def _cast_block(n_rows, n_steps):
    period = 1
    while (n_rows * period) % n_steps or ((n_rows * period) // n_steps) % BF16_SUBLANES:
        period *= 2
        assert period <= n_steps
    return (n_rows * period) // n_steps, period


_PIECE_TILES = (POOL_WIDTH // MXU_COLS, D_INNER // MXU_COLS, CONV_DIM // MXU_COLS, 2 * D_MODEL // MXU_COLS)
_GATE_SKEW = _IN_PROJ_SPLITS[3] - _IN_PROJ_SPLITS[2]


def _store_when(cond, ref, val):
    @pl.when(cond)
    def _():
        ref[...] = val


def _ffn_cast_kernel(xa_ref, xb_ref, g_ref, win_ref, wout_ref, gfin_ref, wt_ref, wdt_f, wg_lo, wg_hi, *rest,
                     n_a, n_plain):
    plain_f = rest[:n_plain]
    oa_ref, ob_ref = rest[n_plain:n_plain + 2]
    wu_o, wz_o, wxbc_o, wdt_o, wgate_o = rest[n_plain + 2:n_plain + 7]
    plain_o = rest[n_plain + 7:2 * n_plain + 7]
    hn_s, act_s = rest[2 * n_plain + 7:]
    i = pl.program_id(0)
    n_u, n_z, n_x, n_g = _PIECE_TILES
    front = n_u + n_z + n_x

    def casts():
        t = wt_ref[...].T.astype(bf16)
        rows = jnp.concatenate([wg_lo[_GATE_SKEW:, :], wg_hi[0:_GATE_SKEW, :]], axis=0)
        tg = rows.T.astype(bf16)
        for src, dst in zip(plain_f, plain_o):
            dst[...] = src[...].astype(bf16)

        def commit():
            _store_when(i < n_u, wu_o, t)
            _store_when(jnp.logical_and(i >= n_u, i < n_u + n_z), wz_o, t)
            _store_when(jnp.logical_and(i >= n_u + n_z, i < front), wxbc_o, t)
            _store_when(jnp.logical_and(i >= front, i < front + n_g), wgate_o, tg)
        return commit

    _ffn_kernel(xa_ref, xb_ref, g_ref, win_ref, wout_ref, gfin_ref, oa_ref, ob_ref, hn_s, act_s,
                n_a=n_a, final_norm=False, side_work=casts)

    @pl.when(i == 0)
    def _():
        t = wdt_f[...].T
        lane = lax.broadcasted_iota(jnp.int32, t.shape, 1)
        wdt_o[...] = jnp.where(lane < N_SSD_HEADS, t, 0.0).astype(bf16)


def _ffn_and_casts(xa, xb, g, w_in_b, w_out_b, gfin, w_in_t, plain):
    n_a, n_b, in_specs, out_specs, out_shapes, scratch = _ffn_specs(xa.shape[0], xb.shape[0])
    n_u, n_z, n_x, n_g = _PIECE_TILES
    front = n_u + n_z + n_x
    assert n_a >= front + n_g and _IN_PROJ_SPLITS[2] == front * MXU_COLS and _IN_PROJ_SPLITS[2] % HEAD_LANES == 0

    def blocked(arr):
        blk, period = _cast_block(arr.shape[0], n_a)
        return (pl.BlockSpec((blk, arr.shape[1]), lambda i: (jnp.minimum(i, n_a - 1) // period, 0)),
                jax.ShapeDtypeStruct(arr.shape, bf16))

    def piece(first_step, n_tiles):
        return (pl.BlockSpec((D_MODEL, MXU_COLS), lambda i: (0, jnp.clip(i - first_step, 0, n_tiles - 1))),
                jax.ShapeDtypeStruct((D_MODEL, n_tiles * MXU_COLS), bf16))

    pieces = [piece(0, n_u), piece(n_u, n_z), piece(n_u + n_z, n_x),
              (pl.BlockSpec((D_MODEL, HEAD_LANES), lambda i: (0, 0)), jax.ShapeDtypeStruct((D_MODEL, HEAD_LANES), bf16)),
              piece(front, n_g)]
    in_proj_specs = [pl.BlockSpec((MXU_COLS, D_MODEL), lambda i: (jnp.minimum(i, front - 1), 0)),
                     pl.BlockSpec((HEAD_LANES, D_MODEL), lambda i: (_IN_PROJ_SPLITS[2] // HEAD_LANES, 0)),
                     pl.BlockSpec((MXU_COLS, D_MODEL), lambda i: (front + jnp.clip(i - front, 0, n_g - 1), 0)),
                     pl.BlockSpec((MXU_COLS, D_MODEL), lambda i: (front + 1 + jnp.clip(i - front, 0, n_g - 1), 0))]
    plains = [blocked(p) for p in plain]
    res = pl.pallas_call(
        functools.partial(_ffn_cast_kernel, n_a=n_a, n_plain=len(plain)),
        grid=(n_a + n_b,),
        in_specs=in_specs + in_proj_specs + [s for s, _ in plains],
        out_specs=tuple(out_specs + [s for s, _ in pieces] + [s for s, _ in plains]),
        out_shape=tuple(out_shapes + [o for _, o in pieces] + [o for _, o in plains]),
        scratch_shapes=scratch,
        compiler_params=_params(1),
        name="ffn_casts",
    )(xa, xb, g, w_in_b, w_out_b, gfin, w_in_t, w_in_t, w_in_t, w_in_t, *plain)
    return res[0], res[1], res[2:7], res[7:]


def _slab(bufs, j):
    return bufs[j // SLABS_PER_BUF], j % SLABS_PER_BUF


def _window_scratch(n_bufs, n_seqs, n_rows):
    return [pltpu.VMEM((SLABS_PER_BUF, n_seqs, n_rows, LANES), f32) for _ in range(n_bufs)]


def _store_rows(bufs, row0, val, T, slab0=0):
    for j in range(val.shape[1] // LANES):
        ref, k = _slab(bufs, slab0 + j)
        S = ref.shape[1]
        blk = val[:, j * LANES:(j + 1) * LANES]
        if S == 1:
            ref[k, 0, row0:row0 + T, :] = blk
        else:
            ref[k, :, row0:row0 + T, :] = blk.reshape(S, T, LANES)


def _window(bufs, j, row0, n):
    ref, k = _slab(bufs, j)
    S = ref.shape[1]
    rows = pl.ds(row0, n) if row0 % SUBLANES == 0 else pl.ds(row0, n, stride=1)
    if S == 1:
        return ref[k, 0, rows, :]
    v = ref[k, :, rows, :]
    return v.reshape(S * n, LANES) if n % SUBLANES == 0 else v


def _pool_branch(hn_s, wu_ref, ubufs, T, pos0, pgw_ref, pscale_ref, pwo_ref):
    S = ubufs[0].shape[1]
    assert T & (T - 1) == 0 and POOL_GROUP == MXU_COLS
    t_idx = jnp.bitwise_and(lax.broadcasted_iota(jnp.int32, (S * T, 1), 0), T - 1)
    n_seen = pos0 + 1 + t_idx
    _store_rows(ubufs, POOL_PAD, _dot(hn_s[...], wu_ref[...]), T)
    mixed = []
    for gi, w in enumerate(POOL_WINDOWS):
        cnt = jnp.minimum(n_seen, w).astype(f32)
        d = []
        for j in range(gi * SLABS_PER_BUF, (gi + 1) * SLABS_PER_BUF):
            cur = _window(ubufs, j, POOL_PAD, T)
            s = cur
            for k in range(1, w):
                s = s + _window(ubufs, j, POOL_PAD - k, T)
            d.append((s / cnt - cur).astype(bf16))
        mixed.append(_dot(jnp.concatenate(d, axis=-1), pgw_ref[gi]))
    y = jnp.concatenate(mixed, axis=-1) * pscale_ref[...]
    return _dot(y.astype(bf16), pwo_ref[...])


def _conv_slab(xbufs, T, convw_ref, convb_ref, j, xs_ref, b_ref, c_ref):
    c0 = j * LANES
    cols = slice(c0, c0 + LANES)
    y = convb_ref[:, cols]
    for k in range(CONV_WIDTH):
        y = y + _window(xbufs, j, CONV_PAD - (CONV_WIDTH - 1) + k, T) * convw_ref[k:k + 1, cols]
    v = _silu(y)
    if c0 < D_INNER:
        xs_ref[:, cols] = v
    elif c0 < D_INNER + BC_DIM:
        b_ref[:, c0 - D_INNER:c0 - D_INNER + LANES] = v
    else:
        c_ref[:, c0 - D_INNER - BC_DIM:c0 - D_INNER - BC_DIM + LANES] = v


def _project_conv(hn_s, wxbc_ref, xbufs, T, convw_ref, convb_ref, xs_ref, b_ref, c_ref):
    for jt in range(N_CONV_BUFS):
        slab0 = jt * SLABS_PER_BUF
        _store_rows(xbufs, CONV_PAD, _dot(hn_s[...], wxbc_ref[:, jt * MXU_COLS:(jt + 1) * MXU_COLS]), T, slab0=slab0)
        for j in range(slab0, slab0 + SLABS_PER_BUF):
            _conv_slab(xbufs, T, convw_ref, convb_ref, j, xs_ref, b_ref, c_ref)


def _gate_merge_out(x, hn_s, y_ref, bp, wz_ref, wgate_ref, snorm_ref, swo_ref, wo_ref, yn_s):
    for g in range(N_SSD_GROUPS):
        cols = slice(g * GROUP_WIDTH, (g + 1) * GROUP_WIDTH)
        z = _dot(hn_s[...], wz_ref[:, cols])
        yg = y_ref[:, cols] * _silu(z)
        yg = yg * lax.rsqrt(jnp.mean(yg * yg, axis=-1, keepdims=True) + EPS)
        yn_s[:, cols] = (yg * snorm_ref[:, cols]).astype(bf16)
    branch_ssd = _dot(yn_s[...], swo_ref[...])
    gate_pool = _sigmoid(_dot(hn_s[...], wgate_ref[:, 0:D_MODEL]))
    gate_ssd = _sigmoid(_dot(hn_s[...], wgate_ref[:, D_MODEL:2 * D_MODEL]))
    merged = (gate_pool * bp + gate_ssd * branch_ssd).astype(bf16)
    return x + _dot(merged, wo_ref[...])


def _ssd_chunk(rows, xs_s, b_s, c_s, dt_s, y_s, ht_s, alog_ref, dskip_ref, e_ref):
    L = CHUNK
    ri = lax.broadcasted_iota(jnp.int32, (L, L), 0)
    ci = lax.broadcasted_iota(jnp.int32, (L, L), 1)
    causal = ri >= ci
    tril = jnp.where(causal, 1.0, 0.0).astype(bf16)
    first_head = lax.broadcasted_iota(jnp.int32, (L, LANES), 1) < SSD_HEAD_DIM

    dt = dt_s[rows, :]
    dA = dt * (-jnp.exp(alog_ref[...]))
    hi, mid, lo = _split3(dA)
    acc = _dot(tril, jnp.concatenate([hi, mid, lo], axis=1))
    a = (acc[:, 0:LANES] + acc[:, LANES:2 * LANES] + acc[:, 2 * LANES:3 * LANES]) * LOG2_E
    aT = a.T[0:N_SSD_HEADS]
    dtT = dt.T[0:N_SSD_HEADS]
    wT = jnp.exp2(aT[:, L - 1:L] - aT) * dtT
    srcT = aT - jnp.log2(dtT)
    end_decay = jnp.broadcast_to(jnp.exp2(a[L - 1:L, :]), (2 * SUBLANES, HEAD_LANES))
    e_hi, e_mid, e_lo = (t.astype(f32) for t in _split3(end_decay))
    sel = lax.broadcasted_iota(jnp.int32, (2 * SUBLANES, HEAD_LANES), 0)
    stacked = jnp.where(sel == 0, e_hi, jnp.where(sel == 1, e_mid, jnp.where(sel == 2, e_lo, 0.0)))
    cdec = jnp.sum(_dot(stacked.astype(bf16), e_ref[...]), axis=0, keepdims=True)

    def head_mats(h, cb, bT):
        acol = jnp.broadcast_to(a[:, h:h + 1], (L, L))
        m = (cb * jnp.exp2(jnp.where(causal, acol - srcT[h:h + 1, :], -jnp.inf))).astype(bf16)
        bw = (bT * wT[h:h + 1, :]).astype(bf16)
        return m, bw, jnp.exp2(acol)

    pairs_per_group = HEADS_PER_GROUP // 2
    for q in range(N_SSD_HEADS // 2):
        g, qg = divmod(q, pairs_per_group)
        if qg == 0:
            gcols = slice(g * D_STATE, (g + 1) * D_STATE)
            b_f = b_s[rows, gcols]
            c_b = c_s[rows, gcols].astype(bf16)
            cb = lax.dot_general(c_b, b_f.astype(bf16), _NT, preferred_element_type=f32)
            bT = b_f.T
            ch = _dot(c_b, ht_s[:, g * GROUP_WIDTH:(g + 1) * GROUP_WIDTH].astype(bf16))
        cols = slice(q * LANES, (q + 1) * LANES)
        xq = xs_s[rows, cols]
        x2 = jnp.concatenate([jnp.where(first_head, xq, 0.0), jnp.where(first_head, 0.0, xq)], axis=0).astype(bf16)
        m_a, bw_a, ea_a = head_mats(2 * q, cb, bT)
        m_b, bw_b, ea_b = head_mats(2 * q + 1, cb, bT)
        y_off = ch[:, qg * LANES:(qg + 1) * LANES] * jnp.where(first_head, ea_a, ea_b)
        y_s[rows, cols] = _dot(jnp.concatenate([m_a, m_b], axis=1), x2) + y_off + dskip_ref[:, cols] * xq
        st = _dot(jnp.concatenate([bw_a, bw_b], axis=1), x2)
        ht_s[:, cols] = ht_s[:, cols] * cdec[:, cols] + st


def _mixer_prompt_kernel(x_ref, gmix_ref, wu_ref, wz_ref, wxbc_ref, wdt_ref, wgate_ref, pgw_ref, pscale_ref,
                         pwo_ref, convw_ref, convb_ref, dtb_ref, alog_ref, dskip_ref, snorm_ref, swo_ref,
                         wo_ref, e_ref,
                         o_ref, npool_ref, nconv_ref, nssm_ref,
                         hn_s, xs_s, b_s, c_s, dt_s, y_s, ht_s, yn_s, *window_bufs):
    T = PROMPT_TILE
    ti = pl.program_id(1)
    ubufs, xbufs = window_bufs[:N_POOL_BUFS], window_bufs[N_POOL_BUFS:]

    @pl.when(ti == 0)
    def _():
        for ref in ubufs:
            ref[:, :, 0:POOL_PAD, :] = jnp.zeros((SLABS_PER_BUF, 1, POOL_PAD, LANES), f32)
        for ref in xbufs:
            ref[:, :, 0:CONV_PAD, :] = jnp.zeros((SLABS_PER_BUF, 1, CONV_PAD, LANES), f32)
        ht_s[...] = jnp.zeros(ht_s.shape, f32)

    x = x_ref[0]
    hn_s[...] = _rms(x, gmix_ref[...]).astype(bf16)

    bp = _pool_branch(hn_s, wu_ref, ubufs, T, ti * T, pgw_ref, pscale_ref, pwo_ref)

    _project_conv(hn_s, wxbc_ref, xbufs, T, convw_ref, convb_ref, xs_s, b_s, c_s)
    dt_s[...] = _softplus(_dot(hn_s[...], wdt_ref[...]) + dtb_ref[...])

    def chunk(c, carry):
        rows = pl.ds(pl.multiple_of(c * CHUNK, CHUNK), CHUNK)
        _ssd_chunk(rows, xs_s, b_s, c_s, dt_s, y_s, ht_s, alog_ref, dskip_ref, e_ref)
        return carry

    lax.fori_loop(0, T // CHUNK, chunk, 0)

    o_ref[0] = _gate_merge_out(x, hn_s, y_s, bp, wz_ref, wgate_ref, snorm_ref, swo_ref, wo_ref, yn_s)

    @pl.when(ti == pl.num_programs(1) - 1)
    def _():
        for j in range(POOL_WIDTH // LANES):
            npool_ref[0, 0, :, j * LANES:(j + 1) * LANES] = _window(ubufs, j, POOL_PAD + T - POOL_HIST, POOL_HIST)
        for j in range(CONV_DIM // LANES):
            nconv_ref[0, 0, :, j * LANES:(j + 1) * LANES] = _window(
                xbufs, j, CONV_PAD + T - (CONV_WIDTH - 1), CONV_WIDTH - 1)
        for g in range(N_SSD_GROUPS):
            hg = ht_s[:, g * GROUP_WIDTH:(g + 1) * GROUP_WIDTH].T
            nssm_ref[0, 0, g * HEADS_PER_GROUP:(g + 1) * HEADS_PER_GROUP] = hg.reshape(
                HEADS_PER_GROUP, SSD_HEAD_DIM, D_STATE)

    for ref in ubufs:
        ref[:, :, 0:POOL_PAD, :] = ref[:, :, T:T + POOL_PAD, :]
    for ref in xbufs:
        ref[:, :, 0:CONV_PAD, :] = ref[:, :, T:T + CONV_PAD, :]


def _mixer_prompt(x1, w):
    B, S, _ = x1.shape
    T = PROMPT_TILE
    assert S % T == 0 and T % CHUNK == 0 and T >= POOL_PAD
    consts = [w["gmix"], w["wu"], w["wz"], w["wxbc"], w["wdt"], w["wgate"], w["pgw"], w["pscale"], w["pwo"],
              w["convw"], w["convb"], w["dtb"], w["alog"], w["dskip"], w["snorm"], w["swo"], w["wo"], w["expand"]]
    return pl.pallas_call(
        _mixer_prompt_kernel,
        grid=(B, S // T),
        in_specs=[pl.BlockSpec((1, T, D_MODEL), lambda b, t: (b, t, 0))] + [_const_spec(c.shape) for c in consts],
        out_specs=(pl.BlockSpec((1, T, D_MODEL), lambda b, t: (b, t, 0)),
                   pl.BlockSpec((1, 1, POOL_HIST, POOL_WIDTH), lambda b, t: (0, b, 0, 0)),
                   pl.BlockSpec((1, 1, CONV_WIDTH - 1, CONV_DIM), lambda b, t: (0, b, 0, 0)),
                   pl.BlockSpec((1, 1, N_SSD_HEADS, SSD_HEAD_DIM, D_STATE), lambda b, t: (0, b, 0, 0, 0))),
        out_shape=(jax.ShapeDtypeStruct((B, S, D_MODEL), f32),
                   jax.ShapeDtypeStruct((1, B, POOL_HIST, POOL_WIDTH), f32),
                   jax.ShapeDtypeStruct((1, B, CONV_WIDTH - 1, CONV_DIM), f32),
                   jax.ShapeDtypeStruct((1, B, N_SSD_HEADS, SSD_HEAD_DIM, D_STATE), f32)),
        scratch_shapes=[pltpu.VMEM((T, D_MODEL), bf16),
                        pltpu.VMEM((T, D_INNER), f32),
                        pltpu.VMEM((T, BC_DIM), f32),
                        pltpu.VMEM((T, BC_DIM), f32),
                        pltpu.VMEM((T, HEAD_LANES), f32),
                        pltpu.VMEM((T, D_INNER), f32),
                        pltpu.VMEM((D_STATE, D_INNER), f32),
                        pltpu.VMEM((T, D_INNER), bf16)]
                       + _window_scratch(N_POOL_BUFS, 1, POOL_PAD + T)
                       + _window_scratch(N_CONV_BUFS, 1, CONV_PAD + T),
        compiler_params=_params(2),
        name="mixer_prompt",
    )(x1, *consts)


def _load_history(bufs, hist_ref, row0):
    for j in range(hist_ref.shape[2] // LANES):
        ref, k = _slab(bufs, j)
        n_slabs, S, n_rows, _ = ref.shape
        flat = ref.reshape(n_slabs, S * n_rows, LANES)
        for r in range(hist_ref.shape[0]):
            flat[k, pl.ds(row0 + r, S, stride=n_rows), :] = hist_ref[r, :, j * LANES:(j + 1) * LANES]


def _store_history(hist_ref, bufs, row0):
    for j in range(hist_ref.shape[2] // LANES):
        ref, k = _slab(bufs, j)
        n_slabs, S, n_rows, _ = ref.shape
        flat = ref.reshape(n_slabs, S * n_rows, LANES)
        for r in range(hist_ref.shape[0]):
            hist_ref[r, :, j * LANES:(j + 1) * LANES] = flat[k, pl.ds(row0 + r, S, stride=n_rows), :]


def _sample_pre_kernel(x_ref, ph_ref, ch_ref, gmix_ref, wu_ref, wxbc_ref, wdt_ref, pgw_ref, pscale_ref, pwo_ref,
                       convw_ref, convb_ref, dtb_ref,
                       bp_ref, xs_ref, b_ref, c_ref, dt_ref, npool_ref, nconv_ref,
                       *window_bufs, T):
    ubufs, xbufs = window_bufs[:N_POOL_BUFS], window_bufs[N_POOL_BUFS:]
    hn = _rms(x_ref[...], gmix_ref[...]).astype(bf16)

    _load_history(ubufs, ph_ref, POOL_PAD - POOL_HIST)
    bp_ref[...] = _pool_branch(hn, wu_ref, ubufs, T, PAST_LEN, pgw_ref, pscale_ref, pwo_ref)
    _store_history(npool_ref, ubufs, POOL_PAD + T - POOL_HIST)

    _load_history(xbufs, ch_ref, CONV_PAD - (CONV_WIDTH - 1))
    _project_conv(hn, wxbc_ref, xbufs, T, convw_ref, convb_ref, xs_ref, b_ref, c_ref)
    _store_history(nconv_ref, xbufs, CONV_PAD + T - (CONV_WIDTH - 1))
    dt_ref[...] = _softplus(_dot(hn, wdt_ref[...]) + dtb_ref[...])


def _sample_pre(x1, pool_hist, conv_hist, w, T):
    rows = x1.shape[0]
    nseq = rows // T
    S = SAMPLE_SEQS
    R = S * T
    assert nseq % S == 0 and T % SUBLANES == 0
    consts = [w["gmix"], w["wu"], w["wxbc"], w["wdt"], w["pgw"], w["pscale"], w["pwo"], w["convw"], w["convb"],
              w["dtb"]]
    row_spec = lambda width: pl.BlockSpec((R, width), lambda i: (i, 0))
    return pl.pallas_call(
        functools.partial(_sample_pre_kernel, T=T),
        grid=(nseq // S,),
        in_specs=[row_spec(D_MODEL),
                  pl.BlockSpec((POOL_HIST, S, POOL_WIDTH), lambda i: (0, i, 0)),
                  pl.BlockSpec((CONV_WIDTH - 1, S, CONV_DIM), lambda i: (0, i, 0))]
                 + [_const_spec(c.shape) for c in consts],
        out_specs=(row_spec(D_MODEL), row_spec(D_INNER), row_spec(BC_DIM), row_spec(BC_DIM), row_spec(HEAD_LANES),
                   pl.BlockSpec((POOL_HIST, S, POOL_WIDTH), lambda i: (0, i, 0)),
                   pl.BlockSpec((CONV_WIDTH - 1, S, CONV_DIM), lambda i: (0, i, 0))),
        out_shape=(jax.ShapeDtypeStruct((rows, D_MODEL), f32),
                   jax.ShapeDtypeStruct((rows, D_INNER), f32),
                   jax.ShapeDtypeStruct((rows, BC_DIM), f32),
                   jax.ShapeDtypeStruct((rows, BC_DIM), f32),
                   jax.ShapeDtypeStruct((rows, HEAD_LANES), f32),
                   jax.ShapeDtypeStruct((POOL_HIST, nseq, POOL_WIDTH), f32),
                   jax.ShapeDtypeStruct((CONV_WIDTH - 1, nseq, CONV_DIM), f32)),
        scratch_shapes=_window_scratch(N_POOL_BUFS, S, POOL_PAD + T) + _window_scratch(N_CONV_BUFS, S, CONV_PAD + T),
        compiler_params=_params(1),
        name="sample_pre",
    )(x1, pool_hist, conv_hist, *consts)


def _ssd_decode_kernel(xs_ref, b_ref, c_ref, dt_ref, h0_ref, alog_ref, dskip_ref, e_ref, y_ref, hn_ref, *, T):
    assert T == SUBLANES
    lane_group = lax.broadcasted_iota(jnp.int32, (T, HEAD_LANES), 1) // HEADS_PER_GROUP
    neg_a = -jnp.exp(alog_ref[...])

    def shift(v, d):
        if d == 0:
            return v
        r = lax.broadcasted_iota(jnp.int32, v.shape, 0)
        return jnp.where(r >= d, pltpu.roll(v, d, axis=0), 0.0)

    def per_seq(s, carry):
        dt = dt_ref[s]
        a = dt * neg_a
        d = 1
        while d < T:
            a = a + shift(a, d)
            d *= 2
        a_end = a[T - 1:T, :]
        x = xs_ref[s]
        bm = b_ref[s]
        cm = c_ref[s]
        terms = []
        for d in range(T):
            cbv = jnp.zeros((T, HEAD_LANES), f32)
            for g in range(N_SSD_GROUPS):
                gc = slice(g * D_STATE, (g + 1) * D_STATE)
                cb = jnp.sum(cm[:, gc] * shift(bm[:, gc], d), axis=-1, keepdims=True)
                cbv = jnp.where(lane_group == g, cb, cbv)
            terms.append(cbv * jnp.exp(a - shift(a, d)) * shift(dt, d))
        terms.append(jnp.exp(a))
        terms.append(jnp.exp(a_end - a) * dt)
        v = jnp.concatenate(terms, axis=0)
        v_hi = v.astype(bf16)
        v_lo = (v - v_hi.astype(f32)).astype(bf16)
        ex = _dot(jnp.concatenate([v_hi, v_lo], axis=0), e_ref[...])
        n = (T + 2) * T
        ex = ex[0:n] + ex[n:2 * n]

        y = dskip_ref[...] * x
        for d in range(T):
            y = y + ex[d * T:(d + 1) * T] * shift(x, d)
        ea = ex[T * T:(T + 1) * T]
        xw = (x * ex[(T + 1) * T:(T + 2) * T]).astype(bf16)
        cdec = jnp.exp(a_end)
        y_groups = []
        for g in range(N_SSD_GROUPS):
            gc = slice(g * D_STATE, (g + 1) * D_STATE)
            cols = slice(g * GROUP_WIDTH, (g + 1) * GROUP_WIDTH)
            heads = slice(g * HEADS_PER_GROUP, (g + 1) * HEADS_PER_GROUP)
            h0g = h0_ref[0, s, heads].reshape(GROUP_WIDTH, D_STATE)
            ch = lax.dot_general(cm[:, gc].astype(bf16), h0g.astype(bf16), _NT, preferred_element_type=f32)
            y_groups.append(y[:, cols] + ea[:, cols] * ch)
            st = lax.dot_general(xw[:, cols], bm[:, gc].astype(bf16), _TN, preferred_element_type=f32)
            for r in range(HEADS_PER_GROUP):
                h = g * HEADS_PER_GROUP + r
                hn_ref[0, s, h] = (h0_ref[0, s, h] * cdec[:, h:h + 1]
                                   + st[r * SSD_HEAD_DIM:(r + 1) * SSD_HEAD_DIM, :])
        y_ref[s] = jnp.concatenate(y_groups, axis=-1)
        return carry

    lax.fori_loop(0, DECODE_SEQS, per_seq, 0, unroll=DECODE_UNROLL)


def _ssd_decode(xs, bm, cm, dt, h0, w, T):
    nseq = xs.shape[0] // T
    Q = DECODE_SEQS
    assert nseq % Q == 0
    seq_spec = lambda width: pl.BlockSpec((Q, T, width), lambda i: (i, 0, 0))
    state_spec = pl.BlockSpec((1, Q, N_SSD_HEADS, SSD_HEAD_DIM, D_STATE), lambda i: (0, i, 0, 0, 0))
    consts = [w["alog"], w["dskip"], w["expand"]]
    return pl.pallas_call(
        functools.partial(_ssd_decode_kernel, T=T),
        grid=(nseq // Q,),
        in_specs=[seq_spec(D_INNER), seq_spec(BC_DIM), seq_spec(BC_DIM), seq_spec(HEAD_LANES), state_spec]
                 + [_const_spec(c.shape) for c in consts],
        out_specs=(seq_spec(D_INNER), state_spec),
        out_shape=(jax.ShapeDtypeStruct((nseq, T, D_INNER), f32), jax.ShapeDtypeStruct(h0.shape, f32)),
        compiler_params=_params(1),
        name="ssd_decode",
    )(xs.reshape(nseq, T, D_INNER), bm.reshape(nseq, T, BC_DIM), cm.reshape(nseq, T, BC_DIM),
      dt.reshape(nseq, T, HEAD_LANES), h0, *consts)


def _sample_post_kernel(x_ref, y_ref, bp_ref, gmix_ref, wz_ref, wgate_ref, snorm_ref, swo_ref, wo_ref,
                        o_ref, hn_s, yn_s):
    x = x_ref[...]
    hn_s[...] = _rms(x, gmix_ref[...]).astype(bf16)
    o_ref[...] = _gate_merge_out(x, hn_s, y_ref, bp_ref[...], wz_ref, wgate_ref, snorm_ref, swo_ref, wo_ref, yn_s)


def _sample_post(x1, y, bp, w, T):
    rows = x1.shape[0]
    R = SAMPLE_POST_SEQS * T
    assert rows % R == 0
    consts = [w["gmix"], w["wz"], w["wgate"], w["snorm"], w["swo"], w["wo"]]
    row_spec = lambda width: pl.BlockSpec((R, width), lambda i: (i, 0))
    return pl.pallas_call(
        _sample_post_kernel,
        grid=(rows // R,),
        in_specs=[row_spec(D_MODEL), row_spec(D_INNER), row_spec(D_MODEL)] + [_const_spec(c.shape) for c in consts],
        out_specs=row_spec(D_MODEL),
        out_shape=jax.ShapeDtypeStruct((rows, D_MODEL), f32),
        scratch_shapes=[pltpu.VMEM((R, D_MODEL), bf16), pltpu.VMEM((R, D_INNER), bf16)],
        compiler_params=_params(1),
        name="sample_post",
    )(x1, y, bp, *consts)


def _small_params(l, norm_mix, pool_scale, conv_w, conv_b, dt_bias, a_log, d_skip, ssd_norm):
    pad_heads = lambda v: jnp.pad(v, ((0, 0), (0, HEAD_LANES - N_SSD_HEADS)))
    head_of_lane = jnp.arange(D_INNER, dtype=jnp.int32) // SSD_HEAD_DIM
    expand = (jnp.arange(HEAD_LANES, dtype=jnp.int32)[:, None] == head_of_lane[None, :]).astype(bf16)
    return dict(
        gmix=norm_mix[l][None, :], pscale=pool_scale[l][None, :], convw=conv_w[l], convb=conv_b[l][None, :],
        dtb=pad_heads(dt_bias[l][None, :]), alog=pad_heads(a_log[l][None, :]),
        dskip=jnp.repeat(d_skip[l], SSD_HEAD_DIM)[None, :], snorm=ssd_norm[l][None, :], expand=expand)


def kernel(x_prompt, x_sample, state_pool, state_conv, state_ssm, norm_ffn1, ffn1_w_in, ffn1_w_out, norm_mix, w_in,
           pool_w_group, pool_scale, pool_w_out, conv_w, conv_b, dt_bias, a_log, d_skip, ssd_norm, ssd_w_out, w_o,
           norm_ffn2, ffn2_w_in, ffn2_w_out, norm_final):
    depth = w_in.shape[0]
    B, S, _ = x_prompt.shape
    DB, T, _ = x_sample.shape
    gfin = norm_final[None, :]
    xp = x_prompt.reshape(B * S, D_MODEL)
    xs = x_sample.reshape(DB * T, D_MODEL)
    outs = [[] for _ in range(6)]
    for l in range(depth):
        last = l == depth - 1
        f1_in, f1_out = ffn1_w_in[l].astype(bf16), ffn1_w_out[l].astype(bf16)
        g1, g2 = norm_ffn1[l][None, :], norm_ffn2[l][None, :]
        w = _small_params(l, norm_mix, pool_scale, conv_w, conv_b, dt_bias, a_log, d_skip, ssd_norm)
        n_pg = pool_w_group.shape[1]
        xp, xs, in_proj, casts = _ffn_and_casts(
            xp, xs, g1, f1_in, f1_out, gfin, w_in[l].T,
            [pool_w_group[l].reshape(n_pg * POOL_GROUP, POOL_GROUP), pool_w_out[l], ssd_w_out[l], w_o[l],
             ffn2_w_in[l], ffn2_w_out[l]])
        w.update(zip(("wu", "wz", "wxbc", "wdt", "wgate"), in_proj))
        w.update(pgw=casts[0].reshape(n_pg, POOL_GROUP, POOL_GROUP), pwo=casts[1], swo=casts[2], wo=casts[3])
        f2_in, f2_out = casts[4], casts[5]
        xp3, npool, nconv, nssm = _mixer_prompt(xp.reshape(B, S, D_MODEL), w)
        hist_major = lambda v: jnp.transpose(v, (1, 0, 2))
        bp, cxs, cb, cc, cdt, spool, sconv = _sample_pre(xs, hist_major(state_pool[l]), hist_major(state_conv[l]), w, T)
        spool, sconv = hist_major(spool)[None], hist_major(sconv)[None]
        y, sssm = _ssd_decode(cxs, cb, cc, cdt, state_ssm[l:l + 1], w, T)
        xs = _sample_post(xs, y.reshape(DB * T, D_INNER), bp, w, T)
        xp, xs = _ffn(xp3.reshape(B * S, D_MODEL), xs, g2, f2_in, f2_out, gfin, final_norm=last)
        for acc, v in zip(outs, (npool, nconv, nssm, spool, sconv, sssm)):
            acc.append(v)
    stack = lambda vs: vs[0] if len(vs) == 1 else jnp.concatenate(vs, axis=0)
    return (xp.reshape(B, S, D_MODEL), xs.reshape(DB, T, D_MODEL),
            stack(outs[0]), stack(outs[1]), stack(outs[2]), stack(outs[3]), stack(outs[4]), stack(outs[5]))
```

```python
import functools

import jax
import jax.numpy as jnp
from jax import lax
from jax.experimental import pallas as pl
from jax.experimental.pallas import tpu as pltpu

f32 = jnp.float32
bf16 = jnp.bfloat16

D_MODEL = 1024
D_FF = 2816
POOL_WINDOWS = (2, 4, 8, 16)
POOL_WIDTH = D_MODEL
POOL_GROUP = POOL_WIDTH // len(POOL_WINDOWS)
POOL_HIST = max(POOL_WINDOWS) - 1
D_INNER = 2 * D_MODEL
SSD_HEAD_DIM = 64
N_SSD_HEADS = D_INNER // SSD_HEAD_DIM
N_SSD_GROUPS = 4
HEADS_PER_GROUP = N_SSD_HEADS // N_SSD_GROUPS
GROUP_WIDTH = D_INNER // N_SSD_GROUPS
D_STATE = 128
CONV_WIDTH = 4
BC_DIM = N_SSD_GROUPS * D_STATE
CONV_DIM = D_INNER + 2 * BC_DIM
CHUNK = 128
PAST_LEN = 16384
EPS = 1e-6
LOG2_E = 1.4426950408889634

LANES = 128
SUBLANES = 8
MXU_COLS = 256
VMEM_LIMIT_BYTES = 56 * 1024 * 1024

FFN_ROWS = 512
FFN_CHUNK = 256
PROMPT_TILE = 256
SAMPLE_SEQS = 32
SAMPLE_POST_SEQS = 32
DECODE_SEQS = 8
DECODE_UNROLL = 4
POOL_PAD = 16
CONV_PAD = 8
HEAD_LANES = LANES
SLABS_PER_BUF = MXU_COLS // LANES
N_POOL_BUFS = POOL_WIDTH // MXU_COLS
N_CONV_BUFS = CONV_DIM // MXU_COLS

_NT = (((1,), (1,)), ((), ()))
_TN = (((0,), (0,)), ((), ()))


def _rms(x, g):
    return x * lax.rsqrt(jnp.mean(x * x, axis=-1, keepdims=True) + EPS) * g


def _sigmoid(v):
    return 0.5 * jnp.tanh(0.5 * v) + 0.5


def _silu(v):
    h = 0.5 * v
    return h * jnp.tanh(h) + h


def _softplus(v):
    return jnp.maximum(v, 0.0) + jnp.log1p(jnp.exp(-jnp.abs(v)))


def _dot(a, b):
    return jnp.dot(a, b, preferred_element_type=f32)


def _split3(v):
    hi = v.astype(bf16)
    r1 = v - hi.astype(f32)
    mid = r1.astype(bf16)
    lo = (r1 - mid.astype(f32)).astype(bf16)
    return hi, mid, lo


def _const_spec(shape):
    nd = len(shape)
    return pl.BlockSpec(shape, lambda *_: (0,) * nd, pipeline_mode=pl.Buffered(1))


def _params(n_grid):
    return pltpu.CompilerParams(dimension_semantics=("arbitrary",) * n_grid,
                                vmem_limit_bytes=VMEM_LIMIT_BYTES)


def _ffn_kernel(xa_ref, xb_ref, g_ref, win_ref, wout_ref, gfin_ref, oa_ref, ob_ref, hn_s, act_s, *, n_a, final_norm,
                side_work=None):
    def tile(x_ref, o_ref, extra=None):
        commit = extra() if extra is not None else None
        hn_s[...] = _rms(x_ref[...], g_ref[...]).astype(bf16)
        for c in range(D_FF // FFN_CHUNK):
            lo = c * FFN_CHUNK
            gate = _dot(hn_s[...], win_ref[:, lo:lo + FFN_CHUNK])
            up = _dot(hn_s[...], win_ref[:, D_FF + lo:D_FF + lo + FFN_CHUNK])
            act_s[:, lo:lo + FFN_CHUNK] = (_silu(gate) * up).astype(bf16)
        out = x_ref[...] + 0.5 * _dot(act_s[...], wout_ref[...])
        if final_norm:
            out = _rms(out, gfin_ref[...])
        o_ref[...] = out
        if commit is not None:
            commit()

    on_a = pl.program_id(0) < n_a
    pl.when(on_a)(lambda: tile(xa_ref, oa_ref, side_work))
    pl.when(jnp.logical_not(on_a))(lambda: tile(xb_ref, ob_ref))


def _ffn_specs(rows_a, rows_b):
    assert rows_a % FFN_ROWS == 0 and rows_b % FFN_ROWS == 0
    n_a, n_b = rows_a // FFN_ROWS, rows_b // FFN_ROWS
    a_spec = pl.BlockSpec((FFN_ROWS, D_MODEL), lambda i: (jnp.minimum(i, n_a - 1), 0))
    b_spec = pl.BlockSpec((FFN_ROWS, D_MODEL), lambda i: (jnp.maximum(i - n_a, 0), 0))
    in_specs = [a_spec, b_spec,
                _const_spec((1, D_MODEL)),
                _const_spec((D_MODEL, 2 * D_FF)),
                _const_spec((D_FF, D_MODEL)),
                _const_spec((1, D_MODEL))]
    out_shapes = [jax.ShapeDtypeStruct((rows_a, D_MODEL), f32), jax.ShapeDtypeStruct((rows_b, D_MODEL), f32)]
    scratch = [pltpu.VMEM((FFN_ROWS, D_MODEL), bf16), pltpu.VMEM((FFN_ROWS, D_FF), bf16)]
    return n_a, n_b, in_specs, [a_spec, b_spec], out_shapes, scratch


def _ffn(xa, xb, g, w_in_b, w_out_b, gfin, *, final_norm):
    n_a, n_b, in_specs, out_specs, out_shapes, scratch = _ffn_specs(xa.shape[0], xb.shape[0])
    return pl.pallas_call(
        functools.partial(_ffn_kernel, n_a=n_a, final_norm=final_norm),
        grid=(n_a + n_b,),
        in_specs=in_specs, out_specs=tuple(out_specs), out_shape=tuple(out_shapes), scratch_shapes=scratch,
        compiler_params=_params(1),
        name="ffn_final" if final_norm else "ffn",
    )(xa, xb, g, w_in_b, w_out_b, gfin)


_IN_PROJ_SPLITS = (POOL_WIDTH, POOL_WIDTH + D_INNER, POOL_WIDTH + D_INNER + CONV_DIM,
                   POOL_WIDTH + D_INNER + CONV_DIM + N_SSD_HEADS)
BF16_SUBLANES = 2 * SUBLANES


def _cast_block(n_rows, n_steps):
    period = 1
    while (n_rows * period) % n_steps or ((n_rows * period) // n_steps) % BF16_SUBLANES:
        period *= 2
        assert period <= n_steps
    return (n_rows * period) // n_steps, period


_PIECE_TILES = (POOL_WIDTH // MXU_COLS, D_INNER // MXU_COLS, CONV_DIM // MXU_COLS, 2 * D_MODEL // MXU_COLS)
_GATE_SKEW = _IN_PROJ_SPLITS[3] - _IN_PROJ_SPLITS[2]


def _store_when(cond, ref, val):
    @pl.when(cond)
    def _():
        ref[...] = val


def _ffn_cast_kernel(xa_ref, xb_ref, g_ref, win_ref, wout_ref, gfin_ref, wt_ref, wdt_f, wg_lo, wg_hi, *rest,
                     n_a, n_plain):
    plain_f = rest[:n_plain]
    oa_ref, ob_ref = rest[n_plain:n_plain + 2]
    wu_o, wz_o, wxbc_o, wdt_o, wgate_o = rest[n_plain + 2:n_plain + 7]
    plain_o = rest[n_plain + 7:2 * n_plain + 7]
    hn_s, act_s = rest[2 * n_plain + 7:]
    i = pl.program_id(0)
    n_u, n_z, n_x, n_g = _PIECE_TILES
    front = n_u + n_z + n_x

    def casts():
        t = wt_ref[...].T.astype(bf16)
        rows = jnp.concatenate([wg_lo[_GATE_SKEW:, :], wg_hi[0:_GATE_SKEW, :]], axis=0)
        tg = rows.T.astype(bf16)
        for src, dst in zip(plain_f, plain_o):
            dst[...] = src[...].astype(bf16)

        def commit():
            _store_when(i < n_u, wu_o, t)
            _store_when(jnp.logical_and(i >= n_u, i < n_u + n_z), wz_o, t)
            _store_when(jnp.logical_and(i >= n_u + n_z, i < front), wxbc_o, t)
            _store_when(jnp.logical_and(i >= front, i < front + n_g), wgate_o, tg)
        return commit

    _ffn_kernel(xa_ref, xb_ref, g_ref, win_ref, wout_ref, gfin_ref, oa_ref, ob_ref, hn_s, act_s,
                n_a=n_a, final_norm=False, side_work=casts)

    @pl.when(i == 0)
    def _():
        t = wdt_f[...].T
        lane = lax.broadcasted_iota(jnp.int32, t.shape, 1)
        wdt_o[...] = jnp.where(lane < N_SSD_HEADS, t, 0.0).astype(bf16)


def _ffn_and_casts(xa, xb, g, w_in_b, w_out_b, gfin, w_in_t, plain):
    n_a, n_b, in_specs, out_specs, out_shapes, scratch = _ffn_specs(xa.shape[0], xb.shape[0])
    n_u, n_z, n_x, n_g = _PIECE_TILES
    front = n_u + n_z + n_x
    assert n_a >= front + n_g and _IN_PROJ_SPLITS[2] == front * MXU_COLS and _IN_PROJ_SPLITS[2] % HEAD_LANES == 0

    def blocked(arr):
        blk, period = _cast_block(arr.shape[0], n_a)
        return (pl.BlockSpec((blk, arr.shape[1]), lambda i: (jnp.minimum(i, n_a - 1) // period, 0)),
                jax.ShapeDtypeStruct(arr.shape, bf16))

    def piece(first_step, n_tiles):
        return (pl.BlockSpec((D_MODEL, MXU_COLS), lambda i: (0, jnp.clip(i - first_step, 0, n_tiles - 1))),
                jax.ShapeDtypeStruct((D_MODEL, n_tiles * MXU_COLS), bf16))

    pieces = [piece(0, n_u), piece(n_u, n_z), piece(n_u + n_z, n_x),
              (pl.BlockSpec((D_MODEL, HEAD_LANES), lambda i: (0, 0)), jax.ShapeDtypeStruct((D_MODEL, HEAD_LANES), bf16)),
              piece(front, n_g)]
    in_proj_specs = [pl.BlockSpec((MXU_COLS, D_MODEL), lambda i: (jnp.minimum(i, front - 1), 0)),
                     pl.BlockSpec((HEAD_LANES, D_MODEL), lambda i: (_IN_PROJ_SPLITS[2] // HEAD_LANES, 0)),
                     pl.BlockSpec((MXU_COLS, D_MODEL), lambda i: (front + jnp.clip(i - front, 0, n_g - 1), 0)),
                     pl.BlockSpec((MXU_COLS, D_MODEL), lambda i: (front + 1 + jnp.clip(i - front, 0, n_g - 1), 0))]
    plains = [blocked(p) for p in plain]
    res = pl.pallas_call(
        functools.partial(_ffn_cast_kernel, n_a=n_a, n_plain=len(plain)),
        grid=(n_a + n_b,),
        in_specs=in_specs + in_proj_specs + [s for s, _ in plains],
        out_specs=tuple(out_specs + [s for s, _ in pieces] + [s for s, _ in plains]),
        out_shape=tuple(out_shapes + [o for _, o in pieces] + [o for _, o in plains]),
        scratch_shapes=scratch,
        compiler_params=_params(1),
        name="ffn_casts",
    )(xa, xb, g, w_in_b, w_out_b, gfin, w_in_t, w_in_t, w_in_t, w_in_t, *plain)
    return res[0], res[1], res[2:7], res[7:]


def _slab(bufs, j):
    return bufs[j // SLABS_PER_BUF], j % SLABS_PER_BUF


def _window_scratch(n_bufs, n_seqs, n_rows):
    return [pltpu.VMEM((SLABS_PER_BUF, n_seqs, n_rows, LANES), f32) for _ in range(n_bufs)]


def _store_rows(bufs, row0, val, T, slab0=0):
    for j in range(val.shape[1] // LANES):
        ref, k = _slab(bufs, slab0 + j)
        S = ref.shape[1]
        blk = val[:, j * LANES:(j + 1) * LANES]
        if S == 1:
            ref[k, 0, row0:row0 + T, :] = blk
        else:
            ref[k, :, row0:row0 + T, :] = blk.reshape(S, T, LANES)


def _window(bufs, j, row0, n):
    ref, k = _slab(bufs, j)
    S = ref.shape[1]
    rows = pl.ds(row0, n) if row0 % SUBLANES == 0 else pl.ds(row0, n, stride=1)
    if S == 1:
        return ref[k, 0, rows, :]
    v = ref[k, :, rows, :]
    return v.reshape(S * n, LANES) if n % SUBLANES == 0 else v


def _pool_branch(hn_s, wu_ref, ubufs, T, pos0, pgw_ref, pscale_ref, pwo_ref):
    S = ubufs[0].shape[1]
    assert T & (T - 1) == 0 and POOL_GROUP == MXU_COLS
    t_idx = jnp.bitwise_and(lax.broadcasted_iota(jnp.int32, (S * T, 1), 0), T - 1)
    n_seen = pos0 + 1 + t_idx
    _store_rows(ubufs, POOL_PAD, _dot(hn_s[...], wu_ref[...]), T)
    mixed = []
    for gi, w in enumerate(POOL_WINDOWS):
        inv_cnt = 1.0 / jnp.minimum(n_seen, w).astype(f32)
        d = []
        for j in range(gi * SLABS_PER_BUF, (gi + 1) * SLABS_PER_BUF):
            cur = _window(ubufs, j, POOL_PAD, T)
            s = cur
            for k in range(1, w):
                s = s + _window(ubufs, j, POOL_PAD - k, T)
            d.append((s * inv_cnt - cur).astype(bf16))
        mixed.append(_dot(jnp.concatenate(d, axis=-1), pgw_ref[gi]))
    y = jnp.concatenate(mixed, axis=-1) * pscale_ref[...]
    return _dot(y.astype(bf16), pwo_ref[...])


def _conv_slab(xbufs, T, convw_ref, convb_ref, j, xs_ref, b_ref, c_ref):
    c0 = j * LANES
    cols = slice(c0, c0 + LANES)
    h = 0.5 * convb_ref[:, cols]
    for k in range(CONV_WIDTH):
        h = h + _window(xbufs, j, CONV_PAD - (CONV_WIDTH - 1) + k, T) * (0.5 * convw_ref[k:k + 1, cols])
    v = h * jnp.tanh(h) + h
    if c0 < D_INNER:
        xs_ref[:, cols] = v
    elif c0 < D_INNER + BC_DIM:
        b_ref[:, c0 - D_INNER:c0 - D_INNER + LANES] = v
    else:
        c_ref[:, c0 - D_INNER - BC_DIM:c0 - D_INNER - BC_DIM + LANES] = v


def _project_conv(hn_s, wxbc_ref, xbufs, T, convw_ref, convb_ref, xs_ref, b_ref, c_ref):
    for jt in range(N_CONV_BUFS):
        slab0 = jt * SLABS_PER_BUF
        _store_rows(xbufs, CONV_PAD, _dot(hn_s[...], wxbc_ref[:, jt * MXU_COLS:(jt + 1) * MXU_COLS]), T, slab0=slab0)
        for j in range(slab0, slab0 + SLABS_PER_BUF):
            _conv_slab(xbufs, T, convw_ref, convb_ref, j, xs_ref, b_ref, c_ref)


def _gate_merge_out(x, hn_s, y_ref, bp, wz_ref, wgate_ref, snorm_ref, swo_ref, wo_ref, yn_s):
    for g in range(N_SSD_GROUPS):
        cols = slice(g * GROUP_WIDTH, (g + 1) * GROUP_WIDTH)
        z = _dot(hn_s[...], wz_ref[:, cols])
        yg = y_ref[:, cols] * _silu(z)
        yg = yg * lax.rsqrt(jnp.mean(yg * yg, axis=-1, keepdims=True) + EPS)
        yn_s[:, cols] = (yg * snorm_ref[:, cols]).astype(bf16)
    branch_ssd = _dot(yn_s[...], swo_ref[...])
    gate_pool = _sigmoid(_dot(hn_s[...], wgate_ref[:, 0:D_MODEL]))
    gate_ssd = _sigmoid(_dot(hn_s[...], wgate_ref[:, D_MODEL:2 * D_MODEL]))
    merged = (gate_pool * bp + gate_ssd * branch_ssd).astype(bf16)
    return x + _dot(merged, wo_ref[...])


def _ssd_chunk(rows, xs_s, b_s, c_s, dt_s, y_s, ht_s, alog_ref, dskip_ref, e_ref):
    L = CHUNK
    ri = lax.broadcasted_iota(jnp.int32, (L, L), 0)
    ci = lax.broadcasted_iota(jnp.int32, (L, L), 1)
    causal = ri >= ci
    tril = jnp.where(causal, 1.0, 0.0).astype(bf16)
    first_head = lax.broadcasted_iota(jnp.int32, (L, LANES), 1) < SSD_HEAD_DIM
    keep_first = jnp.where(first_head, 1.0, 0.0).astype(bf16)
    keep_second = jnp.where(first_head, 0.0, 1.0).astype(bf16)

    dt = dt_s[rows, :]
    dA = dt * (-jnp.exp(alog_ref[...]))
    hi, mid, lo = _split3(dA)
    acc = _dot(tril, jnp.concatenate([hi, mid, lo], axis=1))
    a = (acc[:, 0:LANES] + acc[:, LANES:2 * LANES] + acc[:, 2 * LANES:3 * LANES]) * LOG2_E
    aT = a.T[0:N_SSD_HEADS]
    dtT = dt.T[0:N_SSD_HEADS]
    wT = jnp.exp2(aT[:, L - 1:L] - aT) * dtT
    srcT = aT - jnp.log2(dtT)
    end_decay = jnp.broadcast_to(jnp.exp2(a[L - 1:L, :]), (2 * SUBLANES, HEAD_LANES))
    e_hi, e_mid, e_lo = (t.astype(f32) for t in _split3(end_decay))
    sel = lax.broadcasted_iota(jnp.int32, (2 * SUBLANES, HEAD_LANES), 0)
    stacked = jnp.where(sel == 0, e_hi, jnp.where(sel == 1, e_mid, jnp.where(sel == 2, e_lo, 0.0)))
    cdec = jnp.sum(_dot(stacked.astype(bf16), e_ref[...]), axis=0, keepdims=True)

    def head_mats(h, cb, bT):
        acol = jnp.broadcast_to(a[:, h:h + 1], (L, L))
        m = (cb * jnp.exp2(jnp.where(causal, acol - srcT[h:h + 1, :], -jnp.inf))).astype(bf16)
        bw = (bT * wT[h:h + 1, :]).astype(bf16)
        return m, bw, acol

    pairs_per_group = HEADS_PER_GROUP // 2
    for q in range(N_SSD_HEADS // 2):
        g, qg = divmod(q, pairs_per_group)
        if qg == 0:
            gcols = slice(g * D_STATE, (g + 1) * D_STATE)
            b_f = b_s[rows, gcols]
            c_b = c_s[rows, gcols].astype(bf16)
            cb = lax.dot_general(c_b, b_f.astype(bf16), _NT, preferred_element_type=f32)
            bT = b_f.T
            ch = _dot(c_b, ht_s[:, g * GROUP_WIDTH:(g + 1) * GROUP_WIDTH].astype(bf16))
        cols = slice(q * LANES, (q + 1) * LANES)
        xq = xs_s[rows, cols]
        xq_b = xq.astype(bf16)
        x2 = jnp.concatenate([xq_b * keep_first, xq_b * keep_second], axis=0)
        m_a, bw_a, acol_a = head_mats(2 * q, cb, bT)
        m_b, bw_b, acol_b = head_mats(2 * q + 1, cb, bT)
        y_off = ch[:, qg * LANES:(qg + 1) * LANES] * jnp.exp2(jnp.where(first_head, acol_a, acol_b))
        y_s[rows, cols] = _dot(jnp.concatenate([m_a, m_b], axis=1), x2) + y_off + dskip_ref[:, cols] * xq
        st = _dot(jnp.concatenate([bw_a, bw_b], axis=1), x2)
        ht_s[:, cols] = ht_s[:, cols] * cdec[:, cols] + st


def _mixer_prompt_kernel(x_ref, gmix_ref, wu_ref, wz_ref, wxbc_ref, wdt_ref, wgate_ref, pgw_ref, pscale_ref,
                         pwo_ref, convw_ref, convb_ref, dtb_ref, alog_ref, dskip_ref, snorm_ref, swo_ref,
                         wo_ref, e_ref,
                         o_ref, npool_ref, nconv_ref, nssm_ref,
                         hn_s, xs_s, b_s, c_s, dt_s, y_s, ht_s, yn_s, *window_bufs):
    T = PROMPT_TILE
    ti = pl.program_id(1)
    ubufs, xbufs = window_bufs[:N_POOL_BUFS], window_bufs[N_POOL_BUFS:]

    @pl.when(ti == 0)
    def _():
        for ref in ubufs:
            ref[:, :, 0:POOL_PAD, :] = jnp.zeros((SLABS_PER_BUF, 1, POOL_PAD, LANES), f32)
        for ref in xbufs:
            ref[:, :, 0:CONV_PAD, :] = jnp.zeros((SLABS_PER_BUF, 1, CONV_PAD, LANES), f32)
        ht_s[...] = jnp.zeros(ht_s.shape, f32)

    x = x_ref[0]
    hn_s[...] = _rms(x, gmix_ref[...]).astype(bf16)

    bp = _pool_branch(hn_s, wu_ref, ubufs, T, ti * T, pgw_ref, pscale_ref, pwo_ref)

    _project_conv(hn_s, wxbc_ref, xbufs, T, convw_ref, convb_ref, xs_s, b_s, c_s)
    dt_s[...] = _softplus(_dot(hn_s[...], wdt_ref[...]) + dtb_ref[...])

    def chunk(c, carry):
        rows = pl.ds(pl.multiple_of(c * CHUNK, CHUNK), CHUNK)
        _ssd_chunk(rows, xs_s, b_s, c_s, dt_s, y_s, ht_s, alog_ref, dskip_ref, e_ref)
        return carry

    lax.fori_loop(0, T // CHUNK, chunk, 0)

    o_ref[0] = _gate_merge_out(x, hn_s, y_s, bp, wz_ref, wgate_ref, snorm_ref, swo_ref, wo_ref, yn_s)

    @pl.when(ti == pl.num_programs(1) - 1)
    def _():
        for j in range(POOL_WIDTH // LANES):
            npool_ref[0, 0, :, j * LANES:(j + 1) * LANES] = _window(ubufs, j, POOL_PAD + T - POOL_HIST, POOL_HIST)
        for j in range(CONV_DIM // LANES):
            nconv_ref[0, 0, :, j * LANES:(j + 1) * LANES] = _window(
                xbufs, j, CONV_PAD + T - (CONV_WIDTH - 1), CONV_WIDTH - 1)
        for g in range(N_SSD_GROUPS):
            hg = ht_s[:, g * GROUP_WIDTH:(g + 1) * GROUP_WIDTH].T
            nssm_ref[0, 0, g * HEADS_PER_GROUP:(g + 1) * HEADS_PER_GROUP] = hg.reshape(
                HEADS_PER_GROUP, SSD_HEAD_DIM, D_STATE)

    for ref in ubufs:
        ref[:, :, 0:POOL_PAD, :] = ref[:, :, T:T + POOL_PAD, :]
    for ref in xbufs:
        ref[:, :, 0:CONV_PAD, :] = ref[:, :, T:T + CONV_PAD, :]


def _mixer_prompt(x1, w):
    B, S, _ = x1.shape
    T = PROMPT_TILE
    assert S % T == 0 and T % CHUNK == 0 and T >= POOL_PAD
    consts = [w["gmix"], w["wu"], w["wz"], w["wxbc"], w["wdt"], w["wgate"], w["pgw"], w["pscale"], w["pwo"],
              w["convw"], w["convb"], w["dtb"], w["alog"], w["dskip"], w["snorm"], w["swo"], w["wo"], w["expand"]]
    return pl.pallas_call(
        _mixer_prompt_kernel,
        grid=(B, S // T),
        in_specs=[pl.BlockSpec((1, T, D_MODEL), lambda b, t: (b, t, 0))] + [_const_spec(c.shape) for c in consts],
        out_specs=(pl.BlockSpec((1, T, D_MODEL), lambda b, t: (b, t, 0)),
                   pl.BlockSpec((1, 1, POOL_HIST, POOL_WIDTH), lambda b, t: (0, b, 0, 0)),
                   pl.BlockSpec((1, 1, CONV_WIDTH - 1, CONV_DIM), lambda b, t: (0, b, 0, 0)),
                   pl.BlockSpec((1, 1, N_SSD_HEADS, SSD_HEAD_DIM, D_STATE), lambda b, t: (0, b, 0, 0, 0))),
        out_shape=(jax.ShapeDtypeStruct((B, S, D_MODEL), f32),
                   jax.ShapeDtypeStruct((1, B, POOL_HIST, POOL_WIDTH), f32),
                   jax.ShapeDtypeStruct((1, B, CONV_WIDTH - 1, CONV_DIM), f32),
                   jax.ShapeDtypeStruct((1, B, N_SSD_HEADS, SSD_HEAD_DIM, D_STATE), f32)),
        scratch_shapes=[pltpu.VMEM((T, D_MODEL), bf16),
                        pltpu.VMEM((T, D_INNER), f32),
                        pltpu.VMEM((T, BC_DIM), f32),
                        pltpu.VMEM((T, BC_DIM), f32),
                        pltpu.VMEM((T, HEAD_LANES), f32),
                        pltpu.VMEM((T, D_INNER), f32),
                        pltpu.VMEM((D_STATE, D_INNER), f32),
                        pltpu.VMEM((T, D_INNER), bf16)]
                       + _window_scratch(N_POOL_BUFS, 1, POOL_PAD + T)
                       + _window_scratch(N_CONV_BUFS, 1, CONV_PAD + T),
        compiler_params=_params(2),
        name="mixer_prompt",
    )(x1, *consts)


def _load_history(bufs, hist_ref, row0):
    for j in range(hist_ref.shape[2] // LANES):
        ref, k = _slab(bufs, j)
        n_slabs, S, n_rows, _ = ref.shape
        flat = ref.reshape(n_slabs, S * n_rows, LANES)
        for r in range(hist_ref.shape[0]):
            flat[k, pl.ds(row0 + r, S, stride=n_rows), :] = hist_ref[r, :, j * LANES:(j + 1) * LANES]


def _store_history(hist_ref, bufs, row0):
    for j in range(hist_ref.shape[2] // LANES):
        ref, k = _slab(bufs, j)
        n_slabs, S, n_rows, _ = ref.shape
        flat = ref.reshape(n_slabs, S * n_rows, LANES)
        for r in range(hist_ref.shape[0]):
            hist_ref[r, :, j * LANES:(j + 1) * LANES] = flat[k, pl.ds(row0 + r, S, stride=n_rows), :]


def _sample_pre_kernel(x_ref, ph_ref, ch_ref, gmix_ref, wu_ref, wxbc_ref, wdt_ref, pgw_ref, pscale_ref, pwo_ref,
                       convw_ref, convb_ref, dtb_ref,
                       bp_ref, xs_ref, b_ref, c_ref, dt_ref, npool_ref, nconv_ref,
                       *window_bufs, T):
    ubufs, xbufs = window_bufs[:N_POOL_BUFS], window_bufs[N_POOL_BUFS:]
    hn = _rms(x_ref[...], gmix_ref[...]).astype(bf16)

    _load_history(ubufs, ph_ref, POOL_PAD - POOL_HIST)
    bp_ref[...] = _pool_branch(hn, wu_ref, ubufs, T, PAST_LEN, pgw_ref, pscale_ref, pwo_ref)
    _store_history(npool_ref, ubufs, POOL_PAD + T - POOL_HIST)

    _load_history(xbufs, ch_ref, CONV_PAD - (CONV_WIDTH - 1))
    _project_conv(hn, wxbc_ref, xbufs, T, convw_ref, convb_ref, xs_ref, b_ref, c_ref)
    _store_history(nconv_ref, xbufs, CONV_PAD + T - (CONV_WIDTH - 1))
    dt_ref[...] = _softplus(_dot(hn, wdt_ref[...]) + dtb_ref[...])


def _sample_pre(x1, pool_hist, conv_hist, w, T):
    rows = x1.shape[0]
    nseq = rows // T
    S = SAMPLE_SEQS
    R = S * T
    assert nseq % S == 0 and T % SUBLANES == 0
    consts = [w["gmix"], w["wu"], w["wxbc"], w["wdt"], w["pgw"], w["pscale"], w["pwo"], w["convw"], w["convb"],
              w["dtb"]]
    row_spec = lambda width: pl.BlockSpec((R, width), lambda i: (i, 0))
    return pl.pallas_call(
        functools.partial(_sample_pre_kernel, T=T),
        grid=(nseq // S,),
        in_specs=[row_spec(D_MODEL),
                  pl.BlockSpec((POOL_HIST, S, POOL_WIDTH), lambda i: (0, i, 0)),
                  pl.BlockSpec((CONV_WIDTH - 1, S, CONV_DIM), lambda i: (0, i, 0))]
                 + [_const_spec(c.shape) for c in consts],
        out_specs=(row_spec(D_MODEL), row_spec(D_INNER), row_spec(BC_DIM), row_spec(BC_DIM), row_spec(HEAD_LANES),
                   pl.BlockSpec((POOL_HIST, S, POOL_WIDTH), lambda i: (0, i, 0)),
                   pl.BlockSpec((CONV_WIDTH - 1, S, CONV_DIM), lambda i: (0, i, 0))),
        out_shape=(jax.ShapeDtypeStruct((rows, D_MODEL), f32),
                   jax.ShapeDtypeStruct((rows, D_INNER), f32),
                   jax.ShapeDtypeStruct((rows, BC_DIM), f32),
                   jax.ShapeDtypeStruct((rows, BC_DIM), f32),
                   jax.ShapeDtypeStruct((rows, HEAD_LANES), f32),
                   jax.ShapeDtypeStruct((POOL_HIST, nseq, POOL_WIDTH), f32),
                   jax.ShapeDtypeStruct((CONV_WIDTH - 1, nseq, CONV_DIM), f32)),
        scratch_shapes=_window_scratch(N_POOL_BUFS, S, POOL_PAD + T) + _window_scratch(N_CONV_BUFS, S, CONV_PAD + T),
        compiler_params=_params(1),
        name="sample_pre",
    )(x1, pool_hist, conv_hist, *consts)


def _ssd_decode_kernel(xs_ref, b_ref, c_ref, dt_ref, h0_ref, alog_ref, dskip_ref, e_ref, y_ref, hn_ref, *, T):
    assert T == SUBLANES
    lane_group = lax.broadcasted_iota(jnp.int32, (T, HEAD_LANES), 1) // HEADS_PER_GROUP
    neg_a = -jnp.exp(alog_ref[...])

    def shift(v, d):
        if d == 0:
            return v
        r = lax.broadcasted_iota(jnp.int32, v.shape, 0)
        return jnp.where(r >= d, pltpu.roll(v, d, axis=0), 0.0)

    def per_seq(s, carry):
        dt = dt_ref[s]
        a = dt * neg_a
        d = 1
        while d < T:
            a = a + shift(a, d)
            d *= 2
        a_end = a[T - 1:T, :]
        x = xs_ref[s]
        bm = b_ref[s]
        cm = c_ref[s]
        terms = []
        for d in range(T):
            cbv = jnp.zeros((T, HEAD_LANES), f32)
            for g in range(N_SSD_GROUPS):
                gc = slice(g * D_STATE, (g + 1) * D_STATE)
                cb = jnp.sum(cm[:, gc] * shift(bm[:, gc], d), axis=-1, keepdims=True)
                cbv = jnp.where(lane_group == g, cb, cbv)
            terms.append(cbv * jnp.exp(a - shift(a, d)) * shift(dt, d))
        terms.append(jnp.exp(a))
        terms.append(jnp.exp(a_end - a) * dt)
        v = jnp.concatenate(terms, axis=0)
        ex = _dot(v.astype(bf16), e_ref[...])

        y = dskip_ref[...] * x
        for d in range(T):
            y = y + ex[d * T:(d + 1) * T] * shift(x, d)
        ea = ex[T * T:(T + 1) * T]
        xw = (x * ex[(T + 1) * T:(T + 2) * T]).astype(bf16)
        cdec = jnp.exp(a_end)
        y_groups = []
        for g in range(N_SSD_GROUPS):
            gc = slice(g * D_STATE, (g + 1) * D_STATE)
            cols = slice(g * GROUP_WIDTH, (g + 1) * GROUP_WIDTH)
            heads = slice(g * HEADS_PER_GROUP, (g + 1) * HEADS_PER_GROUP)
            h0g = h0_ref[0, s, heads].reshape(GROUP_WIDTH, D_STATE)
            ch = lax.dot_general(cm[:, gc].astype(bf16), h0g.astype(bf16), _NT, preferred_element_type=f32)
            y_groups.append(y[:, cols] + ea[:, cols] * ch)
            st = lax.dot_general(xw[:, cols], bm[:, gc].astype(bf16), _TN, preferred_element_type=f32)
            for r in range(HEADS_PER_GROUP):
                h = g * HEADS_PER_GROUP + r
                hn_ref[0, s, h] = (h0_ref[0, s, h] * cdec[:, h:h + 1]
                                   + st[r * SSD_HEAD_DIM:(r + 1) * SSD_HEAD_DIM, :])
        y_ref[s] = jnp.concatenate(y_groups, axis=-1)
        return carry

    lax.fori_loop(0, DECODE_SEQS, per_seq, 0, unroll=DECODE_UNROLL)


def _ssd_decode(xs, bm, cm, dt, h0, w, T):
    nseq = xs.shape[0] // T
    Q = DECODE_SEQS
    assert nseq % Q == 0
    seq_spec = lambda width: pl.BlockSpec((Q, T, width), lambda i: (i, 0, 0))
    state_spec = pl.BlockSpec((1, Q, N_SSD_HEADS, SSD_HEAD_DIM, D_STATE), lambda i: (0, i, 0, 0, 0))
    consts = [w["alog"], w["dskip"], w["expand"]]
    return pl.pallas_call(
        functools.partial(_ssd_decode_kernel, T=T),
        grid=(nseq // Q,),
        in_specs=[seq_spec(D_INNER), seq_spec(BC_DIM), seq_spec(BC_DIM), seq_spec(HEAD_LANES), state_spec]
                 + [_const_spec(c.shape) for c in consts],
        out_specs=(seq_spec(D_INNER), state_spec),
        out_shape=(jax.ShapeDtypeStruct((nseq, T, D_INNER), f32), jax.ShapeDtypeStruct(h0.shape, f32)),
        compiler_params=_params(1),
        name="ssd_decode",
    )(xs.reshape(nseq, T, D_INNER), bm.reshape(nseq, T, BC_DIM), cm.reshape(nseq, T, BC_DIM),
      dt.reshape(nseq, T, HEAD_LANES), h0, *consts)


def _sample_post_kernel(x_ref, y_ref, bp_ref, gmix_ref, wz_ref, wgate_ref, snorm_ref, swo_ref, wo_ref,
                        o_ref, hn_s, yn_s):
    x = x_ref[...]
    hn_s[...] = _rms(x, gmix_ref[...]).astype(bf16)
    o_ref[...] = _gate_merge_out(x, hn_s, y_ref, bp_ref[...], wz_ref, wgate_ref, snorm_ref, swo_ref, wo_ref, yn_s)


def _sample_post(x1, y, bp, w, T):
    rows = x1.shape[0]
    R = SAMPLE_POST_SEQS * T
    assert rows % R == 0
    consts = [w["gmix"], w["wz"], w["wgate"], w["snorm"], w["swo"], w["wo"]]
    row_spec = lambda width: pl.BlockSpec((R, width), lambda i: (i, 0))
    return pl.pallas_call(
        _sample_post_kernel,
        grid=(rows // R,),
        in_specs=[row_spec(D_MODEL), row_spec(D_INNER), row_spec(D_MODEL)] + [_const_spec(c.shape) for c in consts],
        out_specs=row_spec(D_MODEL),
        out_shape=jax.ShapeDtypeStruct((rows, D_MODEL), f32),
        scratch_shapes=[pltpu.VMEM((R, D_MODEL), bf16), pltpu.VMEM((R, D_INNER), bf16)],
        compiler_params=_params(1),
        name="sample_post",
    )(x1, y, bp, *consts)


def _small_params(l, norm_mix, pool_scale, conv_w, conv_b, dt_bias, a_log, d_skip, ssd_norm):
    pad_heads = lambda v: jnp.pad(v, ((0, 0), (0, HEAD_LANES - N_SSD_HEADS)))
    head_of_lane = jnp.arange(D_INNER, dtype=jnp.int32) // SSD_HEAD_DIM
    expand = (jnp.arange(HEAD_LANES, dtype=jnp.int32)[:, None] == head_of_lane[None, :]).astype(bf16)
    return dict(
        gmix=norm_mix[l][None, :], pscale=pool_scale[l][None, :], convw=conv_w[l], convb=conv_b[l][None, :],
        dtb=pad_heads(dt_bias[l][None, :]), alog=pad_heads(a_log[l][None, :]),
        dskip=jnp.repeat(d_skip[l], SSD_HEAD_DIM)[None, :], snorm=ssd_norm[l][None, :], expand=expand)


def kernel(x_prompt, x_sample, state_pool, state_conv, state_ssm, norm_ffn1, ffn1_w_in, ffn1_w_out, norm_mix, w_in,
           pool_w_group, pool_scale, pool_w_out, conv_w, conv_b, dt_bias, a_log, d_skip, ssd_norm, ssd_w_out, w_o,
           norm_ffn2, ffn2_w_in, ffn2_w_out, norm_final):
    depth = w_in.shape[0]
    B, S, _ = x_prompt.shape
    DB, T, _ = x_sample.shape
    gfin = norm_final[None, :]
    xp = x_prompt.reshape(B * S, D_MODEL)
    xs = x_sample.reshape(DB * T, D_MODEL)
    outs = [[] for _ in range(6)]
    for l in range(depth):
        last = l == depth - 1
        f1_in, f1_out = ffn1_w_in[l].astype(bf16), ffn1_w_out[l].astype(bf16)
        g1, g2 = norm_ffn1[l][None, :], norm_ffn2[l][None, :]
        w = _small_params(l, norm_mix, pool_scale, conv_w, conv_b, dt_bias, a_log, d_skip, ssd_norm)
        n_pg = pool_w_group.shape[1]
        xp, xs, in_proj, casts = _ffn_and_casts(
            xp, xs, g1, f1_in, f1_out, gfin, w_in[l].T,
            [pool_w_group[l].reshape(n_pg * POOL_GROUP, POOL_GROUP), pool_w_out[l], ssd_w_out[l], w_o[l],
             ffn2_w_in[l], ffn2_w_out[l]])
        w.update(zip(("wu", "wz", "wxbc", "wdt", "wgate"), in_proj))
        w.update(pgw=casts[0].reshape(n_pg, POOL_GROUP, POOL_GROUP), pwo=casts[1], swo=casts[2], wo=casts[3])
        f2_in, f2_out = casts[4], casts[5]
        xp3, npool, nconv, nssm = _mixer_prompt(xp.reshape(B, S, D_MODEL), w)
        hist_major = lambda v: jnp.transpose(v, (1, 0, 2))
        bp, cxs, cb, cc, cdt, spool, sconv = _sample_pre(xs, hist_major(state_pool[l]), hist_major(state_conv[l]), w, T)
        spool, sconv = hist_major(spool)[None], hist_major(sconv)[None]
        y, sssm = _ssd_decode(cxs, cb, cc, cdt, state_ssm[l:l + 1], w, T)
        xs = _sample_post(xs, y.reshape(DB * T, D_INNER), bp, w, T)
        xp, xs = _ffn(xp3.reshape(B * S, D_MODEL), xs, g2, f2_in, f2_out, gfin, final_norm=last)
        for acc, v in zip(outs, (npool, nconv, nssm, spool, sconv, sssm)):
            acc.append(v)
    stack = lambda vs: vs[0] if len(vs) == 1 else jnp.concatenate(vs, axis=0)
    return (xp.reshape(B, S, D_MODEL), xs.reshape(DB, T, D_MODEL),
            stack(outs[0]), stack(outs[1]), stack(outs[2]), stack(outs[3]), stack(outs[4]), stack(outs[5]))
```

```python
import functools

import jax
import jax.numpy as jnp
from jax import lax
from jax.experimental import pallas as pl
from jax.experimental.pallas import tpu as pltpu

f32 = jnp.float32
bf16 = jnp.bfloat16

D_MODEL = 1024
D_FF = 2816
POOL_WINDOWS = (2, 4, 8, 16)
POOL_WIDTH = D_MODEL
POOL_GROUP = POOL_WIDTH // len(POOL_WINDOWS)
POOL_HIST = max(POOL_WINDOWS) - 1
D_INNER = 2 * D_MODEL
SSD_HEAD_DIM = 64
N_SSD_HEADS = D_INNER // SSD_HEAD_DIM
N_SSD_GROUPS = 4
HEADS_PER_GROUP = N_SSD_HEADS // N_SSD_GROUPS
GROUP_WIDTH = D_INNER // N_SSD_GROUPS
D_STATE = 128
CONV_WIDTH = 4
BC_DIM = N_SSD_GROUPS * D_STATE
CONV_DIM = D_INNER + 2 * BC_DIM
CHUNK = 128
PAST_LEN = 16384
EPS = 1e-6
LOG2_E = 1.4426950408889634

LANES = 128
SUBLANES = 8
MXU_COLS = 256
VMEM_LIMIT_BYTES = 56 * 1024 * 1024

FFN_ROWS = 512
FFN_CHUNK = 256
PROMPT_TILE = 256
SAMPLE_SEQS = 32
SAMPLE_POST_SEQS = 32
DECODE_SEQS = 8
DECODE_UNROLL = 4
POOL_PAD = 16
CONV_PAD = 8
HEAD_LANES = LANES
SLABS_PER_BUF = MXU_COLS // LANES
N_POOL_BUFS = POOL_WIDTH // MXU_COLS
N_CONV_BUFS = CONV_DIM // MXU_COLS

_NT = (((1,), (1,)), ((), ()))
_TN = (((0,), (0,)), ((), ()))


def _rms(x, g):
    return x * lax.rsqrt(jnp.mean(x * x, axis=-1, keepdims=True) + EPS) * g


def _sigmoid(v):
    return 0.5 * jnp.tanh(0.5 * v) + 0.5


def _silu(v):
    h = 0.5 * v
    return h * jnp.tanh(h) + h


def _softplus(v):
    return jnp.maximum(v, 0.0) + jnp.log1p(jnp.exp(-jnp.abs(v)))


def _dot(a, b):
    return jnp.dot(a, b, preferred_element_type=f32)


def _split3(v):
    hi = v.astype(bf16)
    r1 = v - hi.astype(f32)
    mid = r1.astype(bf16)
    lo = (r1 - mid.astype(f32)).astype(bf16)
    return hi, mid, lo


def _const_spec(shape):
    nd = len(shape)
    return pl.BlockSpec(shape, lambda *_: (0,) * nd, pipeline_mode=pl.Buffered(1))


def _params(n_grid):
    return pltpu.CompilerParams(dimension_semantics=("arbitrary",) * n_grid,
                                vmem_limit_bytes=VMEM_LIMIT_BYTES)


def _ffn_kernel(xa_ref, xb_ref, g_ref, win_ref, wout_ref, gfin_ref, oa_ref, ob_ref, hn_s, act_s, *, n_a, final_norm,
                side_work=None):
    def tile(x_ref, o_ref, extra=None):
        commit = extra() if extra is not None else None
        hn_s[...] = _rms(x_ref[...], g_ref[...]).astype(bf16)
        for c in range(D_FF // FFN_CHUNK):
            lo = c * FFN_CHUNK
            gate = _dot(hn_s[...], win_ref[:, lo:lo + FFN_CHUNK])
            up = _dot(hn_s[...], win_ref[:, D_FF + lo:D_FF + lo + FFN_CHUNK])
            act_s[:, lo:lo + FFN_CHUNK] = (_silu(gate) * up).astype(bf16)
        out = x_ref[...] + 0.5 * _dot(act_s[...], wout_ref[...])
        if final_norm:
            out = _rms(out, gfin_ref[...])
        o_ref[...] = out
        if commit is not None:
            commit()

    on_a = pl.program_id(0) < n_a
    pl.when(on_a)(lambda: tile(xa_ref, oa_ref, side_work))
    pl.when(jnp.logical_not(on_a))(lambda: tile(xb_ref, ob_ref))


def _ffn_specs(rows_a, rows_b):
    assert rows_a % FFN_ROWS == 0 and rows_b % FFN_ROWS == 0
    n_a, n_b = rows_a // FFN_ROWS, rows_b // FFN_ROWS
    a_spec = pl.BlockSpec((FFN_ROWS, D_MODEL), lambda i: (jnp.minimum(i, n_a - 1), 0))
    b_spec = pl.BlockSpec((FFN_ROWS, D_MODEL), lambda i: (jnp.maximum(i - n_a, 0), 0))
    in_specs = [a_spec, b_spec,
                _const_spec((1, D_MODEL)),
                _const_spec((D_MODEL, 2 * D_FF)),
                _const_spec((D_FF, D_MODEL)),
                _const_spec((1, D_MODEL))]
    out_shapes = [jax.ShapeDtypeStruct((rows_a, D_MODEL), f32), jax.ShapeDtypeStruct((rows_b, D_MODEL), f32)]
    scratch = [pltpu.VMEM((FFN_ROWS, D_MODEL), bf16), pltpu.VMEM((FFN_ROWS, D_FF), bf16)]
    return n_a, n_b, in_specs, [a_spec, b_spec], out_shapes, scratch


def _ffn(xa, xb, g, w_in_b, w_out_b, gfin, *, final_norm):
    n_a, n_b, in_specs, out_specs, out_shapes, scratch = _ffn_specs(xa.shape[0], xb.shape[0])
    return pl.pallas_call(
        functools.partial(_ffn_kernel, n_a=n_a, final_norm=final_norm),
        grid=(n_a + n_b,),
        in_specs=in_specs, out_specs=tuple(out_specs), out_shape=tuple(out_shapes), scratch_shapes=scratch,
        compiler_params=_params(1),
        name="ffn_final" if final_norm else "ffn",
    )(xa, xb, g, w_in_b, w_out_b, gfin)


_IN_PROJ_SPLITS = (POOL_WIDTH, POOL_WIDTH + D_INNER, POOL_WIDTH + D_INNER + CONV_DIM,
                   POOL_WIDTH + D_INNER + CONV_DIM + N_SSD_HEADS)
BF16_SUBLANES = 2 * SUBLANES


def _cast_block(n_rows, n_steps):
    period = 1
    while (n_rows * period) % n_steps or ((n_rows * period) // n_steps) % BF16_SUBLANES:
        period *= 2
        assert period <= n_steps
    return (n_rows * period) // n_steps, period


_PIECE_TILES = (POOL_WIDTH // MXU_COLS, D_INNER // MXU_COLS, CONV_DIM // MXU_COLS, 2 * D_MODEL // MXU_COLS)
_GATE_SKEW = _IN_PROJ_SPLITS[3] - _IN_PROJ_SPLITS[2]


def _store_when(cond, ref, val):
    @pl.when(cond)
    def _():
        ref[...] = val


def _ffn_cast_kernel(xa_ref, xb_ref, g_ref, win_ref, wout_ref, gfin_ref, wt_ref, wdt_f, wg_lo, wg_hi, *rest,
                     n_a, n_plain):
    plain_f = rest[:n_plain]
    oa_ref, ob_ref = rest[n_plain:n_plain + 2]
    wu_o, wz_o, wxbc_o, wdt_o, wgate_o = rest[n_plain + 2:n_plain + 7]
    plain_o = rest[n_plain + 7:2 * n_plain + 7]
    hn_s, act_s = rest[2 * n_plain + 7:]
    i = pl.program_id(0)
    n_u, n_z, n_x, n_g = _PIECE_TILES
    front = n_u + n_z + n_x

    def casts():
        t = wt_ref[...].T.astype(bf16)
        rows = jnp.concatenate([wg_lo[_GATE_SKEW:, :], wg_hi[0:_GATE_SKEW, :]], axis=0)
        tg = rows.T.astype(bf16)
        for src, dst in zip(plain_f, plain_o):
            dst[...] = src[...].astype(bf16)

        def commit():
            _store_when(i < n_u, wu_o, t)
            _store_when(jnp.logical_and(i >= n_u, i < n_u + n_z), wz_o, t)
            _store_when(jnp.logical_and(i >= n_u + n_z, i < front), wxbc_o, t)
            _store_when(jnp.logical_and(i >= front, i < front + n_g), wgate_o, tg)
        return commit

    _ffn_kernel(xa_ref, xb_ref, g_ref, win_ref, wout_ref, gfin_ref, oa_ref, ob_ref, hn_s, act_s,
                n_a=n_a, final_norm=False, side_work=casts)

    @pl.when(i == 0)
    def _():
        t = wdt_f[...].T
        lane = lax.broadcasted_iota(jnp.int32, t.shape, 1)
        wdt_o[...] = jnp.where(lane < N_SSD_HEADS, t, 0.0).astype(bf16)


def _ffn_and_casts(xa, xb, g, w_in_b, w_out_b, gfin, w_in_t, plain):
    n_a, n_b, in_specs, out_specs, out_shapes, scratch = _ffn_specs(xa.shape[0], xb.shape[0])
    n_u, n_z, n_x, n_g = _PIECE_TILES
    front = n_u + n_z + n_x
    assert n_a >= front + n_g and _IN_PROJ_SPLITS[2] == front * MXU_COLS and _IN_PROJ_SPLITS[2] % HEAD_LANES == 0

    def blocked(arr):
        blk, period = _cast_block(arr.shape[0], n_a)
        return (pl.BlockSpec((blk, arr.shape[1]), lambda i: (jnp.minimum(i, n_a - 1) // period, 0)),
                jax.ShapeDtypeStruct(arr.shape, bf16))

    def piece(first_step, n_tiles):
        return (pl.BlockSpec((D_MODEL, MXU_COLS), lambda i: (0, jnp.clip(i - first_step, 0, n_tiles - 1))),
                jax.ShapeDtypeStruct((D_MODEL, n_tiles * MXU_COLS), bf16))

    pieces = [piece(0, n_u), piece(n_u, n_z), piece(n_u + n_z, n_x),
              (pl.BlockSpec((D_MODEL, HEAD_LANES), lambda i: (0, 0)), jax.ShapeDtypeStruct((D_MODEL, HEAD_LANES), bf16)),
              piece(front, n_g)]
    in_proj_specs = [pl.BlockSpec((MXU_COLS, D_MODEL), lambda i: (jnp.minimum(i, front - 1), 0)),
                     pl.BlockSpec((HEAD_LANES, D_MODEL), lambda i: (_IN_PROJ_SPLITS[2] // HEAD_LANES, 0)),
                     pl.BlockSpec((MXU_COLS, D_MODEL), lambda i: (front + jnp.clip(i - front, 0, n_g - 1), 0)),
                     pl.BlockSpec((MXU_COLS, D_MODEL), lambda i: (front + 1 + jnp.clip(i - front, 0, n_g - 1), 0))]
    plains = [blocked(p) for p in plain]
    res = pl.pallas_call(
        functools.partial(_ffn_cast_kernel, n_a=n_a, n_plain=len(plain)),
        grid=(n_a + n_b,),
        in_specs=in_specs + in_proj_specs + [s for s, _ in plains],
        out_specs=tuple(out_specs + [s for s, _ in pieces] + [s for s, _ in plains]),
        out_shape=tuple(out_shapes + [o for _, o in pieces] + [o for _, o in plains]),
        scratch_shapes=scratch,
        compiler_params=_params(1),
        name="ffn_casts",
    )(xa, xb, g, w_in_b, w_out_b, gfin, w_in_t, w_in_t, w_in_t, w_in_t, *plain)
    return res[0], res[1], res[2:7], res[7:]


def _slab(bufs, j):
    return bufs[j // SLABS_PER_BUF], j % SLABS_PER_BUF


def _window_scratch(n_bufs, n_seqs, n_rows):
    return [pltpu.VMEM((SLABS_PER_BUF, n_seqs, n_rows, LANES), f32) for _ in range(n_bufs)]


def _store_rows(bufs, row0, val, T, slab0=0):
    for j in range(val.shape[1] // LANES):
        ref, k = _slab(bufs, slab0 + j)
        S = ref.shape[1]
        blk = val[:, j * LANES:(j + 1) * LANES]
        if S == 1:
            ref[k, 0, row0:row0 + T, :] = blk
        else:
            ref[k, :, row0:row0 + T, :] = blk.reshape(S, T, LANES)


def _window(bufs, j, row0, n):
    ref, k = _slab(bufs, j)
    S = ref.shape[1]
    rows = pl.ds(row0, n) if row0 % SUBLANES == 0 else pl.ds(row0, n, stride=1)
    if S == 1:
        return ref[k, 0, rows, :]
    v = ref[k, :, rows, :]
    return v.reshape(S * n, LANES) if n % SUBLANES == 0 else v


def _pool_branch(hn_s, wu_ref, ubufs, T, pos0, pgw_ref, pscale_ref, pwo_ref):
    S = ubufs[0].shape[1]
    assert T & (T - 1) == 0 and POOL_GROUP == MXU_COLS
    t_idx = jnp.bitwise_and(lax.broadcasted_iota(jnp.int32, (S * T, 1), 0), T - 1)
    n_seen = pos0 + 1 + t_idx
    _store_rows(ubufs, POOL_PAD, _dot(hn_s[...], wu_ref[...]), T)
    mixed = []
    for gi, w in enumerate(POOL_WINDOWS):
        inv_cnt = 1.0 / jnp.minimum(n_seen, w).astype(f32)
        d = []
        for j in range(gi * SLABS_PER_BUF, (gi + 1) * SLABS_PER_BUF):
            cur = _window(ubufs, j, POOL_PAD, T)
            s = cur
            for k in range(1, w):
                s = s + _window(ubufs, j, POOL_PAD - k, T)
            d.append((s * inv_cnt - cur).astype(bf16))
        mixed.append(_dot(jnp.concatenate(d, axis=-1), pgw_ref[gi]))
    y = jnp.concatenate(mixed, axis=-1) * pscale_ref[...]
    return _dot(y.astype(bf16), pwo_ref[...])


def _conv_slab(xbufs, T, convw_ref, convb_ref, j, xs_ref, b_ref, c_ref):
    c0 = j * LANES
    cols = slice(c0, c0 + LANES)
    h = 0.5 * convb_ref[:, cols]
    for k in range(CONV_WIDTH):
        h = h + _window(xbufs, j, CONV_PAD - (CONV_WIDTH - 1) + k, T) * (0.5 * convw_ref[k:k + 1, cols])
    v = h * jnp.tanh(h) + h
    if c0 < D_INNER:
        xs_ref[:, cols] = v
    elif c0 < D_INNER + BC_DIM:
        b_ref[:, c0 - D_INNER:c0 - D_INNER + LANES] = v
    else:
        c_ref[:, c0 - D_INNER - BC_DIM:c0 - D_INNER - BC_DIM + LANES] = v


def _project_conv(hn_s, wxbc_ref, xbufs, T, convw_ref, convb_ref, xs_ref, b_ref, c_ref):
    for jt in range(N_CONV_BUFS):
        slab0 = jt * SLABS_PER_BUF
        _store_rows(xbufs, CONV_PAD, _dot(hn_s[...], wxbc_ref[:, jt * MXU_COLS:(jt + 1) * MXU_COLS]), T, slab0=slab0)
        for j in range(slab0, slab0 + SLABS_PER_BUF):
            _conv_slab(xbufs, T, convw_ref, convb_ref, j, xs_ref, b_ref, c_ref)


def _gate_merge_out(x, hn_s, y_ref, bp, wz_ref, wgate_ref, snorm_ref, swo_ref, wo_ref, yn_s):
    for g in range(N_SSD_GROUPS):
        cols = slice(g * GROUP_WIDTH, (g + 1) * GROUP_WIDTH)
        z = _dot(hn_s[...], wz_ref[:, cols])
        yg = y_ref[:, cols] * _silu(z)
        yg = yg * lax.rsqrt(jnp.mean(yg * yg, axis=-1, keepdims=True) + EPS)
        yn_s[:, cols] = (yg * snorm_ref[:, cols]).astype(bf16)
    branch_ssd = _dot(yn_s[...], swo_ref[...])
    gate_pool = _sigmoid(_dot(hn_s[...], wgate_ref[:, 0:D_MODEL]))
    gate_ssd = _sigmoid(_dot(hn_s[...], wgate_ref[:, D_MODEL:2 * D_MODEL]))
    merged = (gate_pool * bp + gate_ssd * branch_ssd).astype(bf16)
    return x + _dot(merged, wo_ref[...])


def _ssd_chunk(rows, xs_s, b_s, c_s, dt_s, y_s, ht_s, alog_ref, dskip_ref, e_ref):
    L = CHUNK
    ri = lax.broadcasted_iota(jnp.int32, (L, L), 0)
    ci = lax.broadcasted_iota(jnp.int32, (L, L), 1)
    causal = ri >= ci
    tril = jnp.where(causal, 1.0, 0.0).astype(bf16)
    first_head = lax.broadcasted_iota(jnp.int32, (L, LANES), 1) < SSD_HEAD_DIM
    keep_first = jnp.where(first_head, 1.0, 0.0).astype(bf16)
    keep_second = jnp.where(first_head, 0.0, 1.0).astype(bf16)

    dt = dt_s[rows, :]
    dA = dt * (-jnp.exp(alog_ref[...]))
    hi, mid, lo = _split3(dA)
    acc = _dot(tril, jnp.concatenate([hi, mid, lo], axis=1))
    a = (acc[:, 0:LANES] + acc[:, LANES:2 * LANES] + acc[:, 2 * LANES:3 * LANES]) * LOG2_E
    aT = a.T[0:N_SSD_HEADS]
    dtT = dt.T[0:N_SSD_HEADS]
    wT = jnp.exp2(aT[:, L - 1:L] - aT) * dtT
    srcT = aT - jnp.log2(dtT)
    end_decay = jnp.broadcast_to(jnp.exp2(a[L - 1:L, :]), (2 * SUBLANES, HEAD_LANES))
    e_hi, e_mid, e_lo = (t.astype(f32) for t in _split3(end_decay))
    sel = lax.broadcasted_iota(jnp.int32, (2 * SUBLANES, HEAD_LANES), 0)
    stacked = jnp.where(sel == 0, e_hi, jnp.where(sel == 1, e_mid, jnp.where(sel == 2, e_lo, 0.0)))
    cdec = jnp.sum(_dot(stacked.astype(bf16), e_ref[...]), axis=0, keepdims=True)

    def head_mats(h, cb, bT):
        acol = jnp.broadcast_to(a[:, h:h + 1], (L, L))
        m = (cb * jnp.exp2(jnp.where(causal, acol - srcT[h:h + 1, :], -jnp.inf))).astype(bf16)
        bw = (bT * wT[h:h + 1, :]).astype(bf16)
        return m, bw, acol

    pairs_per_group = HEADS_PER_GROUP // 2
    for q in range(N_SSD_HEADS // 2):
        g, qg = divmod(q, pairs_per_group)
        if qg == 0:
            gcols = slice(g * D_STATE, (g + 1) * D_STATE)
            b_f = b_s[rows, gcols]
            c_b = c_s[rows, gcols].astype(bf16)
            bT = b_f.T
            c_all = _dot(c_b, jnp.concatenate(
                [bT.astype(bf16), ht_s[:, g * GROUP_WIDTH:(g + 1) * GROUP_WIDTH].astype(bf16)], axis=1))
            cb, ch = c_all[:, 0:L], c_all[:, L:L + GROUP_WIDTH]
        cols = slice(q * LANES, (q + 1) * LANES)
        xq = xs_s[rows, cols]
        xq_b = xq.astype(bf16)
        x2 = jnp.concatenate([xq_b * keep_first, xq_b * keep_second], axis=0)
        m_a, bw_a, acol_a = head_mats(2 * q, cb, bT)
        m_b, bw_b, acol_b = head_mats(2 * q + 1, cb, bT)
        y_off = ch[:, qg * LANES:(qg + 1) * LANES] * jnp.exp2(jnp.where(first_head, acol_a, acol_b))
        both = _dot(jnp.concatenate([jnp.concatenate([m_a, m_b], axis=1),
                                     jnp.concatenate([bw_a, bw_b], axis=1)], axis=0), x2)
        y_s[rows, cols] = both[0:L] + y_off + dskip_ref[:, cols] * xq
        ht_s[:, cols] = ht_s[:, cols] * cdec[:, cols] + both[L:L + D_STATE]


def _mixer_prompt_kernel(x_ref, gmix_ref, wu_ref, wz_ref, wxbc_ref, wdt_ref, wgate_ref, pgw_ref, pscale_ref,
                         pwo_ref, convw_ref, convb_ref, dtb_ref, alog_ref, dskip_ref, snorm_ref, swo_ref,
                         wo_ref, e_ref,
                         o_ref, npool_ref, nconv_ref, nssm_ref,
                         hn_s, xs_s, b_s, c_s, dt_s, y_s, ht_s, yn_s, *window_bufs):
    T = PROMPT_TILE
    ti = pl.program_id(1)
    ubufs, xbufs = window_bufs[:N_POOL_BUFS], window_bufs[N_POOL_BUFS:]

    @pl.when(ti == 0)
    def _():
        for ref in ubufs:
            ref[:, :, 0:POOL_PAD, :] = jnp.zeros((SLABS_PER_BUF, 1, POOL_PAD, LANES), f32)
        for ref in xbufs:
            ref[:, :, 0:CONV_PAD, :] = jnp.zeros((SLABS_PER_BUF, 1, CONV_PAD, LANES), f32)
        ht_s[...] = jnp.zeros(ht_s.shape, f32)

    x = x_ref[0]
    hn_s[...] = _rms(x, gmix_ref[...]).astype(bf16)

    bp = _pool_branch(hn_s, wu_ref, ubufs, T, ti * T, pgw_ref, pscale_ref, pwo_ref)

    _project_conv(hn_s, wxbc_ref, xbufs, T, convw_ref, convb_ref, xs_s, b_s, c_s)
    dt_s[...] = _softplus(_dot(hn_s[...], wdt_ref[...]) + dtb_ref[...])

    def chunk(c, carry):
        rows = pl.ds(pl.multiple_of(c * CHUNK, CHUNK), CHUNK)
        _ssd_chunk(rows, xs_s, b_s, c_s, dt_s, y_s, ht_s, alog_ref, dskip_ref, e_ref)
        return carry

    lax.fori_loop(0, T // CHUNK, chunk, 0)

    o_ref[0] = _gate_merge_out(x, hn_s, y_s, bp, wz_ref, wgate_ref, snorm_ref, swo_ref, wo_ref, yn_s)

    @pl.when(ti == pl.num_programs(1) - 1)
    def _():
        for j in range(POOL_WIDTH // LANES):
            npool_ref[0, 0, :, j * LANES:(j + 1) * LANES] = _window(ubufs, j, POOL_PAD + T - POOL_HIST, POOL_HIST)
        for j in range(CONV_DIM // LANES):
            nconv_ref[0, 0, :, j * LANES:(j + 1) * LANES] = _window(
                xbufs, j, CONV_PAD + T - (CONV_WIDTH - 1), CONV_WIDTH - 1)
        for g in range(N_SSD_GROUPS):
            hg = ht_s[:, g * GROUP_WIDTH:(g + 1) * GROUP_WIDTH].T
            nssm_ref[0, 0, g * HEADS_PER_GROUP:(g + 1) * HEADS_PER_GROUP] = hg.reshape(
                HEADS_PER_GROUP, SSD_HEAD_DIM, D_STATE)

    for ref in ubufs:
        ref[:, :, 0:POOL_PAD, :] = ref[:, :, T:T + POOL_PAD, :]
    for ref in xbufs:
        ref[:, :, 0:CONV_PAD, :] = ref[:, :, T:T + CONV_PAD, :]


def _mixer_prompt(x1, w):
    B, S, _ = x1.shape
    T = PROMPT_TILE
    assert S % T == 0 and T % CHUNK == 0 and T >= POOL_PAD
    consts = [w["gmix"], w["wu"], w["wz"], w["wxbc"], w["wdt"], w["wgate"], w["pgw"], w["pscale"], w["pwo"],
              w["convw"], w["convb"], w["dtb"], w["alog"], w["dskip"], w["snorm"], w["swo"], w["wo"], w["expand"]]
    return pl.pallas_call(
        _mixer_prompt_kernel,
        grid=(B, S // T),
        in_specs=[pl.BlockSpec((1, T, D_MODEL), lambda b, t: (b, t, 0))] + [_const_spec(c.shape) for c in consts],
        out_specs=(pl.BlockSpec((1, T, D_MODEL), lambda b, t: (b, t, 0)),
                   pl.BlockSpec((1, 1, POOL_HIST, POOL_WIDTH), lambda b, t: (0, b, 0, 0)),
                   pl.BlockSpec((1, 1, CONV_WIDTH - 1, CONV_DIM), lambda b, t: (0, b, 0, 0)),
                   pl.BlockSpec((1, 1, N_SSD_HEADS, SSD_HEAD_DIM, D_STATE), lambda b, t: (0, b, 0, 0, 0))),
        out_shape=(jax.ShapeDtypeStruct((B, S, D_MODEL), f32),
                   jax.ShapeDtypeStruct((1, B, POOL_HIST, POOL_WIDTH), f32),
                   jax.ShapeDtypeStruct((1, B, CONV_WIDTH - 1, CONV_DIM), f32),
                   jax.ShapeDtypeStruct((1, B, N_SSD_HEADS, SSD_HEAD_DIM, D_STATE), f32)),
        scratch_shapes=[pltpu.VMEM((T, D_MODEL), bf16),
                        pltpu.VMEM((T, D_INNER), f32),
                        pltpu.VMEM((T, BC_DIM), f32),
                        pltpu.VMEM((T, BC_DIM), f32),
                        pltpu.VMEM((T, HEAD_LANES), f32),
                        pltpu.VMEM((T, D_INNER), f32),
                        pltpu.VMEM((D_STATE, D_INNER), f32),
                        pltpu.VMEM((T, D_INNER), bf16)]
                       + _window_scratch(N_POOL_BUFS, 1, POOL_PAD + T)
                       + _window_scratch(N_CONV_BUFS, 1, CONV_PAD + T),
        compiler_params=_params(2),
        name="mixer_prompt",
    )(x1, *consts)


def _load_history(bufs, hist_ref, row0):
    for j in range(hist_ref.shape[2] // LANES):
        ref, k = _slab(bufs, j)
        n_slabs, S, n_rows, _ = ref.shape
        flat = ref.reshape(n_slabs, S * n_rows, LANES)
        for r in range(hist_ref.shape[0]):
            flat[k, pl.ds(row0 + r, S, stride=n_rows), :] = hist_ref[r, :, j * LANES:(j + 1) * LANES]


def _store_history(hist_ref, bufs, row0):
    for j in range(hist_ref.shape[2] // LANES):
        ref, k = _slab(bufs, j)
        n_slabs, S, n_rows, _ = ref.shape
        flat = ref.reshape(n_slabs, S * n_rows, LANES)
        for r in range(hist_ref.shape[0]):
            hist_ref[r, :, j * LANES:(j + 1) * LANES] = flat[k, pl.ds(row0 + r, S, stride=n_rows), :]


def _sample_pre_kernel(x_ref, ph_ref, ch_ref, gmix_ref, wu_ref, wxbc_ref, wdt_ref, pgw_ref, pscale_ref, pwo_ref,
                       convw_ref, convb_ref, dtb_ref,
                       bp_ref, xs_ref, b_ref, c_ref, dt_ref, npool_ref, nconv_ref,
                       *window_bufs, T):
    ubufs, xbufs = window_bufs[:N_POOL_BUFS], window_bufs[N_POOL_BUFS:]
    hn = _rms(x_ref[...], gmix_ref[...]).astype(bf16)

    _load_history(ubufs, ph_ref, POOL_PAD - POOL_HIST)
    bp_ref[...] = _pool_branch(hn, wu_ref, ubufs, T, PAST_LEN, pgw_ref, pscale_ref, pwo_ref)
    _store_history(npool_ref, ubufs, POOL_PAD + T - POOL_HIST)

    _load_history(xbufs, ch_ref, CONV_PAD - (CONV_WIDTH - 1))
    _project_conv(hn, wxbc_ref, xbufs, T, convw_ref, convb_ref, xs_ref, b_ref, c_ref)
    _store_history(nconv_ref, xbufs, CONV_PAD + T - (CONV_WIDTH - 1))
    dt_ref[...] = _softplus(_dot(hn, wdt_ref[...]) + dtb_ref[...])


def _sample_pre(x1, pool_hist, conv_hist, w, T):
    rows = x1.shape[0]
    nseq = rows // T
    S = SAMPLE_SEQS
    R = S * T
    assert nseq % S == 0 and T % SUBLANES == 0
    consts = [w["gmix"], w["wu"], w["wxbc"], w["wdt"], w["pgw"], w["pscale"], w["pwo"], w["convw"], w["convb"],
              w["dtb"]]
    row_spec = lambda width: pl.BlockSpec((R, width), lambda i: (i, 0))
    return pl.pallas_call(
        functools.partial(_sample_pre_kernel, T=T),
        grid=(nseq // S,),
        in_specs=[row_spec(D_MODEL),
                  pl.BlockSpec((POOL_HIST, S, POOL_WIDTH), lambda i: (0, i, 0)),
                  pl.BlockSpec((CONV_WIDTH - 1, S, CONV_DIM), lambda i: (0, i, 0))]
                 + [_const_spec(c.shape) for c in consts],
        out_specs=(row_spec(D_MODEL), row_spec(D_INNER), row_spec(BC_DIM), row_spec(BC_DIM), row_spec(HEAD_LANES),
                   pl.BlockSpec((POOL_HIST, S, POOL_WIDTH), lambda i: (0, i, 0)),
                   pl.BlockSpec((CONV_WIDTH - 1, S, CONV_DIM), lambda i: (0, i, 0))),
        out_shape=(jax.ShapeDtypeStruct((rows, D_MODEL), f32),
                   jax.ShapeDtypeStruct((rows, D_INNER), f32),
                   jax.ShapeDtypeStruct((rows, BC_DIM), f32),
                   jax.ShapeDtypeStruct((rows, BC_DIM), f32),
                   jax.ShapeDtypeStruct((rows, HEAD_LANES), f32),
                   jax.ShapeDtypeStruct((POOL_HIST, nseq, POOL_WIDTH), f32),
                   jax.ShapeDtypeStruct((CONV_WIDTH - 1, nseq, CONV_DIM), f32)),
        scratch_shapes=_window_scratch(N_POOL_BUFS, S, POOL_PAD + T) + _window_scratch(N_CONV_BUFS, S, CONV_PAD + T),
        compiler_params=_params(1),
        name="sample_pre",
    )(x1, pool_hist, conv_hist, *consts)


def _ssd_decode_kernel(xs_ref, b_ref, c_ref, dt_ref, h0_ref, alog_ref, dskip_ref, e_ref, y_ref, hn_ref, *, T):
    assert T == SUBLANES
    lane_group = lax.broadcasted_iota(jnp.int32, (T, HEAD_LANES), 1) // HEADS_PER_GROUP
    neg_a = -jnp.exp(alog_ref[...])

    def shift(v, d):
        if d == 0:
            return v
        r = lax.broadcasted_iota(jnp.int32, v.shape, 0)
        return jnp.where(r >= d, pltpu.roll(v, d, axis=0), 0.0)

    def per_seq(s, carry):
        dt = dt_ref[s]
        a = dt * neg_a
        d = 1
        while d < T:
            a = a + shift(a, d)
            d *= 2
        a_end = a[T - 1:T, :]
        x = xs_ref[s]
        bm = b_ref[s]
        cm = c_ref[s]
        terms = []
        for d in range(T):
            cbv = jnp.zeros((T, HEAD_LANES), f32)
            for g in range(N_SSD_GROUPS):
                gc = slice(g * D_STATE, (g + 1) * D_STATE)
                cb = jnp.sum(cm[:, gc] * shift(bm[:, gc], d), axis=-1, keepdims=True)
                cbv = jnp.where(lane_group == g, cb, cbv)
            terms.append(cbv * jnp.exp(a - shift(a, d)) * shift(dt, d))
        terms.append(jnp.exp(a))
        terms.append(jnp.exp(a_end - a) * dt)
        v = jnp.concatenate(terms, axis=0)
        ex = _dot(v.astype(bf16), e_ref[...])

        y = dskip_ref[...] * x
        for d in range(T):
            y = y + ex[d * T:(d + 1) * T] * shift(x, d)
        ea = ex[T * T:(T + 1) * T]
        xw = (x * ex[(T + 1) * T:(T + 2) * T]).astype(bf16)
        cdec = jnp.exp(a_end)
        y_groups = []
        for g in range(N_SSD_GROUPS):
            gc = slice(g * D_STATE, (g + 1) * D_STATE)
            cols = slice(g * GROUP_WIDTH, (g + 1) * GROUP_WIDTH)
            heads = slice(g * HEADS_PER_GROUP, (g + 1) * HEADS_PER_GROUP)
            h0g = h0_ref[0, s, heads].reshape(GROUP_WIDTH, D_STATE)
            ch = lax.dot_general(cm[:, gc].astype(bf16), h0g.astype(bf16), _NT, preferred_element_type=f32)
            y_groups.append(y[:, cols] + ea[:, cols] * ch)
            st = lax.dot_general(xw[:, cols], bm[:, gc].astype(bf16), _TN, preferred_element_type=f32)
            for r in range(HEADS_PER_GROUP):
                h = g * HEADS_PER_GROUP + r
                hn_ref[0, s, h] = (h0_ref[0, s, h] * cdec[:, h:h + 1]
                                   + st[r * SSD_HEAD_DIM:(r + 1) * SSD_HEAD_DIM, :])
        y_ref[s] = jnp.concatenate(y_groups, axis=-1)
        return carry

    lax.fori_loop(0, DECODE_SEQS, per_seq, 0, unroll=DECODE_UNROLL)


def _ssd_decode(xs, bm, cm, dt, h0, w, T):
    nseq = xs.shape[0] // T
    Q = DECODE_SEQS
    assert nseq % Q == 0
    seq_spec = lambda width: pl.BlockSpec((Q, T, width), lambda i: (i, 0, 0))
    state_spec = pl.BlockSpec((1, Q, N_SSD_HEADS, SSD_HEAD_DIM, D_STATE), lambda i: (0, i, 0, 0, 0))
    consts = [w["alog"], w["dskip"], w["expand"]]
    return pl.pallas_call(
        functools.partial(_ssd_decode_kernel, T=T),
        grid=(nseq // Q,),
        in_specs=[seq_spec(D_INNER), seq_spec(BC_DIM), seq_spec(BC_DIM), seq_spec(HEAD_LANES), state_spec]
                 + [_const_spec(c.shape) for c in consts],
        out_specs=(seq_spec(D_INNER), state_spec),
        out_shape=(jax.ShapeDtypeStruct((nseq, T, D_INNER), f32), jax.ShapeDtypeStruct(h0.shape, f32)),
        compiler_params=_params(1),
        name="ssd_decode",
    )(xs.reshape(nseq, T, D_INNER), bm.reshape(nseq, T, BC_DIM), cm.reshape(nseq, T, BC_DIM),
      dt.reshape(nseq, T, HEAD_LANES), h0, *consts)


def _sample_post_kernel(x_ref, y_ref, bp_ref, gmix_ref, wz_ref, wgate_ref, snorm_ref, swo_ref, wo_ref,
                        o_ref, hn_s, yn_s):
    x = x_ref[...]
    hn_s[...] = _rms(x, gmix_ref[...]).astype(bf16)
    o_ref[...] = _gate_merge_out(x, hn_s, y_ref, bp_ref[...], wz_ref, wgate_ref, snorm_ref, swo_ref, wo_ref, yn_s)


def _sample_post(x1, y, bp, w, T):
    rows = x1.shape[0]
    R = SAMPLE_POST_SEQS * T
    assert rows % R == 0
    consts = [w["gmix"], w["wz"], w["wgate"], w["snorm"], w["swo"], w["wo"]]
    row_spec = lambda width: pl.BlockSpec((R, width), lambda i: (i, 0))
    return pl.pallas_call(
        _sample_post_kernel,
        grid=(rows // R,),
        in_specs=[row_spec(D_MODEL), row_spec(D_INNER), row_spec(D_MODEL)] + [_const_spec(c.shape) for c in consts],
        out_specs=row_spec(D_MODEL),
        out_shape=jax.ShapeDtypeStruct((rows, D_MODEL), f32),
        scratch_shapes=[pltpu.VMEM((R, D_MODEL), bf16), pltpu.VMEM((R, D_INNER), bf16)],
        compiler_params=_params(1),
        name="sample_post",
    )(x1, y, bp, *consts)


def _small_params(l, norm_mix, pool_scale, conv_w, conv_b, dt_bias, a_log, d_skip, ssd_norm):
    pad_heads = lambda v: jnp.pad(v, ((0, 0), (0, HEAD_LANES - N_SSD_HEADS)))
    head_of_lane = jnp.arange(D_INNER, dtype=jnp.int32) // SSD_HEAD_DIM
    expand = (jnp.arange(HEAD_LANES, dtype=jnp.int32)[:, None] == head_of_lane[None, :]).astype(bf16)
    return dict(
        gmix=norm_mix[l][None, :], pscale=pool_scale[l][None, :], convw=conv_w[l], convb=conv_b[l][None, :],
        dtb=pad_heads(dt_bias[l][None, :]), alog=pad_heads(a_log[l][None, :]),
        dskip=jnp.repeat(d_skip[l], SSD_HEAD_DIM)[None, :], snorm=ssd_norm[l][None, :], expand=expand)


def kernel(x_prompt, x_sample, state_pool, state_conv, state_ssm, norm_ffn1, ffn1_w_in, ffn1_w_out, norm_mix, w_in,
           pool_w_group, pool_scale, pool_w_out, conv_w, conv_b, dt_bias, a_log, d_skip, ssd_norm, ssd_w_out, w_o,
           norm_ffn2, ffn2_w_in, ffn2_w_out, norm_final):
    depth = w_in.shape[0]
    B, S, _ = x_prompt.shape
    DB, T, _ = x_sample.shape
    gfin = norm_final[None, :]
    xp = x_prompt.reshape(B * S, D_MODEL)
    xs = x_sample.reshape(DB * T, D_MODEL)
    outs = [[] for _ in range(6)]
    for l in range(depth):
        last = l == depth - 1
        f1_in, f1_out = ffn1_w_in[l].astype(bf16), ffn1_w_out[l].astype(bf16)
        g1, g2 = norm_ffn1[l][None, :], norm_ffn2[l][None, :]
        w = _small_params(l, norm_mix, pool_scale, conv_w, conv_b, dt_bias, a_log, d_skip, ssd_norm)
        n_pg = pool_w_group.shape[1]
        xp, xs, in_proj, casts = _ffn_and_casts(
            xp, xs, g1, f1_in, f1_out, gfin, w_in[l].T,
            [pool_w_group[l].reshape(n_pg * POOL_GROUP, POOL_GROUP), pool_w_out[l], ssd_w_out[l], w_o[l],
             ffn2_w_in[l], ffn2_w_out[l]])
        w.update(zip(("wu", "wz", "wxbc", "wdt", "wgate"), in_proj))
        w.update(pgw=casts[0].reshape(n_pg, POOL_GROUP, POOL_GROUP), pwo=casts[1], swo=casts[2], wo=casts[3])
        f2_in, f2_out = casts[4], casts[5]
        xp3, npool, nconv, nssm = _mixer_prompt(xp.reshape(B, S, D_MODEL), w)
        hist_major = lambda v: jnp.transpose(v, (1, 0, 2))
        bp, cxs, cb, cc, cdt, spool, sconv = _sample_pre(xs, hist_major(state_pool[l]), hist_major(state_conv[l]), w, T)
        spool, sconv = hist_major(spool)[None], hist_major(sconv)[None]
        y, sssm = _ssd_decode(cxs, cb, cc, cdt, state_ssm[l:l + 1], w, T)
        xs = _sample_post(xs, y.reshape(DB * T, D_INNER), bp, w, T)
        xp, xs = _ffn(xp3.reshape(B * S, D_MODEL), xs, g2, f2_in, f2_out, gfin, final_norm=last)
        for acc, v in zip(outs, (npool, nconv, nssm, spool, sconv, sssm)):
            acc.append(v)
    stack = lambda vs: vs[0] if len(vs) == 1 else jnp.concatenate(vs, axis=0)
    return (xp.reshape(B, S, D_MODEL), xs.reshape(DB, T, D_MODEL),
            stack(outs[0]), stack(outs[1]), stack(outs[2]), stack(outs[3]), stack(outs[4]), stack(outs[5]))
```

```python
import functools

import jax
import jax.numpy as jnp
from jax import lax
from jax.experimental import pallas as pl
from jax.experimental.pallas import tpu as pltpu

f32 = jnp.float32
bf16 = jnp.bfloat16

D_MODEL = 1024
D_FF = 2816
POOL_WINDOWS = (2, 4, 8, 16)
POOL_WIDTH = D_MODEL
POOL_GROUP = POOL_WIDTH // len(POOL_WINDOWS)
POOL_HIST = max(POOL_WINDOWS) - 1
D_INNER = 2 * D_MODEL
SSD_HEAD_DIM = 64
N_SSD_HEADS = D_INNER // SSD_HEAD_DIM
N_SSD_GROUPS = 4
HEADS_PER_GROUP = N_SSD_HEADS // N_SSD_GROUPS
GROUP_WIDTH = D_INNER // N_SSD_GROUPS
D_STATE = 128
CONV_WIDTH = 4
BC_DIM = N_SSD_GROUPS * D_STATE
CONV_DIM = D_INNER + 2 * BC_DIM
CHUNK = 128
PAST_LEN = 16384
EPS = 1e-6
LOG2_E = 1.4426950408889634

LANES = 128
SUBLANES = 8
MXU_COLS = 256
VMEM_LIMIT_BYTES = 56 * 1024 * 1024
MIXER_VMEM_LIMIT_BYTES = 60 * 1024 * 1024

FFN_ROWS = 512
FFN_CHUNK = 256
PROMPT_TILE = 256
SAMPLE_SEQS = 32
SAMPLE_POST_SEQS = 32
POOL_PAD = 16
CONV_PAD = 8
HEAD_LANES = LANES
SLABS_PER_BUF = MXU_COLS // LANES
N_POOL_BUFS = POOL_WIDTH // MXU_COLS
N_CONV_BUFS = CONV_DIM // MXU_COLS

_NT = (((1,), (1,)), ((), ()))
_TN = (((0,), (0,)), ((), ()))


def _rms(x, g):
    return x * lax.rsqrt(jnp.mean(x * x, axis=-1, keepdims=True) + EPS) * g


def _sigmoid(v):
    return 0.5 * jnp.tanh(0.5 * v) + 0.5


def _silu(v):
    h = 0.5 * v
    return h * jnp.tanh(h) + h


def _softplus(v):
    return jnp.maximum(v, 0.0) + jnp.log1p(jnp.exp(-jnp.abs(v)))


def _dot(a, b):
    return jnp.dot(a, b, preferred_element_type=f32)


def _split3(v):
    hi = v.astype(bf16)
    r1 = v - hi.astype(f32)
    mid = r1.astype(bf16)
    lo = (r1 - mid.astype(f32)).astype(bf16)
    return hi, mid, lo


def _const_spec(shape):
    nd = len(shape)
    return pl.BlockSpec(shape, lambda *_: (0,) * nd, pipeline_mode=pl.Buffered(1))


def _params(n_grid, vmem_limit_bytes=VMEM_LIMIT_BYTES):
    return pltpu.CompilerParams(dimension_semantics=("arbitrary",) * n_grid,
                                vmem_limit_bytes=vmem_limit_bytes)


def _ffn_kernel(xa_ref, xb_ref, g_ref, win_ref, wout_ref, gfin_ref, oa_ref, ob_ref, hn_s, act_s, *, n_a, final_norm,
                side_work=None):
    def tile(x_ref, o_ref, extra=None):
        commit = extra() if extra is not None else None
        hn_s[...] = _rms(x_ref[...], g_ref[...]).astype(bf16)
        for c in range(D_FF // FFN_CHUNK):
            lo = c * FFN_CHUNK
            gate = _dot(hn_s[...], win_ref[:, lo:lo + FFN_CHUNK])
            up = _dot(hn_s[...], win_ref[:, D_FF + lo:D_FF + lo + FFN_CHUNK])
            act_s[:, lo:lo + FFN_CHUNK] = (_silu(gate) * up).astype(bf16)
        out = x_ref[...] + 0.5 * _dot(act_s[...], wout_ref[...])
        if final_norm:
            out = _rms(out, gfin_ref[...])
        o_ref[...] = out
        if commit is not None:
            commit()

    on_a = pl.program_id(0) < n_a
    pl.when(on_a)(lambda: tile(xa_ref, oa_ref, side_work))
    pl.when(jnp.logical_not(on_a))(lambda: tile(xb_ref, ob_ref))


def _ffn_specs(rows_a, rows_b):
    assert rows_a % FFN_ROWS == 0 and rows_b % FFN_ROWS == 0
    n_a, n_b = rows_a // FFN_ROWS, rows_b // FFN_ROWS
    a_spec = pl.BlockSpec((FFN_ROWS, D_MODEL), lambda i: (jnp.minimum(i, n_a - 1), 0))
    b_spec = pl.BlockSpec((FFN_ROWS, D_MODEL), lambda i: (jnp.maximum(i - n_a, 0), 0))
    in_specs = [a_spec, b_spec,
                _const_spec((1, D_MODEL)),
                _const_spec((D_MODEL, 2 * D_FF)),
                _const_spec((D_FF, D_MODEL)),
                _const_spec((1, D_MODEL))]
    out_shapes = [jax.ShapeDtypeStruct((rows_a, D_MODEL), f32), jax.ShapeDtypeStruct((rows_b, D_MODEL), f32)]
    scratch = [pltpu.VMEM((FFN_ROWS, D_MODEL), bf16), pltpu.VMEM((FFN_ROWS, D_FF), bf16)]
    return n_a, n_b, in_specs, [a_spec, b_spec], out_shapes, scratch


def _ffn(xa, xb, g, w_in_b, w_out_b, gfin, *, final_norm):
    n_a, n_b, in_specs, out_specs, out_shapes, scratch = _ffn_specs(xa.shape[0], xb.shape[0])
    return pl.pallas_call(
        functools.partial(_ffn_kernel, n_a=n_a, final_norm=final_norm),
        grid=(n_a + n_b,),
        in_specs=in_specs, out_specs=tuple(out_specs), out_shape=tuple(out_shapes), scratch_shapes=scratch,
        compiler_params=_params(1),
        name="ffn_final" if final_norm else "ffn",
    )(xa, xb, g, w_in_b, w_out_b, gfin)


_IN_PROJ_SPLITS = (POOL_WIDTH, POOL_WIDTH + D_INNER, POOL_WIDTH + D_INNER + CONV_DIM,
                   POOL_WIDTH + D_INNER + CONV_DIM + N_SSD_HEADS)
BF16_SUBLANES = 2 * SUBLANES


def _cast_block(n_rows, n_steps):
    period = 1
    while (n_rows * period) % n_steps or ((n_rows * period) // n_steps) % BF16_SUBLANES:
        period *= 2
        assert period <= n_steps
    return (n_rows * period) // n_steps, period


_PIECE_TILES = (POOL_WIDTH // MXU_COLS, D_INNER // MXU_COLS, CONV_DIM // MXU_COLS, 2 * D_MODEL // MXU_COLS)
_GATE_SKEW = _IN_PROJ_SPLITS[3] - _IN_PROJ_SPLITS[2]


def _store_when(cond, ref, val):
    @pl.when(cond)
    def _():
        ref[...] = val


def _ffn_cast_kernel(xa_ref, xb_ref, g_ref, win_ref, wout_ref, gfin_ref, wt_ref, wdt_f, wg_lo, wg_hi, *rest,
                     n_a, n_plain):
    plain_f = rest[:n_plain]
    oa_ref, ob_ref = rest[n_plain:n_plain + 2]
    wu_o, wz_o, wxbc_o, wdt_o, wgate_o = rest[n_plain + 2:n_plain + 7]
    plain_o = rest[n_plain + 7:2 * n_plain + 7]
    hn_s, act_s = rest[2 * n_plain + 7:]
    i = pl.program_id(0)
    n_u, n_z, n_x, n_g = _PIECE_TILES
    front = n_u + n_z + n_x

    def casts():
        t = wt_ref[...].T.astype(bf16)
        rows = jnp.concatenate([wg_lo[_GATE_SKEW:, :], wg_hi[0:_GATE_SKEW, :]], axis=0)
        tg = rows.T.astype(bf16)
        for src, dst in zip(plain_f, plain_o):
            dst[...] = src[...].astype(bf16)

        def commit():
            _store_when(i < n_u, wu_o, t)
            _store_when(jnp.logical_and(i >= n_u, i < n_u + n_z), wz_o, t)
            _store_when(jnp.logical_and(i >= n_u + n_z, i < front), wxbc_o, t)
            _store_when(jnp.logical_and(i >= front, i < front + n_g), wgate_o, tg)
        return commit

    _ffn_kernel(xa_ref, xb_ref, g_ref, win_ref, wout_ref, gfin_ref, oa_ref, ob_ref, hn_s, act_s,
                n_a=n_a, final_norm=False, side_work=casts)

    @pl.when(i == 0)
    def _():
        t = wdt_f[...].T
        lane = lax.broadcasted_iota(jnp.int32, t.shape, 1)
        wdt_o[...] = jnp.where(lane < N_SSD_HEADS, t, 0.0).astype(bf16)


def _ffn_and_casts(xa, xb, g, w_in_b, w_out_b, gfin, w_in_t, plain):
    n_a, n_b, in_specs, out_specs, out_shapes, scratch = _ffn_specs(xa.shape[0], xb.shape[0])
    n_u, n_z, n_x, n_g = _PIECE_TILES
    front = n_u + n_z + n_x
    assert n_a >= front + n_g and _IN_PROJ_SPLITS[2] == front * MXU_COLS and _IN_PROJ_SPLITS[2] % HEAD_LANES == 0

    def blocked(arr):
        blk, period = _cast_block(arr.shape[0], n_a)
        return (pl.BlockSpec((blk, arr.shape[1]), lambda i: (jnp.minimum(i, n_a - 1) // period, 0)),
                jax.ShapeDtypeStruct(arr.shape, bf16))

    def piece(first_step, n_tiles):
        return (pl.BlockSpec((D_MODEL, MXU_COLS), lambda i: (0, jnp.clip(i - first_step, 0, n_tiles - 1))),
                jax.ShapeDtypeStruct((D_MODEL, n_tiles * MXU_COLS), bf16))

    pieces = [piece(0, n_u), piece(n_u, n_z), piece(n_u + n_z, n_x),
              (pl.BlockSpec((D_MODEL, HEAD_LANES), lambda i: (0, 0)), jax.ShapeDtypeStruct((D_MODEL, HEAD_LANES), bf16)),
              piece(front, n_g)]
    in_proj_specs = [pl.BlockSpec((MXU_COLS, D_MODEL), lambda i: (jnp.minimum(i, front - 1), 0)),
                     pl.BlockSpec((HEAD_LANES, D_MODEL), lambda i: (_IN_PROJ_SPLITS[2] // HEAD_LANES, 0)),
                     pl.BlockSpec((MXU_COLS, D_MODEL), lambda i: (front + jnp.clip(i - front, 0, n_g - 1), 0)),
                     pl.BlockSpec((MXU_COLS, D_MODEL), lambda i: (front + 1 + jnp.clip(i - front, 0, n_g - 1), 0))]
    plains = [blocked(p) for p in plain]
    res = pl.pallas_call(
        functools.partial(_ffn_cast_kernel, n_a=n_a, n_plain=len(plain)),
        grid=(n_a + n_b,),
        in_specs=in_specs + in_proj_specs + [s for s, _ in plains],
        out_specs=tuple(out_specs + [s for s, _ in pieces] + [s for s, _ in plains]),
        out_shape=tuple(out_shapes + [o for _, o in pieces] + [o for _, o in plains]),
        scratch_shapes=scratch,
        compiler_params=_params(1),
        name="ffn_casts",
    )(xa, xb, g, w_in_b, w_out_b, gfin, w_in_t, w_in_t, w_in_t, w_in_t, *plain)
    return res[0], res[1], res[2:7], res[7:]


def _slab(bufs, j):
    return bufs[j // SLABS_PER_BUF], j % SLABS_PER_BUF


def _window_scratch(n_bufs, n_seqs, n_rows):
    return [pltpu.VMEM((SLABS_PER_BUF, n_seqs, n_rows, LANES), f32) for _ in range(n_bufs)]


def _store_rows(bufs, row0, val, T, slab0=0):
    for j in range(val.shape[1] // LANES):
        ref, k = _slab(bufs, slab0 + j)
        S = ref.shape[1]
        blk = val[:, j * LANES:(j + 1) * LANES]
        if S == 1:
            ref[k, 0, row0:row0 + T, :] = blk
        else:
            ref[k, :, row0:row0 + T, :] = blk.reshape(S, T, LANES)


def _window(bufs, j, row0, n):
    ref, k = _slab(bufs, j)
    S = ref.shape[1]
    rows = pl.ds(row0, n) if row0 % SUBLANES == 0 else pl.ds(row0, n, stride=1)
    if S == 1:
        return ref[k, 0, rows, :]
    v = ref[k, :, rows, :]
    return v.reshape(S * n, LANES) if n % SUBLANES == 0 else v


def _pool_branch(hn_s, wu_ref, ubufs, T, pos0, pgw_ref, pscale_ref, pwo_ref):
    S = ubufs[0].shape[1]
    assert T & (T - 1) == 0 and POOL_GROUP == MXU_COLS
    t_idx = jnp.bitwise_and(lax.broadcasted_iota(jnp.int32, (S * T, 1), 0), T - 1)
    n_seen = pos0 + 1 + t_idx
    _store_rows(ubufs, POOL_PAD, _dot(hn_s[...], wu_ref[...]), T)
    mixed = []
    for gi, w in enumerate(POOL_WINDOWS):
        inv_cnt = 1.0 / jnp.minimum(n_seen, w).astype(f32)
        d = []
        for j in range(gi * SLABS_PER_BUF, (gi + 1) * SLABS_PER_BUF):
            cur = _window(ubufs, j, POOL_PAD, T)
            s = cur
            for k in range(1, w):
                s = s + _window(ubufs, j, POOL_PAD - k, T)
            d.append((s * inv_cnt - cur).astype(bf16))
        mixed.append(_dot(jnp.concatenate(d, axis=-1), pgw_ref[gi]))
    y = jnp.concatenate(mixed, axis=-1) * pscale_ref[...]
    return _dot(y.astype(bf16), pwo_ref[...])


def _conv_slab(xbufs, T, convw_ref, convb_ref, j, xs_ref, b_ref, c_ref):
    c0 = j * LANES
    cols = slice(c0, c0 + LANES)
    h = 0.5 * convb_ref[:, cols]
    for k in range(CONV_WIDTH):
        h = h + _window(xbufs, j, CONV_PAD - (CONV_WIDTH - 1) + k, T) * (0.5 * convw_ref[k:k + 1, cols])
    v = h * jnp.tanh(h) + h
    if c0 < D_INNER:
        xs_ref[:, cols] = v
    elif c0 < D_INNER + BC_DIM:
        b_ref[:, c0 - D_INNER:c0 - D_INNER + LANES] = v
    else:
        c_ref[:, c0 - D_INNER - BC_DIM:c0 - D_INNER - BC_DIM + LANES] = v


def _project_conv(hn_s, wxbc_ref, xbufs, T, convw_ref, convb_ref, xs_ref, b_ref, c_ref):
    for jt in range(N_CONV_BUFS):
        slab0 = jt * SLABS_PER_BUF
        _store_rows(xbufs, CONV_PAD, _dot(hn_s[...], wxbc_ref[:, jt * MXU_COLS:(jt + 1) * MXU_COLS]), T, slab0=slab0)
        for j in range(slab0, slab0 + SLABS_PER_BUF):
            _conv_slab(xbufs, T, convw_ref, convb_ref, j, xs_ref, b_ref, c_ref)


def _gate_merge_out(x, hn_s, y_ref, bp, wz_ref, wgate_ref, snorm_ref, swo_ref, wo_ref, yn_s):
    for g in range(N_SSD_GROUPS):
        cols = slice(g * GROUP_WIDTH, (g + 1) * GROUP_WIDTH)
        z = _dot(hn_s[...], wz_ref[:, cols])
        yg = y_ref[:, cols] * _silu(z)
        yg = yg * lax.rsqrt(jnp.mean(yg * yg, axis=-1, keepdims=True) + EPS)
        yn_s[:, cols] = (yg * snorm_ref[:, cols]).astype(bf16)
    branch_ssd = _dot(yn_s[...], swo_ref[...])
    gate_pool = _sigmoid(_dot(hn_s[...], wgate_ref[:, 0:D_MODEL]))
    gate_ssd = _sigmoid(_dot(hn_s[...], wgate_ref[:, D_MODEL:2 * D_MODEL]))
    merged = (gate_pool * bp + gate_ssd * branch_ssd).astype(bf16)
    return x + _dot(merged, wo_ref[...])


def _ssd_chunk(rows, xs_s, b_s, c_s, dt_s, y_s, ht_s, alog_ref, dskip_ref, e_ref):
    L = CHUNK
    ri = lax.broadcasted_iota(jnp.int32, (L, L), 0)
    ci = lax.broadcasted_iota(jnp.int32, (L, L), 1)
    causal = ri >= ci
    tril = jnp.where(causal, 1.0, 0.0).astype(bf16)
    first_head = lax.broadcasted_iota(jnp.int32, (L, LANES), 1) < SSD_HEAD_DIM
    keep_first = jnp.where(first_head, 1.0, 0.0).astype(bf16)
    keep_second = jnp.where(first_head, 0.0, 1.0).astype(bf16)

    dt = dt_s[rows, :]
    dA = dt * (-jnp.exp(alog_ref[...]))
    hi, mid, lo = _split3(dA)
    acc = _dot(tril, jnp.concatenate([hi, mid, lo], axis=1))
    a = (acc[:, 0:LANES] + acc[:, LANES:2 * LANES] + acc[:, 2 * LANES:3 * LANES]) * LOG2_E
    aT = a.T[0:N_SSD_HEADS]
    dtT = dt.T[0:N_SSD_HEADS]
    wT = jnp.exp2(aT[:, L - 1:L] - aT) * dtT
    srcT = aT - jnp.log2(dtT)
    end_decay = jnp.broadcast_to(jnp.exp2(a[L - 1:L, :]), (2 * SUBLANES, HEAD_LANES))
    e_hi, e_mid, e_lo = (t.astype(f32) for t in _split3(end_decay))
    sel = lax.broadcasted_iota(jnp.int32, (2 * SUBLANES, HEAD_LANES), 0)
    stacked = jnp.where(sel == 0, e_hi, jnp.where(sel == 1, e_mid, jnp.where(sel == 2, e_lo, 0.0)))
    cdec = jnp.sum(_dot(stacked.astype(bf16), e_ref[...]), axis=0, keepdims=True)

    def head_mats(h, cb, bT):
        acol = jnp.broadcast_to(a[:, h:h + 1], (L, L))
        m = (cb * jnp.exp2(jnp.where(causal, acol - srcT[h:h + 1, :], -jnp.inf))).astype(bf16)
        bw = (bT * wT[h:h + 1, :]).astype(bf16)
        return m, bw, acol

    pairs_per_group = HEADS_PER_GROUP // 2
    for q in range(N_SSD_HEADS // 2):
        g, qg = divmod(q, pairs_per_group)
        if qg == 0:
            gcols = slice(g * D_STATE, (g + 1) * D_STATE)
            b_f = b_s[rows, gcols]
            c_b = c_s[rows, gcols].astype(bf16)
            cb = lax.dot_general(c_b, b_f.astype(bf16), _NT, preferred_element_type=f32)
            bT = b_f.T
            ch = _dot(c_b, ht_s[:, g * GROUP_WIDTH:(g + 1) * GROUP_WIDTH].astype(bf16))
        cols = slice(q * LANES, (q + 1) * LANES)
        xq = xs_s[rows, cols]
        xq_b = xq.astype(bf16)
        x2 = jnp.concatenate([xq_b * keep_first, xq_b * keep_second], axis=0)
        m_a, bw_a, acol_a = head_mats(2 * q, cb, bT)
        m_b, bw_b, acol_b = head_mats(2 * q + 1, cb, bT)
        y_off = ch[:, qg * LANES:(qg + 1) * LANES] * jnp.exp2(jnp.where(first_head, acol_a, acol_b))
        y_s[rows, cols] = _dot(jnp.concatenate([m_a, m_b], axis=1), x2) + y_off + dskip_ref[:, cols] * xq
        st = _dot(jnp.concatenate([bw_a, bw_b], axis=1), x2)
        ht_s[:, cols] = ht_s[:, cols] * cdec[:, cols] + st


def _mixer_prompt_kernel(x_ref, dxs_ref, db_ref, dc_ref, ddt_ref, dh0_ref,
                         gmix_ref, wu_ref, wz_ref, wxbc_ref, wdt_ref, wgate_ref, pgw_ref, pscale_ref,
                         pwo_ref, convw_ref, convb_ref, dtb_ref, alog_ref, dskip_ref, snorm_ref, swo_ref,
                         wo_ref, e_ref,
                         o_ref, npool_ref, nconv_ref, nssm_ref, dy_ref, dhn_ref,
                         hn_s, xs_s, b_s, c_s, dt_s, y_s, ht_s, yn_s, *window_bufs, decode_T):
    T = PROMPT_TILE
    ti = pl.program_id(1)
    ubufs, xbufs = window_bufs[:N_POOL_BUFS], window_bufs[N_POOL_BUFS:]

    @pl.when(ti == 0)
    def _():
        for ref in ubufs:
            ref[:, :, 0:POOL_PAD, :] = jnp.zeros((SLABS_PER_BUF, 1, POOL_PAD, LANES), f32)
        for ref in xbufs:
            ref[:, :, 0:CONV_PAD, :] = jnp.zeros((SLABS_PER_BUF, 1, CONV_PAD, LANES), f32)
        ht_s[...] = jnp.zeros(ht_s.shape, f32)

    x = x_ref[0]
    hn_s[...] = _rms(x, gmix_ref[...]).astype(bf16)

    bp = _pool_branch(hn_s, wu_ref, ubufs, T, ti * T, pgw_ref, pscale_ref, pwo_ref)

    _project_conv(hn_s, wxbc_ref, xbufs, T, convw_ref, convb_ref, xs_s, b_s, c_s)
    dt_s[...] = _softplus(_dot(hn_s[...], wdt_ref[...]) + dtb_ref[...])

    def chunk(c, carry):
        rows = pl.ds(pl.multiple_of(c * CHUNK, CHUNK), CHUNK)
        _ssd_chunk(rows, xs_s, b_s, c_s, dt_s, y_s, ht_s, alog_ref, dskip_ref, e_ref)
        return carry

    lax.fori_loop(0, T // CHUNK, chunk, 0)

    o_ref[0] = _gate_merge_out(x, hn_s, y_s, bp, wz_ref, wgate_ref, snorm_ref, swo_ref, wo_ref, yn_s)

    @pl.when(ti == pl.num_programs(1) - 1)
    def _():
        for j in range(POOL_WIDTH // LANES):
            npool_ref[0, 0, :, j * LANES:(j + 1) * LANES] = _window(ubufs, j, POOL_PAD + T - POOL_HIST, POOL_HIST)
        for j in range(CONV_DIM // LANES):
            nconv_ref[0, 0, :, j * LANES:(j + 1) * LANES] = _window(
                xbufs, j, CONV_PAD + T - (CONV_WIDTH - 1), CONV_WIDTH - 1)
        for g in range(N_SSD_GROUPS):
            hg = ht_s[:, g * GROUP_WIDTH:(g + 1) * GROUP_WIDTH].T
            nssm_ref[0, 0, g * HEADS_PER_GROUP:(g + 1) * HEADS_PER_GROUP] = hg.reshape(
                HEADS_PER_GROUP, SSD_HEAD_DIM, D_STATE)

    _decode_seqs(dxs_ref, db_ref, dc_ref, ddt_ref, dh0_ref, alog_ref, dskip_ref, e_ref, dy_ref, dhn_ref,
                 T=decode_T, n_seqs=dxs_ref.shape[0])

    for ref in ubufs:
        ref[:, :, 0:POOL_PAD, :] = ref[:, :, T:T + POOL_PAD, :]
    for ref in xbufs:
        ref[:, :, 0:CONV_PAD, :] = ref[:, :, T:T + CONV_PAD, :]


def _mixer_prompt(x1, w, dxs, dbm, dcm, ddt, dh0, decode_T):
    B, S, _ = x1.shape
    T = PROMPT_TILE
    assert S % T == 0 and T % CHUNK == 0 and T >= POOL_PAD
    n_tiles = S // T
    nseq = dxs.shape[0] // decode_T
    assert nseq % (B * n_tiles) == 0
    Q = nseq // (B * n_tiles)
    seq_spec = lambda width: pl.BlockSpec((Q, decode_T, width), lambda b, t: (b * n_tiles + t, 0, 0))
    state_spec = pl.BlockSpec((1, Q, N_SSD_HEADS, SSD_HEAD_DIM, D_STATE), lambda b, t: (0, b * n_tiles + t, 0, 0, 0))
    as_seqs = lambda v: v.reshape(nseq, decode_T, v.shape[1])
    consts = [w["gmix"], w["wu"], w["wz"], w["wxbc"], w["wdt"], w["wgate"], w["pgw"], w["pscale"], w["pwo"],
              w["convw"], w["convb"], w["dtb"], w["alog"], w["dskip"], w["snorm"], w["swo"], w["wo"], w["expand"]]
    return pl.pallas_call(
        functools.partial(_mixer_prompt_kernel, decode_T=decode_T),
        grid=(B, n_tiles),
        in_specs=[pl.BlockSpec((1, T, D_MODEL), lambda b, t: (b, t, 0)),
                  seq_spec(D_INNER), seq_spec(BC_DIM), seq_spec(BC_DIM), seq_spec(HEAD_LANES), state_spec]
                 + [_const_spec(c.shape) for c in consts],
        out_specs=(pl.BlockSpec((1, T, D_MODEL), lambda b, t: (b, t, 0)),
                   pl.BlockSpec((1, 1, POOL_HIST, POOL_WIDTH), lambda b, t: (0, b, 0, 0)),
                   pl.BlockSpec((1, 1, CONV_WIDTH - 1, CONV_DIM), lambda b, t: (0, b, 0, 0)),
                   pl.BlockSpec((1, 1, N_SSD_HEADS, SSD_HEAD_DIM, D_STATE), lambda b, t: (0, b, 0, 0, 0)),
                   seq_spec(D_INNER), state_spec),
        out_shape=(jax.ShapeDtypeStruct((B, S, D_MODEL), f32),
                   jax.ShapeDtypeStruct((1, B, POOL_HIST, POOL_WIDTH), f32),
                   jax.ShapeDtypeStruct((1, B, CONV_WIDTH - 1, CONV_DIM), f32),
                   jax.ShapeDtypeStruct((1, B, N_SSD_HEADS, SSD_HEAD_DIM, D_STATE), f32),
                   jax.ShapeDtypeStruct((nseq, decode_T, D_INNER), f32),
                   jax.ShapeDtypeStruct(dh0.shape, f32)),
        scratch_shapes=[pltpu.VMEM((T, D_MODEL), bf16),
                        pltpu.VMEM((T, D_INNER), f32),
                        pltpu.VMEM((T, BC_DIM), f32),
                        pltpu.VMEM((T, BC_DIM), f32),
                        pltpu.VMEM((T, HEAD_LANES), f32),
                        pltpu.VMEM((T, D_INNER), f32),
                        pltpu.VMEM((D_STATE, D_INNER), f32),
                        pltpu.VMEM((T, D_INNER), bf16)]
                       + _window_scratch(N_POOL_BUFS, 1, POOL_PAD + T)
                       + _window_scratch(N_CONV_BUFS, 1, CONV_PAD + T),
        compiler_params=_params(2, MIXER_VMEM_LIMIT_BYTES),
        name="mixer_prompt",
    )(x1, as_seqs(dxs), as_seqs(dbm), as_seqs(dcm), as_seqs(ddt), dh0, *consts)


def _load_history(bufs, hist_ref, row0):
    for j in range(hist_ref.shape[2] // LANES):
        ref, k = _slab(bufs, j)
        n_slabs, S, n_rows, _ = ref.shape
        flat = ref.reshape(n_slabs, S * n_rows, LANES)
        for r in range(hist_ref.shape[0]):
            flat[k, pl.ds(row0 + r, S, stride=n_rows), :] = hist_ref[r, :, j * LANES:(j + 1) * LANES]


def _store_history(hist_ref, bufs, row0):
    for j in range(hist_ref.shape[2] // LANES):
        ref, k = _slab(bufs, j)
        n_slabs, S, n_rows, _ = ref.shape
        flat = ref.reshape(n_slabs, S * n_rows, LANES)
        for r in range(hist_ref.shape[0]):
            hist_ref[r, :, j * LANES:(j + 1) * LANES] = flat[k, pl.ds(row0 + r, S, stride=n_rows), :]


def _sample_pre_kernel(x_ref, ph_ref, ch_ref, gmix_ref, wu_ref, wxbc_ref, wdt_ref, pgw_ref, pscale_ref, pwo_ref,
                       convw_ref, convb_ref, dtb_ref,
                       bp_ref, xs_ref, b_ref, c_ref, dt_ref, npool_ref, nconv_ref,
                       *window_bufs, T):
    ubufs, xbufs = window_bufs[:N_POOL_BUFS], window_bufs[N_POOL_BUFS:]
    hn = _rms(x_ref[...], gmix_ref[...]).astype(bf16)

    _load_history(ubufs, ph_ref, POOL_PAD - POOL_HIST)
    bp_ref[...] = _pool_branch(hn, wu_ref, ubufs, T, PAST_LEN, pgw_ref, pscale_ref, pwo_ref)
    _store_history(npool_ref, ubufs, POOL_PAD + T - POOL_HIST)

    _load_history(xbufs, ch_ref, CONV_PAD - (CONV_WIDTH - 1))
    _project_conv(hn, wxbc_ref, xbufs, T, convw_ref, convb_ref, xs_ref, b_ref, c_ref)
    _store_history(nconv_ref, xbufs, CONV_PAD + T - (CONV_WIDTH - 1))
    dt_ref[...] = _softplus(_dot(hn, wdt_ref[...]) + dtb_ref[...])


def _sample_pre(x1, pool_hist, conv_hist, w, T):
    rows = x1.shape[0]
    nseq = rows // T
    S = SAMPLE_SEQS
    R = S * T
    assert nseq % S == 0 and T % SUBLANES == 0
    consts = [w["gmix"], w["wu"], w["wxbc"], w["wdt"], w["pgw"], w["pscale"], w["pwo"], w["convw"], w["convb"],
              w["dtb"]]
    row_spec = lambda width: pl.BlockSpec((R, width), lambda i: (i, 0))
    return pl.pallas_call(
        functools.partial(_sample_pre_kernel, T=T),
        grid=(nseq // S,),
        in_specs=[row_spec(D_MODEL),
                  pl.BlockSpec((POOL_HIST, S, POOL_WIDTH), lambda i: (0, i, 0)),
                  pl.BlockSpec((CONV_WIDTH - 1, S, CONV_DIM), lambda i: (0, i, 0))]
                 + [_const_spec(c.shape) for c in consts],
        out_specs=(row_spec(D_MODEL), row_spec(D_INNER), row_spec(BC_DIM), row_spec(BC_DIM), row_spec(HEAD_LANES),
                   pl.BlockSpec((POOL_HIST, S, POOL_WIDTH), lambda i: (0, i, 0)),
                   pl.BlockSpec((CONV_WIDTH - 1, S, CONV_DIM), lambda i: (0, i, 0))),
        out_shape=(jax.ShapeDtypeStruct((rows, D_MODEL), f32),
                   jax.ShapeDtypeStruct((rows, D_INNER), f32),
                   jax.ShapeDtypeStruct((rows, BC_DIM), f32),
                   jax.ShapeDtypeStruct((rows, BC_DIM), f32),
                   jax.ShapeDtypeStruct((rows, HEAD_LANES), f32),
                   jax.ShapeDtypeStruct((POOL_HIST, nseq, POOL_WIDTH), f32),
                   jax.ShapeDtypeStruct((CONV_WIDTH - 1, nseq, CONV_DIM), f32)),
        scratch_shapes=_window_scratch(N_POOL_BUFS, S, POOL_PAD + T) + _window_scratch(N_CONV_BUFS, S, CONV_PAD + T),
        compiler_params=_params(1),
        name="sample_pre",
    )(x1, pool_hist, conv_hist, *consts)


def _decode_seqs(xs_ref, b_ref, c_ref, dt_ref, h0_ref, alog_ref, dskip_ref, e_ref, y_ref, hn_ref, *, T, n_seqs):
    assert T == SUBLANES
    lane_group = lax.broadcasted_iota(jnp.int32, (T, HEAD_LANES), 1) // HEADS_PER_GROUP
    neg_a = -jnp.exp(alog_ref[...])

    def shift(v, d):
        if d == 0:
            return v
        r = lax.broadcasted_iota(jnp.int32, v.shape, 0)
        return jnp.where(r >= d, pltpu.roll(v, d, axis=0), 0.0)

    def per_seq(s, carry):
        dt = dt_ref[s]
        a = dt * neg_a
        d = 1
        while d < T:
            a = a + shift(a, d)
            d *= 2
        a_end = a[T - 1:T, :]
        x = xs_ref[s]
        bm = b_ref[s]
        cm = c_ref[s]
        terms = []
        for d in range(T):
            cbv = jnp.zeros((T, HEAD_LANES), f32)
            for g in range(N_SSD_GROUPS):
                gc = slice(g * D_STATE, (g + 1) * D_STATE)
                cb = jnp.sum(cm[:, gc] * shift(bm[:, gc], d), axis=-1, keepdims=True)
                cbv = jnp.where(lane_group == g, cb, cbv)
            terms.append(cbv * jnp.exp(a - shift(a, d)) * shift(dt, d))
        terms.append(jnp.exp(a))
        terms.append(jnp.exp(a_end - a) * dt)
        v = jnp.concatenate(terms, axis=0)
        ex = _dot(v.astype(bf16), e_ref[...])

        y = dskip_ref[...] * x
        for d in range(T):
            y = y + ex[d * T:(d + 1) * T] * shift(x, d)
        ea = ex[T * T:(T + 1) * T]
        xw = (x * ex[(T + 1) * T:(T + 2) * T]).astype(bf16)
        cdec = jnp.exp(a_end)
        y_groups = []
        for g in range(N_SSD_GROUPS):
            gc = slice(g * D_STATE, (g + 1) * D_STATE)
            cols = slice(g * GROUP_WIDTH, (g + 1) * GROUP_WIDTH)
            heads = slice(g * HEADS_PER_GROUP, (g + 1) * HEADS_PER_GROUP)
            h0g = h0_ref[0, s, heads].reshape(GROUP_WIDTH, D_STATE)
            ch = lax.dot_general(cm[:, gc].astype(bf16), h0g.astype(bf16), _NT, preferred_element_type=f32)
            y_groups.append(y[:, cols] + ea[:, cols] * ch)
            st = lax.dot_general(xw[:, cols], bm[:, gc].astype(bf16), _TN, preferred_element_type=f32)
            for r in range(HEADS_PER_GROUP):
                h = g * HEADS_PER_GROUP + r
                hn_ref[0, s, h] = (h0_ref[0, s, h] * cdec[:, h:h + 1]
                                   + st[r * SSD_HEAD_DIM:(r + 1) * SSD_HEAD_DIM, :])
        y_ref[s] = jnp.concatenate(y_groups, axis=-1)
        return carry

    lax.fori_loop(0, n_seqs, per_seq, 0, unroll=True)


def _sample_post_kernel(x_ref, y_ref, bp_ref, gmix_ref, wz_ref, wgate_ref, snorm_ref, swo_ref, wo_ref,
                        o_ref, hn_s, yn_s):
    x = x_ref[...]
    hn_s[...] = _rms(x, gmix_ref[...]).astype(bf16)
    o_ref[...] = _gate_merge_out(x, hn_s, y_ref, bp_ref[...], wz_ref, wgate_ref, snorm_ref, swo_ref, wo_ref, yn_s)


def _sample_post(x1, y, bp, w, T):
    rows = x1.shape[0]
    R = SAMPLE_POST_SEQS * T
    assert rows % R == 0
    consts = [w["gmix"], w["wz"], w["wgate"], w["snorm"], w["swo"], w["wo"]]
    row_spec = lambda width: pl.BlockSpec((R, width), lambda i: (i, 0))
    return pl.pallas_call(
        _sample_post_kernel,
        grid=(rows // R,),
        in_specs=[row_spec(D_MODEL), row_spec(D_INNER), row_spec(D_MODEL)] + [_const_spec(c.shape) for c in consts],
        out_specs=row_spec(D_MODEL),
        out_shape=jax.ShapeDtypeStruct((rows, D_MODEL), f32),
        scratch_shapes=[pltpu.VMEM((R, D_MODEL), bf16), pltpu.VMEM((R, D_INNER), bf16)],
        compiler_params=_params(1),
        name="sample_post",
    )(x1, y, bp, *consts)


def _small_params(l, norm_mix, pool_scale, conv_w, conv_b, dt_bias, a_log, d_skip, ssd_norm):
    pad_heads = lambda v: jnp.pad(v, ((0, 0), (0, HEAD_LANES - N_SSD_HEADS)))
    head_of_lane = jnp.arange(D_INNER, dtype=jnp.int32) // SSD_HEAD_DIM
    expand = (jnp.arange(HEAD_LANES, dtype=jnp.int32)[:, None] == head_of_lane[None, :]).astype(bf16)
    return dict(
        gmix=norm_mix[l][None, :], pscale=pool_scale[l][None, :], convw=conv_w[l], convb=conv_b[l][None, :],
        dtb=pad_heads(dt_bias[l][None, :]), alog=pad_heads(a_log[l][None, :]),
        dskip=jnp.repeat(d_skip[l], SSD_HEAD_DIM)[None, :], snorm=ssd_norm[l][None, :], expand=expand)


def kernel(x_prompt, x_sample, state_pool, state_conv, state_ssm, norm_ffn1, ffn1_w_in, ffn1_w_out, norm_mix, w_in,
           pool_w_group, pool_scale, pool_w_out, conv_w, conv_b, dt_bias, a_log, d_skip, ssd_norm, ssd_w_out, w_o,
           norm_ffn2, ffn2_w_in, ffn2_w_out, norm_final):
    depth = w_in.shape[0]
    B, S, _ = x_prompt.shape
    DB, T, _ = x_sample.shape
    gfin = norm_final[None, :]
    xp = x_prompt.reshape(B * S, D_MODEL)
    xs = x_sample.reshape(DB * T, D_MODEL)
    outs = [[] for _ in range(6)]
    for l in range(depth):
        last = l == depth - 1
        f1_in, f1_out = ffn1_w_in[l].astype(bf16), ffn1_w_out[l].astype(bf16)
        g1, g2 = norm_ffn1[l][None, :], norm_ffn2[l][None, :]
        w = _small_params(l, norm_mix, pool_scale, conv_w, conv_b, dt_bias, a_log, d_skip, ssd_norm)
        n_pg = pool_w_group.shape[1]
        xp, xs, in_proj, casts = _ffn_and_casts(
            xp, xs, g1, f1_in, f1_out, gfin, w_in[l].T,
            [pool_w_group[l].reshape(n_pg * POOL_GROUP, POOL_GROUP), pool_w_out[l], ssd_w_out[l], w_o[l],
             ffn2_w_in[l], ffn2_w_out[l]])
        w.update(zip(("wu", "wz", "wxbc", "wdt", "wgate"), in_proj))
        w.update(pgw=casts[0].reshape(n_pg, POOL_GROUP, POOL_GROUP), pwo=casts[1], swo=casts[2], wo=casts[3])
        f2_in, f2_out = casts[4], casts[5]
        hist_major = lambda v: jnp.transpose(v, (1, 0, 2))
        bp, cxs, cb, cc, cdt, spool, sconv = _sample_pre(xs, hist_major(state_pool[l]), hist_major(state_conv[l]), w, T)
        spool, sconv = hist_major(spool)[None], hist_major(sconv)[None]
        xp3, npool, nconv, nssm, y, sssm = _mixer_prompt(xp.reshape(B, S, D_MODEL), w, cxs, cb, cc, cdt,
                                                         state_ssm[l:l + 1], T)
        xs = _sample_post(xs, y.reshape(DB * T, D_INNER), bp, w, T)
        xp, xs = _ffn(xp3.reshape(B * S, D_MODEL), xs, g2, f2_in, f2_out, gfin, final_norm=last)
        for acc, v in zip(outs, (npool, nconv, nssm, spool, sconv, sssm)):
            acc.append(v)
    stack = lambda vs: vs[0] if len(vs) == 1 else jnp.concatenate(vs, axis=0)
    return (xp.reshape(B, S, D_MODEL), xs.reshape(DB, T, D_MODEL),
            stack(outs[0]), stack(outs[1]), stack(outs[2]), stack(outs[3]), stack(outs[4]), stack(outs[5]))
```

```python
import functools

import jax
import jax.numpy as jnp
from jax import lax
from jax.experimental import pallas as pl
from jax.experimental.pallas import tpu as pltpu

f32 = jnp.float32
bf16 = jnp.bfloat16

D_MODEL = 1024
D_FF = 2816
POOL_WINDOWS = (2, 4, 8, 16)
POOL_WIDTH = D_MODEL
POOL_GROUP = POOL_WIDTH // len(POOL_WINDOWS)
POOL_HIST = max(POOL_WINDOWS) - 1
D_INNER = 2 * D_MODEL
SSD_HEAD_DIM = 64
N_SSD_HEADS = D_INNER // SSD_HEAD_DIM
N_SSD_GROUPS = 4
HEADS_PER_GROUP = N_SSD_HEADS // N_SSD_GROUPS
GROUP_WIDTH = D_INNER // N_SSD_GROUPS
D_STATE = 128
CONV_WIDTH = 4
BC_DIM = N_SSD_GROUPS * D_STATE
CONV_DIM = D_INNER + 2 * BC_DIM
CHUNK = 128
PAST_LEN = 16384
EPS = 1e-6
LOG2_E = 1.4426950408889634

LANES = 128
SUBLANES = 8
MXU_COLS = 256
VMEM_LIMIT_BYTES = 56 * 1024 * 1024
LARGE_VMEM_LIMIT_BYTES = 60 * 1024 * 1024

FFN_ROWS = 512
FFN_CHUNK = 256
FFN_STAGE_COLS = 128
FFN_STAGE_SLOTS = 2
PROMPT_TILE = 256
SAMPLE_SEQS = 32
SAMPLE_POST_SEQS = 32
POOL_PAD = 16
CONV_PAD = 8
HEAD_LANES = LANES
SLABS_PER_BUF = MXU_COLS // LANES
N_POOL_BUFS = POOL_WIDTH // MXU_COLS
N_CONV_BUFS = CONV_DIM // MXU_COLS

_NT = (((1,), (1,)), ((), ()))
_TN = (((0,), (0,)), ((), ()))


def _rms(x, g):
    return x * lax.rsqrt(jnp.mean(x * x, axis=-1, keepdims=True) + EPS) * g


def _sigmoid(v):
    return 0.5 * jnp.tanh(0.5 * v) + 0.5


def _silu(v):
    h = 0.5 * v
    return h * jnp.tanh(h) + h


def _softplus(v):
    return jnp.maximum(v, 0.0) + jnp.log1p(jnp.exp(-jnp.abs(v)))


def _dot(a, b):
    return jnp.dot(a, b, preferred_element_type=f32)


def _split3(v):
    hi = v.astype(bf16)
    r1 = v - hi.astype(f32)
    mid = r1.astype(bf16)
    lo = (r1 - mid.astype(f32)).astype(bf16)
    return hi, mid, lo


def _const_spec(shape):
    nd = len(shape)
    return pl.BlockSpec(shape, lambda *_: (0,) * nd, pipeline_mode=pl.Buffered(1))


def _params(n_grid, vmem_limit_bytes=VMEM_LIMIT_BYTES):
    return pltpu.CompilerParams(dimension_semantics=("arbitrary",) * n_grid,
                                vmem_limit_bytes=vmem_limit_bytes)


def _ffn_kernel(xa_ref, xb_ref, g_ref, win_ref, wout_ref, gfin_ref, oa_ref, ob_ref, hn_s, act_s, *, n_a, final_norm,
                side_work=None, fetch=None):
    def tile(x_ref, o_ref, extra=None, fetch=None):
        commit = extra() if extra is not None else None
        hn_s[...] = _rms(x_ref[...], g_ref[...]).astype(bf16)
        for c in range(D_FF // FFN_CHUNK):
            lo = c * FFN_CHUNK
            if fetch is not None:
                fetch(c)
            gate = _dot(hn_s[...], win_ref[:, lo:lo + FFN_CHUNK])
            up = _dot(hn_s[...], win_ref[:, D_FF + lo:D_FF + lo + FFN_CHUNK])
            act_s[:, lo:lo + FFN_CHUNK] = (_silu(gate) * up).astype(bf16)
        out = x_ref[...] + 0.5 * _dot(act_s[...], wout_ref[...])
        if final_norm:
            out = _rms(out, gfin_ref[...])
        o_ref[...] = out
        if commit is not None:
            commit()

    i = pl.program_id(0)
    on_a = i < n_a
    if fetch is None:
        pl.when(on_a)(lambda: tile(xa_ref, oa_ref, side_work))
    else:
        pl.when(i == 0)(lambda: tile(xa_ref, oa_ref, side_work, fetch))
        pl.when(jnp.logical_and(i > 0, on_a))(lambda: tile(xa_ref, oa_ref, side_work))
    pl.when(jnp.logical_not(on_a))(lambda: tile(xb_ref, ob_ref))


def _ffn_specs(rows_a, rows_b):
    assert rows_a % FFN_ROWS == 0 and rows_b % FFN_ROWS == 0
    n_a, n_b = rows_a // FFN_ROWS, rows_b // FFN_ROWS
    a_spec = pl.BlockSpec((FFN_ROWS, D_MODEL), lambda i: (jnp.minimum(i, n_a - 1), 0))
    b_spec = pl.BlockSpec((FFN_ROWS, D_MODEL), lambda i: (jnp.maximum(i - n_a, 0), 0))
    in_specs = [a_spec, b_spec,
                _const_spec((1, D_MODEL)),
                _const_spec((D_MODEL, 2 * D_FF)),
                _const_spec((D_FF, D_MODEL)),
                _const_spec((1, D_MODEL))]
    out_shapes = [jax.ShapeDtypeStruct((rows_a, D_MODEL), f32), jax.ShapeDtypeStruct((rows_b, D_MODEL), f32)]
    scratch = [pltpu.VMEM((FFN_ROWS, D_MODEL), bf16), pltpu.VMEM((FFN_ROWS, D_FF), bf16)]
    return n_a, n_b, in_specs, [a_spec, b_spec], out_shapes, scratch


def _ffn(xa, xb, g, w_in_b, w_out_b, gfin, *, final_norm):
    n_a, n_b, in_specs, out_specs, out_shapes, scratch = _ffn_specs(xa.shape[0], xb.shape[0])
    return pl.pallas_call(
        functools.partial(_ffn_kernel, n_a=n_a, final_norm=final_norm),
        grid=(n_a + n_b,),
        in_specs=in_specs, out_specs=tuple(out_specs), out_shape=tuple(out_shapes), scratch_shapes=scratch,
        compiler_params=_params(1),
        name="ffn_final" if final_norm else "ffn",
    )(xa, xb, g, w_in_b, w_out_b, gfin)


_IN_PROJ_SPLITS = (POOL_WIDTH, POOL_WIDTH + D_INNER, POOL_WIDTH + D_INNER + CONV_DIM,
                   POOL_WIDTH + D_INNER + CONV_DIM + N_SSD_HEADS)
BF16_SUBLANES = 2 * SUBLANES


def _cast_block(n_rows, n_steps):
    period = 1
    while (n_rows * period) % n_steps or ((n_rows * period) // n_steps) % BF16_SUBLANES:
        period *= 2
        assert period <= n_steps
    return (n_rows * period) // n_steps, period


_PIECE_TILES = (POOL_WIDTH // MXU_COLS, D_INNER // MXU_COLS, CONV_DIM // MXU_COLS, 2 * D_MODEL // MXU_COLS)
_GATE_SKEW = _IN_PROJ_SPLITS[3] - _IN_PROJ_SPLITS[2]


def _store_when(cond, ref, val):
    @pl.when(cond)
    def _():
        ref[...] = val


def _ffn_cast_kernel(xa_ref, xb_ref, g_ref, win_hbm, wout_hbm, gfin_ref, wt_ref, wdt_f, wg_lo, wg_hi, *rest,
                     n_a, n_plain):
    plain_f = rest[:n_plain]
    oa_ref, ob_ref = rest[n_plain:n_plain + 2]
    wu_o, wz_o, wxbc_o, wdt_o, wgate_o = rest[n_plain + 2:n_plain + 7]
    plain_o = rest[n_plain + 7:2 * n_plain + 7]
    hn_s, act_s, win_s, wout_s, stage_in, stage_out, sems = rest[2 * n_plain + 7:]
    i = pl.program_id(0)
    n_u, n_z, n_x, n_g = _PIECE_TILES
    front = n_u + n_z + n_x

    def stage_copies(k):
        slot, lo = k % FFN_STAGE_SLOTS, k * FFN_STAGE_COLS
        return (pltpu.make_async_copy(win_hbm.at[:, pl.ds(lo, FFN_STAGE_COLS)], stage_in.at[slot, 0],
                                      sems.at[slot, 0]),
                pltpu.make_async_copy(win_hbm.at[:, pl.ds(D_FF + lo, FFN_STAGE_COLS)], stage_in.at[slot, 1],
                                      sems.at[slot, 1]),
                pltpu.make_async_copy(wout_hbm.at[pl.ds(lo, FFN_STAGE_COLS), :], stage_out.at[slot],
                                      sems.at[slot, 2]))

    def fetch(c):
        per_chunk = FFN_CHUNK // FFN_STAGE_COLS
        for k in range(c * per_chunk, (c + 1) * per_chunk):
            slot, lo = k % FFN_STAGE_SLOTS, k * FFN_STAGE_COLS
            for nxt in (range(FFN_STAGE_SLOTS) if k == 0 else [k + FFN_STAGE_SLOTS - 1]):
                if nxt < D_FF // FFN_STAGE_COLS:
                    for cp in stage_copies(nxt):
                        cp.start()
            for cp in stage_copies(k):
                cp.wait()
            win_s[:, lo:lo + FFN_STAGE_COLS] = stage_in[slot, 0].astype(bf16)
            win_s[:, D_FF + lo:D_FF + lo + FFN_STAGE_COLS] = stage_in[slot, 1].astype(bf16)
            wout_s[lo:lo + FFN_STAGE_COLS, :] = stage_out[slot].astype(bf16)

    def casts():
        t = wt_ref[...].T.astype(bf16)
        rows = jnp.concatenate([wg_lo[_GATE_SKEW:, :], wg_hi[0:_GATE_SKEW, :]], axis=0)
        tg = rows.T.astype(bf16)
        for src, dst in zip(plain_f, plain_o):
            dst[...] = src[...].astype(bf16)

        def commit():
            _store_when(i < n_u, wu_o, t)
            _store_when(jnp.logical_and(i >= n_u, i < n_u + n_z), wz_o, t)
            _store_when(jnp.logical_and(i >= n_u + n_z, i < front), wxbc_o, t)
            _store_when(jnp.logical_and(i >= front, i < front + n_g), wgate_o, tg)
        return commit

    _ffn_kernel(xa_ref, xb_ref, g_ref, win_s, wout_s, gfin_ref, oa_ref, ob_ref, hn_s, act_s,
                n_a=n_a, final_norm=False, side_work=casts, fetch=fetch)

    @pl.when(i == 0)
    def _():
        t = wdt_f[...].T
        lane = lax.broadcasted_iota(jnp.int32, t.shape, 1)
        wdt_o[...] = jnp.where(lane < N_SSD_HEADS, t, 0.0).astype(bf16)


def _ffn_and_casts(xa, xb, g, w_in_f, w_out_f, gfin, w_in_t, plain):
    n_a, n_b, in_specs, out_specs, out_shapes, scratch = _ffn_specs(xa.shape[0], xb.shape[0])
    in_specs[3] = in_specs[4] = pl.BlockSpec(memory_space=pl.ANY)
    scratch = scratch + [pltpu.VMEM((D_MODEL, 2 * D_FF), bf16), pltpu.VMEM((D_FF, D_MODEL), bf16),
                         pltpu.VMEM((FFN_STAGE_SLOTS, 2, D_MODEL, FFN_STAGE_COLS), f32),
                         pltpu.VMEM((FFN_STAGE_SLOTS, FFN_STAGE_COLS, D_MODEL), f32),
                         pltpu.SemaphoreType.DMA((FFN_STAGE_SLOTS, 3))]
    n_u, n_z, n_x, n_g = _PIECE_TILES
    front = n_u + n_z + n_x
    assert n_a >= front + n_g and _IN_PROJ_SPLITS[2] == front * MXU_COLS and _IN_PROJ_SPLITS[2] % HEAD_LANES == 0

    def blocked(arr):
        blk, period = _cast_block(arr.shape[0], n_a)
        return (pl.BlockSpec((blk, arr.shape[1]), lambda i: (jnp.minimum(i, n_a - 1) // period, 0)),
                jax.ShapeDtypeStruct(arr.shape, bf16))

    def piece(first_step, n_tiles):
        return (pl.BlockSpec((D_MODEL, MXU_COLS), lambda i: (0, jnp.clip(i - first_step, 0, n_tiles - 1))),
                jax.ShapeDtypeStruct((D_MODEL, n_tiles * MXU_COLS), bf16))

    pieces = [piece(0, n_u), piece(n_u, n_z), piece(n_u + n_z, n_x),
              (pl.BlockSpec((D_MODEL, HEAD_LANES), lambda i: (0, 0)), jax.ShapeDtypeStruct((D_MODEL, HEAD_LANES), bf16)),
              piece(front, n_g)]
    in_proj_specs = [pl.BlockSpec((MXU_COLS, D_MODEL), lambda i: (jnp.minimum(i, front - 1), 0)),
                     pl.BlockSpec((HEAD_LANES, D_MODEL), lambda i: (_IN_PROJ_SPLITS[2] // HEAD_LANES, 0)),
                     pl.BlockSpec((MXU_COLS, D_MODEL), lambda i: (front + jnp.clip(i - front, 0, n_g - 1), 0)),
                     pl.BlockSpec((MXU_COLS, D_MODEL), lambda i: (front + 1 + jnp.clip(i - front, 0, n_g - 1), 0))]
    plains = [blocked(p) for p in plain]
    res = pl.pallas_call(
        functools.partial(_ffn_cast_kernel, n_a=n_a, n_plain=len(plain)),
        grid=(n_a + n_b,),
        in_specs=in_specs + in_proj_specs + [s for s, _ in plains],
        out_specs=tuple(out_specs + [s for s, _ in pieces] + [s for s, _ in plains]),
        out_shape=tuple(out_shapes + [o for _, o in pieces] + [o for _, o in plains]),
        scratch_shapes=scratch,
        compiler_params=_params(1, LARGE_VMEM_LIMIT_BYTES),
        name="ffn_casts",
    )(xa, xb, g, w_in_f, w_out_f, gfin, w_in_t, w_in_t, w_in_t, w_in_t, *plain)
    return res[0], res[1], res[2:7], res[7:]


def _slab(bufs, j):
    return bufs[j // SLABS_PER_BUF], j % SLABS_PER_BUF


def _window_scratch(n_bufs, n_seqs, n_rows):
    return [pltpu.VMEM((SLABS_PER_BUF, n_seqs, n_rows, LANES), f32) for _ in range(n_bufs)]


def _store_rows(bufs, row0, val, T, slab0=0):
    for j in range(val.shape[1] // LANES):
        ref, k = _slab(bufs, slab0 + j)
        S = ref.shape[1]
        blk = val[:, j * LANES:(j + 1) * LANES]
        if S == 1:
            ref[k, 0, row0:row0 + T, :] = blk
        else:
            ref[k, :, row0:row0 + T, :] = blk.reshape(S, T, LANES)


def _window(bufs, j, row0, n):
    ref, k = _slab(bufs, j)
    S = ref.shape[1]
    rows = pl.ds(row0, n) if row0 % SUBLANES == 0 else pl.ds(row0, n, stride=1)
    if S == 1:
        return ref[k, 0, rows, :]
    v = ref[k, :, rows, :]
    return v.reshape(S * n, LANES) if n % SUBLANES == 0 else v


def _pool_branch(hn_s, wu_ref, ubufs, T, pos0, pgw_ref, pscale_ref, pwo_ref):
    S = ubufs[0].shape[1]
    assert T & (T - 1) == 0 and POOL_GROUP == MXU_COLS
    t_idx = jnp.bitwise_and(lax.broadcasted_iota(jnp.int32, (S * T, 1), 0), T - 1)
    n_seen = pos0 + 1 + t_idx
    _store_rows(ubufs, POOL_PAD, _dot(hn_s[...], wu_ref[...]), T)
    mixed = []
    for gi, w in enumerate(POOL_WINDOWS):
        inv_cnt = 1.0 / jnp.minimum(n_seen, w).astype(f32)
        d = []
        for j in range(gi * SLABS_PER_BUF, (gi + 1) * SLABS_PER_BUF):
            cur = _window(ubufs, j, POOL_PAD, T)
            s = cur
            for k in range(1, w):
                s = s + _window(ubufs, j, POOL_PAD - k, T)
            d.append((s * inv_cnt - cur).astype(bf16))
        mixed.append(_dot(jnp.concatenate(d, axis=-1), pgw_ref[gi]))
    y = jnp.concatenate(mixed, axis=-1) * pscale_ref[...]
    return _dot(y.astype(bf16), pwo_ref[...])


def _conv_slab(xbufs, T, convw_ref, convb_ref, j, xs_ref, b_ref, c_ref):
    c0 = j * LANES
    cols = slice(c0, c0 + LANES)
    h = 0.5 * convb_ref[:, cols]
    for k in range(CONV_WIDTH):
        h = h + _window(xbufs, j, CONV_PAD - (CONV_WIDTH - 1) + k, T) * (0.5 * convw_ref[k:k + 1, cols])
    v = h * jnp.tanh(h) + h
    if c0 < D_INNER:
        xs_ref[:, cols] = v
    elif c0 < D_INNER + BC_DIM:
        b_ref[:, c0 - D_INNER:c0 - D_INNER + LANES] = v
    else:
        c_ref[:, c0 - D_INNER - BC_DIM:c0 - D_INNER - BC_DIM + LANES] = v


def _project_conv(hn_s, wxbc_ref, xbufs, T, convw_ref, convb_ref, xs_ref, b_ref, c_ref):
    for jt in range(N_CONV_BUFS):
        slab0 = jt * SLABS_PER_BUF
        _store_rows(xbufs, CONV_PAD, _dot(hn_s[...], wxbc_ref[:, jt * MXU_COLS:(jt + 1) * MXU_COLS]), T, slab0=slab0)
        for j in range(slab0, slab0 + SLABS_PER_BUF):
            _conv_slab(xbufs, T, convw_ref, convb_ref, j, xs_ref, b_ref, c_ref)


def _gate_merge_out(x, hn_s, y_ref, bp, wz_ref, wgate_ref, snorm_ref, swo_ref, wo_ref, yn_s):
    for g in range(N_SSD_GROUPS):
        cols = slice(g * GROUP_WIDTH, (g + 1) * GROUP_WIDTH)
        z = _dot(hn_s[...], wz_ref[:, cols])
        yg = y_ref[:, cols] * _silu(z)
        yg = yg * lax.rsqrt(jnp.mean(yg * yg, axis=-1, keepdims=True) + EPS)
        yn_s[:, cols] = (yg * snorm_ref[:, cols]).astype(bf16)
    branch_ssd = _dot(yn_s[...], swo_ref[...])
    gate_pool = _sigmoid(_dot(hn_s[...], wgate_ref[:, 0:D_MODEL]))
    gate_ssd = _sigmoid(_dot(hn_s[...], wgate_ref[:, D_MODEL:2 * D_MODEL]))
    merged = (gate_pool * bp + gate_ssd * branch_ssd).astype(bf16)
    return x + _dot(merged, wo_ref[...])


def _ssd_chunk(rows, xs_s, b_s, c_s, dt_s, y_s, ht_s, alog_ref, dskip_ref, e_ref):
    L = CHUNK
    ri = lax.broadcasted_iota(jnp.int32, (L, L), 0)
    ci = lax.broadcasted_iota(jnp.int32, (L, L), 1)
    causal = ri >= ci
    tril = jnp.where(causal, 1.0, 0.0).astype(bf16)
    first_head = lax.broadcasted_iota(jnp.int32, (L, LANES), 1) < SSD_HEAD_DIM
    keep_first = jnp.where(first_head, 1.0, 0.0).astype(bf16)
    keep_second = jnp.where(first_head, 0.0, 1.0).astype(bf16)

    dt = dt_s[rows, :]
    dA = dt * (-jnp.exp(alog_ref[...]))
    hi, mid, lo = _split3(dA)
    acc = _dot(tril, jnp.concatenate([hi, mid, lo], axis=1))
    a = (acc[:, 0:LANES] + acc[:, LANES:2 * LANES] + acc[:, 2 * LANES:3 * LANES]) * LOG2_E
    aT = a.T[0:N_SSD_HEADS]
    dtT = dt.T[0:N_SSD_HEADS]
    wT = jnp.exp2(aT[:, L - 1:L] - aT) * dtT
    srcT = aT - jnp.log2(dtT)
    end_decay = jnp.broadcast_to(jnp.exp2(a[L - 1:L, :]), (2 * SUBLANES, HEAD_LANES))
    e_hi, e_mid, e_lo = (t.astype(f32) for t in _split3(end_decay))
    sel = lax.broadcasted_iota(jnp.int32, (2 * SUBLANES, HEAD_LANES), 0)
    stacked = jnp.where(sel == 0, e_hi, jnp.where(sel == 1, e_mid, jnp.where(sel == 2, e_lo, 0.0)))
    cdec = jnp.sum(_dot(stacked.astype(bf16), e_ref[...]), axis=0, keepdims=True)

    def head_mats(h, cb, bT):
        acol = jnp.broadcast_to(a[:, h:h + 1], (L, L))
        m = (cb * jnp.exp2(jnp.where(causal, acol - srcT[h:h + 1, :], -jnp.inf))).astype(bf16)
        bw = (bT * wT[h:h + 1, :]).astype(bf16)
        return m, bw, acol

    pairs_per_group = HEADS_PER_GROUP // 2
    for q in range(N_SSD_HEADS // 2):
        g, qg = divmod(q, pairs_per_group)
        if qg == 0:
            gcols = slice(g * D_STATE, (g + 1) * D_STATE)
            b_f = b_s[rows, gcols]
            c_b = c_s[rows, gcols].astype(bf16)
            cb = lax.dot_general(c_b, b_f.astype(bf16), _NT, preferred_element_type=f32)
            bT = b_f.T
            ch = _dot(c_b, ht_s[:, g * GROUP_WIDTH:(g + 1) * GROUP_WIDTH].astype(bf16))
        cols = slice(q * LANES, (q + 1) * LANES)
        xq = xs_s[rows, cols]
        xq_b = xq.astype(bf16)
        x2 = jnp.concatenate([xq_b * keep_first, xq_b * keep_second], axis=0)
        m_a, bw_a, acol_a = head_mats(2 * q, cb, bT)
        m_b, bw_b, acol_b = head_mats(2 * q + 1, cb, bT)
        y_off = ch[:, qg * LANES:(qg + 1) * LANES] * jnp.exp2(jnp.where(first_head, acol_a, acol_b))
        y_s[rows, cols] = _dot(jnp.concatenate([m_a, m_b], axis=1), x2) + y_off + dskip_ref[:, cols] * xq
        st = _dot(jnp.concatenate([bw_a, bw_b], axis=1), x2)
        ht_s[:, cols] = ht_s[:, cols] * cdec[:, cols] + st


def _mixer_prompt_kernel(x_ref, dxs_ref, db_ref, dc_ref, ddt_ref, dh0_ref,
                         gmix_ref, wu_ref, wz_ref, wxbc_ref, wdt_ref, wgate_ref, pgw_ref, pscale_ref,
                         pwo_ref, convw_ref, convb_ref, dtb_ref, alog_ref, dskip_ref, snorm_ref, swo_ref,
                         wo_ref, e_ref,
                         o_ref, npool_ref, nconv_ref, nssm_ref, dy_ref, dhn_ref,
                         hn_s, xs_s, b_s, c_s, dt_s, y_s, ht_s, yn_s, *window_bufs, decode_T):
    T = PROMPT_TILE
    ti = pl.program_id(1)
    ubufs, xbufs = window_bufs[:N_POOL_BUFS], window_bufs[N_POOL_BUFS:]

    @pl.when(ti == 0)
    def _():
        for ref in ubufs:
            ref[:, :, 0:POOL_PAD, :] = jnp.zeros((SLABS_PER_BUF, 1, POOL_PAD, LANES), f32)
        for ref in xbufs:
            ref[:, :, 0:CONV_PAD, :] = jnp.zeros((SLABS_PER_BUF, 1, CONV_PAD, LANES), f32)
        ht_s[...] = jnp.zeros(ht_s.shape, f32)

    x = x_ref[0]
    hn_s[...] = _rms(x, gmix_ref[...]).astype(bf16)

    bp = _pool_branch(hn_s, wu_ref, ubufs, T, ti * T, pgw_ref, pscale_ref, pwo_ref)

    _project_conv(hn_s, wxbc_ref, xbufs, T, convw_ref, convb_ref, xs_s, b_s, c_s)
    dt_s[...] = _softplus(_dot(hn_s[...], wdt_ref[...]) + dtb_ref[...])

    def chunk(c, carry):
        rows = pl.ds(pl.multiple_of(c * CHUNK, CHUNK), CHUNK)
        _ssd_chunk(rows, xs_s, b_s, c_s, dt_s, y_s, ht_s, alog_ref, dskip_ref, e_ref)
        return carry

    lax.fori_loop(0, T // CHUNK, chunk, 0)

    o_ref[0] = _gate_merge_out(x, hn_s, y_s, bp, wz_ref, wgate_ref, snorm_ref, swo_ref, wo_ref, yn_s)

    @pl.when(ti == pl.num_programs(1) - 1)
    def _():
        for j in range(POOL_WIDTH // LANES):
            npool_ref[0, 0, :, j * LANES:(j + 1) * LANES] = _window(ubufs, j, POOL_PAD + T - POOL_HIST, POOL_HIST)
        for j in range(CONV_DIM // LANES):
            nconv_ref[0, 0, :, j * LANES:(j + 1) * LANES] = _window(
                xbufs, j, CONV_PAD + T - (CONV_WIDTH - 1), CONV_WIDTH - 1)
        for g in range(N_SSD_GROUPS):
            hg = ht_s[:, g * GROUP_WIDTH:(g + 1) * GROUP_WIDTH].T
            nssm_ref[0, 0, g * HEADS_PER_GROUP:(g + 1) * HEADS_PER_GROUP] = hg.reshape(
                HEADS_PER_GROUP, SSD_HEAD_DIM, D_STATE)

    _decode_seqs(dxs_ref, db_ref, dc_ref, ddt_ref, dh0_ref, alog_ref, dskip_ref, e_ref, dy_ref, dhn_ref,
                 T=decode_T, n_seqs=dxs_ref.shape[0])

    for ref in ubufs:
        ref[:, :, 0:POOL_PAD, :] = ref[:, :, T:T + POOL_PAD, :]
    for ref in xbufs:
        ref[:, :, 0:CONV_PAD, :] = ref[:, :, T:T + CONV_PAD, :]


def _mixer_prompt(x1, w, dxs, dbm, dcm, ddt, dh0, decode_T):
    B, S, _ = x1.shape
    T = PROMPT_TILE
    assert S % T == 0 and T % CHUNK == 0 and T >= POOL_PAD
    n_tiles = S // T
    nseq = dxs.shape[0] // decode_T
    assert nseq % (B * n_tiles) == 0
    Q = nseq // (B * n_tiles)
    seq_spec = lambda width: pl.BlockSpec((Q, decode_T, width), lambda b, t: (b * n_tiles + t, 0, 0))
    state_spec = pl.BlockSpec((1, Q, N_SSD_HEADS, SSD_HEAD_DIM, D_STATE), lambda b, t: (0, b * n_tiles + t, 0, 0, 0))
    as_seqs = lambda v: v.reshape(nseq, decode_T, v.shape[1])
    consts = [w["gmix"], w["wu"], w["wz"], w["wxbc"], w["wdt"], w["wgate"], w["pgw"], w["pscale"], w["pwo"],
              w["convw"], w["convb"], w["dtb"], w["alog"], w["dskip"], w["snorm"], w["swo"], w["wo"], w["expand"]]
    return pl.pallas_call(
        functools.partial(_mixer_prompt_kernel, decode_T=decode_T),
        grid=(B, n_tiles),
        in_specs=[pl.BlockSpec((1, T, D_MODEL), lambda b, t: (b, t, 0)),
                  seq_spec(D_INNER), seq_spec(BC_DIM), seq_spec(BC_DIM), seq_spec(HEAD_LANES), state_spec]
                 + [_const_spec(c.shape) for c in consts],
        out_specs=(pl.BlockSpec((1, T, D_MODEL), lambda b, t: (b, t, 0)),
                   pl.BlockSpec((1, 1, POOL_HIST, POOL_WIDTH), lambda b, t: (0, b, 0, 0)),
                   pl.BlockSpec((1, 1, CONV_WIDTH - 1, CONV_DIM), lambda b, t: (0, b, 0, 0)),
                   pl.BlockSpec((1, 1, N_SSD_HEADS, SSD_HEAD_DIM, D_STATE), lambda b, t: (0, b, 0, 0, 0)),
                   seq_spec(D_INNER), state_spec),
        out_shape=(jax.ShapeDtypeStruct((B, S, D_MODEL), f32),
                   jax.ShapeDtypeStruct((1, B, POOL_HIST, POOL_WIDTH), f32),
                   jax.ShapeDtypeStruct((1, B, CONV_WIDTH - 1, CONV_DIM), f32),
                   jax.ShapeDtypeStruct((1, B, N_SSD_HEADS, SSD_HEAD_DIM, D_STATE), f32),
                   jax.ShapeDtypeStruct((nseq, decode_T, D_INNER), f32),
                   jax.ShapeDtypeStruct(dh0.shape, f32)),
        scratch_shapes=[pltpu.VMEM((T, D_MODEL), bf16),
                        pltpu.VMEM((T, D_INNER), f32),
                        pltpu.VMEM((T, BC_DIM), f32),
                        pltpu.VMEM((T, BC_DIM), f32),
                        pltpu.VMEM((T, HEAD_LANES), f32),
                        pltpu.VMEM((T, D_INNER), f32),
                        pltpu.VMEM((D_STATE, D_INNER), f32),
                        pltpu.VMEM((T, D_INNER), bf16)]
                       + _window_scratch(N_POOL_BUFS, 1, POOL_PAD + T)
                       + _window_scratch(N_CONV_BUFS, 1, CONV_PAD + T),
        compiler_params=_params(2, LARGE_VMEM_LIMIT_BYTES),
        name="mixer_prompt",
    )(x1, as_seqs(dxs), as_seqs(dbm), as_seqs(dcm), as_seqs(ddt), dh0, *consts)


def _load_history(bufs, hist_ref, row0):
    for j in range(hist_ref.shape[2] // LANES):
        ref, k = _slab(bufs, j)
        n_slabs, S, n_rows, _ = ref.shape
        flat = ref.reshape(n_slabs, S * n_rows, LANES)
        for r in range(hist_ref.shape[0]):
            flat[k, pl.ds(row0 + r, S, stride=n_rows), :] = hist_ref[r, :, j * LANES:(j + 1) * LANES]


def _store_history(hist_ref, bufs, row0):
    for j in range(hist_ref.shape[2] // LANES):
        ref, k = _slab(bufs, j)
        n_slabs, S, n_rows, _ = ref.shape
        flat = ref.reshape(n_slabs, S * n_rows, LANES)
        for r in range(hist_ref.shape[0]):
            hist_ref[r, :, j * LANES:(j + 1) * LANES] = flat[k, pl.ds(row0 + r, S, stride=n_rows), :]


def _sample_pre_kernel(x_ref, ph_ref, ch_ref, gmix_ref, wu_ref, wxbc_ref, wdt_ref, pgw_ref, pscale_ref, pwo_ref,
                       convw_ref, convb_ref, dtb_ref,
                       bp_ref, xs_ref, b_ref, c_ref, dt_ref, npool_ref, nconv_ref,
                       *window_bufs, T):
    ubufs, xbufs = window_bufs[:N_POOL_BUFS], window_bufs[N_POOL_BUFS:]
    hn = _rms(x_ref[...], gmix_ref[...]).astype(bf16)

    _load_history(ubufs, ph_ref, POOL_PAD - POOL_HIST)
    bp_ref[...] = _pool_branch(hn, wu_ref, ubufs, T, PAST_LEN, pgw_ref, pscale_ref, pwo_ref)
    _store_history(npool_ref, ubufs, POOL_PAD + T - POOL_HIST)

    _load_history(xbufs, ch_ref, CONV_PAD - (CONV_WIDTH - 1))
    _project_conv(hn, wxbc_ref, xbufs, T, convw_ref, convb_ref, xs_ref, b_ref, c_ref)
    _store_history(nconv_ref, xbufs, CONV_PAD + T - (CONV_WIDTH - 1))
    dt_ref[...] = _softplus(_dot(hn, wdt_ref[...]) + dtb_ref[...])


def _sample_pre(x1, pool_hist, conv_hist, w, T):
    rows = x1.shape[0]
    nseq = rows // T
    S = SAMPLE_SEQS
    R = S * T
    assert nseq % S == 0 and T % SUBLANES == 0
    consts = [w["gmix"], w["wu"], w["wxbc"], w["wdt"], w["pgw"], w["pscale"], w["pwo"], w["convw"], w["convb"],
              w["dtb"]]
    row_spec = lambda width: pl.BlockSpec((R, width), lambda i: (i, 0))
    return pl.pallas_call(
        functools.partial(_sample_pre_kernel, T=T),
        grid=(nseq // S,),
        in_specs=[row_spec(D_MODEL),
                  pl.BlockSpec((POOL_HIST, S, POOL_WIDTH), lambda i: (0, i, 0)),
                  pl.BlockSpec((CONV_WIDTH - 1, S, CONV_DIM), lambda i: (0, i, 0))]
                 + [_const_spec(c.shape) for c in consts],
        out_specs=(row_spec(D_MODEL), row_spec(D_INNER), row_spec(BC_DIM), row_spec(BC_DIM), row_spec(HEAD_LANES),
                   pl.BlockSpec((POOL_HIST, S, POOL_WIDTH), lambda i: (0, i, 0)),
                   pl.BlockSpec((CONV_WIDTH - 1, S, CONV_DIM), lambda i: (0, i, 0))),
        out_shape=(jax.ShapeDtypeStruct((rows, D_MODEL), f32),
                   jax.ShapeDtypeStruct((rows, D_INNER), f32),
                   jax.ShapeDtypeStruct((rows, BC_DIM), f32),
                   jax.ShapeDtypeStruct((rows, BC_DIM), f32),
                   jax.ShapeDtypeStruct((rows, HEAD_LANES), f32),
                   jax.ShapeDtypeStruct((POOL_HIST, nseq, POOL_WIDTH), f32),
                   jax.ShapeDtypeStruct((CONV_WIDTH - 1, nseq, CONV_DIM), f32)),
        scratch_shapes=_window_scratch(N_POOL_BUFS, S, POOL_PAD + T) + _window_scratch(N_CONV_BUFS, S, CONV_PAD + T),
        compiler_params=_params(1),
        name="sample_pre",
    )(x1, pool_hist, conv_hist, *consts)


def _decode_seqs(xs_ref, b_ref, c_ref, dt_ref, h0_ref, alog_ref, dskip_ref, e_ref, y_ref, hn_ref, *, T, n_seqs):
    assert T == SUBLANES
    lane_group = lax.broadcasted_iota(jnp.int32, (T, HEAD_LANES), 1) // HEADS_PER_GROUP
    neg_a = -jnp.exp(alog_ref[...])

    def shift(v, d):
        if d == 0:
            return v
        r = lax.broadcasted_iota(jnp.int32, v.shape, 0)
        return jnp.where(r >= d, pltpu.roll(v, d, axis=0), 0.0)

    def per_seq(s, carry):
        dt = dt_ref[s]
        a = dt * neg_a
        d = 1
        while d < T:
            a = a + shift(a, d)
            d *= 2
        a_end = a[T - 1:T, :]
        x = xs_ref[s]
        bm = b_ref[s]
        cm = c_ref[s]
        terms = []
        for d in range(T):
            cbv = jnp.zeros((T, HEAD_LANES), f32)
            for g in range(N_SSD_GROUPS):
                gc = slice(g * D_STATE, (g + 1) * D_STATE)
                cb = jnp.sum(cm[:, gc] * shift(bm[:, gc], d), axis=-1, keepdims=True)
                cbv = jnp.where(lane_group == g, cb, cbv)
            terms.append(cbv * jnp.exp(a - shift(a, d)) * shift(dt, d))
        terms.append(jnp.exp(a))
        terms.append(jnp.exp(a_end - a) * dt)
        v = jnp.concatenate(terms, axis=0)
        ex = _dot(v.astype(bf16), e_ref[...])

        y = dskip_ref[...] * x
        for d in range(T):
            y = y + ex[d * T:(d + 1) * T] * shift(x, d)
        ea = ex[T * T:(T + 1) * T]
        xw = (x * ex[(T + 1) * T:(T + 2) * T]).astype(bf16)
        cdec = jnp.exp(a_end)
        y_groups = []
        for g in range(N_SSD_GROUPS):
            gc = slice(g * D_STATE, (g + 1) * D_STATE)
            cols = slice(g * GROUP_WIDTH, (g + 1) * GROUP_WIDTH)
            heads = slice(g * HEADS_PER_GROUP, (g + 1) * HEADS_PER_GROUP)
            h0g = h0_ref[0, s, heads].reshape(GROUP_WIDTH, D_STATE)
            ch = lax.dot_general(cm[:, gc].astype(bf16), h0g.astype(bf16), _NT, preferred_element_type=f32)
            y_groups.append(y[:, cols] + ea[:, cols] * ch)
            st = lax.dot_general(xw[:, cols], bm[:, gc].astype(bf16), _TN, preferred_element_type=f32)
            for r in range(HEADS_PER_GROUP):
                h = g * HEADS_PER_GROUP + r
                hn_ref[0, s, h] = (h0_ref[0, s, h] * cdec[:, h:h + 1]
                                   + st[r * SSD_HEAD_DIM:(r + 1) * SSD_HEAD_DIM, :])
        y_ref[s] = jnp.concatenate(y_groups, axis=-1)
        return carry

    lax.fori_loop(0, n_seqs, per_seq, 0, unroll=True)


def _sample_post_kernel(x_ref, y_ref, bp_ref, gmix_ref, wz_ref, wgate_ref, snorm_ref, swo_ref, wo_ref,
                        o_ref, hn_s, yn_s):
    x = x_ref[...]
    hn_s[...] = _rms(x, gmix_ref[...]).astype(bf16)
    o_ref[...] = _gate_merge_out(x, hn_s, y_ref, bp_ref[...], wz_ref, wgate_ref, snorm_ref, swo_ref, wo_ref, yn_s)


def _sample_post(x1, y, bp, w, T):
    rows = x1.shape[0]
    R = SAMPLE_POST_SEQS * T
    assert rows % R == 0
    consts = [w["gmix"], w["wz"], w["wgate"], w["snorm"], w["swo"], w["wo"]]
    row_spec = lambda width: pl.BlockSpec((R, width), lambda i: (i, 0))
    return pl.pallas_call(
        _sample_post_kernel,
        grid=(rows // R,),
        in_specs=[row_spec(D_MODEL), row_spec(D_INNER), row_spec(D_MODEL)] + [_const_spec(c.shape) for c in consts],
        out_specs=row_spec(D_MODEL),
        out_shape=jax.ShapeDtypeStruct((rows, D_MODEL), f32),
        scratch_shapes=[pltpu.VMEM((R, D_MODEL), bf16), pltpu.VMEM((R, D_INNER), bf16)],
        compiler_params=_params(1),
        name="sample_post",
    )(x1, y, bp, *consts)


def _small_params(l, norm_mix, pool_scale, conv_w, conv_b, dt_bias, a_log, d_skip, ssd_norm):
    pad_heads = lambda v: jnp.pad(v, ((0, 0), (0, HEAD_LANES - N_SSD_HEADS)))
    head_of_lane = jnp.arange(D_INNER, dtype=jnp.int32) // SSD_HEAD_DIM
    expand = (jnp.arange(HEAD_LANES, dtype=jnp.int32)[:, None] == head_of_lane[None, :]).astype(bf16)
    return dict(
        gmix=norm_mix[l][None, :], pscale=pool_scale[l][None, :], convw=conv_w[l], convb=conv_b[l][None, :],
        dtb=pad_heads(dt_bias[l][None, :]), alog=pad_heads(a_log[l][None, :]),
        dskip=jnp.repeat(d_skip[l], SSD_HEAD_DIM)[None, :], snorm=ssd_norm[l][None, :], expand=expand)


def kernel(x_prompt, x_sample, state_pool, state_conv, state_ssm, norm_ffn1, ffn1_w_in, ffn1_w_out, norm_mix, w_in,
           pool_w_group, pool_scale, pool_w_out, conv_w, conv_b, dt_bias, a_log, d_skip, ssd_norm, ssd_w_out, w_o,
           norm_ffn2, ffn2_w_in, ffn2_w_out, norm_final):
    depth = w_in.shape[0]
    B, S, _ = x_prompt.shape
    DB, T, _ = x_sample.shape
    gfin = norm_final[None, :]
    xp = x_prompt.reshape(B * S, D_MODEL)
    xs = x_sample.reshape(DB * T, D_MODEL)
    outs = [[] for _ in range(6)]
    for l in range(depth):
        last = l == depth - 1
        g1, g2 = norm_ffn1[l][None, :], norm_ffn2[l][None, :]
        w = _small_params(l, norm_mix, pool_scale, conv_w, conv_b, dt_bias, a_log, d_skip, ssd_norm)
        n_pg = pool_w_group.shape[1]
        xp, xs, in_proj, casts = _ffn_and_casts(
            xp, xs, g1, ffn1_w_in[l], ffn1_w_out[l], gfin, w_in[l].T,
            [pool_w_group[l].reshape(n_pg * POOL_GROUP, POOL_GROUP), pool_w_out[l], ssd_w_out[l], w_o[l],
             ffn2_w_in[l], ffn2_w_out[l]])
        w.update(zip(("wu", "wz", "wxbc", "wdt", "wgate"), in_proj))
        w.update(pgw=casts[0].reshape(n_pg, POOL_GROUP, POOL_GROUP), pwo=casts[1], swo=casts[2], wo=casts[3])
        f2_in, f2_out = casts[4], casts[5]
        hist_major = lambda v: jnp.transpose(v, (1, 0, 2))
        bp, cxs, cb, cc, cdt, spool, sconv = _sample_pre(xs, hist_major(state_pool[l]), hist_major(state_conv[l]), w, T)
        spool, sconv = hist_major(spool)[None], hist_major(sconv)[None]
        xp3, npool, nconv, nssm, y, sssm = _mixer_prompt(xp.reshape(B, S, D_MODEL), w, cxs, cb, cc, cdt,
                                                         state_ssm[l:l + 1], T)
        xs = _sample_post(xs, y.reshape(DB * T, D_INNER), bp, w, T)
        xp, xs = _ffn(xp3.reshape(B * S, D_MODEL), xs, g2, f2_in, f2_out, gfin, final_norm=last)
        for acc, v in zip(outs, (npool, nconv, nssm, spool, sconv, sssm)):
            acc.append(v)
    stack = lambda vs: vs[0] if len(vs) == 1 else jnp.concatenate(vs, axis=0)
    return (xp.reshape(B, S, D_MODEL), xs.reshape(DB, T, D_MODEL),
            stack(outs[0]), stack(outs[1]), stack(outs[2]), stack(outs[3]), stack(outs[4]), stack(outs[5]))
```

```python
import functools

import jax
import jax.numpy as jnp
from jax import lax
from jax.experimental import pallas as pl
from jax.experimental.pallas import tpu as pltpu

f32 = jnp.float32
bf16 = jnp.bfloat16

D_MODEL = 1024
D_FF = 2816
POOL_WINDOWS = (2, 4, 8, 16)
POOL_WIDTH = D_MODEL
POOL_GROUP = POOL_WIDTH // len(POOL_WINDOWS)
POOL_HIST = max(POOL_WINDOWS) - 1
D_INNER = 2 * D_MODEL
SSD_HEAD_DIM = 64
N_SSD_HEADS = D_INNER // SSD_HEAD_DIM
N_SSD_GROUPS = 4
HEADS_PER_GROUP = N_SSD_HEADS // N_SSD_GROUPS
GROUP_WIDTH = D_INNER // N_SSD_GROUPS
D_STATE = 128
CONV_WIDTH = 4
BC_DIM = N_SSD_GROUPS * D_STATE
CONV_DIM = D_INNER + 2 * BC_DIM
CHUNK = 128
PAST_LEN = 16384
EPS = 1e-6
LOG2_E = 1.4426950408889634

LANES = 128
SUBLANES = 8
MXU_COLS = 256
VMEM_LIMIT_BYTES = 56 * 1024 * 1024
LARGE_VMEM_LIMIT_BYTES = 60 * 1024 * 1024

FFN_ROWS = 512
FFN_CHUNK = 256
FFN_STAGE_COLS = 128
FFN_STAGE_SLOTS = 4
PROMPT_TILE = 256
SAMPLE_SEQS = 32
SAMPLE_POST_SEQS = 32
POOL_PAD = 16
CONV_PAD = 8
HEAD_LANES = LANES
SLABS_PER_BUF = MXU_COLS // LANES
N_POOL_BUFS = POOL_WIDTH // MXU_COLS
N_CONV_BUFS = CONV_DIM // MXU_COLS

_NT = (((1,), (1,)), ((), ()))
_TN = (((0,), (0,)), ((), ()))


def _rms(x, g):
    return x * lax.rsqrt(jnp.mean(x * x, axis=-1, keepdims=True) + EPS) * g


def _sigmoid(v):
    return 0.5 * jnp.tanh(0.5 * v) + 0.5


def _silu(v):
    h = 0.5 * v
    return h * jnp.tanh(h) + h


def _softplus(v):
    return jnp.maximum(v, 0.0) + jnp.log1p(jnp.exp(-jnp.abs(v)))


def _dot(a, b):
    return jnp.dot(a, b, preferred_element_type=f32)


def _split3(v):
    hi = v.astype(bf16)
    r1 = v - hi.astype(f32)
    mid = r1.astype(bf16)
    lo = (r1 - mid.astype(f32)).astype(bf16)
    return hi, mid, lo


def _const_spec(shape):
    nd = len(shape)
    return pl.BlockSpec(shape, lambda *_: (0,) * nd, pipeline_mode=pl.Buffered(1))


def _params(n_grid, vmem_limit_bytes=VMEM_LIMIT_BYTES):
    return pltpu.CompilerParams(dimension_semantics=("arbitrary",) * n_grid,
                                vmem_limit_bytes=vmem_limit_bytes)


def _ffn_kernel(xa_ref, xb_ref, g_ref, win_ref, wout_ref, gfin_ref, oa_ref, ob_ref, hn_s, act_s, *, n_a, final_norm,
                side_work=None, fetch=None):
    def tile(x_ref, o_ref, extra=None, fetch=None):
        commit = extra() if extra is not None else None
        hn_s[...] = _rms(x_ref[...], g_ref[...]).astype(bf16)
        for c in range(D_FF // FFN_CHUNK):
            lo = c * FFN_CHUNK
            if fetch is not None:
                fetch(c)
            gate = _dot(hn_s[...], win_ref[:, lo:lo + FFN_CHUNK])
            up = _dot(hn_s[...], win_ref[:, D_FF + lo:D_FF + lo + FFN_CHUNK])
            act_s[:, lo:lo + FFN_CHUNK] = (_silu(gate) * up).astype(bf16)
        out = x_ref[...] + 0.5 * _dot(act_s[...], wout_ref[...])
        if final_norm:
            out = _rms(out, gfin_ref[...])
        o_ref[...] = out
        if commit is not None:
            commit()

    i = pl.program_id(0)
    on_a = i < n_a
    if fetch is None:
        pl.when(on_a)(lambda: tile(xa_ref, oa_ref, side_work))
    else:
        pl.when(i == 0)(lambda: tile(xa_ref, oa_ref, side_work, fetch))
        pl.when(jnp.logical_and(i > 0, on_a))(lambda: tile(xa_ref, oa_ref, side_work))
    pl.when(jnp.logical_not(on_a))(lambda: tile(xb_ref, ob_ref))


def _ffn_specs(rows_a, rows_b):
    assert rows_a % FFN_ROWS == 0 and rows_b % FFN_ROWS == 0
    n_a, n_b = rows_a // FFN_ROWS, rows_b // FFN_ROWS
    a_spec = pl.BlockSpec((FFN_ROWS, D_MODEL), lambda i: (jnp.minimum(i, n_a - 1), 0))
    b_spec = pl.BlockSpec((FFN_ROWS, D_MODEL), lambda i: (jnp.maximum(i - n_a, 0), 0))
    in_specs = [a_spec, b_spec,
                _const_spec((1, D_MODEL)),
                _const_spec((D_MODEL, 2 * D_FF)),
                _const_spec((D_FF, D_MODEL)),
                _const_spec((1, D_MODEL))]
    out_shapes = [jax.ShapeDtypeStruct((rows_a, D_MODEL), f32), jax.ShapeDtypeStruct((rows_b, D_MODEL), f32)]
    scratch = [pltpu.VMEM((FFN_ROWS, D_MODEL), bf16), pltpu.VMEM((FFN_ROWS, D_FF), bf16)]
    return n_a, n_b, in_specs, [a_spec, b_spec], out_shapes, scratch


def _ffn(xa, xb, g, w_in_b, w_out_b, gfin, *, final_norm):
    n_a, n_b, in_specs, out_specs, out_shapes, scratch = _ffn_specs(xa.shape[0], xb.shape[0])
    return pl.pallas_call(
        functools.partial(_ffn_kernel, n_a=n_a, final_norm=final_norm),
        grid=(n_a + n_b,),
        in_specs=in_specs, out_specs=tuple(out_specs), out_shape=tuple(out_shapes), scratch_shapes=scratch,
        compiler_params=_params(1),
        name="ffn_final" if final_norm else "ffn",
    )(xa, xb, g, w_in_b, w_out_b, gfin)


_IN_PROJ_SPLITS = (POOL_WIDTH, POOL_WIDTH + D_INNER, POOL_WIDTH + D_INNER + CONV_DIM,
                   POOL_WIDTH + D_INNER + CONV_DIM + N_SSD_HEADS)
BF16_SUBLANES = 2 * SUBLANES


def _cast_block(n_rows, n_steps):
    period = 1
    while (n_rows * period) % n_steps or ((n_rows * period) // n_steps) % BF16_SUBLANES:
        period *= 2
        assert period <= n_steps
    return (n_rows * period) // n_steps, period


_PIECE_TILES = (POOL_WIDTH // MXU_COLS, D_INNER // MXU_COLS, CONV_DIM // MXU_COLS, 2 * D_MODEL // MXU_COLS)
_GATE_SKEW = _IN_PROJ_SPLITS[3] - _IN_PROJ_SPLITS[2]


def _store_when(cond, ref, val):
    @pl.when(cond)
    def _():
        ref[...] = val


def _ffn_cast_kernel(xa_ref, xb_ref, g_ref, win_hbm, wout_hbm, gfin_ref, wt_ref, wdt_f, wg_hi, *rest,
                     n_a, n_plain):
    plain_f = rest[:n_plain]
    oa_ref, ob_ref = rest[n_plain:n_plain + 2]
    wu_o, wz_o, wxbc_o, wdt_o, wgate_o = rest[n_plain + 2:n_plain + 7]
    plain_o = rest[n_plain + 7:2 * n_plain + 7]
    hn_s, act_s, win_s, wout_s, stage_in, stage_out, sems = rest[2 * n_plain + 7:]
    i = pl.program_id(0)
    n_u, n_z, n_x, n_g = _PIECE_TILES
    front = n_u + n_z + n_x

    def stage_copies(k):
        slot, lo = k % FFN_STAGE_SLOTS, k * FFN_STAGE_COLS
        return (pltpu.make_async_copy(win_hbm.at[:, pl.ds(lo, FFN_STAGE_COLS)], stage_in.at[slot, 0],
                                      sems.at[slot, 0]),
                pltpu.make_async_copy(win_hbm.at[:, pl.ds(D_FF + lo, FFN_STAGE_COLS)], stage_in.at[slot, 1],
                                      sems.at[slot, 1]),
                pltpu.make_async_copy(wout_hbm.at[pl.ds(lo, FFN_STAGE_COLS), :], stage_out.at[slot],
                                      sems.at[slot, 2]))

    def fetch(c):
        per_chunk = FFN_CHUNK // FFN_STAGE_COLS
        for k in range(c * per_chunk, (c + 1) * per_chunk):
            slot, lo = k % FFN_STAGE_SLOTS, k * FFN_STAGE_COLS
            for nxt in (range(FFN_STAGE_SLOTS) if k == 0 else [k + FFN_STAGE_SLOTS - 1]):
                if nxt < D_FF // FFN_STAGE_COLS:
                    for cp in stage_copies(nxt):
                        cp.start()
            for cp in stage_copies(k):
                cp.wait()
            win_s[:, lo:lo + FFN_STAGE_COLS] = stage_in[slot, 0].astype(bf16)
            win_s[:, D_FF + lo:D_FF + lo + FFN_STAGE_COLS] = stage_in[slot, 1].astype(bf16)
            wout_s[lo:lo + FFN_STAGE_COLS, :] = stage_out[slot].astype(bf16)

    def casts():
        t = wt_ref[...].T.astype(bf16)
        rows = jnp.concatenate([wt_ref[_GATE_SKEW:, :], wg_hi[0:_GATE_SKEW, :]], axis=0)
        tg = rows.T.astype(bf16)
        for src, dst in zip(plain_f, plain_o):
            dst[...] = src[...].astype(bf16)

        def commit():
            _store_when(i < n_u, wu_o, t)
            _store_when(jnp.logical_and(i >= n_u, i < n_u + n_z), wz_o, t)
            _store_when(jnp.logical_and(i >= n_u + n_z, i < front), wxbc_o, t)
            _store_when(jnp.logical_and(i >= front, i < front + n_g), wgate_o, tg)
        return commit

    _ffn_kernel(xa_ref, xb_ref, g_ref, win_s, wout_s, gfin_ref, oa_ref, ob_ref, hn_s, act_s,
                n_a=n_a, final_norm=False, side_work=casts, fetch=fetch)

    @pl.when(i == 0)
    def _():
        t = wdt_f[...].T
        lane = lax.broadcasted_iota(jnp.int32, t.shape, 1)
        wdt_o[...] = jnp.where(lane < N_SSD_HEADS, t, 0.0).astype(bf16)


def _ffn_and_casts(xa, xb, g, w_in_f, w_out_f, gfin, w_in_t, plain):
    n_a, n_b, in_specs, out_specs, out_shapes, scratch = _ffn_specs(xa.shape[0], xb.shape[0])
    in_specs[3] = in_specs[4] = pl.BlockSpec(memory_space=pl.ANY)
    scratch = scratch + [pltpu.VMEM((D_MODEL, 2 * D_FF), bf16), pltpu.VMEM((D_FF, D_MODEL), bf16),
                         pltpu.VMEM((FFN_STAGE_SLOTS, 2, D_MODEL, FFN_STAGE_COLS), f32),
                         pltpu.VMEM((FFN_STAGE_SLOTS, FFN_STAGE_COLS, D_MODEL), f32),
                         pltpu.SemaphoreType.DMA((FFN_STAGE_SLOTS, 3))]
    n_u, n_z, n_x, n_g = _PIECE_TILES
    front = n_u + n_z + n_x
    assert n_a >= front + n_g and _IN_PROJ_SPLITS[2] == front * MXU_COLS and _IN_PROJ_SPLITS[2] % HEAD_LANES == 0

    def blocked(arr):
        blk, period = _cast_block(arr.shape[0], n_a)
        return (pl.BlockSpec((blk, arr.shape[1]), lambda i: (jnp.minimum(i, n_a - 1) // period, 0)),
                jax.ShapeDtypeStruct(arr.shape, bf16))

    def piece(first_step, n_tiles):
        return (pl.BlockSpec((D_MODEL, MXU_COLS), lambda i: (0, jnp.clip(i - first_step, 0, n_tiles - 1))),
                jax.ShapeDtypeStruct((D_MODEL, n_tiles * MXU_COLS), bf16))

    pieces = [piece(0, n_u), piece(n_u, n_z), piece(n_u + n_z, n_x),
              (pl.BlockSpec((D_MODEL, HEAD_LANES), lambda i: (0, 0)), jax.ShapeDtypeStruct((D_MODEL, HEAD_LANES), bf16)),
              piece(front, n_g)]
    in_proj_specs = [pl.BlockSpec((MXU_COLS, D_MODEL), lambda i: (jnp.minimum(i, front + n_g - 1), 0)),
                     pl.BlockSpec((HEAD_LANES, D_MODEL), lambda i: (_IN_PROJ_SPLITS[2] // HEAD_LANES, 0)),
                     pl.BlockSpec((MXU_COLS, D_MODEL), lambda i: (front + 1 + jnp.clip(i - front, 0, n_g - 1), 0))]
    plains = [blocked(p) for p in plain]
    res = pl.pallas_call(
        functools.partial(_ffn_cast_kernel, n_a=n_a, n_plain=len(plain)),
        grid=(n_a + n_b,),
        in_specs=in_specs + in_proj_specs + [s for s, _ in plains],
        out_specs=tuple(out_specs + [s for s, _ in pieces] + [s for s, _ in plains]),
        out_shape=tuple(out_shapes + [o for _, o in pieces] + [o for _, o in plains]),
        scratch_shapes=scratch,
        compiler_params=_params(1, LARGE_VMEM_LIMIT_BYTES),
        name="ffn_casts",
    )(xa, xb, g, w_in_f, w_out_f, gfin, w_in_t, w_in_t, w_in_t, *plain)
    return res[0], res[1], res[2:7], res[7:]


def _slab(bufs, j):
    return bufs[j // SLABS_PER_BUF], j % SLABS_PER_BUF


def _window_scratch(n_bufs, n_seqs, n_rows):
    return [pltpu.VMEM((SLABS_PER_BUF, n_seqs, n_rows, LANES), f32) for _ in range(n_bufs)]


def _store_rows(bufs, row0, val, T, slab0=0):
    for j in range(val.shape[1] // LANES):
        ref, k = _slab(bufs, slab0 + j)
        S = ref.shape[1]
        blk = val[:, j * LANES:(j + 1) * LANES]
        if S == 1:
            ref[k, 0, row0:row0 + T, :] = blk
        else:
            ref[k, :, row0:row0 + T, :] = blk.reshape(S, T, LANES)


def _window(bufs, j, row0, n):
    ref, k = _slab(bufs, j)
    S = ref.shape[1]
    rows = pl.ds(row0, n) if row0 % SUBLANES == 0 else pl.ds(row0, n, stride=1)
    if S == 1:
        return ref[k, 0, rows, :]
    v = ref[k, :, rows, :]
    return v.reshape(S * n, LANES) if n % SUBLANES == 0 else v


def _pool_branch(hn_s, wu_ref, ubufs, T, pos0, pgw_ref, pscale_ref, pwo_ref):
    S = ubufs[0].shape[1]
    assert T & (T - 1) == 0 and POOL_GROUP == MXU_COLS
    t_idx = jnp.bitwise_and(lax.broadcasted_iota(jnp.int32, (S * T, 1), 0), T - 1)
    n_seen = pos0 + 1 + t_idx
    _store_rows(ubufs, POOL_PAD, _dot(hn_s[...], wu_ref[...]), T)
    mixed = []
    for gi, w in enumerate(POOL_WINDOWS):
        inv_cnt = 1.0 / jnp.minimum(n_seen, w).astype(f32)
        d = []
        for j in range(gi * SLABS_PER_BUF, (gi + 1) * SLABS_PER_BUF):
            cur = _window(ubufs, j, POOL_PAD, T)
            s = cur
            for k in range(1, w):
                s = s + _window(ubufs, j, POOL_PAD - k, T)
            d.append((s * inv_cnt - cur).astype(bf16))
        mixed.append(_dot(jnp.concatenate(d, axis=-1), pgw_ref[gi]))
    y = jnp.concatenate(mixed, axis=-1) * pscale_ref[...]
    return _dot(y.astype(bf16), pwo_ref[...])


def _conv_slab(xbufs, T, convw_ref, convb_ref, j, xs_ref, b_ref, c_ref):
    c0 = j * LANES
    cols = slice(c0, c0 + LANES)
    h = 0.5 * convb_ref[:, cols]
    for k in range(CONV_WIDTH):
        h = h + _window(xbufs, j, CONV_PAD - (CONV_WIDTH - 1) + k, T) * (0.5 * convw_ref[k:k + 1, cols])
    v = h * jnp.tanh(h) + h
    if c0 < D_INNER:
        xs_ref[:, cols] = v
    elif c0 < D_INNER + BC_DIM:
        b_ref[:, c0 - D_INNER:c0 - D_INNER + LANES] = v
    else:
        c_ref[:, c0 - D_INNER - BC_DIM:c0 - D_INNER - BC_DIM + LANES] = v


def _project_conv(hn_s, wxbc_ref, xbufs, T, convw_ref, convb_ref, xs_ref, b_ref, c_ref):
    for jt in range(N_CONV_BUFS):
        slab0 = jt * SLABS_PER_BUF
        _store_rows(xbufs, CONV_PAD, _dot(hn_s[...], wxbc_ref[:, jt * MXU_COLS:(jt + 1) * MXU_COLS]), T, slab0=slab0)
        for j in range(slab0, slab0 + SLABS_PER_BUF):
            _conv_slab(xbufs, T, convw_ref, convb_ref, j, xs_ref, b_ref, c_ref)


def _gate_merge_out(x, hn_s, y_ref, bp, wz_ref, wgate_ref, snorm_ref, swo_ref, wo_ref, yn_s):
    for g in range(N_SSD_GROUPS):
        cols = slice(g * GROUP_WIDTH, (g + 1) * GROUP_WIDTH)
        z = _dot(hn_s[...], wz_ref[:, cols])
        yg = y_ref[:, cols] * _silu(z)
        yg = yg * lax.rsqrt(jnp.mean(yg * yg, axis=-1, keepdims=True) + EPS)
        yn_s[:, cols] = (yg * snorm_ref[:, cols]).astype(bf16)
    branch_ssd = _dot(yn_s[...], swo_ref[...])
    gate_pool = _sigmoid(_dot(hn_s[...], wgate_ref[:, 0:D_MODEL]))
    gate_ssd = _sigmoid(_dot(hn_s[...], wgate_ref[:, D_MODEL:2 * D_MODEL]))
    merged = (gate_pool * bp + gate_ssd * branch_ssd).astype(bf16)
    return x + _dot(merged, wo_ref[...])


def _ssd_chunk(rows, xs_s, b_s, c_s, dt_s, y_s, ht_s, alog_ref, dskip_ref, e_ref):
    L = CHUNK
    ri = lax.broadcasted_iota(jnp.int32, (L, L), 0)
    ci = lax.broadcasted_iota(jnp.int32, (L, L), 1)
    causal = ri >= ci
    tril = jnp.where(causal, 1.0, 0.0).astype(bf16)
    first_head = lax.broadcasted_iota(jnp.int32, (L, LANES), 1) < SSD_HEAD_DIM
    keep_first = jnp.where(first_head, 1.0, 0.0).astype(bf16)
    keep_second = jnp.where(first_head, 0.0, 1.0).astype(bf16)

    dt = dt_s[rows, :]
    dA = dt * (-jnp.exp(alog_ref[...]))
    hi, mid, lo = _split3(dA)
    acc = _dot(tril, jnp.concatenate([hi, mid, lo], axis=1))
    a = (acc[:, 0:LANES] + acc[:, LANES:2 * LANES] + acc[:, 2 * LANES:3 * LANES]) * LOG2_E
    aT = a.T[0:N_SSD_HEADS]
    dtT = dt.T[0:N_SSD_HEADS]
    wT = jnp.exp2(aT[:, L - 1:L] - aT) * dtT
    srcT = aT - jnp.log2(dtT)
    end_decay = jnp.broadcast_to(jnp.exp2(a[L - 1:L, :]), (2 * SUBLANES, HEAD_LANES))
    e_hi, e_mid, e_lo = (t.astype(f32) for t in _split3(end_decay))
    sel = lax.broadcasted_iota(jnp.int32, (2 * SUBLANES, HEAD_LANES), 0)
    stacked = jnp.where(sel == 0, e_hi, jnp.where(sel == 1, e_mid, jnp.where(sel == 2, e_lo, 0.0)))
    cdec = jnp.sum(_dot(stacked.astype(bf16), e_ref[...]), axis=0, keepdims=True)

    def head_mats(h, cb, bT):
        acol = jnp.broadcast_to(a[:, h:h + 1], (L, L))
        m = (cb * jnp.exp2(jnp.where(causal, acol - srcT[h:h + 1, :], -jnp.inf))).astype(bf16)
        bw = (bT * wT[h:h + 1, :]).astype(bf16)
        return m, bw, acol

    pairs_per_group = HEADS_PER_GROUP // 2
    for q in range(N_SSD_HEADS // 2):
        g, qg = divmod(q, pairs_per_group)
        if qg == 0:
            gcols = slice(g * D_STATE, (g + 1) * D_STATE)
            b_f = b_s[rows, gcols]
            c_b = c_s[rows, gcols].astype(bf16)
            cb = lax.dot_general(c_b, b_f.astype(bf16), _NT, preferred_element_type=f32)
            bT = b_f.T
            ch = _dot(c_b, ht_s[:, g * GROUP_WIDTH:(g + 1) * GROUP_WIDTH].astype(bf16))
        cols = slice(q * LANES, (q + 1) * LANES)
        xq = xs_s[rows, cols]
        xq_b = xq.astype(bf16)
        x2 = jnp.concatenate([xq_b * keep_first, xq_b * keep_second], axis=0)
        m_a, bw_a, acol_a = head_mats(2 * q, cb, bT)
        m_b, bw_b, acol_b = head_mats(2 * q + 1, cb, bT)
        y_off = ch[:, qg * LANES:(qg + 1) * LANES] * jnp.exp2(jnp.where(first_head, acol_a, acol_b))
        y_s[rows, cols] = _dot(jnp.concatenate([m_a, m_b], axis=1), x2) + y_off + dskip_ref[:, cols] * xq
        st = _dot(jnp.concatenate([bw_a, bw_b], axis=1), x2)
        ht_s[:, cols] = ht_s[:, cols] * cdec[:, cols] + st


def _mixer_prompt_kernel(x_ref, dxs_ref, db_ref, dc_ref, ddt_ref, dh0_ref,
                         gmix_ref, wu_ref, wz_ref, wxbc_ref, wdt_ref, wgate_ref, pgw_ref, pscale_ref,
                         pwo_ref, convw_ref, convb_ref, dtb_ref, alog_ref, dskip_ref, snorm_ref, swo_ref,
                         wo_ref, e_ref,
                         o_ref, npool_ref, nconv_ref, nssm_ref, dy_ref, dhn_ref,
                         hn_s, xs_s, b_s, c_s, dt_s, y_s, ht_s, yn_s, *window_bufs, decode_T):
    T = PROMPT_TILE
    ti = pl.program_id(1)
    ubufs, xbufs = window_bufs[:N_POOL_BUFS], window_bufs[N_POOL_BUFS:]

    @pl.when(ti == 0)
    def _():
        for ref in ubufs:
            ref[:, :, 0:POOL_PAD, :] = jnp.zeros((SLABS_PER_BUF, 1, POOL_PAD, LANES), f32)
        for ref in xbufs:
            ref[:, :, 0:CONV_PAD, :] = jnp.zeros((SLABS_PER_BUF, 1, CONV_PAD, LANES), f32)
        ht_s[...] = jnp.zeros(ht_s.shape, f32)

    x = x_ref[0]
    hn_s[...] = _rms(x, gmix_ref[...]).astype(bf16)

    bp = _pool_branch(hn_s, wu_ref, ubufs, T, ti * T, pgw_ref, pscale_ref, pwo_ref)

    _project_conv(hn_s, wxbc_ref, xbufs, T, convw_ref, convb_ref, xs_s, b_s, c_s)
    dt_s[...] = _softplus(_dot(hn_s[...], wdt_ref[...]) + dtb_ref[...])

    def chunk(c, carry):
        rows = pl.ds(pl.multiple_of(c * CHUNK, CHUNK), CHUNK)
        _ssd_chunk(rows, xs_s, b_s, c_s, dt_s, y_s, ht_s, alog_ref, dskip_ref, e_ref)
        return carry

    lax.fori_loop(0, T // CHUNK, chunk, 0)

    o_ref[0] = _gate_merge_out(x, hn_s, y_s, bp, wz_ref, wgate_ref, snorm_ref, swo_ref, wo_ref, yn_s)

    @pl.when(ti == pl.num_programs(1) - 1)
    def _():
        for j in range(POOL_WIDTH // LANES):
            npool_ref[0, 0, :, j * LANES:(j + 1) * LANES] = _window(ubufs, j, POOL_PAD + T - POOL_HIST, POOL_HIST)
        for j in range(CONV_DIM // LANES):
            nconv_ref[0, 0, :, j * LANES:(j + 1) * LANES] = _window(
                xbufs, j, CONV_PAD + T - (CONV_WIDTH - 1), CONV_WIDTH - 1)
        for g in range(N_SSD_GROUPS):
            hg = ht_s[:, g * GROUP_WIDTH:(g + 1) * GROUP_WIDTH].T
            nssm_ref[0, 0, g * HEADS_PER_GROUP:(g + 1) * HEADS_PER_GROUP] = hg.reshape(
                HEADS_PER_GROUP, SSD_HEAD_DIM, D_STATE)

    _decode_seqs(dxs_ref, db_ref, dc_ref, ddt_ref, dh0_ref, alog_ref, dskip_ref, e_ref, dy_ref, dhn_ref,
                 T=decode_T, n_seqs=dxs_ref.shape[0])

    for ref in ubufs:
        ref[:, :, 0:POOL_PAD, :] = ref[:, :, T:T + POOL_PAD, :]
    for ref in xbufs:
        ref[:, :, 0:CONV_PAD, :] = ref[:, :, T:T + CONV_PAD, :]


def _mixer_prompt(x1, w, dxs, dbm, dcm, ddt, dh0, decode_T):
    B, S, _ = x1.shape
    T = PROMPT_TILE
    assert S % T == 0 and T % CHUNK == 0 and T >= POOL_PAD
    n_tiles = S // T
    nseq = dxs.shape[0] // decode_T
    assert nseq % (B * n_tiles) == 0
    Q = nseq // (B * n_tiles)
    seq_spec = lambda width: pl.BlockSpec((Q, decode_T, width), lambda b, t: (b * n_tiles + t, 0, 0))
    state_spec = pl.BlockSpec((1, Q, N_SSD_HEADS, SSD_HEAD_DIM, D_STATE), lambda b, t: (0, b * n_tiles + t, 0, 0, 0))
    as_seqs = lambda v: v.reshape(nseq, decode_T, v.shape[1])
    consts = [w["gmix"], w["wu"], w["wz"], w["wxbc"], w["wdt"], w["wgate"], w["pgw"], w["pscale"], w["pwo"],
              w["convw"], w["convb"], w["dtb"], w["alog"], w["dskip"], w["snorm"], w["swo"], w["wo"], w["expand"]]
    return pl.pallas_call(
        functools.partial(_mixer_prompt_kernel, decode_T=decode_T),
        grid=(B, n_tiles),
        in_specs=[pl.BlockSpec((1, T, D_MODEL), lambda b, t: (b, t, 0)),
                  seq_spec(D_INNER), seq_spec(BC_DIM), seq_spec(BC_DIM), seq_spec(HEAD_LANES), state_spec]
                 + [_const_spec(c.shape) for c in consts],
        out_specs=(pl.BlockSpec((1, T, D_MODEL), lambda b, t: (b, t, 0)),
                   pl.BlockSpec((1, 1, POOL_HIST, POOL_WIDTH), lambda b, t: (0, b, 0, 0)),
                   pl.BlockSpec((1, 1, CONV_WIDTH - 1, CONV_DIM), lambda b, t: (0, b, 0, 0)),
                   pl.BlockSpec((1, 1, N_SSD_HEADS, SSD_HEAD_DIM, D_STATE), lambda b, t: (0, b, 0, 0, 0)),
                   seq_spec(D_INNER), state_spec),
        out_shape=(jax.ShapeDtypeStruct((B, S, D_MODEL), f32),
                   jax.ShapeDtypeStruct((1, B, POOL_HIST, POOL_WIDTH), f32),
                   jax.ShapeDtypeStruct((1, B, CONV_WIDTH - 1, CONV_DIM), f32),
                   jax.ShapeDtypeStruct((1, B, N_SSD_HEADS, SSD_HEAD_DIM, D_STATE), f32),
                   jax.ShapeDtypeStruct((nseq, decode_T, D_INNER), f32),
                   jax.ShapeDtypeStruct(dh0.shape, f32)),
        scratch_shapes=[pltpu.VMEM((T, D_MODEL), bf16),
                        pltpu.VMEM((T, D_INNER), f32),
                        pltpu.VMEM((T, BC_DIM), f32),
                        pltpu.VMEM((T, BC_DIM), f32),
                        pltpu.VMEM((T, HEAD_LANES), f32),
                        pltpu.VMEM((T, D_INNER), f32),
                        pltpu.VMEM((D_STATE, D_INNER), f32),
                        pltpu.VMEM((T, D_INNER), bf16)]
                       + _window_scratch(N_POOL_BUFS, 1, POOL_PAD + T)
                       + _window_scratch(N_CONV_BUFS, 1, CONV_PAD + T),
        compiler_params=_params(2, LARGE_VMEM_LIMIT_BYTES),
        name="mixer_prompt",
    )(x1, as_seqs(dxs), as_seqs(dbm), as_seqs(dcm), as_seqs(ddt), dh0, *consts)


def _load_history(bufs, hist_ref, row0):
    for j in range(hist_ref.shape[2] // LANES):
        ref, k = _slab(bufs, j)
        n_slabs, S, n_rows, _ = ref.shape
        flat = ref.reshape(n_slabs, S * n_rows, LANES)
        for r in range(hist_ref.shape[0]):
            flat[k, pl.ds(row0 + r, S, stride=n_rows), :] = hist_ref[r, :, j * LANES:(j + 1) * LANES]


def _store_history(hist_ref, bufs, row0):
    for j in range(hist_ref.shape[2] // LANES):
        ref, k = _slab(bufs, j)
        n_slabs, S, n_rows, _ = ref.shape
        flat = ref.reshape(n_slabs, S * n_rows, LANES)
        for r in range(hist_ref.shape[0]):
            hist_ref[r, :, j * LANES:(j + 1) * LANES] = flat[k, pl.ds(row0 + r, S, stride=n_rows), :]


def _sample_pre_kernel(x_ref, ph_ref, ch_ref, gmix_ref, wu_ref, wxbc_ref, wdt_ref, pgw_ref, pscale_ref, pwo_ref,
                       convw_ref, convb_ref, dtb_ref,
                       bp_ref, xs_ref, b_ref, c_ref, dt_ref, npool_ref, nconv_ref,
                       *window_bufs, T):
    ubufs, xbufs = window_bufs[:N_POOL_BUFS], window_bufs[N_POOL_BUFS:]
    hn = _rms(x_ref[...], gmix_ref[...]).astype(bf16)

    _load_history(ubufs, ph_ref, POOL_PAD - POOL_HIST)
    bp_ref[...] = _pool_branch(hn, wu_ref, ubufs, T, PAST_LEN, pgw_ref, pscale_ref, pwo_ref)
    _store_history(npool_ref, ubufs, POOL_PAD + T - POOL_HIST)

    _load_history(xbufs, ch_ref, CONV_PAD - (CONV_WIDTH - 1))
    _project_conv(hn, wxbc_ref, xbufs, T, convw_ref, convb_ref, xs_ref, b_ref, c_ref)
    _store_history(nconv_ref, xbufs, CONV_PAD + T - (CONV_WIDTH - 1))
    dt_ref[...] = _softplus(_dot(hn, wdt_ref[...]) + dtb_ref[...])


def _sample_pre(x1, pool_hist, conv_hist, w, T):
    rows = x1.shape[0]
    nseq = rows // T
    S = SAMPLE_SEQS
    R = S * T
    assert nseq % S == 0 and T % SUBLANES == 0
    consts = [w["gmix"], w["wu"], w["wxbc"], w["wdt"], w["pgw"], w["pscale"], w["pwo"], w["convw"], w["convb"],
              w["dtb"]]
    row_spec = lambda width: pl.BlockSpec((R, width), lambda i: (i, 0))
    return pl.pallas_call(
        functools.partial(_sample_pre_kernel, T=T),
        grid=(nseq // S,),
        in_specs=[row_spec(D_MODEL),
                  pl.BlockSpec((POOL_HIST, S, POOL_WIDTH), lambda i: (0, i, 0)),
                  pl.BlockSpec((CONV_WIDTH - 1, S, CONV_DIM), lambda i: (0, i, 0))]
                 + [_const_spec(c.shape) for c in consts],
        out_specs=(row_spec(D_MODEL), row_spec(D_INNER), row_spec(BC_DIM), row_spec(BC_DIM), row_spec(HEAD_LANES),
                   pl.BlockSpec((POOL_HIST, S, POOL_WIDTH), lambda i: (0, i, 0)),
                   pl.BlockSpec((CONV_WIDTH - 1, S, CONV_DIM), lambda i: (0, i, 0))),
        out_shape=(jax.ShapeDtypeStruct((rows, D_MODEL), f32),
                   jax.ShapeDtypeStruct((rows, D_INNER), f32),
                   jax.ShapeDtypeStruct((rows, BC_DIM), f32),
                   jax.ShapeDtypeStruct((rows, BC_DIM), f32),
                   jax.ShapeDtypeStruct((rows, HEAD_LANES), f32),
                   jax.ShapeDtypeStruct((POOL_HIST, nseq, POOL_WIDTH), f32),
                   jax.ShapeDtypeStruct((CONV_WIDTH - 1, nseq, CONV_DIM), f32)),
        scratch_shapes=_window_scratch(N_POOL_BUFS, S, POOL_PAD + T) + _window_scratch(N_CONV_BUFS, S, CONV_PAD + T),
        compiler_params=_params(1),
        name="sample_pre",
    )(x1, pool_hist, conv_hist, *consts)


def _decode_seqs(xs_ref, b_ref, c_ref, dt_ref, h0_ref, alog_ref, dskip_ref, e_ref, y_ref, hn_ref, *, T, n_seqs):
    assert T == SUBLANES
    lane_group = lax.broadcasted_iota(jnp.int32, (T, HEAD_LANES), 1) // HEADS_PER_GROUP
    neg_a = -jnp.exp(alog_ref[...])

    def shift(v, d):
        if d == 0:
            return v
        r = lax.broadcasted_iota(jnp.int32, v.shape, 0)
        return jnp.where(r >= d, pltpu.roll(v, d, axis=0), 0.0)

    def per_seq(s, carry):
        dt = dt_ref[s]
        a = dt * neg_a
        d = 1
        while d < T:
            a = a + shift(a, d)
            d *= 2
        a_end = a[T - 1:T, :]
        x = xs_ref[s]
        bm = b_ref[s]
        cm = c_ref[s]
        terms = []
        for d in range(T):
            cbv = jnp.zeros((T, HEAD_LANES), f32)
            for g in range(N_SSD_GROUPS):
                gc = slice(g * D_STATE, (g + 1) * D_STATE)
                cb = jnp.sum(cm[:, gc] * shift(bm[:, gc], d), axis=-1, keepdims=True)
                cbv = jnp.where(lane_group == g, cb, cbv)
            terms.append(cbv * jnp.exp(a - shift(a, d)) * shift(dt, d))
        terms.append(jnp.exp(a))
        terms.append(jnp.exp(a_end - a) * dt)
        v = jnp.concatenate(terms, axis=0)
        ex = _dot(v.astype(bf16), e_ref[...])

        y = dskip_ref[...] * x
        for d in range(T):
            y = y + ex[d * T:(d + 1) * T] * shift(x, d)
        ea = ex[T * T:(T + 1) * T]
        xw = (x * ex[(T + 1) * T:(T + 2) * T]).astype(bf16)
        cdec = jnp.exp(a_end)
        y_groups = []
        for g in range(N_SSD_GROUPS):
            gc = slice(g * D_STATE, (g + 1) * D_STATE)
            cols = slice(g * GROUP_WIDTH, (g + 1) * GROUP_WIDTH)
            heads = slice(g * HEADS_PER_GROUP, (g + 1) * HEADS_PER_GROUP)
            h0g = h0_ref[0, s, heads].reshape(GROUP_WIDTH, D_STATE)
            ch = lax.dot_general(cm[:, gc].astype(bf16), h0g.astype(bf16), _NT, preferred_element_type=f32)
            y_groups.append(y[:, cols] + ea[:, cols] * ch)
            st = lax.dot_general(xw[:, cols], bm[:, gc].astype(bf16), _TN, preferred_element_type=f32)
            for r in range(HEADS_PER_GROUP):
                h = g * HEADS_PER_GROUP + r
                hn_ref[0, s, h] = (h0_ref[0, s, h] * cdec[:, h:h + 1]
                                   + st[r * SSD_HEAD_DIM:(r + 1) * SSD_HEAD_DIM, :])
        y_ref[s] = jnp.concatenate(y_groups, axis=-1)
        return carry

    lax.fori_loop(0, n_seqs, per_seq, 0, unroll=True)


def _sample_post_kernel(x_ref, y_ref, bp_ref, gmix_ref, wz_ref, wgate_ref, snorm_ref, swo_ref, wo_ref,
                        o_ref, hn_s, yn_s):
    x = x_ref[...]
    hn_s[...] = _rms(x, gmix_ref[...]).astype(bf16)
    o_ref[...] = _gate_merge_out(x, hn_s, y_ref, bp_ref[...], wz_ref, wgate_ref, snorm_ref, swo_ref, wo_ref, yn_s)


def _sample_post(x1, y, bp, w, T):
    rows = x1.shape[0]
    R = SAMPLE_POST_SEQS * T
    assert rows % R == 0
    consts = [w["gmix"], w["wz"], w["wgate"], w["snorm"], w["swo"], w["wo"]]
    row_spec = lambda width: pl.BlockSpec((R, width), lambda i: (i, 0))
    return pl.pallas_call(
        _sample_post_kernel,
        grid=(rows // R,),
        in_specs=[row_spec(D_MODEL), row_spec(D_INNER), row_spec(D_MODEL)] + [_const_spec(c.shape) for c in consts],
        out_specs=row_spec(D_MODEL),
        out_shape=jax.ShapeDtypeStruct((rows, D_MODEL), f32),
        scratch_shapes=[pltpu.VMEM((R, D_MODEL), bf16), pltpu.VMEM((R, D_INNER), bf16)],
        compiler_params=_params(1),
        name="sample_post",
    )(x1, y, bp, *consts)


def _small_params(l, norm_mix, pool_scale, conv_w, conv_b, dt_bias, a_log, d_skip, ssd_norm):
    pad_heads = lambda v: jnp.pad(v, ((0, 0), (0, HEAD_LANES - N_SSD_HEADS)))
    head_of_lane = jnp.arange(D_INNER, dtype=jnp.int32) // SSD_HEAD_DIM
    expand = (jnp.arange(HEAD_LANES, dtype=jnp.int32)[:, None] == head_of_lane[None, :]).astype(bf16)
    return dict(
        gmix=norm_mix[l][None, :], pscale=pool_scale[l][None, :], convw=conv_w[l], convb=conv_b[l][None, :],
        dtb=pad_heads(dt_bias[l][None, :]), alog=pad_heads(a_log[l][None, :]),
        dskip=jnp.repeat(d_skip[l], SSD_HEAD_DIM)[None, :], snorm=ssd_norm[l][None, :], expand=expand)


def kernel(x_prompt, x_sample, state_pool, state_conv, state_ssm, norm_ffn1, ffn1_w_in, ffn1_w_out, norm_mix, w_in,
           pool_w_group, pool_scale, pool_w_out, conv_w, conv_b, dt_bias, a_log, d_skip, ssd_norm, ssd_w_out, w_o,
           norm_ffn2, ffn2_w_in, ffn2_w_out, norm_final):
    depth = w_in.shape[0]
    B, S, _ = x_prompt.shape
    DB, T, _ = x_sample.shape
    gfin = norm_final[None, :]
    xp = x_prompt.reshape(B * S, D_MODEL)
    xs = x_sample.reshape(DB * T, D_MODEL)
    outs = [[] for _ in range(6)]
    for l in range(depth):
        last = l == depth - 1
        g1, g2 = norm_ffn1[l][None, :], norm_ffn2[l][None, :]
        w = _small_params(l, norm_mix, pool_scale, conv_w, conv_b, dt_bias, a_log, d_skip, ssd_norm)
        n_pg = pool_w_group.shape[1]
        xp, xs, in_proj, casts = _ffn_and_casts(
            xp, xs, g1, ffn1_w_in[l], ffn1_w_out[l], gfin, w_in[l].T,
            [pool_w_group[l].reshape(n_pg * POOL_GROUP, POOL_GROUP), pool_w_out[l], ssd_w_out[l], w_o[l],
             ffn2_w_in[l], ffn2_w_out[l]])
        w.update(zip(("wu", "wz", "wxbc", "wdt", "wgate"), in_proj))
        w.update(pgw=casts[0].reshape(n_pg, POOL_GROUP, POOL_GROUP), pwo=casts[1], swo=casts[2], wo=casts[3])
        f2_in, f2_out = casts[4], casts[5]
        hist_major = lambda v: jnp.transpose(v, (1, 0, 2))
        bp, cxs, cb, cc, cdt, spool, sconv = _sample_pre(xs, hist_major(state_pool[l]), hist_major(state_conv[l]), w, T)
        spool, sconv = hist_major(spool)[None], hist_major(sconv)[None]
        xp3, npool, nconv, nssm, y, sssm = _mixer_prompt(xp.reshape(B, S, D_MODEL), w, cxs, cb, cc, cdt,
                                                         state_ssm[l:l + 1], T)
        xs = _sample_post(xs, y.reshape(DB * T, D_INNER), bp, w, T)
        xp, xs = _ffn(xp3.reshape(B * S, D_MODEL), xs, g2, f2_in, f2_out, gfin, final_norm=last)
        for acc, v in zip(outs, (npool, nconv, nssm, spool, sconv, sssm)):
            acc.append(v)
    stack = lambda vs: vs[0] if len(vs) == 1 else jnp.concatenate(vs, axis=0)
    return (xp.reshape(B, S, D_MODEL), xs.reshape(DB, T, D_MODEL),
            stack(outs[0]), stack(outs[1]), stack(outs[2]), stack(outs[3]), stack(outs[4]), stack(outs[5]))
```

```python
import functools

import jax
import jax.numpy as jnp
from jax import lax
from jax.experimental import pallas as pl
from jax.experimental.pallas import tpu as pltpu

f32 = jnp.float32
bf16 = jnp.bfloat16

D_MODEL = 1024
D_FF = 2816
POOL_WINDOWS = (2, 4, 8, 16)
POOL_WIDTH = D_MODEL
POOL_GROUP = POOL_WIDTH // len(POOL_WINDOWS)
POOL_HIST = max(POOL_WINDOWS) - 1
D_INNER = 2 * D_MODEL
SSD_HEAD_DIM = 64
N_SSD_HEADS = D_INNER // SSD_HEAD_DIM
N_SSD_GROUPS = 4
HEADS_PER_GROUP = N_SSD_HEADS // N_SSD_GROUPS
GROUP_WIDTH = D_INNER // N_SSD_GROUPS
D_STATE = 128
CONV_WIDTH = 4
BC_DIM = N_SSD_GROUPS * D_STATE
CONV_DIM = D_INNER + 2 * BC_DIM
CHUNK = 128
PAST_LEN = 16384
EPS = 1e-6
LOG2_E = 1.4426950408889634

LANES = 128
SUBLANES = 8
MXU_COLS = 256
VMEM_LIMIT_BYTES = 56 * 1024 * 1024
LARGE_VMEM_LIMIT_BYTES = 60 * 1024 * 1024

FFN_ROWS = 512
FFN_CHUNK = 256
FFN_STAGE_COLS = 128
FFN_STAGE_SLOTS = 4
PROMPT_TILE = 256
SAMPLE_SEQS = 32
SAMPLE_POST_SEQS = 32
POOL_PAD = 16
CONV_PAD = 8
HEAD_LANES = LANES
SLABS_PER_BUF = MXU_COLS // LANES
N_POOL_BUFS = POOL_WIDTH // MXU_COLS
N_CONV_BUFS = CONV_DIM // MXU_COLS

_NT = (((1,), (1,)), ((), ()))
_TN = (((0,), (0,)), ((), ()))


def _rms(x, g):
    return x * lax.rsqrt(jnp.mean(x * x, axis=-1, keepdims=True) + EPS) * g


def _sigmoid(v):
    return 0.5 * jnp.tanh(0.5 * v) + 0.5


def _silu(v):
    h = 0.5 * v
    return h * jnp.tanh(h) + h


def _softplus(v):
    return jnp.maximum(v, 0.0) + jnp.log1p(jnp.exp(-jnp.abs(v)))


def _dot(a, b):
    return jnp.dot(a, b, preferred_element_type=f32)


def _split3(v):
    hi = v.astype(bf16)
    r1 = v - hi.astype(f32)
    mid = r1.astype(bf16)
    lo = (r1 - mid.astype(f32)).astype(bf16)
    return hi, mid, lo


def _const_spec(shape):
    nd = len(shape)
    return pl.BlockSpec(shape, lambda *_: (0,) * nd, pipeline_mode=pl.Buffered(1))


def _params(n_grid, vmem_limit_bytes=VMEM_LIMIT_BYTES):
    return pltpu.CompilerParams(dimension_semantics=("arbitrary",) * n_grid,
                                vmem_limit_bytes=vmem_limit_bytes)


def _ffn_kernel(xa_ref, xb_ref, g_ref, win_ref, wout_ref, gfin_ref, oa_ref, ob_ref, hn_s, act_s, *, n_a, final_norm,
                side_work=None, fetch=None):
    def tile(x_ref, o_ref, extra=None, fetch=None):
        commit = extra() if extra is not None else None
        hn_s[...] = _rms(x_ref[...], g_ref[...]).astype(bf16)
        for c in range(D_FF // FFN_CHUNK):
            lo = c * FFN_CHUNK
            if fetch is not None:
                fetch(c)
            gate = _dot(hn_s[...], win_ref[:, lo:lo + FFN_CHUNK])
            up = _dot(hn_s[...], win_ref[:, D_FF + lo:D_FF + lo + FFN_CHUNK])
            act_s[:, lo:lo + FFN_CHUNK] = (_silu(gate) * up).astype(bf16)
        out = x_ref[...] + 0.5 * _dot(act_s[...], wout_ref[...])
        if final_norm:
            out = _rms(out, gfin_ref[...])
        o_ref[...] = out
        if commit is not None:
            commit()

    i = pl.program_id(0)
    on_a = i < n_a
    if fetch is None:
        pl.when(on_a)(lambda: tile(xa_ref, oa_ref, side_work))
    else:
        pl.when(i == 0)(lambda: tile(xa_ref, oa_ref, side_work, fetch))
        pl.when(jnp.logical_and(i > 0, on_a))(lambda: tile(xa_ref, oa_ref, side_work))
    pl.when(jnp.logical_not(on_a))(lambda: tile(xb_ref, ob_ref))


def _ffn_specs(rows_a, rows_b):
    assert rows_a % FFN_ROWS == 0 and rows_b % FFN_ROWS == 0
    n_a, n_b = rows_a // FFN_ROWS, rows_b // FFN_ROWS
    a_spec = pl.BlockSpec((FFN_ROWS, D_MODEL), lambda i: (jnp.minimum(i, n_a - 1), 0))
    b_spec = pl.BlockSpec((FFN_ROWS, D_MODEL), lambda i: (jnp.maximum(i - n_a, 0), 0))
    in_specs = [a_spec, b_spec,
                _const_spec((1, D_MODEL)),
                pl.BlockSpec(memory_space=pl.ANY),
                pl.BlockSpec(memory_space=pl.ANY),
                _const_spec((1, D_MODEL))]
    out_shapes = [jax.ShapeDtypeStruct((rows_a, D_MODEL), f32), jax.ShapeDtypeStruct((rows_b, D_MODEL), f32)]
    scratch = [pltpu.VMEM((FFN_ROWS, D_MODEL), bf16), pltpu.VMEM((FFN_ROWS, D_FF), bf16),
               pltpu.VMEM((D_MODEL, 2 * D_FF), bf16), pltpu.VMEM((D_FF, D_MODEL), bf16)]
    return n_a, n_b, in_specs, [a_spec, b_spec], out_shapes, scratch


def _ffn_stream_kernel(xa_ref, xb_ref, g_ref, win_hbm, wout_hbm, gfin_ref, oa_ref, ob_ref, hn_s, act_s, win_s, wout_s,
                       sems, *, n_a, final_norm):
    def chunk_copies(c):
        lo = c * FFN_CHUNK
        cols, up_cols = pl.ds(lo, FFN_CHUNK), pl.ds(D_FF + lo, FFN_CHUNK)
        return (pltpu.make_async_copy(win_hbm.at[:, cols], win_s.at[:, cols], sems.at[c, 0]),
                pltpu.make_async_copy(win_hbm.at[:, up_cols], win_s.at[:, up_cols], sems.at[c, 1]),
                pltpu.make_async_copy(wout_hbm.at[cols, :], wout_s.at[cols, :], sems.at[c, 2]))

    def fetch(c):
        if c == 0:
            for k in range(D_FF // FFN_CHUNK):
                for cp in chunk_copies(k):
                    cp.start()
        for cp in chunk_copies(c):
            cp.wait()

    _ffn_kernel(xa_ref, xb_ref, g_ref, win_s, wout_s, gfin_ref, oa_ref, ob_ref, hn_s, act_s,
                n_a=n_a, final_norm=final_norm, fetch=fetch)


def _ffn(xa, xb, g, w_in_b, w_out_b, gfin, *, final_norm):
    n_a, n_b, in_specs, out_specs, out_shapes, scratch = _ffn_specs(xa.shape[0], xb.shape[0])
    return pl.pallas_call(
        functools.partial(_ffn_stream_kernel, n_a=n_a, final_norm=final_norm),
        grid=(n_a + n_b,),
        in_specs=in_specs, out_specs=tuple(out_specs), out_shape=tuple(out_shapes),
        scratch_shapes=scratch + [pltpu.SemaphoreType.DMA((D_FF // FFN_CHUNK, 3))],
        compiler_params=_params(1),
        name="ffn_final" if final_norm else "ffn",
    )(xa, xb, g, w_in_b, w_out_b, gfin)


_IN_PROJ_SPLITS = (POOL_WIDTH, POOL_WIDTH + D_INNER, POOL_WIDTH + D_INNER + CONV_DIM,
                   POOL_WIDTH + D_INNER + CONV_DIM + N_SSD_HEADS)
BF16_SUBLANES = 2 * SUBLANES


def _cast_block(n_rows, n_steps):
    period = 1
    while (n_rows * period) % n_steps or ((n_rows * period) // n_steps) % BF16_SUBLANES:
        period *= 2
        assert period <= n_steps
    return (n_rows * period) // n_steps, period


_PIECE_TILES = (POOL_WIDTH // MXU_COLS, D_INNER // MXU_COLS, CONV_DIM // MXU_COLS, 2 * D_MODEL // MXU_COLS)
_GATE_SKEW = _IN_PROJ_SPLITS[3] - _IN_PROJ_SPLITS[2]


def _store_when(cond, ref, val):
    @pl.when(cond)
    def _():
        ref[...] = val


def _ffn_cast_kernel(xa_ref, xb_ref, g_ref, win_hbm, wout_hbm, gfin_ref, wt_ref, wdt_f, wg_hi, *rest,
                     n_a, n_plain):
    plain_f = rest[:n_plain]
    oa_ref, ob_ref = rest[n_plain:n_plain + 2]
    wu_o, wz_o, wxbc_o, wdt_o, wgate_o = rest[n_plain + 2:n_plain + 7]
    plain_o = rest[n_plain + 7:2 * n_plain + 7]
    hn_s, act_s, win_s, wout_s, stage_in, stage_out, sems = rest[2 * n_plain + 7:]
    i = pl.program_id(0)
    n_u, n_z, n_x, n_g = _PIECE_TILES
    front = n_u + n_z + n_x

    def stage_copies(k):
        slot, lo = k % FFN_STAGE_SLOTS, k * FFN_STAGE_COLS
        return (pltpu.make_async_copy(win_hbm.at[:, pl.ds(lo, FFN_STAGE_COLS)], stage_in.at[slot, 0],
                                      sems.at[slot, 0]),
                pltpu.make_async_copy(win_hbm.at[:, pl.ds(D_FF + lo, FFN_STAGE_COLS)], stage_in.at[slot, 1],
                                      sems.at[slot, 1]),
                pltpu.make_async_copy(wout_hbm.at[pl.ds(lo, FFN_STAGE_COLS), :], stage_out.at[slot],
                                      sems.at[slot, 2]))

    def fetch(c):
        per_chunk = FFN_CHUNK // FFN_STAGE_COLS
        for k in range(c * per_chunk, (c + 1) * per_chunk):
            slot, lo = k % FFN_STAGE_SLOTS, k * FFN_STAGE_COLS
            for nxt in (range(FFN_STAGE_SLOTS) if k == 0 else [k + FFN_STAGE_SLOTS - 1]):
                if nxt < D_FF // FFN_STAGE_COLS:
                    for cp in stage_copies(nxt):
                        cp.start()
            for cp in stage_copies(k):
                cp.wait()
            win_s[:, lo:lo + FFN_STAGE_COLS] = stage_in[slot, 0].astype(bf16)
            win_s[:, D_FF + lo:D_FF + lo + FFN_STAGE_COLS] = stage_in[slot, 1].astype(bf16)
            wout_s[lo:lo + FFN_STAGE_COLS, :] = stage_out[slot].astype(bf16)

    def casts():
        t = wt_ref[...].T.astype(bf16)
        rows = jnp.concatenate([wt_ref[_GATE_SKEW:, :], wg_hi[0:_GATE_SKEW, :]], axis=0)
        tg = rows.T.astype(bf16)
        for src, dst in zip(plain_f, plain_o):
            dst[...] = src[...].astype(bf16)

        def commit():
            _store_when(i < n_u, wu_o, t)
            _store_when(jnp.logical_and(i >= n_u, i < n_u + n_z), wz_o, t)
            _store_when(jnp.logical_and(i >= n_u + n_z, i < front), wxbc_o, t)
            _store_when(jnp.logical_and(i >= front, i < front + n_g), wgate_o, tg)
        return commit

    _ffn_kernel(xa_ref, xb_ref, g_ref, win_s, wout_s, gfin_ref, oa_ref, ob_ref, hn_s, act_s,
                n_a=n_a, final_norm=False, side_work=casts, fetch=fetch)

    @pl.when(i == 0)
    def _():
        t = wdt_f[...].T
        lane = lax.broadcasted_iota(jnp.int32, t.shape, 1)
        wdt_o[...] = jnp.where(lane < N_SSD_HEADS, t, 0.0).astype(bf16)


def _ffn_and_casts(xa, xb, g, w_in_f, w_out_f, gfin, w_in_t, plain):
    n_a, n_b, in_specs, out_specs, out_shapes, scratch = _ffn_specs(xa.shape[0], xb.shape[0])
    scratch = scratch + [pltpu.VMEM((FFN_STAGE_SLOTS, 2, D_MODEL, FFN_STAGE_COLS), f32),
                         pltpu.VMEM((FFN_STAGE_SLOTS, FFN_STAGE_COLS, D_MODEL), f32),
                         pltpu.SemaphoreType.DMA((FFN_STAGE_SLOTS, 3))]
    n_u, n_z, n_x, n_g = _PIECE_TILES
    front = n_u + n_z + n_x
    assert n_a >= front + n_g and _IN_PROJ_SPLITS[2] == front * MXU_COLS and _IN_PROJ_SPLITS[2] % HEAD_LANES == 0

    def blocked(arr):
        blk, period = _cast_block(arr.shape[0], n_a)
        return (pl.BlockSpec((blk, arr.shape[1]), lambda i: (jnp.minimum(i, n_a - 1) // period, 0)),
                jax.ShapeDtypeStruct(arr.shape, bf16))

    def piece(first_step, n_tiles):
        return (pl.BlockSpec((D_MODEL, MXU_COLS), lambda i: (0, jnp.clip(i - first_step, 0, n_tiles - 1))),
                jax.ShapeDtypeStruct((D_MODEL, n_tiles * MXU_COLS), bf16))

    pieces = [piece(0, n_u), piece(n_u, n_z), piece(n_u + n_z, n_x),
              (pl.BlockSpec((D_MODEL, HEAD_LANES), lambda i: (0, 0)), jax.ShapeDtypeStruct((D_MODEL, HEAD_LANES), bf16)),
              piece(front, n_g)]
    in_proj_specs = [pl.BlockSpec((MXU_COLS, D_MODEL), lambda i: (jnp.minimum(i, front + n_g - 1), 0)),
                     pl.BlockSpec((HEAD_LANES, D_MODEL), lambda i: (_IN_PROJ_SPLITS[2] // HEAD_LANES, 0)),
                     pl.BlockSpec((MXU_COLS, D_MODEL), lambda i: (front + 1 + jnp.clip(i - front, 0, n_g - 1), 0))]
    plains = [blocked(p) for p in plain]
    res = pl.pallas_call(
        functools.partial(_ffn_cast_kernel, n_a=n_a, n_plain=len(plain)),
        grid=(n_a + n_b,),
        in_specs=in_specs + in_proj_specs + [s for s, _ in plains],
        out_specs=tuple(out_specs + [s for s, _ in pieces] + [s for s, _ in plains]),
        out_shape=tuple(out_shapes + [o for _, o in pieces] + [o for _, o in plains]),
        scratch_shapes=scratch,
        compiler_params=_params(1, LARGE_VMEM_LIMIT_BYTES),
        name="ffn_casts",
    )(xa, xb, g, w_in_f, w_out_f, gfin, w_in_t, w_in_t, w_in_t, *plain)
    return res[0], res[1], res[2:7], res[7:]


def _slab(bufs, j):
    return bufs[j // SLABS_PER_BUF], j % SLABS_PER_BUF


def _window_scratch(n_bufs, n_seqs, n_rows):
    return [pltpu.VMEM((SLABS_PER_BUF, n_seqs, n_rows, LANES), f32) for _ in range(n_bufs)]


def _store_rows(bufs, row0, val, T, slab0=0):
    for j in range(val.shape[1] // LANES):
        ref, k = _slab(bufs, slab0 + j)
        S = ref.shape[1]
        blk = val[:, j * LANES:(j + 1) * LANES]
        if S == 1:
            ref[k, 0, row0:row0 + T, :] = blk
        else:
            ref[k, :, row0:row0 + T, :] = blk.reshape(S, T, LANES)


def _window(bufs, j, row0, n):
    ref, k = _slab(bufs, j)
    S = ref.shape[1]
    rows = pl.ds(row0, n) if row0 % SUBLANES == 0 else pl.ds(row0, n, stride=1)
    if S == 1:
        return ref[k, 0, rows, :]
    v = ref[k, :, rows, :]
    return v.reshape(S * n, LANES) if n % SUBLANES == 0 else v


def _pool_branch(hn_s, wu_ref, ubufs, T, pos0, pgw_ref, pscale_ref, pwo_ref):
    S = ubufs[0].shape[1]
    assert T & (T - 1) == 0 and POOL_GROUP == MXU_COLS
    t_idx = jnp.bitwise_and(lax.broadcasted_iota(jnp.int32, (S * T, 1), 0), T - 1)
    n_seen = pos0 + 1 + t_idx
    _store_rows(ubufs, POOL_PAD, _dot(hn_s[...], wu_ref[...]), T)
    mixed = []
    for gi, w in enumerate(POOL_WINDOWS):
        inv_cnt = 1.0 / jnp.minimum(n_seen, w).astype(f32)
        d = []
        for j in range(gi * SLABS_PER_BUF, (gi + 1) * SLABS_PER_BUF):
            cur = _window(ubufs, j, POOL_PAD, T)
            s = cur
            for k in range(1, w):
                s = s + _window(ubufs, j, POOL_PAD - k, T)
            d.append((s * inv_cnt - cur).astype(bf16))
        mixed.append(_dot(jnp.concatenate(d, axis=-1), pgw_ref[gi]))
    y = jnp.concatenate(mixed, axis=-1) * pscale_ref[...]
    return _dot(y.astype(bf16), pwo_ref[...])


def _conv_slab(xbufs, T, convw_ref, convb_ref, j, xs_ref, b_ref, c_ref):
    c0 = j * LANES
    cols = slice(c0, c0 + LANES)
    h = 0.5 * convb_ref[:, cols]
    for k in range(CONV_WIDTH):
        h = h + _window(xbufs, j, CONV_PAD - (CONV_WIDTH - 1) + k, T) * (0.5 * convw_ref[k:k + 1, cols])
    v = h * jnp.tanh(h) + h
    if c0 < D_INNER:
        xs_ref[:, cols] = v
    elif c0 < D_INNER + BC_DIM:
        b_ref[:, c0 - D_INNER:c0 - D_INNER + LANES] = v
    else:
        c_ref[:, c0 - D_INNER - BC_DIM:c0 - D_INNER - BC_DIM + LANES] = v


def _project_conv(hn_s, wxbc_ref, xbufs, T, convw_ref, convb_ref, xs_ref, b_ref, c_ref):
    for jt in range(N_CONV_BUFS):
        slab0 = jt * SLABS_PER_BUF
        _store_rows(xbufs, CONV_PAD, _dot(hn_s[...], wxbc_ref[:, jt * MXU_COLS:(jt + 1) * MXU_COLS]), T, slab0=slab0)
        for j in range(slab0, slab0 + SLABS_PER_BUF):
            _conv_slab(xbufs, T, convw_ref, convb_ref, j, xs_ref, b_ref, c_ref)


def _gate_merge_out(x, hn_s, y_ref, bp, wz_ref, wgate_ref, snorm_ref, swo_ref, wo_ref, yn_s):
    for g in range(N_SSD_GROUPS):
        cols = slice(g * GROUP_WIDTH, (g + 1) * GROUP_WIDTH)
        z = _dot(hn_s[...], wz_ref[:, cols])
        yg = y_ref[:, cols] * _silu(z)
        yg = yg * lax.rsqrt(jnp.mean(yg * yg, axis=-1, keepdims=True) + EPS)
        yn_s[:, cols] = (yg * snorm_ref[:, cols]).astype(bf16)
    branch_ssd = _dot(yn_s[...], swo_ref[...])
    gate_pool = _sigmoid(_dot(hn_s[...], wgate_ref[:, 0:D_MODEL]))
    gate_ssd = _sigmoid(_dot(hn_s[...], wgate_ref[:, D_MODEL:2 * D_MODEL]))
    merged = (gate_pool * bp + gate_ssd * branch_ssd).astype(bf16)
    return x + _dot(merged, wo_ref[...])


def _ssd_chunk(rows, xs_s, b_s, c_s, dt_s, y_s, ht_s, alog_ref, dskip_ref, e_ref):
    L = CHUNK
    ri = lax.broadcasted_iota(jnp.int32, (L, L), 0)
    ci = lax.broadcasted_iota(jnp.int32, (L, L), 1)
    causal = ri >= ci
    tril = jnp.where(causal, 1.0, 0.0).astype(bf16)
    first_head = lax.broadcasted_iota(jnp.int32, (L, LANES), 1) < SSD_HEAD_DIM
    keep_first = jnp.where(first_head, 1.0, 0.0).astype(bf16)
    keep_second = jnp.where(first_head, 0.0, 1.0).astype(bf16)

    dt = dt_s[rows, :]
    dA = dt * (-jnp.exp(alog_ref[...]))
    hi, mid, lo = _split3(dA)
    acc = _dot(tril, jnp.concatenate([hi, mid, lo], axis=1))
    a = (acc[:, 0:LANES] + acc[:, LANES:2 * LANES] + acc[:, 2 * LANES:3 * LANES]) * LOG2_E
    aT = a.T[0:N_SSD_HEADS]
    dtT = dt.T[0:N_SSD_HEADS]
    wT = jnp.exp2(aT[:, L - 1:L] - aT) * dtT
    srcT = aT - jnp.log2(dtT)
    end_decay = jnp.broadcast_to(jnp.exp2(a[L - 1:L, :]), (2 * SUBLANES, HEAD_LANES))
    e_hi, e_mid, e_lo = (t.astype(f32) for t in _split3(end_decay))
    sel = lax.broadcasted_iota(jnp.int32, (2 * SUBLANES, HEAD_LANES), 0)
    stacked = jnp.where(sel == 0, e_hi, jnp.where(sel == 1, e_mid, jnp.where(sel == 2, e_lo, 0.0)))
    cdec = jnp.sum(_dot(stacked.astype(bf16), e_ref[...]), axis=0, keepdims=True)

    def head_mats(h, cb, bT):
        acol = jnp.broadcast_to(a[:, h:h + 1], (L, L))
        m = (cb * jnp.exp2(jnp.where(causal, acol - srcT[h:h + 1, :], -jnp.inf))).astype(bf16)
        bw = (bT * wT[h:h + 1, :]).astype(bf16)
        return m, bw, acol

    pairs_per_group = HEADS_PER_GROUP // 2
    for q in range(N_SSD_HEADS // 2):
        g, qg = divmod(q, pairs_per_group)
        if qg == 0:
            gcols = slice(g * D_STATE, (g + 1) * D_STATE)
            b_f = b_s[rows, gcols]
            c_b = c_s[rows, gcols].astype(bf16)
            cb = lax.dot_general(c_b, b_f.astype(bf16), _NT, preferred_element_type=f32)
            bT = b_f.T
            ch = _dot(c_b, ht_s[:, g * GROUP_WIDTH:(g + 1) * GROUP_WIDTH].astype(bf16))
        cols = slice(q * LANES, (q + 1) * LANES)
        xq = xs_s[rows, cols]
        xq_b = xq.astype(bf16)
        x2 = jnp.concatenate([xq_b * keep_first, xq_b * keep_second], axis=0)
        m_a, bw_a, acol_a = head_mats(2 * q, cb, bT)
        m_b, bw_b, acol_b = head_mats(2 * q + 1, cb, bT)
        y_off = ch[:, qg * LANES:(qg + 1) * LANES] * jnp.exp2(jnp.where(first_head, acol_a, acol_b))
        y_s[rows, cols] = _dot(jnp.concatenate([m_a, m_b], axis=1), x2) + y_off + dskip_ref[:, cols] * xq
        st = _dot(jnp.concatenate([bw_a, bw_b], axis=1), x2)
        ht_s[:, cols] = ht_s[:, cols] * cdec[:, cols] + st


def _mixer_prompt_kernel(x_ref, dxs_ref, db_ref, dc_ref, ddt_ref, dh0_ref,
                         gmix_ref, wu_ref, wz_ref, wxbc_ref, wdt_ref, wgate_ref, pgw_ref, pscale_ref,
                         pwo_ref, convw_ref, convb_ref, dtb_ref, alog_ref, dskip_ref, snorm_ref, swo_ref,
                         wo_ref, e_ref,
                         o_ref, npool_ref, nconv_ref, nssm_ref, dy_ref, dhn_ref,
                         hn_s, xs_s, b_s, c_s, dt_s, y_s, ht_s, yn_s, *window_bufs, decode_T):
    T = PROMPT_TILE
    ti = pl.program_id(1)
    ubufs, xbufs = window_bufs[:N_POOL_BUFS], window_bufs[N_POOL_BUFS:]

    @pl.when(ti == 0)
    def _():
        for ref in ubufs:
            ref[:, :, 0:POOL_PAD, :] = jnp.zeros((SLABS_PER_BUF, 1, POOL_PAD, LANES), f32)
        for ref in xbufs:
            ref[:, :, 0:CONV_PAD, :] = jnp.zeros((SLABS_PER_BUF, 1, CONV_PAD, LANES), f32)
        ht_s[...] = jnp.zeros(ht_s.shape, f32)

    x = x_ref[0]
    hn_s[...] = _rms(x, gmix_ref[...]).astype(bf16)

    bp = _pool_branch(hn_s, wu_ref, ubufs, T, ti * T, pgw_ref, pscale_ref, pwo_ref)

    _project_conv(hn_s, wxbc_ref, xbufs, T, convw_ref, convb_ref, xs_s, b_s, c_s)
    dt_s[...] = _softplus(_dot(hn_s[...], wdt_ref[...]) + dtb_ref[...])

    def chunk(c, carry):
        rows = pl.ds(pl.multiple_of(c * CHUNK, CHUNK), CHUNK)
        _ssd_chunk(rows, xs_s, b_s, c_s, dt_s, y_s, ht_s, alog_ref, dskip_ref, e_ref)
        return carry

    lax.fori_loop(0, T // CHUNK, chunk, 0)

    o_ref[0] = _gate_merge_out(x, hn_s, y_s, bp, wz_ref, wgate_ref, snorm_ref, swo_ref, wo_ref, yn_s)

    @pl.when(ti == pl.num_programs(1) - 1)
    def _():
        for j in range(POOL_WIDTH // LANES):
            npool_ref[0, 0, :, j * LANES:(j + 1) * LANES] = _window(ubufs, j, POOL_PAD + T - POOL_HIST, POOL_HIST)
        for j in range(CONV_DIM // LANES):
            nconv_ref[0, 0, :, j * LANES:(j + 1) * LANES] = _window(
                xbufs, j, CONV_PAD + T - (CONV_WIDTH - 1), CONV_WIDTH - 1)
        for g in range(N_SSD_GROUPS):
            hg = ht_s[:, g * GROUP_WIDTH:(g + 1) * GROUP_WIDTH].T
            nssm_ref[0, 0, g * HEADS_PER_GROUP:(g + 1) * HEADS_PER_GROUP] = hg.reshape(
                HEADS_PER_GROUP, SSD_HEAD_DIM, D_STATE)

    _decode_seqs(dxs_ref, db_ref, dc_ref, ddt_ref, dh0_ref, alog_ref, dskip_ref, e_ref, dy_ref, dhn_ref,
                 T=decode_T, n_seqs=dxs_ref.shape[0])

    for ref in ubufs:
        ref[:, :, 0:POOL_PAD, :] = ref[:, :, T:T + POOL_PAD, :]
    for ref in xbufs:
        ref[:, :, 0:CONV_PAD, :] = ref[:, :, T:T + CONV_PAD, :]


def _mixer_prompt(x1, w, dxs, dbm, dcm, ddt, dh0, decode_T):
    B, S, _ = x1.shape
    T = PROMPT_TILE
    assert S % T == 0 and T % CHUNK == 0 and T >= POOL_PAD
    n_tiles = S // T
    nseq = dxs.shape[0] // decode_T
    assert nseq % (B * n_tiles) == 0
    Q = nseq // (B * n_tiles)
    seq_spec = lambda width: pl.BlockSpec((Q, decode_T, width), lambda b, t: (b * n_tiles + t, 0, 0))
    state_spec = pl.BlockSpec((1, Q, N_SSD_HEADS, SSD_HEAD_DIM, D_STATE), lambda b, t: (0, b * n_tiles + t, 0, 0, 0))
    as_seqs = lambda v: v.reshape(nseq, decode_T, v.shape[1])
    consts = [w["gmix"], w["wu"], w["wz"], w["wxbc"], w["wdt"], w["wgate"], w["pgw"], w["pscale"], w["pwo"],
              w["convw"], w["convb"], w["dtb"], w["alog"], w["dskip"], w["snorm"], w["swo"], w["wo"], w["expand"]]
    return pl.pallas_call(
        functools.partial(_mixer_prompt_kernel, decode_T=decode_T),
        grid=(B, n_tiles),
        in_specs=[pl.BlockSpec((1, T, D_MODEL), lambda b, t: (b, t, 0)),
                  seq_spec(D_INNER), seq_spec(BC_DIM), seq_spec(BC_DIM), seq_spec(HEAD_LANES), state_spec]
                 + [_const_spec(c.shape) for c in consts],
        out_specs=(pl.BlockSpec((1, T, D_MODEL), lambda b, t: (b, t, 0)),
                   pl.BlockSpec((1, 1, POOL_HIST, POOL_WIDTH), lambda b, t: (0, b, 0, 0)),
                   pl.BlockSpec((1, 1, CONV_WIDTH - 1, CONV_DIM), lambda b, t: (0, b, 0, 0)),
                   pl.BlockSpec((1, 1, N_SSD_HEADS, SSD_HEAD_DIM, D_STATE), lambda b, t: (0, b, 0, 0, 0)),
                   seq_spec(D_INNER), state_spec),
        out_shape=(jax.ShapeDtypeStruct((B, S, D_MODEL), f32),
                   jax.ShapeDtypeStruct((1, B, POOL_HIST, POOL_WIDTH), f32),
                   jax.ShapeDtypeStruct((1, B, CONV_WIDTH - 1, CONV_DIM), f32),
                   jax.ShapeDtypeStruct((1, B, N_SSD_HEADS, SSD_HEAD_DIM, D_STATE), f32),
                   jax.ShapeDtypeStruct((nseq, decode_T, D_INNER), f32),
                   jax.ShapeDtypeStruct(dh0.shape, f32)),
        scratch_shapes=[pltpu.VMEM((T, D_MODEL), bf16),
                        pltpu.VMEM((T, D_INNER), f32),
                        pltpu.VMEM((T, BC_DIM), f32),
                        pltpu.VMEM((T, BC_DIM), f32),
                        pltpu.VMEM((T, HEAD_LANES), f32),
                        pltpu.VMEM((T, D_INNER), f32),
                        pltpu.VMEM((D_STATE, D_INNER), f32),
                        pltpu.VMEM((T, D_INNER), bf16)]
                       + _window_scratch(N_POOL_BUFS, 1, POOL_PAD + T)
                       + _window_scratch(N_CONV_BUFS, 1, CONV_PAD + T),
        compiler_params=_params(2, LARGE_VMEM_LIMIT_BYTES),
        name="mixer_prompt",
    )(x1, as_seqs(dxs), as_seqs(dbm), as_seqs(dcm), as_seqs(ddt), dh0, *consts)


def _load_history(bufs, hist_ref, row0):
    for j in range(hist_ref.shape[2] // LANES):
        ref, k = _slab(bufs, j)
        n_slabs, S, n_rows, _ = ref.shape
        flat = ref.reshape(n_slabs, S * n_rows, LANES)
        for r in range(hist_ref.shape[0]):
            flat[k, pl.ds(row0 + r, S, stride=n_rows), :] = hist_ref[r, :, j * LANES:(j + 1) * LANES]


def _store_history(hist_ref, bufs, row0):
    for j in range(hist_ref.shape[2] // LANES):
        ref, k = _slab(bufs, j)
        n_slabs, S, n_rows, _ = ref.shape
        flat = ref.reshape(n_slabs, S * n_rows, LANES)
        for r in range(hist_ref.shape[0]):
            hist_ref[r, :, j * LANES:(j + 1) * LANES] = flat[k, pl.ds(row0 + r, S, stride=n_rows), :]


def _sample_pre_kernel(x_ref, ph_ref, ch_ref, gmix_ref, wu_ref, wxbc_ref, wdt_ref, pgw_ref, pscale_ref, pwo_ref,
                       convw_ref, convb_ref, dtb_ref,
                       bp_ref, xs_ref, b_ref, c_ref, dt_ref, npool_ref, nconv_ref,
                       *window_bufs, T):
    ubufs, xbufs = window_bufs[:N_POOL_BUFS], window_bufs[N_POOL_BUFS:]
    hn = _rms(x_ref[...], gmix_ref[...]).astype(bf16)

    _load_history(ubufs, ph_ref, POOL_PAD - POOL_HIST)
    bp_ref[...] = _pool_branch(hn, wu_ref, ubufs, T, PAST_LEN, pgw_ref, pscale_ref, pwo_ref)
    _store_history(npool_ref, ubufs, POOL_PAD + T - POOL_HIST)

    _load_history(xbufs, ch_ref, CONV_PAD - (CONV_WIDTH - 1))
    _project_conv(hn, wxbc_ref, xbufs, T, convw_ref, convb_ref, xs_ref, b_ref, c_ref)
    _store_history(nconv_ref, xbufs, CONV_PAD + T - (CONV_WIDTH - 1))
    dt_ref[...] = _softplus(_dot(hn, wdt_ref[...]) + dtb_ref[...])


def _sample_pre(x1, pool_hist, conv_hist, w, T):
    rows = x1.shape[0]
    nseq = rows // T
    S = SAMPLE_SEQS
    R = S * T
    assert nseq % S == 0 and T % SUBLANES == 0
    consts = [w["gmix"], w["wu"], w["wxbc"], w["wdt"], w["pgw"], w["pscale"], w["pwo"], w["convw"], w["convb"],
              w["dtb"]]
    row_spec = lambda width: pl.BlockSpec((R, width), lambda i: (i, 0))
    return pl.pallas_call(
        functools.partial(_sample_pre_kernel, T=T),
        grid=(nseq // S,),
        in_specs=[row_spec(D_MODEL),
                  pl.BlockSpec((POOL_HIST, S, POOL_WIDTH), lambda i: (0, i, 0)),
                  pl.BlockSpec((CONV_WIDTH - 1, S, CONV_DIM), lambda i: (0, i, 0))]
                 + [_const_spec(c.shape) for c in consts],
        out_specs=(row_spec(D_MODEL), row_spec(D_INNER), row_spec(BC_DIM), row_spec(BC_DIM), row_spec(HEAD_LANES),
                   pl.BlockSpec((POOL_HIST, S, POOL_WIDTH), lambda i: (0, i, 0)),
                   pl.BlockSpec((CONV_WIDTH - 1, S, CONV_DIM), lambda i: (0, i, 0))),
        out_shape=(jax.ShapeDtypeStruct((rows, D_MODEL), f32),
                   jax.ShapeDtypeStruct((rows, D_INNER), f32),
                   jax.ShapeDtypeStruct((rows, BC_DIM), f32),
                   jax.ShapeDtypeStruct((rows, BC_DIM), f32),
                   jax.ShapeDtypeStruct((rows, HEAD_LANES), f32),
                   jax.ShapeDtypeStruct((POOL_HIST, nseq, POOL_WIDTH), f32),
                   jax.ShapeDtypeStruct((CONV_WIDTH - 1, nseq, CONV_DIM), f32)),
        scratch_shapes=_window_scratch(N_POOL_BUFS, S, POOL_PAD + T) + _window_scratch(N_CONV_BUFS, S, CONV_PAD + T),
        compiler_params=_params(1),
        name="sample_pre",
    )(x1, pool_hist, conv_hist, *consts)


def _decode_seqs(xs_ref, b_ref, c_ref, dt_ref, h0_ref, alog_ref, dskip_ref, e_ref, y_ref, hn_ref, *, T, n_seqs):
    assert T == SUBLANES
    lane_group = lax.broadcasted_iota(jnp.int32, (T, HEAD_LANES), 1) // HEADS_PER_GROUP
    neg_a = -jnp.exp(alog_ref[...])

    def shift(v, d):
        if d == 0:
            return v
        r = lax.broadcasted_iota(jnp.int32, v.shape, 0)
        return jnp.where(r >= d, pltpu.roll(v, d, axis=0), 0.0)

    def per_seq(s, carry):
        dt = dt_ref[s]
        a = dt * neg_a
        d = 1
        while d < T:
            a = a + shift(a, d)
            d *= 2
        a_end = a[T - 1:T, :]
        x = xs_ref[s]
        bm = b_ref[s]
        cm = c_ref[s]
        terms = []
        for d in range(T):
            cbv = jnp.zeros((T, HEAD_LANES), f32)
            for g in range(N_SSD_GROUPS):
                gc = slice(g * D_STATE, (g + 1) * D_STATE)
                cb = jnp.sum(cm[:, gc] * shift(bm[:, gc], d), axis=-1, keepdims=True)
                cbv = jnp.where(lane_group == g, cb, cbv)
            terms.append(cbv * jnp.exp(a - shift(a, d)) * shift(dt, d))
        terms.append(jnp.exp(a))
        terms.append(jnp.exp(a_end - a) * dt)
        v = jnp.concatenate(terms, axis=0)
        ex = _dot(v.astype(bf16), e_ref[...])

        y = dskip_ref[...] * x
        for d in range(T):
            y = y + ex[d * T:(d + 1) * T] * shift(x, d)
        ea = ex[T * T:(T + 1) * T]
        xw = (x * ex[(T + 1) * T:(T + 2) * T]).astype(bf16)
        cdec = jnp.exp(a_end)
        y_groups = []
        for g in range(N_SSD_GROUPS):
            gc = slice(g * D_STATE, (g + 1) * D_STATE)
            cols = slice(g * GROUP_WIDTH, (g + 1) * GROUP_WIDTH)
            heads = slice(g * HEADS_PER_GROUP, (g + 1) * HEADS_PER_GROUP)
            h0g = h0_ref[0, s, heads].reshape(GROUP_WIDTH, D_STATE)
            ch = lax.dot_general(cm[:, gc].astype(bf16), h0g.astype(bf16), _NT, preferred_element_type=f32)
            y_groups.append(y[:, cols] + ea[:, cols] * ch)
            st = lax.dot_general(xw[:, cols], bm[:, gc].astype(bf16), _TN, preferred_element_type=f32)
            for r in range(HEADS_PER_GROUP):
                h = g * HEADS_PER_GROUP + r
                hn_ref[0, s, h] = (h0_ref[0, s, h] * cdec[:, h:h + 1]
                                   + st[r * SSD_HEAD_DIM:(r + 1) * SSD_HEAD_DIM, :])
        y_ref[s] = jnp.concatenate(y_groups, axis=-1)
        return carry

    lax.fori_loop(0, n_seqs, per_seq, 0, unroll=True)


def _sample_post_kernel(x_ref, y_ref, bp_ref, gmix_ref, wz_ref, wgate_ref, snorm_ref, swo_ref, wo_ref,
                        o_ref, hn_s, yn_s):
    x = x_ref[...]
    hn_s[...] = _rms(x, gmix_ref[...]).astype(bf16)
    o_ref[...] = _gate_merge_out(x, hn_s, y_ref, bp_ref[...], wz_ref, wgate_ref, snorm_ref, swo_ref, wo_ref, yn_s)


def _sample_post(x1, y, bp, w, T):
    rows = x1.shape[0]
    R = SAMPLE_POST_SEQS * T
    assert rows % R == 0
    consts = [w["gmix"], w["wz"], w["wgate"], w["snorm"], w["swo"], w["wo"]]
    row_spec = lambda width: pl.BlockSpec((R, width), lambda i: (i, 0))
    return pl.pallas_call(
        _sample_post_kernel,
        grid=(rows // R,),
        in_specs=[row_spec(D_MODEL), row_spec(D_INNER), row_spec(D_MODEL)] + [_const_spec(c.shape) for c in consts],
        out_specs=row_spec(D_MODEL),
        out_shape=jax.ShapeDtypeStruct((rows, D_MODEL), f32),
        scratch_shapes=[pltpu.VMEM((R, D_MODEL), bf16), pltpu.VMEM((R, D_INNER), bf16)],
        compiler_params=_params(1),
        name="sample_post",
    )(x1, y, bp, *consts)


def _small_params(l, norm_mix, pool_scale, conv_w, conv_b, dt_bias, a_log, d_skip, ssd_norm):
    pad_heads = lambda v: jnp.pad(v, ((0, 0), (0, HEAD_LANES - N_SSD_HEADS)))
    head_of_lane = jnp.arange(D_INNER, dtype=jnp.int32) // SSD_HEAD_DIM
    expand = (jnp.arange(HEAD_LANES, dtype=jnp.int32)[:, None] == head_of_lane[None, :]).astype(bf16)
    return dict(
        gmix=norm_mix[l][None, :], pscale=pool_scale[l][None, :], convw=conv_w[l], convb=conv_b[l][None, :],
        dtb=pad_heads(dt_bias[l][None, :]), alog=pad_heads(a_log[l][None, :]),
        dskip=jnp.repeat(d_skip[l], SSD_HEAD_DIM)[None, :], snorm=ssd_norm[l][None, :], expand=expand)


def kernel(x_prompt, x_sample, state_pool, state_conv, state_ssm, norm_ffn1, ffn1_w_in, ffn1_w_out, norm_mix, w_in,
           pool_w_group, pool_scale, pool_w_out, conv_w, conv_b, dt_bias, a_log, d_skip, ssd_norm, ssd_w_out, w_o,
           norm_ffn2, ffn2_w_in, ffn2_w_out, norm_final):
    depth = w_in.shape[0]
    B, S, _ = x_prompt.shape
    DB, T, _ = x_sample.shape
    gfin = norm_final[None, :]
    xp = x_prompt.reshape(B * S, D_MODEL)
    xs = x_sample.reshape(DB * T, D_MODEL)
    outs = [[] for _ in range(6)]
    for l in range(depth):
        last = l == depth - 1
        g1, g2 = norm_ffn1[l][None, :], norm_ffn2[l][None, :]
        w = _small_params(l, norm_mix, pool_scale, conv_w, conv_b, dt_bias, a_log, d_skip, ssd_norm)
        n_pg = pool_w_group.shape[1]
        xp, xs, in_proj, casts = _ffn_and_casts(
            xp, xs, g1, ffn1_w_in[l], ffn1_w_out[l], gfin, w_in[l].T,
            [pool_w_group[l].reshape(n_pg * POOL_GROUP, POOL_GROUP), pool_w_out[l], ssd_w_out[l], w_o[l],
             ffn2_w_in[l], ffn2_w_out[l]])
        w.update(zip(("wu", "wz", "wxbc", "wdt", "wgate"), in_proj))
        w.update(pgw=casts[0].reshape(n_pg, POOL_GROUP, POOL_GROUP), pwo=casts[1], swo=casts[2], wo=casts[3])
        f2_in, f2_out = casts[4], casts[5]
        hist_major = lambda v: jnp.transpose(v, (1, 0, 2))
        bp, cxs, cb, cc, cdt, spool, sconv = _sample_pre(xs, hist_major(state_pool[l]), hist_major(state_conv[l]), w, T)
        spool, sconv = hist_major(spool)[None], hist_major(sconv)[None]
        xp3, npool, nconv, nssm, y, sssm = _mixer_prompt(xp.reshape(B, S, D_MODEL), w, cxs, cb, cc, cdt,
                                                         state_ssm[l:l + 1], T)
        xs = _sample_post(xs, y.reshape(DB * T, D_INNER), bp, w, T)
        xp, xs = _ffn(xp3.reshape(B * S, D_MODEL), xs, g2, f2_in, f2_out, gfin, final_norm=last)
        for acc, v in zip(outs, (npool, nconv, nssm, spool, sconv, sssm)):
            acc.append(v)
    stack = lambda vs: vs[0] if len(vs) == 1 else jnp.concatenate(vs, axis=0)
    return (xp.reshape(B, S, D_MODEL), xs.reshape(DB, T, D_MODEL),
            stack(outs[0]), stack(outs[1]), stack(outs[2]), stack(outs[3]), stack(outs[4]), stack(outs[5]))
```

```python
import functools

import jax
import jax.numpy as jnp
from jax import lax
from jax.experimental import pallas as pl
from jax.experimental.pallas import tpu as pltpu

f32 = jnp.float32
bf16 = jnp.bfloat16

D_MODEL = 1024
D_FF = 2816
POOL_WINDOWS = (2, 4, 8, 16)
POOL_WIDTH = D_MODEL
POOL_GROUP = POOL_WIDTH // len(POOL_WINDOWS)
POOL_HIST = max(POOL_WINDOWS) - 1
D_INNER = 2 * D_MODEL
SSD_HEAD_DIM = 64
N_SSD_HEADS = D_INNER // SSD_HEAD_DIM
N_SSD_GROUPS = 4
HEADS_PER_GROUP = N_SSD_HEADS // N_SSD_GROUPS
GROUP_WIDTH = D_INNER // N_SSD_GROUPS
D_STATE = 128
CONV_WIDTH = 4
BC_DIM = N_SSD_GROUPS * D_STATE
CONV_DIM = D_INNER + 2 * BC_DIM
CHUNK = 128
PAST_LEN = 16384
EPS = 1e-6
LOG2_E = 1.4426950408889634

LANES = 128
SUBLANES = 8
MXU_COLS = 256
VMEM_LIMIT_BYTES = 56 * 1024 * 1024
LARGE_VMEM_LIMIT_BYTES = 60 * 1024 * 1024

FFN_ROWS = 512
FFN_CHUNK = 256
FFN_STAGE_COLS = 128
FFN_STAGE_SLOTS = 4
PROMPT_TILE = 256
SAMPLE_SEQS = 32
POOL_PAD = 16
CONV_PAD = 8
HEAD_LANES = LANES
SLABS_PER_BUF = MXU_COLS // LANES
N_POOL_BUFS = POOL_WIDTH // MXU_COLS
N_CONV_BUFS = CONV_DIM // MXU_COLS

_NT = (((1,), (1,)), ((), ()))
_TN = (((0,), (0,)), ((), ()))


def _rms(x, g):
    return x * lax.rsqrt(jnp.mean(x * x, axis=-1, keepdims=True) + EPS) * g


def _sigmoid(v):
    return 0.5 * jnp.tanh(0.5 * v) + 0.5


def _silu(v):
    h = 0.5 * v
    return h * jnp.tanh(h) + h


def _softplus(v):
    return jnp.maximum(v, 0.0) + jnp.log1p(jnp.exp(-jnp.abs(v)))


def _dot(a, b):
    return jnp.dot(a, b, preferred_element_type=f32)


def _split3(v):
    hi = v.astype(bf16)
    r1 = v - hi.astype(f32)
    mid = r1.astype(bf16)
    lo = (r1 - mid.astype(f32)).astype(bf16)
    return hi, mid, lo


def _const_spec(shape):
    nd = len(shape)
    return pl.BlockSpec(shape, lambda *_: (0,) * nd, pipeline_mode=pl.Buffered(1))


def _params(n_grid, vmem_limit_bytes=VMEM_LIMIT_BYTES):
    return pltpu.CompilerParams(dimension_semantics=("arbitrary",) * n_grid,
                                vmem_limit_bytes=vmem_limit_bytes)


def _ffn_kernel(xa_ref, xb_ref, g_ref, win_ref, wout_ref, gfin_ref, oa_ref, ob_ref, hn_s, act_s, *, n_a, final_norm,
                side_work=None, fetch=None):
    def tile(x_ref, o_ref, extra=None, fetch=None):
        commit = extra() if extra is not None else None
        hn_s[...] = _rms(x_ref[...], g_ref[...]).astype(bf16)
        for c in range(D_FF // FFN_CHUNK):
            lo = c * FFN_CHUNK
            if fetch is not None:
                fetch(c)
            gate = _dot(hn_s[...], win_ref[:, lo:lo + FFN_CHUNK])
            up = _dot(hn_s[...], win_ref[:, D_FF + lo:D_FF + lo + FFN_CHUNK])
            act_s[:, lo:lo + FFN_CHUNK] = (_silu(gate) * up).astype(bf16)
        out = x_ref[...] + 0.5 * _dot(act_s[...], wout_ref[...])
        if final_norm:
            out = _rms(out, gfin_ref[...])
        o_ref[...] = out
        if commit is not None:
            commit()

    i = pl.program_id(0)
    on_a = i < n_a
    if fetch is None:
        pl.when(on_a)(lambda: tile(xa_ref, oa_ref, side_work))
    else:
        pl.when(i == 0)(lambda: tile(xa_ref, oa_ref, side_work, fetch))
        pl.when(jnp.logical_and(i > 0, on_a))(lambda: tile(xa_ref, oa_ref, side_work))
    pl.when(jnp.logical_not(on_a))(lambda: tile(xb_ref, ob_ref))


def _ffn_specs(rows_a, rows_b):
    assert rows_a % FFN_ROWS == 0 and rows_b % FFN_ROWS == 0
    n_a, n_b = rows_a // FFN_ROWS, rows_b // FFN_ROWS
    a_spec = pl.BlockSpec((FFN_ROWS, D_MODEL), lambda i: (jnp.minimum(i, n_a - 1), 0))
    b_spec = pl.BlockSpec((FFN_ROWS, D_MODEL), lambda i: (jnp.maximum(i - n_a, 0), 0))
    in_specs = [a_spec, b_spec,
                _const_spec((1, D_MODEL)),
                _const_spec((D_MODEL, 2 * D_FF)),
                _const_spec((D_FF, D_MODEL)),
                _const_spec((1, D_MODEL))]
    out_shapes = [jax.ShapeDtypeStruct((rows_a, D_MODEL), f32), jax.ShapeDtypeStruct((rows_b, D_MODEL), f32)]
    scratch = [pltpu.VMEM((FFN_ROWS, D_MODEL), bf16), pltpu.VMEM((FFN_ROWS, D_FF), bf16)]
    return n_a, n_b, in_specs, [a_spec, b_spec], out_shapes, scratch


def _ffn(xa, xb, g, w_in_b, w_out_b, gfin, *, final_norm):
    n_a, n_b, in_specs, out_specs, out_shapes, scratch = _ffn_specs(xa.shape[0], xb.shape[0])
    return pl.pallas_call(
        functools.partial(_ffn_kernel, n_a=n_a, final_norm=final_norm),
        grid=(n_a + n_b,),
        in_specs=in_specs, out_specs=tuple(out_specs), out_shape=tuple(out_shapes), scratch_shapes=scratch,
        compiler_params=_params(1),
        name="ffn_final" if final_norm else "ffn",
    )(xa, xb, g, w_in_b, w_out_b, gfin)


_IN_PROJ_SPLITS = (POOL_WIDTH, POOL_WIDTH + D_INNER, POOL_WIDTH + D_INNER + CONV_DIM,
                   POOL_WIDTH + D_INNER + CONV_DIM + N_SSD_HEADS)
BF16_SUBLANES = 2 * SUBLANES


def _cast_block(n_rows, n_steps):
    period = 1
    while (n_rows * period) % n_steps or ((n_rows * period) // n_steps) % BF16_SUBLANES:
        period *= 2
        assert period <= n_steps
    return (n_rows * period) // n_steps, period


_PIECE_TILES = (POOL_WIDTH // MXU_COLS, D_INNER // MXU_COLS, CONV_DIM // MXU_COLS, 2 * D_MODEL // MXU_COLS)
_GATE_SKEW = _IN_PROJ_SPLITS[3] - _IN_PROJ_SPLITS[2]


def _store_when(cond, ref, val):
    @pl.when(cond)
    def _():
        ref[...] = val


def _ffn_cast_kernel(xa_ref, xb_ref, g_ref, win_hbm, wout_hbm, gfin_ref, wt_ref, wdt_f, wg_hi, *rest,
                     n_a, n_plain):
    plain_f = rest[:n_plain]
    oa_ref, ob_ref = rest[n_plain:n_plain + 2]
    wu_o, wz_o, wxbc_o, wdt_o, wgate_o = rest[n_plain + 2:n_plain + 7]
    plain_o = rest[n_plain + 7:2 * n_plain + 7]
    hn_s, act_s, win_s, wout_s, stage_in, stage_out, sems = rest[2 * n_plain + 7:]
    i = pl.program_id(0)
    n_u, n_z, n_x, n_g = _PIECE_TILES
    front = n_u + n_z + n_x

    def stage_copies(k):
        slot, lo = k % FFN_STAGE_SLOTS, k * FFN_STAGE_COLS
        return (pltpu.make_async_copy(win_hbm.at[:, pl.ds(lo, FFN_STAGE_COLS)], stage_in.at[slot, 0],
                                      sems.at[slot, 0]),
                pltpu.make_async_copy(win_hbm.at[:, pl.ds(D_FF + lo, FFN_STAGE_COLS)], stage_in.at[slot, 1],
                                      sems.at[slot, 1]),
                pltpu.make_async_copy(wout_hbm.at[pl.ds(lo, FFN_STAGE_COLS), :], stage_out.at[slot],
                                      sems.at[slot, 2]))

    def fetch(c):
        per_chunk = FFN_CHUNK // FFN_STAGE_COLS
        for k in range(c * per_chunk, (c + 1) * per_chunk):
            slot, lo = k % FFN_STAGE_SLOTS, k * FFN_STAGE_COLS
            for nxt in (range(FFN_STAGE_SLOTS) if k == 0 else [k + FFN_STAGE_SLOTS - 1]):
                if nxt < D_FF // FFN_STAGE_COLS:
                    for cp in stage_copies(nxt):
                        cp.start()
            for cp in stage_copies(k):
                cp.wait()
            win_s[:, lo:lo + FFN_STAGE_COLS] = stage_in[slot, 0].astype(bf16)
            win_s[:, D_FF + lo:D_FF + lo + FFN_STAGE_COLS] = stage_in[slot, 1].astype(bf16)
            wout_s[lo:lo + FFN_STAGE_COLS, :] = stage_out[slot].astype(bf16)

    def casts():
        t = wt_ref[...].T.astype(bf16)
        rows = jnp.concatenate([wt_ref[_GATE_SKEW:, :], wg_hi[0:_GATE_SKEW, :]], axis=0)
        tg = rows.T.astype(bf16)
        for src, dst in zip(plain_f, plain_o):
            dst[...] = src[...].astype(bf16)

        def commit():
            _store_when(i < n_u, wu_o, t)
            _store_when(jnp.logical_and(i >= n_u, i < n_u + n_z), wz_o, t)
            _store_when(jnp.logical_and(i >= n_u + n_z, i < front), wxbc_o, t)
            _store_when(jnp.logical_and(i >= front, i < front + n_g), wgate_o, tg)
        return commit

    _ffn_kernel(xa_ref, xb_ref, g_ref, win_s, wout_s, gfin_ref, oa_ref, ob_ref, hn_s, act_s,
                n_a=n_a, final_norm=False, side_work=casts, fetch=fetch)

    @pl.when(i == 0)
    def _():
        t = wdt_f[...].T
        lane = lax.broadcasted_iota(jnp.int32, t.shape, 1)
        wdt_o[...] = jnp.where(lane < N_SSD_HEADS, t, 0.0).astype(bf16)


def _ffn_and_casts(xa, xb, g, w_in_f, w_out_f, gfin, w_in_t, plain):
    n_a, n_b, in_specs, out_specs, out_shapes, scratch = _ffn_specs(xa.shape[0], xb.shape[0])
    in_specs[3] = in_specs[4] = pl.BlockSpec(memory_space=pl.ANY)
    scratch = scratch + [pltpu.VMEM((D_MODEL, 2 * D_FF), bf16), pltpu.VMEM((D_FF, D_MODEL), bf16),
                         pltpu.VMEM((FFN_STAGE_SLOTS, 2, D_MODEL, FFN_STAGE_COLS), f32),
                         pltpu.VMEM((FFN_STAGE_SLOTS, FFN_STAGE_COLS, D_MODEL), f32),
                         pltpu.SemaphoreType.DMA((FFN_STAGE_SLOTS, 3))]
    n_u, n_z, n_x, n_g = _PIECE_TILES
    front = n_u + n_z + n_x
    assert n_a >= front + n_g and _IN_PROJ_SPLITS[2] == front * MXU_COLS and _IN_PROJ_SPLITS[2] % HEAD_LANES == 0

    def blocked(arr):
        blk, period = _cast_block(arr.shape[0], n_a)
        return (pl.BlockSpec((blk, arr.shape[1]), lambda i: (jnp.minimum(i, n_a - 1) // period, 0)),
                jax.ShapeDtypeStruct(arr.shape, bf16))

    def piece(first_step, n_tiles):
        return (pl.BlockSpec((D_MODEL, MXU_COLS), lambda i: (0, jnp.clip(i - first_step, 0, n_tiles - 1))),
                jax.ShapeDtypeStruct((D_MODEL, n_tiles * MXU_COLS), bf16))

    pieces = [piece(0, n_u), piece(n_u, n_z), piece(n_u + n_z, n_x),
              (pl.BlockSpec((D_MODEL, HEAD_LANES), lambda i: (0, 0)), jax.ShapeDtypeStruct((D_MODEL, HEAD_LANES), bf16)),
              piece(front, n_g)]
    in_proj_specs = [pl.BlockSpec((MXU_COLS, D_MODEL), lambda i: (jnp.minimum(i, front + n_g - 1), 0)),
                     pl.BlockSpec((HEAD_LANES, D_MODEL), lambda i: (_IN_PROJ_SPLITS[2] // HEAD_LANES, 0)),
                     pl.BlockSpec((MXU_COLS, D_MODEL), lambda i: (front + 1 + jnp.clip(i - front, 0, n_g - 1), 0))]
    plains = [blocked(p) for p in plain]
    res = pl.pallas_call(
        functools.partial(_ffn_cast_kernel, n_a=n_a, n_plain=len(plain)),
        grid=(n_a + n_b,),
        in_specs=in_specs + in_proj_specs + [s for s, _ in plains],
        out_specs=tuple(out_specs + [s for s, _ in pieces] + [s for s, _ in plains]),
        out_shape=tuple(out_shapes + [o for _, o in pieces] + [o for _, o in plains]),
        scratch_shapes=scratch,
        compiler_params=_params(1, LARGE_VMEM_LIMIT_BYTES),
        name="ffn_casts",
    )(xa, xb, g, w_in_f, w_out_f, gfin, w_in_t, w_in_t, w_in_t, *plain)
    return res[0], res[1], res[2:7], res[7:]


def _slab(bufs, j):
    return bufs[j // SLABS_PER_BUF], j % SLABS_PER_BUF


def _window_scratch(n_bufs, n_seqs, n_rows):
    return [pltpu.VMEM((SLABS_PER_BUF, n_seqs, n_rows, LANES), f32) for _ in range(n_bufs)]


def _store_rows(bufs, row0, val, T, slab0=0):
    for j in range(val.shape[1] // LANES):
        ref, k = _slab(bufs, slab0 + j)
        S = ref.shape[1]
        blk = val[:, j * LANES:(j + 1) * LANES]
        if S == 1:
            ref[k, 0, row0:row0 + T, :] = blk
        else:
            ref[k, :, row0:row0 + T, :] = blk.reshape(S, T, LANES)


def _window(bufs, j, row0, n):
    ref, k = _slab(bufs, j)
    S = ref.shape[1]
    rows = pl.ds(row0, n) if row0 % SUBLANES == 0 else pl.ds(row0, n, stride=1)
    if S == 1:
        return ref[k, 0, rows, :]
    v = ref[k, :, rows, :]
    return v.reshape(S * n, LANES) if n % SUBLANES == 0 else v


def _pool_branch(hn_s, wu_ref, ubufs, T, pos0, pgw_ref, pscale_ref, pwo_ref):
    S = ubufs[0].shape[1]
    assert T & (T - 1) == 0 and POOL_GROUP == MXU_COLS
    t_idx = jnp.bitwise_and(lax.broadcasted_iota(jnp.int32, (S * T, 1), 0), T - 1)
    n_seen = pos0 + 1 + t_idx
    _store_rows(ubufs, POOL_PAD, _dot(hn_s[...], wu_ref[...]), T)
    mixed = []
    for gi, w in enumerate(POOL_WINDOWS):
        inv_cnt = 1.0 / jnp.minimum(n_seen, w).astype(f32)
        d = []
        for j in range(gi * SLABS_PER_BUF, (gi + 1) * SLABS_PER_BUF):
            cur = _window(ubufs, j, POOL_PAD, T)
            s = cur
            for k in range(1, w):
                s = s + _window(ubufs, j, POOL_PAD - k, T)
            d.append((s * inv_cnt - cur).astype(bf16))
        mixed.append(_dot(jnp.concatenate(d, axis=-1), pgw_ref[gi]))
    y = jnp.concatenate(mixed, axis=-1) * pscale_ref[...]
    return _dot(y.astype(bf16), pwo_ref[...])


def _conv_slab(xbufs, T, convw_ref, convb_ref, j, xs_ref, b_ref, c_ref):
    c0 = j * LANES
    cols = slice(c0, c0 + LANES)
    h = 0.5 * convb_ref[:, cols]
    for k in range(CONV_WIDTH):
        h = h + _window(xbufs, j, CONV_PAD - (CONV_WIDTH - 1) + k, T) * (0.5 * convw_ref[k:k + 1, cols])
    v = h * jnp.tanh(h) + h
    if c0 < D_INNER:
        xs_ref[:, cols] = v
    elif c0 < D_INNER + BC_DIM:
        b_ref[:, c0 - D_INNER:c0 - D_INNER + LANES] = v
    else:
        c_ref[:, c0 - D_INNER - BC_DIM:c0 - D_INNER - BC_DIM + LANES] = v


def _project_conv(hn_s, wxbc_ref, xbufs, T, convw_ref, convb_ref, xs_ref, b_ref, c_ref):
    for jt in range(N_CONV_BUFS):
        slab0 = jt * SLABS_PER_BUF
        _store_rows(xbufs, CONV_PAD, _dot(hn_s[...], wxbc_ref[:, jt * MXU_COLS:(jt + 1) * MXU_COLS]), T, slab0=slab0)
        for j in range(slab0, slab0 + SLABS_PER_BUF):
            _conv_slab(xbufs, T, convw_ref, convb_ref, j, xs_ref, b_ref, c_ref)


def _gate_merge_out(x, hn_s, y_ref, bp, wz_ref, wgate_ref, snorm_ref, swo_ref, wo_ref, yn_s):
    for g in range(N_SSD_GROUPS):
        cols = slice(g * GROUP_WIDTH, (g + 1) * GROUP_WIDTH)
        z = _dot(hn_s[...], wz_ref[:, cols])
        yg = y_ref[:, cols] * _silu(z)
        yg = yg * lax.rsqrt(jnp.mean(yg * yg, axis=-1, keepdims=True) + EPS)
        yn_s[:, cols] = (yg * snorm_ref[:, cols]).astype(bf16)
    branch_ssd = _dot(yn_s[...], swo_ref[...])
    gate_pool = _sigmoid(_dot(hn_s[...], wgate_ref[:, 0:D_MODEL]))
    gate_ssd = _sigmoid(_dot(hn_s[...], wgate_ref[:, D_MODEL:2 * D_MODEL]))
    merged = (gate_pool * bp + gate_ssd * branch_ssd).astype(bf16)
    return x + _dot(merged, wo_ref[...])


def _ssd_chunk(rows, xs_s, b_s, c_s, dt_s, y_s, ht_s, alog_ref, dskip_ref, e_ref):
    L = CHUNK
    ri = lax.broadcasted_iota(jnp.int32, (L, L), 0)
    ci = lax.broadcasted_iota(jnp.int32, (L, L), 1)
    causal = ri >= ci
    tril = jnp.where(causal, 1.0, 0.0).astype(bf16)
    first_head = lax.broadcasted_iota(jnp.int32, (L, LANES), 1) < SSD_HEAD_DIM
    keep_first = jnp.where(first_head, 1.0, 0.0).astype(bf16)
    keep_second = jnp.where(first_head, 0.0, 1.0).astype(bf16)

    dt = dt_s[rows, :]
    dA = dt * (-jnp.exp(alog_ref[...]))
    hi, mid, lo = _split3(dA)
    acc = _dot(tril, jnp.concatenate([hi, mid, lo], axis=1))
    a = (acc[:, 0:LANES] + acc[:, LANES:2 * LANES] + acc[:, 2 * LANES:3 * LANES]) * LOG2_E
    aT = a.T[0:N_SSD_HEADS]
    dtT = dt.T[0:N_SSD_HEADS]
    wT = jnp.exp2(aT[:, L - 1:L] - aT) * dtT
    srcT = aT - jnp.log2(dtT)
    end_decay = jnp.broadcast_to(jnp.exp2(a[L - 1:L, :]), (2 * SUBLANES, HEAD_LANES))
    e_hi, e_mid, e_lo = (t.astype(f32) for t in _split3(end_decay))
    sel = lax.broadcasted_iota(jnp.int32, (2 * SUBLANES, HEAD_LANES), 0)
    stacked = jnp.where(sel == 0, e_hi, jnp.where(sel == 1, e_mid, jnp.where(sel == 2, e_lo, 0.0)))
    cdec = jnp.sum(_dot(stacked.astype(bf16), e_ref[...]), axis=0, keepdims=True)

    def head_mats(h, cb, bT):
        acol = jnp.broadcast_to(a[:, h:h + 1], (L, L))
        m = (cb * jnp.exp2(jnp.where(causal, acol - srcT[h:h + 1, :], -jnp.inf))).astype(bf16)
        bw = (bT * wT[h:h + 1, :]).astype(bf16)
        return m, bw, acol

    pairs_per_group = HEADS_PER_GROUP // 2
    for q in range(N_SSD_HEADS // 2):
        g, qg = divmod(q, pairs_per_group)
        if qg == 0:
            gcols = slice(g * D_STATE, (g + 1) * D_STATE)
            b_f = b_s[rows, gcols]
            c_b = c_s[rows, gcols].astype(bf16)
            cb = lax.dot_general(c_b, b_f.astype(bf16), _NT, preferred_element_type=f32)
            bT = b_f.T
            ch = _dot(c_b, ht_s[:, g * GROUP_WIDTH:(g + 1) * GROUP_WIDTH].astype(bf16))
        cols = slice(q * LANES, (q + 1) * LANES)
        xq = xs_s[rows, cols]
        xq_b = xq.astype(bf16)
        x2 = jnp.concatenate([xq_b * keep_first, xq_b * keep_second], axis=0)
        m_a, bw_a, acol_a = head_mats(2 * q, cb, bT)
        m_b, bw_b, acol_b = head_mats(2 * q + 1, cb, bT)
        y_off = ch[:, qg * LANES:(qg + 1) * LANES] * jnp.exp2(jnp.where(first_head, acol_a, acol_b))
        y_s[rows, cols] = _dot(jnp.concatenate([m_a, m_b], axis=1), x2) + y_off + dskip_ref[:, cols] * xq
        st = _dot(jnp.concatenate([bw_a, bw_b], axis=1), x2)
        ht_s[:, cols] = ht_s[:, cols] * cdec[:, cols] + st


def _mixer_prompt_kernel(x_ref, dx_ref, dbp_ref, dxs_ref, db_ref, dc_ref, ddt_ref, dh0_ref,
                         gmix_ref, wu_ref, wz_ref, wxbc_ref, wdt_ref, wgate_ref, pgw_ref, pscale_ref,
                         pwo_ref, convw_ref, convb_ref, dtb_ref, alog_ref, dskip_ref, snorm_ref, swo_ref,
                         wo_ref, e_ref,
                         o_ref, npool_ref, nconv_ref, nssm_ref, do_ref, dhn_ref,
                         hn_all, xs_s, b_s, c_s, dt_s, y_s, ht_s, yn_s, *window_bufs, decode_T):
    T = PROMPT_TILE
    ti = pl.program_id(1)
    ubufs, xbufs = window_bufs[:N_POOL_BUFS], window_bufs[N_POOL_BUFS:]
    hn_s = hn_all.at[0:T]

    @pl.when(ti == 0)
    def _():
        for ref in ubufs:
            ref[:, :, 0:POOL_PAD, :] = jnp.zeros((SLABS_PER_BUF, 1, POOL_PAD, LANES), f32)
        for ref in xbufs:
            ref[:, :, 0:CONV_PAD, :] = jnp.zeros((SLABS_PER_BUF, 1, CONV_PAD, LANES), f32)
        ht_s[...] = jnp.zeros(ht_s.shape, f32)

    x = x_ref[0]
    hn_s[...] = _rms(x, gmix_ref[...]).astype(bf16)

    bp = _pool_branch(hn_s, wu_ref, ubufs, T, ti * T, pgw_ref, pscale_ref, pwo_ref)

    _project_conv(hn_s, wxbc_ref, xbufs, T, convw_ref, convb_ref, xs_s, b_s, c_s)
    dt_s[...] = _softplus(_dot(hn_s[...], wdt_ref[...]) + dtb_ref[...])

    def chunk(c, carry):
        rows = pl.ds(pl.multiple_of(c * CHUNK, CHUNK), CHUNK)
        _ssd_chunk(rows, xs_s, b_s, c_s, dt_s, y_s, ht_s, alog_ref, dskip_ref, e_ref)
        return carry

    lax.fori_loop(0, T // CHUNK, chunk, 0)

    _decode_seqs(dxs_ref, db_ref, dc_ref, ddt_ref, dh0_ref, alog_ref, dskip_ref, e_ref, y_s, dhn_ref,
                 T=decode_T, n_seqs=dxs_ref.shape[0], y_row0=T)
    dx = dx_ref[...]
    hn_all[T:, :] = _rms(dx, gmix_ref[...]).astype(bf16)
    out = _gate_merge_out(jnp.concatenate([x, dx], axis=0), hn_all, y_s,
                          jnp.concatenate([bp, dbp_ref[...]], axis=0),
                          wz_ref, wgate_ref, snorm_ref, swo_ref, wo_ref, yn_s)
    o_ref[0] = out[0:T]
    do_ref[...] = out[T:]

    @pl.when(ti == pl.num_programs(1) - 1)
    def _():
        for j in range(POOL_WIDTH // LANES):
            npool_ref[0, 0, :, j * LANES:(j + 1) * LANES] = _window(ubufs, j, POOL_PAD + T - POOL_HIST, POOL_HIST)
        for j in range(CONV_DIM // LANES):
            nconv_ref[0, 0, :, j * LANES:(j + 1) * LANES] = _window(
                xbufs, j, CONV_PAD + T - (CONV_WIDTH - 1), CONV_WIDTH - 1)
        for g in range(N_SSD_GROUPS):
            hg = ht_s[:, g * GROUP_WIDTH:(g + 1) * GROUP_WIDTH].T
            nssm_ref[0, 0, g * HEADS_PER_GROUP:(g + 1) * HEADS_PER_GROUP] = hg.reshape(
                HEADS_PER_GROUP, SSD_HEAD_DIM, D_STATE)

    for ref in ubufs:
        ref[:, :, 0:POOL_PAD, :] = ref[:, :, T:T + POOL_PAD, :]
    for ref in xbufs:
        ref[:, :, 0:CONV_PAD, :] = ref[:, :, T:T + CONV_PAD, :]


def _mixer_prompt(x1, w, dx1, dbp, dxs, dbm, dcm, ddt, dh0, decode_T):
    B, S, _ = x1.shape
    T = PROMPT_TILE
    assert S % T == 0 and T % CHUNK == 0 and T >= POOL_PAD
    n_tiles = S // T
    nseq = dxs.shape[0] // decode_T
    assert nseq % (B * n_tiles) == 0
    Q = nseq // (B * n_tiles)
    seq_spec = lambda width: pl.BlockSpec((Q, decode_T, width), lambda b, t: (b * n_tiles + t, 0, 0))
    state_spec = pl.BlockSpec((1, Q, N_SSD_HEADS, SSD_HEAD_DIM, D_STATE), lambda b, t: (0, b * n_tiles + t, 0, 0, 0))
    R = Q * decode_T
    assert R % BF16_SUBLANES == 0
    row_spec = pl.BlockSpec((R, D_MODEL), lambda b, t: (b * n_tiles + t, 0))
    as_seqs = lambda v: v.reshape(nseq, decode_T, v.shape[1])
    consts = [w["gmix"], w["wu"], w["wz"], w["wxbc"], w["wdt"], w["wgate"], w["pgw"], w["pscale"], w["pwo"],
              w["convw"], w["convb"], w["dtb"], w["alog"], w["dskip"], w["snorm"], w["swo"], w["wo"], w["expand"]]
    return pl.pallas_call(
        functools.partial(_mixer_prompt_kernel, decode_T=decode_T),
        grid=(B, n_tiles),
        in_specs=[pl.BlockSpec((1, T, D_MODEL), lambda b, t: (b, t, 0)), row_spec, row_spec,
                  seq_spec(D_INNER), seq_spec(BC_DIM), seq_spec(BC_DIM), seq_spec(HEAD_LANES), state_spec]
                 + [_const_spec(c.shape) for c in consts],
        out_specs=(pl.BlockSpec((1, T, D_MODEL), lambda b, t: (b, t, 0)),
                   pl.BlockSpec((1, 1, POOL_HIST, POOL_WIDTH), lambda b, t: (0, b, 0, 0)),
                   pl.BlockSpec((1, 1, CONV_WIDTH - 1, CONV_DIM), lambda b, t: (0, b, 0, 0)),
                   pl.BlockSpec((1, 1, N_SSD_HEADS, SSD_HEAD_DIM, D_STATE), lambda b, t: (0, b, 0, 0, 0)),
                   row_spec, state_spec),
        out_shape=(jax.ShapeDtypeStruct((B, S, D_MODEL), f32),
                   jax.ShapeDtypeStruct((1, B, POOL_HIST, POOL_WIDTH), f32),
                   jax.ShapeDtypeStruct((1, B, CONV_WIDTH - 1, CONV_DIM), f32),
                   jax.ShapeDtypeStruct((1, B, N_SSD_HEADS, SSD_HEAD_DIM, D_STATE), f32),
                   jax.ShapeDtypeStruct(dx1.shape, f32),
                   jax.ShapeDtypeStruct(dh0.shape, f32)),
        scratch_shapes=[pltpu.VMEM((T + R, D_MODEL), bf16),
                        pltpu.VMEM((T, D_INNER), f32),
                        pltpu.VMEM((T, BC_DIM), f32),
                        pltpu.VMEM((T, BC_DIM), f32),
                        pltpu.VMEM((T, HEAD_LANES), f32),
                        pltpu.VMEM((T + R, D_INNER), f32),
                        pltpu.VMEM((D_STATE, D_INNER), f32),
                        pltpu.VMEM((T + R, D_INNER), bf16)]
                       + _window_scratch(N_POOL_BUFS, 1, POOL_PAD + T)
                       + _window_scratch(N_CONV_BUFS, 1, CONV_PAD + T),
        compiler_params=_params(2, LARGE_VMEM_LIMIT_BYTES),
        name="mixer_prompt",
    )(x1, dx1, dbp, as_seqs(dxs), as_seqs(dbm), as_seqs(dcm), as_seqs(ddt), dh0, *consts)


def _load_history(bufs, hist_ref, row0):
    for j in range(hist_ref.shape[2] // LANES):
        ref, k = _slab(bufs, j)
        n_slabs, S, n_rows, _ = ref.shape
        flat = ref.reshape(n_slabs, S * n_rows, LANES)
        for r in range(hist_ref.shape[0]):
            flat[k, pl.ds(row0 + r, S, stride=n_rows), :] = hist_ref[r, :, j * LANES:(j + 1) * LANES]


def _store_history(hist_ref, bufs, row0):
    for j in range(hist_ref.shape[2] // LANES):
        ref, k = _slab(bufs, j)
        n_slabs, S, n_rows, _ = ref.shape
        flat = ref.reshape(n_slabs, S * n_rows, LANES)
        for r in range(hist_ref.shape[0]):
            hist_ref[r, :, j * LANES:(j + 1) * LANES] = flat[k, pl.ds(row0 + r, S, stride=n_rows), :]


def _sample_pre_kernel(x_ref, ph_ref, ch_ref, gmix_ref, wu_ref, wxbc_ref, wdt_ref, pgw_ref, pscale_ref, pwo_ref,
                       convw_ref, convb_ref, dtb_ref,
                       bp_ref, xs_ref, b_ref, c_ref, dt_ref, npool_ref, nconv_ref,
                       *window_bufs, T):
    ubufs, xbufs = window_bufs[:N_POOL_BUFS], window_bufs[N_POOL_BUFS:]
    hn = _rms(x_ref[...], gmix_ref[...]).astype(bf16)

    _load_history(ubufs, ph_ref, POOL_PAD - POOL_HIST)
    bp_ref[...] = _pool_branch(hn, wu_ref, ubufs, T, PAST_LEN, pgw_ref, pscale_ref, pwo_ref)
    _store_history(npool_ref, ubufs, POOL_PAD + T - POOL_HIST)

    _load_history(xbufs, ch_ref, CONV_PAD - (CONV_WIDTH - 1))
    _project_conv(hn, wxbc_ref, xbufs, T, convw_ref, convb_ref, xs_ref, b_ref, c_ref)
    _store_history(nconv_ref, xbufs, CONV_PAD + T - (CONV_WIDTH - 1))
    dt_ref[...] = _softplus(_dot(hn, wdt_ref[...]) + dtb_ref[...])


def _sample_pre(x1, pool_hist, conv_hist, w, T):
    rows = x1.shape[0]
    nseq = rows // T
    S = SAMPLE_SEQS
    R = S * T
    assert nseq % S == 0 and T % SUBLANES == 0
    consts = [w["gmix"], w["wu"], w["wxbc"], w["wdt"], w["pgw"], w["pscale"], w["pwo"], w["convw"], w["convb"],
              w["dtb"]]
    row_spec = lambda width: pl.BlockSpec((R, width), lambda i: (i, 0))
    return pl.pallas_call(
        functools.partial(_sample_pre_kernel, T=T),
        grid=(nseq // S,),
        in_specs=[row_spec(D_MODEL),
                  pl.BlockSpec((POOL_HIST, S, POOL_WIDTH), lambda i: (0, i, 0)),
                  pl.BlockSpec((CONV_WIDTH - 1, S, CONV_DIM), lambda i: (0, i, 0))]
                 + [_const_spec(c.shape) for c in consts],
        out_specs=(row_spec(D_MODEL), row_spec(D_INNER), row_spec(BC_DIM), row_spec(BC_DIM), row_spec(HEAD_LANES),
                   pl.BlockSpec((POOL_HIST, S, POOL_WIDTH), lambda i: (0, i, 0)),
                   pl.BlockSpec((CONV_WIDTH - 1, S, CONV_DIM), lambda i: (0, i, 0))),
        out_shape=(jax.ShapeDtypeStruct((rows, D_MODEL), f32),
                   jax.ShapeDtypeStruct((rows, D_INNER), f32),
                   jax.ShapeDtypeStruct((rows, BC_DIM), f32),
                   jax.ShapeDtypeStruct((rows, BC_DIM), f32),
                   jax.ShapeDtypeStruct((rows, HEAD_LANES), f32),
                   jax.ShapeDtypeStruct((POOL_HIST, nseq, POOL_WIDTH), f32),
                   jax.ShapeDtypeStruct((CONV_WIDTH - 1, nseq, CONV_DIM), f32)),
        scratch_shapes=_window_scratch(N_POOL_BUFS, S, POOL_PAD + T) + _window_scratch(N_CONV_BUFS, S, CONV_PAD + T),
        compiler_params=_params(1),
        name="sample_pre",
    )(x1, pool_hist, conv_hist, *consts)


def _decode_seqs(xs_ref, b_ref, c_ref, dt_ref, h0_ref, alog_ref, dskip_ref, e_ref, y_ref, hn_ref, *, T, n_seqs,
                 y_row0):
    assert T == SUBLANES
    lane_group = lax.broadcasted_iota(jnp.int32, (T, HEAD_LANES), 1) // HEADS_PER_GROUP
    neg_a = -jnp.exp(alog_ref[...])

    def shift(v, d):
        if d == 0:
            return v
        r = lax.broadcasted_iota(jnp.int32, v.shape, 0)
        return jnp.where(r >= d, pltpu.roll(v, d, axis=0), 0.0)

    def per_seq(s, carry):
        dt = dt_ref[s]
        a = dt * neg_a
        d = 1
        while d < T:
            a = a + shift(a, d)
            d *= 2
        a_end = a[T - 1:T, :]
        x = xs_ref[s]
        bm = b_ref[s]
        cm = c_ref[s]
        terms = []
        for d in range(T):
            cbv = jnp.zeros((T, HEAD_LANES), f32)
            for g in range(N_SSD_GROUPS):
                gc = slice(g * D_STATE, (g + 1) * D_STATE)
                cb = jnp.sum(cm[:, gc] * shift(bm[:, gc], d), axis=-1, keepdims=True)
                cbv = jnp.where(lane_group == g, cb, cbv)
            terms.append(cbv * jnp.exp(a - shift(a, d)) * shift(dt, d))
        terms.append(jnp.exp(a))
        terms.append(jnp.exp(a_end - a) * dt)
        v = jnp.concatenate(terms, axis=0)
        ex = _dot(v.astype(bf16), e_ref[...])

        y = dskip_ref[...] * x
        for d in range(T):
            y = y + ex[d * T:(d + 1) * T] * shift(x, d)
        ea = ex[T * T:(T + 1) * T]
        xw = (x * ex[(T + 1) * T:(T + 2) * T]).astype(bf16)
        cdec = jnp.exp(a_end)
        y_groups = []
        for g in range(N_SSD_GROUPS):
            gc = slice(g * D_STATE, (g + 1) * D_STATE)
            cols = slice(g * GROUP_WIDTH, (g + 1) * GROUP_WIDTH)
            heads = slice(g * HEADS_PER_GROUP, (g + 1) * HEADS_PER_GROUP)
            h0g = h0_ref[0, s, heads].reshape(GROUP_WIDTH, D_STATE)
            ch = lax.dot_general(cm[:, gc].astype(bf16), h0g.astype(bf16), _NT, preferred_element_type=f32)
            y_groups.append(y[:, cols] + ea[:, cols] * ch)
            st = lax.dot_general(xw[:, cols], bm[:, gc].astype(bf16), _TN, preferred_element_type=f32)
            for r in range(HEADS_PER_GROUP):
                h = g * HEADS_PER_GROUP + r
                hn_ref[0, s, h] = (h0_ref[0, s, h] * cdec[:, h:h + 1]
                                   + st[r * SSD_HEAD_DIM:(r + 1) * SSD_HEAD_DIM, :])
        y_ref[pl.ds(pl.multiple_of(y_row0 + s * T, T), T), :] = jnp.concatenate(y_groups, axis=-1)
        return carry

    lax.fori_loop(0, n_seqs, per_seq, 0, unroll=True)


def _small_params(l, norm_mix, pool_scale, conv_w, conv_b, dt_bias, a_log, d_skip, ssd_norm):
    pad_heads = lambda v: jnp.pad(v, ((0, 0), (0, HEAD_LANES - N_SSD_HEADS)))
    head_of_lane = jnp.arange(D_INNER, dtype=jnp.int32) // SSD_HEAD_DIM
    expand = (jnp.arange(HEAD_LANES, dtype=jnp.int32)[:, None] == head_of_lane[None, :]).astype(bf16)
    return dict(
        gmix=norm_mix[l][None, :], pscale=pool_scale[l][None, :], convw=conv_w[l], convb=conv_b[l][None, :],
        dtb=pad_heads(dt_bias[l][None, :]), alog=pad_heads(a_log[l][None, :]),
        dskip=jnp.repeat(d_skip[l], SSD_HEAD_DIM)[None, :], snorm=ssd_norm[l][None, :], expand=expand)


def kernel(x_prompt, x_sample, state_pool, state_conv, state_ssm, norm_ffn1, ffn1_w_in, ffn1_w_out, norm_mix, w_in,
           pool_w_group, pool_scale, pool_w_out, conv_w, conv_b, dt_bias, a_log, d_skip, ssd_norm, ssd_w_out, w_o,
           norm_ffn2, ffn2_w_in, ffn2_w_out, norm_final):
    depth = w_in.shape[0]
    B, S, _ = x_prompt.shape
    DB, T, _ = x_sample.shape
    gfin = norm_final[None, :]
    xp = x_prompt.reshape(B * S, D_MODEL)
    xs = x_sample.reshape(DB * T, D_MODEL)
    outs = [[] for _ in range(6)]
    for l in range(depth):
        last = l == depth - 1
        g1, g2 = norm_ffn1[l][None, :], norm_ffn2[l][None, :]
        w = _small_params(l, norm_mix, pool_scale, conv_w, conv_b, dt_bias, a_log, d_skip, ssd_norm)
        n_pg = pool_w_group.shape[1]
        xp, xs, in_proj, casts = _ffn_and_casts(
            xp, xs, g1, ffn1_w_in[l], ffn1_w_out[l], gfin, w_in[l].T,
            [pool_w_group[l].reshape(n_pg * POOL_GROUP, POOL_GROUP), pool_w_out[l], ssd_w_out[l], w_o[l],
             ffn2_w_in[l], ffn2_w_out[l]])
        w.update(zip(("wu", "wz", "wxbc", "wdt", "wgate"), in_proj))
        w.update(pgw=casts[0].reshape(n_pg, POOL_GROUP, POOL_GROUP), pwo=casts[1], swo=casts[2], wo=casts[3])
        f2_in, f2_out = casts[4], casts[5]
        hist_major = lambda v: jnp.transpose(v, (1, 0, 2))
        bp, cxs, cb, cc, cdt, spool, sconv = _sample_pre(xs, hist_major(state_pool[l]), hist_major(state_conv[l]), w, T)
        spool, sconv = hist_major(spool)[None], hist_major(sconv)[None]
        xp3, npool, nconv, nssm, xs, sssm = _mixer_prompt(xp.reshape(B, S, D_MODEL), w, xs, bp, cxs, cb, cc, cdt,
                                                          state_ssm[l:l + 1], T)
        xp, xs = _ffn(xp3.reshape(B * S, D_MODEL), xs, g2, f2_in, f2_out, gfin, final_norm=last)
        for acc, v in zip(outs, (npool, nconv, nssm, spool, sconv, sssm)):
            acc.append(v)
    stack = lambda vs: vs[0] if len(vs) == 1 else jnp.concatenate(vs, axis=0)
    return (xp.reshape(B, S, D_MODEL), xs.reshape(DB, T, D_MODEL),
            stack(outs[0]), stack(outs[1]), stack(outs[2]), stack(outs[3]), stack(outs[4]), stack(outs[5]))
```

```python
import functools

import jax
import jax.numpy as jnp
from jax import lax
from jax.experimental import pallas as pl
from jax.experimental.pallas import tpu as pltpu

f32 = jnp.float32
bf16 = jnp.bfloat16

D_MODEL = 1024
D_FF = 2816
POOL_WINDOWS = (2, 4, 8, 16)
POOL_WIDTH = D_MODEL
POOL_GROUP = POOL_WIDTH // len(POOL_WINDOWS)
POOL_HIST = max(POOL_WINDOWS) - 1
D_INNER = 2 * D_MODEL
SSD_HEAD_DIM = 64
N_SSD_HEADS = D_INNER // SSD_HEAD_DIM
N_SSD_GROUPS = 4
HEADS_PER_GROUP = N_SSD_HEADS // N_SSD_GROUPS
GROUP_WIDTH = D_INNER // N_SSD_GROUPS
D_STATE = 128
CONV_WIDTH = 4
BC_DIM = N_SSD_GROUPS * D_STATE
CONV_DIM = D_INNER + 2 * BC_DIM
CHUNK = 128
PAST_LEN = 16384
EPS = 1e-6
LOG2_E = 1.4426950408889634

LANES = 128
SUBLANES = 8
MXU_COLS = 256
VMEM_LIMIT_BYTES = 56 * 1024 * 1024
LARGE_VMEM_LIMIT_BYTES = 60 * 1024 * 1024

FFN_ROWS = 512
FFN_CHUNK = 256
FFN_STAGE_COLS = 128
FFN_STAGE_SLOTS = 4
WEIGHT_STREAM_DMA_PRIORITY = 1
PROMPT_TILE = 256
SAMPLE_SEQS = 32
POOL_PAD = 16
CONV_PAD = 8
HEAD_LANES = LANES
SLABS_PER_BUF = MXU_COLS // LANES
N_POOL_BUFS = POOL_WIDTH // MXU_COLS
N_CONV_BUFS = CONV_DIM // MXU_COLS

_NT = (((1,), (1,)), ((), ()))
_TN = (((0,), (0,)), ((), ()))


def _rms(x, g):
    return x * lax.rsqrt(jnp.mean(x * x, axis=-1, keepdims=True) + EPS) * g


def _sigmoid(v):
    return 0.5 * jnp.tanh(0.5 * v) + 0.5


def _silu(v):
    h = 0.5 * v
    return h * jnp.tanh(h) + h


def _softplus(v):
    return jnp.maximum(v, 0.0) + jnp.log1p(jnp.exp(-jnp.abs(v)))


def _dot(a, b):
    return jnp.dot(a, b, preferred_element_type=f32)


def _split3(v):
    hi = v.astype(bf16)
    r1 = v - hi.astype(f32)
    mid = r1.astype(bf16)
    lo = (r1 - mid.astype(f32)).astype(bf16)
    return hi, mid, lo


def _const_spec(shape):
    nd = len(shape)
    return pl.BlockSpec(shape, lambda *_: (0,) * nd, pipeline_mode=pl.Buffered(1))


def _params(n_grid, vmem_limit_bytes=VMEM_LIMIT_BYTES):
    return pltpu.CompilerParams(dimension_semantics=("arbitrary",) * n_grid,
                                vmem_limit_bytes=vmem_limit_bytes)


def _ffn_kernel(xa_ref, xb_ref, g_ref, win_ref, wout_ref, gfin_ref, oa_ref, ob_ref, hn_s, act_s, *, n_a, final_norm,
                side_work=None, fetch=None):
    def tile(x_ref, o_ref, extra=None, fetch=None):
        commit = extra() if extra is not None else None
        hn_s[...] = _rms(x_ref[...], g_ref[...]).astype(bf16)
        for c in range(D_FF // FFN_CHUNK):
            lo = c * FFN_CHUNK
            if fetch is not None:
                fetch(c)
            gate = _dot(hn_s[...], win_ref[:, lo:lo + FFN_CHUNK])
            up = _dot(hn_s[...], win_ref[:, D_FF + lo:D_FF + lo + FFN_CHUNK])
            act_s[:, lo:lo + FFN_CHUNK] = (_silu(gate) * up).astype(bf16)
        out = x_ref[...] + 0.5 * _dot(act_s[...], wout_ref[...])
        if final_norm:
            out = _rms(out, gfin_ref[...])
        o_ref[...] = out
        if commit is not None:
            commit()

    i = pl.program_id(0)
    on_a = i < n_a
    if fetch is None:
        pl.when(on_a)(lambda: tile(xa_ref, oa_ref, side_work))
    else:
        pl.when(i == 0)(lambda: tile(xa_ref, oa_ref, side_work, fetch))
        pl.when(jnp.logical_and(i > 0, on_a))(lambda: tile(xa_ref, oa_ref, side_work))
    pl.when(jnp.logical_not(on_a))(lambda: tile(xb_ref, ob_ref))


def _ffn_specs(rows_a, rows_b):
    assert rows_a % FFN_ROWS == 0 and rows_b % FFN_ROWS == 0
    n_a, n_b = rows_a // FFN_ROWS, rows_b // FFN_ROWS
    a_spec = pl.BlockSpec((FFN_ROWS, D_MODEL), lambda i: (jnp.minimum(i, n_a - 1), 0))
    b_spec = pl.BlockSpec((FFN_ROWS, D_MODEL), lambda i: (jnp.maximum(i - n_a, 0), 0))
    in_specs = [a_spec, b_spec,
                _const_spec((1, D_MODEL)),
                _const_spec((D_MODEL, 2 * D_FF)),
                _const_spec((D_FF, D_MODEL)),
                _const_spec((1, D_MODEL))]
    out_shapes = [jax.ShapeDtypeStruct((rows_a, D_MODEL), f32), jax.ShapeDtypeStruct((rows_b, D_MODEL), f32)]
    scratch = [pltpu.VMEM((FFN_ROWS, D_MODEL), bf16), pltpu.VMEM((FFN_ROWS, D_FF), bf16)]
    return n_a, n_b, in_specs, [a_spec, b_spec], out_shapes, scratch


def _ffn(xa, xb, g, w_in_b, w_out_b, gfin, *, final_norm):
    n_a, n_b, in_specs, out_specs, out_shapes, scratch = _ffn_specs(xa.shape[0], xb.shape[0])
    return pl.pallas_call(
        functools.partial(_ffn_kernel, n_a=n_a, final_norm=final_norm),
        grid=(n_a + n_b,),
        in_specs=in_specs, out_specs=tuple(out_specs), out_shape=tuple(out_shapes), scratch_shapes=scratch,
        compiler_params=_params(1),
        name="ffn_final" if final_norm else "ffn",
    )(xa, xb, g, w_in_b, w_out_b, gfin)


_IN_PROJ_SPLITS = (POOL_WIDTH, POOL_WIDTH + D_INNER, POOL_WIDTH + D_INNER + CONV_DIM,
                   POOL_WIDTH + D_INNER + CONV_DIM + N_SSD_HEADS)
BF16_SUBLANES = 2 * SUBLANES


def _cast_block(n_rows, n_steps):
    period = 1
    while (n_rows * period) % n_steps or ((n_rows * period) // n_steps) % BF16_SUBLANES:
        period *= 2
        assert period <= n_steps
    return (n_rows * period) // n_steps, period


_PIECE_TILES = (POOL_WIDTH // MXU_COLS, D_INNER // MXU_COLS, CONV_DIM // MXU_COLS, 2 * D_MODEL // MXU_COLS)
_GATE_SKEW = _IN_PROJ_SPLITS[3] - _IN_PROJ_SPLITS[2]


def _store_when(cond, ref, val):
    @pl.when(cond)
    def _():
        ref[...] = val


def _ffn_cast_kernel(xa_ref, xb_ref, g_ref, win_hbm, wout_hbm, gfin_ref, wt_ref, wdt_f, wg_hi, *rest,
                     n_a, n_plain):
    plain_f = rest[:n_plain]
    oa_ref, ob_ref = rest[n_plain:n_plain + 2]
    wu_o, wz_o, wxbc_o, wdt_o, wgate_o = rest[n_plain + 2:n_plain + 7]
    plain_o = rest[n_plain + 7:2 * n_plain + 7]
    hn_s, act_s, win_s, wout_s, stage_in, stage_out, sems = rest[2 * n_plain + 7:]
    i = pl.program_id(0)
    n_u, n_z, n_x, n_g = _PIECE_TILES
    front = n_u + n_z + n_x

    def stage_copies(k):
        slot, lo = k % FFN_STAGE_SLOTS, k * FFN_STAGE_COLS
        return (pltpu.make_async_copy(win_hbm.at[:, pl.ds(lo, FFN_STAGE_COLS)], stage_in.at[slot, 0],
                                      sems.at[slot, 0]),
                pltpu.make_async_copy(win_hbm.at[:, pl.ds(D_FF + lo, FFN_STAGE_COLS)], stage_in.at[slot, 1],
                                      sems.at[slot, 1]),
                pltpu.make_async_copy(wout_hbm.at[pl.ds(lo, FFN_STAGE_COLS), :], stage_out.at[slot],
                                      sems.at[slot, 2]))

    def fetch(c):
        per_chunk = FFN_CHUNK // FFN_STAGE_COLS
        for k in range(c * per_chunk, (c + 1) * per_chunk):
            slot, lo = k % FFN_STAGE_SLOTS, k * FFN_STAGE_COLS
            for nxt in (range(FFN_STAGE_SLOTS) if k == 0 else [k + FFN_STAGE_SLOTS - 1]):
                if nxt < D_FF // FFN_STAGE_COLS:
                    for cp in stage_copies(nxt):
                        cp.start(priority=WEIGHT_STREAM_DMA_PRIORITY)
            for cp in stage_copies(k):
                cp.wait()
            win_s[:, lo:lo + FFN_STAGE_COLS] = stage_in[slot, 0].astype(bf16)
            win_s[:, D_FF + lo:D_FF + lo + FFN_STAGE_COLS] = stage_in[slot, 1].astype(bf16)
            wout_s[lo:lo + FFN_STAGE_COLS, :] = stage_out[slot].astype(bf16)

    def casts():
        t = wt_ref[...].T.astype(bf16)
        rows = jnp.concatenate([wt_ref[_GATE_SKEW:, :], wg_hi[0:_GATE_SKEW, :]], axis=0)
        tg = rows.T.astype(bf16)
        for src, dst in zip(plain_f, plain_o):
            dst[...] = src[...].astype(bf16)

        def commit():
            _store_when(i < n_u, wu_o, t)
            _store_when(jnp.logical_and(i >= n_u, i < n_u + n_z), wz_o, t)
            _store_when(jnp.logical_and(i >= n_u + n_z, i < front), wxbc_o, t)
            _store_when(jnp.logical_and(i >= front, i < front + n_g), wgate_o, tg)
        return commit

    _ffn_kernel(xa_ref, xb_ref, g_ref, win_s, wout_s, gfin_ref, oa_ref, ob_ref, hn_s, act_s,
                n_a=n_a, final_norm=False, side_work=casts, fetch=fetch)

    @pl.when(i == 0)
    def _():
        t = wdt_f[...].T
        lane = lax.broadcasted_iota(jnp.int32, t.shape, 1)
        wdt_o[...] = jnp.where(lane < N_SSD_HEADS, t, 0.0).astype(bf16)


def _ffn_and_casts(xa, xb, g, w_in_f, w_out_f, gfin, w_in_t, plain):
    n_a, n_b, in_specs, out_specs, out_shapes, scratch = _ffn_specs(xa.shape[0], xb.shape[0])
    in_specs[3] = in_specs[4] = pl.BlockSpec(memory_space=pl.ANY)
    scratch = scratch + [pltpu.VMEM((D_MODEL, 2 * D_FF), bf16), pltpu.VMEM((D_FF, D_MODEL), bf16),
                         pltpu.VMEM((FFN_STAGE_SLOTS, 2, D_MODEL, FFN_STAGE_COLS), f32),
                         pltpu.VMEM((FFN_STAGE_SLOTS, FFN_STAGE_COLS, D_MODEL), f32),
                         pltpu.SemaphoreType.DMA((FFN_STAGE_SLOTS, 3))]
    n_u, n_z, n_x, n_g = _PIECE_TILES
    front = n_u + n_z + n_x
    assert n_a >= front + n_g and _IN_PROJ_SPLITS[2] == front * MXU_COLS and _IN_PROJ_SPLITS[2] % HEAD_LANES == 0

    def blocked(arr):
        blk, period = _cast_block(arr.shape[0], n_a)
        return (pl.BlockSpec((blk, arr.shape[1]), lambda i: (jnp.minimum(i, n_a - 1) // period, 0)),
                jax.ShapeDtypeStruct(arr.shape, bf16))

    def piece(first_step, n_tiles):
        return (pl.BlockSpec((D_MODEL, MXU_COLS), lambda i: (0, jnp.clip(i - first_step, 0, n_tiles - 1))),
                jax.ShapeDtypeStruct((D_MODEL, n_tiles * MXU_COLS), bf16))

    pieces = [piece(0, n_u), piece(n_u, n_z), piece(n_u + n_z, n_x),
              (pl.BlockSpec((D_MODEL, HEAD_LANES), lambda i: (0, 0)), jax.ShapeDtypeStruct((D_MODEL, HEAD_LANES), bf16)),
              piece(front, n_g)]
    in_proj_specs = [pl.BlockSpec((MXU_COLS, D_MODEL), lambda i: (jnp.minimum(i, front + n_g - 1), 0)),
                     pl.BlockSpec((HEAD_LANES, D_MODEL), lambda i: (_IN_PROJ_SPLITS[2] // HEAD_LANES, 0)),
                     pl.BlockSpec((MXU_COLS, D_MODEL), lambda i: (front + 1 + jnp.clip(i - front, 0, n_g - 1), 0))]
    plains = [blocked(p) for p in plain]
    res = pl.pallas_call(
        functools.partial(_ffn_cast_kernel, n_a=n_a, n_plain=len(plain)),
        grid=(n_a + n_b,),
        in_specs=in_specs + in_proj_specs + [s for s, _ in plains],
        out_specs=tuple(out_specs + [s for s, _ in pieces] + [s for s, _ in plains]),
        out_shape=tuple(out_shapes + [o for _, o in pieces] + [o for _, o in plains]),
        scratch_shapes=scratch,
        compiler_params=_params(1, LARGE_VMEM_LIMIT_BYTES),
        name="ffn_casts",
    )(xa, xb, g, w_in_f, w_out_f, gfin, w_in_t, w_in_t, w_in_t, *plain)
    return res[0], res[1], res[2:7], res[7:]


def _slab(bufs, j):
    return bufs[j // SLABS_PER_BUF], j % SLABS_PER_BUF


def _window_scratch(n_bufs, n_seqs, n_rows):
    return [pltpu.VMEM((SLABS_PER_BUF, n_seqs, n_rows, LANES), f32) for _ in range(n_bufs)]


def _store_rows(bufs, row0, val, T, slab0=0):
    for j in range(val.shape[1] // LANES):
        ref, k = _slab(bufs, slab0 + j)
        S = ref.shape[1]
        blk = val[:, j * LANES:(j + 1) * LANES]
        if S == 1:
            ref[k, 0, row0:row0 + T, :] = blk
        else:
            ref[k, :, row0:row0 + T, :] = blk.reshape(S, T, LANES)


def _window(bufs, j, row0, n):
    ref, k = _slab(bufs, j)
    S = ref.shape[1]
    rows = pl.ds(row0, n) if row0 % SUBLANES == 0 else pl.ds(row0, n, stride=1)
    if S == 1:
        return ref[k, 0, rows, :]
    v = ref[k, :, rows, :]
    return v.reshape(S * n, LANES) if n % SUBLANES == 0 else v


def _pool_branch(hn_s, wu_ref, ubufs, T, pos0, pgw_ref, pscale_ref, pwo_ref):
    S = ubufs[0].shape[1]
    assert T & (T - 1) == 0 and POOL_GROUP == MXU_COLS
    t_idx = jnp.bitwise_and(lax.broadcasted_iota(jnp.int32, (S * T, 1), 0), T - 1)
    n_seen = pos0 + 1 + t_idx
    _store_rows(ubufs, POOL_PAD, _dot(hn_s[...], wu_ref[...]), T)
    mixed = []
    for gi, w in enumerate(POOL_WINDOWS):
        inv_cnt = 1.0 / jnp.minimum(n_seen, w).astype(f32)
        d = []
        for j in range(gi * SLABS_PER_BUF, (gi + 1) * SLABS_PER_BUF):
            cur = _window(ubufs, j, POOL_PAD, T)
            s = cur
            for k in range(1, w):
                s = s + _window(ubufs, j, POOL_PAD - k, T)
            d.append((s * inv_cnt - cur).astype(bf16))
        mixed.append(_dot(jnp.concatenate(d, axis=-1), pgw_ref[gi]))
    y = jnp.concatenate(mixed, axis=-1) * pscale_ref[...]
    return _dot(y.astype(bf16), pwo_ref[...])


def _conv_slab(xbufs, T, convw_ref, convb_ref, j, xs_ref, b_ref, c_ref):
    c0 = j * LANES
    cols = slice(c0, c0 + LANES)
    h = 0.5 * convb_ref[:, cols]
    for k in range(CONV_WIDTH):
        h = h + _window(xbufs, j, CONV_PAD - (CONV_WIDTH - 1) + k, T) * (0.5 * convw_ref[k:k + 1, cols])
    v = h * jnp.tanh(h) + h
    if c0 < D_INNER:
        xs_ref[:, cols] = v
    elif c0 < D_INNER + BC_DIM:
        b_ref[:, c0 - D_INNER:c0 - D_INNER + LANES] = v
    else:
        c_ref[:, c0 - D_INNER - BC_DIM:c0 - D_INNER - BC_DIM + LANES] = v


def _project_conv(hn_s, wxbc_ref, xbufs, T, convw_ref, convb_ref, xs_ref, b_ref, c_ref):
    for jt in range(N_CONV_BUFS):
        slab0 = jt * SLABS_PER_BUF
        _store_rows(xbufs, CONV_PAD, _dot(hn_s[...], wxbc_ref[:, jt * MXU_COLS:(jt + 1) * MXU_COLS]), T, slab0=slab0)
        for j in range(slab0, slab0 + SLABS_PER_BUF):
            _conv_slab(xbufs, T, convw_ref, convb_ref, j, xs_ref, b_ref, c_ref)


def _gate_merge_out(x, hn_s, y_ref, bp, wz_ref, wgate_ref, snorm_ref, swo_ref, wo_ref, yn_s):
    for g in range(N_SSD_GROUPS):
        cols = slice(g * GROUP_WIDTH, (g + 1) * GROUP_WIDTH)
        z = _dot(hn_s[...], wz_ref[:, cols])
        yg = y_ref[:, cols] * _silu(z)
        yg = yg * lax.rsqrt(jnp.mean(yg * yg, axis=-1, keepdims=True) + EPS)
        yn_s[:, cols] = (yg * snorm_ref[:, cols]).astype(bf16)
    branch_ssd = _dot(yn_s[...], swo_ref[...])
    gate_pool = _sigmoid(_dot(hn_s[...], wgate_ref[:, 0:D_MODEL]))
    gate_ssd = _sigmoid(_dot(hn_s[...], wgate_ref[:, D_MODEL:2 * D_MODEL]))
    merged = (gate_pool * bp + gate_ssd * branch_ssd).astype(bf16)
    return x + _dot(merged, wo_ref[...])


def _ssd_chunk(rows, xs_s, b_s, c_s, dt_s, y_s, ht_s, alog_ref, dskip_ref, e_ref):
    L = CHUNK
    ri = lax.broadcasted_iota(jnp.int32, (L, L), 0)
    ci = lax.broadcasted_iota(jnp.int32, (L, L), 1)
    causal = ri >= ci
    tril = jnp.where(causal, 1.0, 0.0).astype(bf16)
    first_head = lax.broadcasted_iota(jnp.int32, (L, LANES), 1) < SSD_HEAD_DIM
    keep_first = jnp.where(first_head, 1.0, 0.0).astype(bf16)
    keep_second = jnp.where(first_head, 0.0, 1.0).astype(bf16)

    dt = dt_s[rows, :]
    dA = dt * (-jnp.exp(alog_ref[...]))
    hi, mid, lo = _split3(dA)
    acc = _dot(tril, jnp.concatenate([hi, mid, lo], axis=1))
    a = (acc[:, 0:LANES] + acc[:, LANES:2 * LANES] + acc[:, 2 * LANES:3 * LANES]) * LOG2_E
    aT = a.T[0:N_SSD_HEADS]
    dtT = dt.T[0:N_SSD_HEADS]
    wT = jnp.exp2(aT[:, L - 1:L] - aT) * dtT
    srcT = aT - jnp.log2(dtT)
    end_decay = jnp.broadcast_to(jnp.exp2(a[L - 1:L, :]), (2 * SUBLANES, HEAD_LANES))
    e_hi, e_mid, e_lo = (t.astype(f32) for t in _split3(end_decay))
    sel = lax.broadcasted_iota(jnp.int32, (2 * SUBLANES, HEAD_LANES), 0)
    stacked = jnp.where(sel == 0, e_hi, jnp.where(sel == 1, e_mid, jnp.where(sel == 2, e_lo, 0.0)))
    cdec = jnp.sum(_dot(stacked.astype(bf16), e_ref[...]), axis=0, keepdims=True)

    def head_mats(h, cb, bT):
        acol = jnp.broadcast_to(a[:, h:h + 1], (L, L))
        m = (cb * jnp.exp2(jnp.where(causal, acol - srcT[h:h + 1, :], -jnp.inf))).astype(bf16)
        bw = (bT * wT[h:h + 1, :]).astype(bf16)
        return m, bw, acol

    pairs_per_group = HEADS_PER_GROUP // 2
    for q in range(N_SSD_HEADS // 2):
        g, qg = divmod(q, pairs_per_group)
        if qg == 0:
            gcols = slice(g * D_STATE, (g + 1) * D_STATE)
            b_f = b_s[rows, gcols]
            c_b = c_s[rows, gcols].astype(bf16)
            cb = lax.dot_general(c_b, b_f.astype(bf16), _NT, preferred_element_type=f32)
            bT = b_f.T
            ch = _dot(c_b, ht_s[:, g * GROUP_WIDTH:(g + 1) * GROUP_WIDTH].astype(bf16))
        cols = slice(q * LANES, (q + 1) * LANES)
        xq = xs_s[rows, cols]
        xq_b = xq.astype(bf16)
        x2 = jnp.concatenate([xq_b * keep_first, xq_b * keep_second], axis=0)
        m_a, bw_a, acol_a = head_mats(2 * q, cb, bT)
        m_b, bw_b, acol_b = head_mats(2 * q + 1, cb, bT)
        y_off = ch[:, qg * LANES:(qg + 1) * LANES] * jnp.exp2(jnp.where(first_head, acol_a, acol_b))
        y_s[rows, cols] = _dot(jnp.concatenate([m_a, m_b], axis=1), x2) + y_off + dskip_ref[:, cols] * xq
        st = _dot(jnp.concatenate([bw_a, bw_b], axis=1), x2)
        ht_s[:, cols] = ht_s[:, cols] * cdec[:, cols] + st


def _mixer_prompt_kernel(x_ref, dx_ref, dbp_ref, dxs_ref, db_ref, dc_ref, ddt_ref, dh0_ref,
                         gmix_ref, wu_ref, wz_ref, wxbc_ref, wdt_ref, wgate_ref, pgw_ref, pscale_ref,
                         pwo_ref, convw_ref, convb_ref, dtb_ref, alog_ref, dskip_ref, snorm_ref, swo_ref,
                         wo_ref, e_ref,
                         o_ref, npool_ref, nconv_ref, nssm_ref, do_ref, dhn_ref,
                         hn_all, xs_s, b_s, c_s, dt_s, y_s, ht_s, yn_s, *window_bufs, decode_T):
    T = PROMPT_TILE
    ti = pl.program_id(1)
    ubufs, xbufs = window_bufs[:N_POOL_BUFS], window_bufs[N_POOL_BUFS:]
    hn_s = hn_all.at[0:T]

    @pl.when(ti == 0)
    def _():
        for ref in ubufs:
            ref[:, :, 0:POOL_PAD, :] = jnp.zeros((SLABS_PER_BUF, 1, POOL_PAD, LANES), f32)
        for ref in xbufs:
            ref[:, :, 0:CONV_PAD, :] = jnp.zeros((SLABS_PER_BUF, 1, CONV_PAD, LANES), f32)
        ht_s[...] = jnp.zeros(ht_s.shape, f32)

    x = x_ref[0]
    hn_s[...] = _rms(x, gmix_ref[...]).astype(bf16)

    bp = _pool_branch(hn_s, wu_ref, ubufs, T, ti * T, pgw_ref, pscale_ref, pwo_ref)

    _project_conv(hn_s, wxbc_ref, xbufs, T, convw_ref, convb_ref, xs_s, b_s, c_s)
    dt_s[...] = _softplus(_dot(hn_s[...], wdt_ref[...]) + dtb_ref[...])

    def chunk(c, carry):
        rows = pl.ds(pl.multiple_of(c * CHUNK, CHUNK), CHUNK)
        _ssd_chunk(rows, xs_s, b_s, c_s, dt_s, y_s, ht_s, alog_ref, dskip_ref, e_ref)
        return carry

    lax.fori_loop(0, T // CHUNK, chunk, 0)

    _decode_seqs(dxs_ref, db_ref, dc_ref, ddt_ref, dh0_ref, alog_ref, dskip_ref, e_ref, y_s, dhn_ref,
                 T=decode_T, n_seqs=dxs_ref.shape[0], y_row0=T)
    dx = dx_ref[...]
    hn_all[T:, :] = _rms(dx, gmix_ref[...]).astype(bf16)
    out = _gate_merge_out(jnp.concatenate([x, dx], axis=0), hn_all, y_s,
                          jnp.concatenate([bp, dbp_ref[...]], axis=0),
                          wz_ref, wgate_ref, snorm_ref, swo_ref, wo_ref, yn_s)
    o_ref[0] = out[0:T]
    do_ref[...] = out[T:]

    @pl.when(ti == pl.num_programs(1) - 1)
    def _():
        for j in range(POOL_WIDTH // LANES):
            npool_ref[0, 0, :, j * LANES:(j + 1) * LANES] = _window(ubufs, j, POOL_PAD + T - POOL_HIST, POOL_HIST)
        for j in range(CONV_DIM // LANES):
            nconv_ref[0, 0, :, j * LANES:(j + 1) * LANES] = _window(
                xbufs, j, CONV_PAD + T - (CONV_WIDTH - 1), CONV_WIDTH - 1)
        for g in range(N_SSD_GROUPS):
            hg = ht_s[:, g * GROUP_WIDTH:(g + 1) * GROUP_WIDTH].T
            nssm_ref[0, 0, g * HEADS_PER_GROUP:(g + 1) * HEADS_PER_GROUP] = hg.reshape(
                HEADS_PER_GROUP, SSD_HEAD_DIM, D_STATE)

    for ref in ubufs:
        ref[:, :, 0:POOL_PAD, :] = ref[:, :, T:T + POOL_PAD, :]
    for ref in xbufs:
        ref[:, :, 0:CONV_PAD, :] = ref[:, :, T:T + CONV_PAD, :]


def _mixer_prompt(x1, w, dx1, dbp, dxs, dbm, dcm, ddt, dh0, decode_T):
    B, S, _ = x1.shape
    T = PROMPT_TILE
    assert S % T == 0 and T % CHUNK == 0 and T >= POOL_PAD
    n_tiles = S // T
    nseq = dxs.shape[0] // decode_T
    assert nseq % (B * n_tiles) == 0
    Q = nseq // (B * n_tiles)
    seq_spec = lambda width: pl.BlockSpec((Q, decode_T, width), lambda b, t: (b * n_tiles + t, 0, 0))
    state_spec = pl.BlockSpec((1, Q, N_SSD_HEADS, SSD_HEAD_DIM, D_STATE), lambda b, t: (0, b * n_tiles + t, 0, 0, 0))
    R = Q * decode_T
    assert R % BF16_SUBLANES == 0
    row_spec = pl.BlockSpec((R, D_MODEL), lambda b, t: (b * n_tiles + t, 0))
    as_seqs = lambda v: v.reshape(nseq, decode_T, v.shape[1])
    consts = [w["gmix"], w["wu"], w["wz"], w["wxbc"], w["wdt"], w["wgate"], w["pgw"], w["pscale"], w["pwo"],
              w["convw"], w["convb"], w["dtb"], w["alog"], w["dskip"], w["snorm"], w["swo"], w["wo"], w["expand"]]
    return pl.pallas_call(
        functools.partial(_mixer_prompt_kernel, decode_T=decode_T),
        grid=(B, n_tiles),
        in_specs=[pl.BlockSpec((1, T, D_MODEL), lambda b, t: (b, t, 0)), row_spec, row_spec,
                  seq_spec(D_INNER), seq_spec(BC_DIM), seq_spec(BC_DIM), seq_spec(HEAD_LANES), state_spec]
                 + [_const_spec(c.shape) for c in consts],
        out_specs=(pl.BlockSpec((1, T, D_MODEL), lambda b, t: (b, t, 0)),
                   pl.BlockSpec((1, 1, POOL_HIST, POOL_WIDTH), lambda b, t: (0, b, 0, 0)),
                   pl.BlockSpec((1, 1, CONV_WIDTH - 1, CONV_DIM), lambda b, t: (0, b, 0, 0)),
                   pl.BlockSpec((1, 1, N_SSD_HEADS, SSD_HEAD_DIM, D_STATE), lambda b, t: (0, b, 0, 0, 0)),
                   row_spec, state_spec),
        out_shape=(jax.ShapeDtypeStruct((B, S, D_MODEL), f32),
                   jax.ShapeDtypeStruct((1, B, POOL_HIST, POOL_WIDTH), f32),
                   jax.ShapeDtypeStruct((1, B, CONV_WIDTH - 1, CONV_DIM), f32),
                   jax.ShapeDtypeStruct((1, B, N_SSD_HEADS, SSD_HEAD_DIM, D_STATE), f32),
                   jax.ShapeDtypeStruct(dx1.shape, f32),
                   jax.ShapeDtypeStruct(dh0.shape, f32)),
        scratch_shapes=[pltpu.VMEM((T + R, D_MODEL), bf16),
                        pltpu.VMEM((T, D_INNER), f32),
                        pltpu.VMEM((T, BC_DIM), f32),
                        pltpu.VMEM((T, BC_DIM), f32),
                        pltpu.VMEM((T, HEAD_LANES), f32),
                        pltpu.VMEM((T + R, D_INNER), f32),
                        pltpu.VMEM((D_STATE, D_INNER), f32),
                        pltpu.VMEM((T + R, D_INNER), bf16)]
                       + _window_scratch(N_POOL_BUFS, 1, POOL_PAD + T)
                       + _window_scratch(N_CONV_BUFS, 1, CONV_PAD + T),
        compiler_params=_params(2, LARGE_VMEM_LIMIT_BYTES),
        name="mixer_prompt",
    )(x1, dx1, dbp, as_seqs(dxs), as_seqs(dbm), as_seqs(dcm), as_seqs(ddt), dh0, *consts)


def _load_history(bufs, hist_ref, row0):
    for j in range(hist_ref.shape[2] // LANES):
        ref, k = _slab(bufs, j)
        n_slabs, S, n_rows, _ = ref.shape
        flat = ref.reshape(n_slabs, S * n_rows, LANES)
        for r in range(hist_ref.shape[0]):
            flat[k, pl.ds(row0 + r, S, stride=n_rows), :] = hist_ref[r, :, j * LANES:(j + 1) * LANES]


def _store_history(hist_ref, bufs, row0):
    for j in range(hist_ref.shape[2] // LANES):
        ref, k = _slab(bufs, j)
        n_slabs, S, n_rows, _ = ref.shape
        flat = ref.reshape(n_slabs, S * n_rows, LANES)
        for r in range(hist_ref.shape[0]):
            hist_ref[r, :, j * LANES:(j + 1) * LANES] = flat[k, pl.ds(row0 + r, S, stride=n_rows), :]


def _sample_pre_kernel(x_ref, ph_ref, ch_ref, gmix_ref, wu_ref, wxbc_ref, wdt_ref, pgw_ref, pscale_ref, pwo_ref,
                       convw_ref, convb_ref, dtb_ref,
                       bp_ref, xs_ref, b_ref, c_ref, dt_ref, npool_ref, nconv_ref,
                       *window_bufs, T):
    ubufs, xbufs = window_bufs[:N_POOL_BUFS], window_bufs[N_POOL_BUFS:]
    hn = _rms(x_ref[...], gmix_ref[...]).astype(bf16)

    _load_history(ubufs, ph_ref, POOL_PAD - POOL_HIST)
    bp_ref[...] = _pool_branch(hn, wu_ref, ubufs, T, PAST_LEN, pgw_ref, pscale_ref, pwo_ref)
    _store_history(npool_ref, ubufs, POOL_PAD + T - POOL_HIST)

    _load_history(xbufs, ch_ref, CONV_PAD - (CONV_WIDTH - 1))
    _project_conv(hn, wxbc_ref, xbufs, T, convw_ref, convb_ref, xs_ref, b_ref, c_ref)
    _store_history(nconv_ref, xbufs, CONV_PAD + T - (CONV_WIDTH - 1))
    dt_ref[...] = _softplus(_dot(hn, wdt_ref[...]) + dtb_ref[...])


def _sample_pre(x1, pool_hist, conv_hist, w, T):
    rows = x1.shape[0]
    nseq = rows // T
    S = SAMPLE_SEQS
    R = S * T
    assert nseq % S == 0 and T % SUBLANES == 0
    consts = [w["gmix"], w["wu"], w["wxbc"], w["wdt"], w["pgw"], w["pscale"], w["pwo"], w["convw"], w["convb"],
              w["dtb"]]
    row_spec = lambda width: pl.BlockSpec((R, width), lambda i: (i, 0))
    return pl.pallas_call(
        functools.partial(_sample_pre_kernel, T=T),
        grid=(nseq // S,),
        in_specs=[row_spec(D_MODEL),
                  pl.BlockSpec((POOL_HIST, S, POOL_WIDTH), lambda i: (0, i, 0)),
                  pl.BlockSpec((CONV_WIDTH - 1, S, CONV_DIM), lambda i: (0, i, 0))]
                 + [_const_spec(c.shape) for c in consts],
        out_specs=(row_spec(D_MODEL), row_spec(D_INNER), row_spec(BC_DIM), row_spec(BC_DIM), row_spec(HEAD_LANES),
                   pl.BlockSpec((POOL_HIST, S, POOL_WIDTH), lambda i: (0, i, 0)),
                   pl.BlockSpec((CONV_WIDTH - 1, S, CONV_DIM), lambda i: (0, i, 0))),
        out_shape=(jax.ShapeDtypeStruct((rows, D_MODEL), f32),
                   jax.ShapeDtypeStruct((rows, D_INNER), f32),
                   jax.ShapeDtypeStruct((rows, BC_DIM), f32),
                   jax.ShapeDtypeStruct((rows, BC_DIM), f32),
                   jax.ShapeDtypeStruct((rows, HEAD_LANES), f32),
                   jax.ShapeDtypeStruct((POOL_HIST, nseq, POOL_WIDTH), f32),
                   jax.ShapeDtypeStruct((CONV_WIDTH - 1, nseq, CONV_DIM), f32)),
        scratch_shapes=_window_scratch(N_POOL_BUFS, S, POOL_PAD + T) + _window_scratch(N_CONV_BUFS, S, CONV_PAD + T),
        compiler_params=_params(1),
        name="sample_pre",
    )(x1, pool_hist, conv_hist, *consts)


def _decode_seqs(xs_ref, b_ref, c_ref, dt_ref, h0_ref, alog_ref, dskip_ref, e_ref, y_ref, hn_ref, *, T, n_seqs,
                 y_row0):
    assert T == SUBLANES
    lane_group = lax.broadcasted_iota(jnp.int32, (T, HEAD_LANES), 1) // HEADS_PER_GROUP
    neg_a = -jnp.exp(alog_ref[...])

    def shift(v, d):
        if d == 0:
            return v
        r = lax.broadcasted_iota(jnp.int32, v.shape, 0)
        return jnp.where(r >= d, pltpu.roll(v, d, axis=0), 0.0)

    def per_seq(s, carry):
        dt = dt_ref[s]
        a = dt * neg_a
        d = 1
        while d < T:
            a = a + shift(a, d)
            d *= 2
        a_end = a[T - 1:T, :]
        x = xs_ref[s]
        bm = b_ref[s]
        cm = c_ref[s]
        terms = []
        for d in range(T):
            cbv = jnp.zeros((T, HEAD_LANES), f32)
            for g in range(N_SSD_GROUPS):
                gc = slice(g * D_STATE, (g + 1) * D_STATE)
                cb = jnp.sum(cm[:, gc] * shift(bm[:, gc], d), axis=-1, keepdims=True)
                cbv = jnp.where(lane_group == g, cb, cbv)
            terms.append(cbv * jnp.exp(a - shift(a, d)) * shift(dt, d))
        terms.append(jnp.exp(a))
        terms.append(jnp.exp(a_end - a) * dt)
        v = jnp.concatenate(terms, axis=0)
        ex = _dot(v.astype(bf16), e_ref[...])

        y = dskip_ref[...] * x
        for d in range(T):
            y = y + ex[d * T:(d + 1) * T] * shift(x, d)
        ea = ex[T * T:(T + 1) * T]
        xw = (x * ex[(T + 1) * T:(T + 2) * T]).astype(bf16)
        cdec = jnp.exp(a_end)
        y_groups = []
        for g in range(N_SSD_GROUPS):
            gc = slice(g * D_STATE, (g + 1) * D_STATE)
            cols = slice(g * GROUP_WIDTH, (g + 1) * GROUP_WIDTH)
            heads = slice(g * HEADS_PER_GROUP, (g + 1) * HEADS_PER_GROUP)
            h0g = h0_ref[0, s, heads].reshape(GROUP_WIDTH, D_STATE)
            ch = lax.dot_general(cm[:, gc].astype(bf16), h0g.astype(bf16), _NT, preferred_element_type=f32)
            y_groups.append(y[:, cols] + ea[:, cols] * ch)
            st = lax.dot_general(xw[:, cols], bm[:, gc].astype(bf16), _TN, preferred_element_type=f32)
            for r in range(HEADS_PER_GROUP):
                h = g * HEADS_PER_GROUP + r
                hn_ref[0, s, h] = (h0_ref[0, s, h] * cdec[:, h:h + 1]
                                   + st[r * SSD_HEAD_DIM:(r + 1) * SSD_HEAD_DIM, :])
        y_ref[pl.ds(pl.multiple_of(y_row0 + s * T, T), T), :] = jnp.concatenate(y_groups, axis=-1)
        return carry

    lax.fori_loop(0, n_seqs, per_seq, 0, unroll=True)


def _small_params(l, norm_mix, pool_scale, conv_w, conv_b, dt_bias, a_log, d_skip, ssd_norm):
    pad_heads = lambda v: jnp.pad(v, ((0, 0), (0, HEAD_LANES - N_SSD_HEADS)))
    head_of_lane = jnp.arange(D_INNER, dtype=jnp.int32) // SSD_HEAD_DIM
    expand = (jnp.arange(HEAD_LANES, dtype=jnp.int32)[:, None] == head_of_lane[None, :]).astype(bf16)
    return dict(
        gmix=norm_mix[l][None, :], pscale=pool_scale[l][None, :], convw=conv_w[l], convb=conv_b[l][None, :],
        dtb=pad_heads(dt_bias[l][None, :]), alog=pad_heads(a_log[l][None, :]),
        dskip=jnp.repeat(d_skip[l], SSD_HEAD_DIM)[None, :], snorm=ssd_norm[l][None, :], expand=expand)


def kernel(x_prompt, x_sample, state_pool, state_conv, state_ssm, norm_ffn1, ffn1_w_in, ffn1_w_out, norm_mix, w_in,
           pool_w_group, pool_scale, pool_w_out, conv_w, conv_b, dt_bias, a_log, d_skip, ssd_norm, ssd_w_out, w_o,
           norm_ffn2, ffn2_w_in, ffn2_w_out, norm_final):
    depth = w_in.shape[0]
    B, S, _ = x_prompt.shape
    DB, T, _ = x_sample.shape
    gfin = norm_final[None, :]
    xp = x_prompt.reshape(B * S, D_MODEL)
    xs = x_sample.reshape(DB * T, D_MODEL)
    outs = [[] for _ in range(6)]
    for l in range(depth):
        last = l == depth - 1
        g1, g2 = norm_ffn1[l][None, :], norm_ffn2[l][None, :]
        w = _small_params(l, norm_mix, pool_scale, conv_w, conv_b, dt_bias, a_log, d_skip, ssd_norm)
        n_pg = pool_w_group.shape[1]
        xp, xs, in_proj, casts = _ffn_and_casts(
            xp, xs, g1, ffn1_w_in[l], ffn1_w_out[l], gfin, w_in[l].T,
            [pool_w_group[l].reshape(n_pg * POOL_GROUP, POOL_GROUP), pool_w_out[l], ssd_w_out[l], w_o[l],
             ffn2_w_in[l], ffn2_w_out[l]])
        w.update(zip(("wu", "wz", "wxbc", "wdt", "wgate"), in_proj))
        w.update(pgw=casts[0].reshape(n_pg, POOL_GROUP, POOL_GROUP), pwo=casts[1], swo=casts[2], wo=casts[3])
        f2_in, f2_out = casts[4], casts[5]
        hist_major = lambda v: jnp.transpose(v, (1, 0, 2))
        bp, cxs, cb, cc, cdt, spool, sconv = _sample_pre(xs, hist_major(state_pool[l]), hist_major(state_conv[l]), w, T)
        spool, sconv = hist_major(spool)[None], hist_major(sconv)[None]
        xp3, npool, nconv, nssm, xs, sssm = _mixer_prompt(xp.reshape(B, S, D_MODEL), w, xs, bp, cxs, cb, cc, cdt,
                                                          state_ssm[l:l + 1], T)
        xp, xs = _ffn(xp3.reshape(B * S, D_MODEL), xs, g2, f2_in, f2_out, gfin, final_norm=last)
        for acc, v in zip(outs, (npool, nconv, nssm, spool, sconv, sssm)):
            acc.append(v)
    stack = lambda vs: vs[0] if len(vs) == 1 else jnp.concatenate(vs, axis=0)
    return (xp.reshape(B, S, D_MODEL), xs.reshape(DB, T, D_MODEL),
            stack(outs[0]), stack(outs[1]), stack(outs[2]), stack(outs[3]), stack(outs[4]), stack(outs[5]))
```
